```python
import math
import jax
import jax.numpy as jnp
from jax import lax
import numpy as np

D_MODEL = 1024
BATCH = 4
SEQ = 8192
DEPTH = 2

GRID_W = 64
CTX_LEN = 256
EPS = 1e-6
HEAD_DIM = 64
D_MIX = D_MODEL
W_GROUP = D_MIX // 4
SSD_HEADS = W_GROUP // HEAD_DIM
SSD_GROUPS = 2
SSD_STATE = 64
SSD_CONV = 5
SSD_CHUNK = 128
SSD_XBC = W_GROUP + 2 * SSD_GROUPS * SSD_STATE
COLS_A = W_GROUP + SSD_XBC + 2 * SSD_HEADS
HY_CH = W_GROUP
HY_ORDER = 2
HY_SHORT = 3
HY_EMB = 33
HY_BANDS = (HY_EMB - 1) // 2
HY_FILT = 64
HY_TARGET = 1e-2
HY_FAST_PCT = 0.3
HY_SLOW_PCT = 1.5
HY_MAX_DECAY = math.log(HY_TARGET) / HY_FAST_PCT
HY_MIN_DECAY = math.log(HY_TARGET) / HY_SLOW_PCT
COLS_B = (HY_ORDER + 1) * HY_CH
ATT_HEADS = W_GROUP // HEAD_DIM
ATT_KV = 2
COLS_ATT = ATT_HEADS * HEAD_DIM + 2 * ATT_KV * HEAD_DIM
WINDOW = 128
BLOCK = 128
ROPE_THETA = 10000.0
OFF_B = COLS_A
OFF_C = OFF_B + COLS_B
OFF_D = OFF_C + COLS_ATT
D_IN = OFF_D + COLS_ATT
D_FF = 4 * D_MODEL

kernel_name = 'hybrid_parallel_groups_flow_block'


def rms_norm(x, g):
    xf = x.astype(jnp.float32)
    y = xf * lax.rsqrt(jnp.mean(xf * xf, axis=-1, keepdims=True) + EPS)
    return (y * g.astype(jnp.float32)).astype(x.dtype)


def dw_conv_centred(u, w, b):
    k = w.shape[0]
    out = lax.conv_general_dilated(
        u, w[:, None, :].astype(u.dtype), window_strides=(1,), padding=[(k // 2, k // 2)],
        dimension_numbers=('NWC', 'WIO', 'NWC'), feature_group_count=u.shape[-1])
    return out + b.astype(u.dtype)


def rope_tables(rows):
    row = jnp.repeat(jnp.arange(rows, dtype=jnp.float32), GRID_W)
    col = jnp.tile(jnp.arange(GRID_W, dtype=jnp.float32), rows)
    n_freq = HEAD_DIM // 4
    inv = ROPE_THETA ** (-jnp.arange(n_freq, dtype=jnp.float32) / n_freq)
    ang = jnp.concatenate([row[:, None] * inv, col[:, None] * inv], axis=-1)
    return jnp.cos(ang), jnp.sin(ang)


def apply_rope(t, cos, sin):
    tf = t.astype(jnp.float32)
    t1, t2 = jnp.split(tf, 2, axis=-1)
    c = cos[None, :, None, :]
    s = sin[None, :, None, :]
    return jnp.concatenate([t1 * c - t2 * s, t1 * s + t2 * c], axis=-1).astype(t.dtype)


def ssd_scan(xs, dt, a_h, bm, cm, h0):
    b, n, nh, hp = xs.shape
    nchunk = n // SSD_CHUNK
    rep = nh // SSD_GROUPS
    bh = jnp.repeat(bm, rep, axis=2).reshape(b, nchunk, SSD_CHUNK, nh, SSD_STATE)
    ch = jnp.repeat(cm, rep, axis=2).reshape(b, nchunk, SSD_CHUNK, nh, SSD_STATE)
    xc = xs.reshape(b, nchunk, SSD_CHUNK, nh, hp)
    dtc = dt.reshape(b, nchunk, SSD_CHUNK, nh)
    a_cum = jnp.cumsum(dtc * a_h, axis=2)
    seg = a_cum[:, :, :, None, :] - a_cum[:, :, None, :, :]
    lower = jnp.tril(jnp.ones((SSD_CHUNK, SSD_CHUNK), dtype=bool))[None, None, :, :, None]
    decay_in = jnp.exp(jnp.where(lower, seg, -jnp.inf))
    scores = jnp.einsum('bcihn,bcjhn->bcijh', ch, bh) * decay_in
    y_diag = jnp.einsum('bcijh,bcjh,bcjhp->bcihp', scores, dtc, xc)
    decay_to_end = jnp.exp(a_cum[:, :, -1:, :] - a_cum)
    chunk_states = jnp.einsum('bcjhn,bcjh,bcjhp->bchpn', bh, decay_to_end * dtc, xc)
    chunk_decay = jnp.exp(a_cum[:, :, -1, :])

    def step(h, inp):
        st, dec = inp
        return dec[:, :, None, None] * h + st, h

    h_final, h_prev = lax.scan(step, h0, (jnp.moveaxis(chunk_states, 1, 0), jnp.moveaxis(chunk_decay, 1, 0)))
    h_prev = jnp.moveaxis(h_prev, 0, 1)
    y_off = jnp.einsum('bcihn,bchpn,bcih->bcihp', ch, h_prev, jnp.exp(a_cum))
    return (y_diag + y_off).reshape(b, n, nh, hp), h_final


def _maybe_flip(t, rev):
    return jnp.flip(t, axis=1) if rev else t


def _ssd_prep(a, conv_w, conv_b):
    b, n, _ = a.shape
    gn = SSD_GROUPS * SSD_STATE
    z = a[..., :W_GROUP].astype(jnp.float32)
    xbc = jax.nn.silu(dw_conv_centred(a[..., W_GROUP:W_GROUP + SSD_XBC], conv_w, conv_b)).astype(jnp.float32)
    xs = xbc[..., :W_GROUP].reshape(b, n, SSD_HEADS, HEAD_DIM)
    bm = xbc[..., W_GROUP:W_GROUP + gn].reshape(b, n, SSD_GROUPS, SSD_STATE)
    cm = xbc[..., W_GROUP + gn:].reshape(b, n, SSD_GROUPS, SSD_STATE)
    dt_raw = a[..., W_GROUP + SSD_XBC:].astype(jnp.float32).reshape(b, n, 2, SSD_HEADS)
    return z, xs, bm, cm, dt_raw


def ssd_branch(a_lat, a_ctx, conv_w, conv_b, a_log, dt_bias, d_skip, norm_g, need_ctx):
    zl, xl, bl, cl, dtl = _ssd_prep(a_lat, conv_w, conv_b)
    zc, xc, bc, cc, dtc = _ssd_prep(a_ctx, conv_w, conv_b)
    b, n = a_lat.shape[:2]
    n_ctx = a_ctx.shape[1]
    skip = d_skip.astype(jnp.float32)[:, None]
    yl = skip * xl
    yc = skip * xc
    for d in range(2):
        rev = d == 1
        a_h = -jnp.exp(a_log[d].astype(jnp.float32))
        bias = dt_bias[d].astype(jnp.float32)
        dt_c = jax.nn.softplus(dtc[:, :, d] + bias)
        dt_l = jax.nn.softplus(dtl[:, :, d] + bias)
        h0 = jnp.zeros((b, SSD_HEADS, HEAD_DIM, SSD_STATE), jnp.float32)
        yc_d, h_ctx = ssd_scan(_maybe_flip(xc, rev), _maybe_flip(dt_c, rev), a_h,
                               _maybe_flip(bc, rev), _maybe_flip(cc, rev), h0)
        yl_d, _ = ssd_scan(_maybe_flip(xl, rev), _maybe_flip(dt_l, rev), a_h,
                           _maybe_flip(bl, rev), _maybe_flip(cl, rev), h_ctx)
        yl = yl + _maybe_flip(yl_d, rev)
        if need_ctx:
            yc = yc + _maybe_flip(yc_d, rev)
    out_l = rms_norm(yl.reshape(b, n, W_GROUP) * jax.nn.silu(zl), norm_g).astype(a_lat.dtype)
    if not need_ctx:
        return out_l, None
    out_c = rms_norm(yc.reshape(b, n_ctx, W_GROUP) * jax.nn.silu(zc), norm_g).astype(a_ctx.dtype)
    return out_l, out_c


def hyena_spectra(n, w1, b1, freq1, w2, b2, freq2, w3, b3):
    f32 = jnp.float32
    pos = jnp.arange(n, dtype=f32)
    t = jnp.linspace(0.0, 1.0, n, dtype=f32)
    f = jnp.linspace(1e-4, HY_BANDS - 1, HY_BANDS, dtype=f32)
    ang = 2.0 * math.pi * pos[:, None] * f[None, :] / n
    z = jnp.concatenate([t[:, None], jnp.cos(ang), -jnp.sin(ang)], axis=-1)
    h = jnp.sin(freq1.astype(f32) * (z @ w1.astype(f32) + b1.astype(f32)))
    h = jnp.sin(freq2.astype(f32) * (h @ w2.astype(f32) + b2.astype(f32)))
    k = (h @ w3.astype(f32) + b3.astype(f32)).reshape(n, HY_ORDER, 2, HY_CH)
    deltas = jnp.abs(jnp.linspace(HY_MIN_DECAY, HY_MAX_DECAY, HY_CH, dtype=f32))
    k = k * jnp.exp(-t[:, None] * deltas[None, :])[:, None, None, :]
    k = k * lax.rsqrt(jnp.sum(k * k, axis=(0, 2), keepdims=True) + EPS)
    k_fwd, k_bwd = k[:, :, 0], k[:, :, 1]
    full = jnp.concatenate([k_fwd, jnp.zeros((1, HY_ORDER, HY_CH), f32), jnp.flip(k_bwd[1:], axis=0)], axis=0)
    return jnp.fft.rfft(full, axis=0)


def hyena_seq(u, conv_w, conv_b, spec, bias):
    n = u.shape[1]
    u = dw_conv_centred(u, conv_w, conv_b).astype(jnp.float32)
    v, x1, x2 = jnp.split(u, 3, axis=-1)
    gates = (x1, x2)
    bias = bias.astype(jnp.float32)
    z = v
    for o in range(HY_ORDER):
        zf = jnp.fft.irfft(jnp.fft.rfft(z, n=2 * n, axis=1) * spec[:, o], n=2 * n, axis=1)[:, :n]
        z = gates[o] * (zf + z * bias[o])
    return z


def split_qkv(p):
    b, n, _ = p.shape
    nq = ATT_HEADS * HEAD_DIM
    nk = ATT_KV * HEAD_DIM
    q = p[..., :nq].reshape(b, n, ATT_HEADS, HEAD_DIM)
    k = p[..., nq:nq + nk].reshape(b, n, ATT_KV, HEAD_DIM)
    v = p[..., nq + nk:].reshape(b, n, ATT_KV, HEAD_DIM)
    return q, k, v


def window_attention(q, k, v, kc, vc, sink):
    b, n, nh, hd = q.shape
    nb = n // BLOCK
    rep = nh // ATT_KV
    scale = hd ** -0.5
    qb = q.reshape(b, nb, BLOCK, ATT_KV, rep, hd)

    def band(t):
        tp = jnp.pad(t, ((0, 0), (BLOCK, BLOCK), (0, 0), (0, 0))).reshape(b, nb + 2, BLOCK, ATT_KV, hd)
        return jnp.concatenate([tp[:, :-2], tp[:, 1:-1], tp[:, 2:]], axis=2)

    kb, vb = band(k), band(v)
    qpos = jnp.arange(nb)[:, None] * BLOCK + jnp.arange(BLOCK)[None, :]
    kpos = jnp.arange(nb)[:, None] * BLOCK - BLOCK + jnp.arange(3 * BLOCK)[None, :]
    valid = ((jnp.abs(qpos[:, :, None] - kpos[:, None, :]) <= WINDOW)
             & (kpos[:, None, :] >= 0) & (kpos[:, None, :] < n))
    s_loc = jnp.einsum('bnqgrd,bnkgd->bngrqk', qb, kb, preferred_element_type=jnp.float32) * scale
    s_loc = jnp.where(valid[None, :, None, None], s_loc, -jnp.inf)
    s_ctx = jnp.einsum('bnqgrd,bcgd->bngrqc', qb, kc, preferred_element_type=jnp.float32) * scale
    snk = jnp.broadcast_to(sink.astype(jnp.float32).reshape(1, 1, ATT_KV, rep, 1, 1), s_loc.shape[:-1] + (1,))
    p = jax.nn.softmax(jnp.concatenate([s_loc, s_ctx, snk], axis=-1), axis=-1).astype(v.dtype)
    kl = 3 * BLOCK
    o = (jnp.einsum('bngrqk,bnkgd->bnqgrd', p[..., :kl], vb)
         + jnp.einsum('bngrqc,bcgd->bnqgrd', p[..., kl:kl + kc.shape[1]], vc))
    return o.reshape(b, n, nh * hd)


def dense_attention(q, k, v, kc, vc):
    b, n, nh, hd = q.shape
    nb = n // BLOCK
    rep = nh // ATT_KV
    k_all = jnp.concatenate([k, kc], axis=1)
    v_all = jnp.concatenate([v, vc], axis=1)
    qb = jnp.moveaxis(q.reshape(b, nb, BLOCK, ATT_KV, rep, hd), 1, 0)

    def one_block(q_blk):
        s = jnp.einsum('bqgrd,bkgd->bgrqk', q_blk, k_all, preferred_element_type=jnp.float32) * hd ** -0.5
        p = jax.nn.softmax(s, axis=-1).astype(v_all.dtype)
        return jnp.einsum('bgrqk,bkgd->bqgrd', p, v_all)

    o = lax.map(one_block, qb)
    return jnp.moveaxis(o, 0, 1).reshape(b, n, nh * hd)


def context_attention(q, k, v, sink):
    b, n, nh, hd = q.shape
    rep = nh // ATT_KV
    nk = k.shape[1]
    s = jnp.einsum('bqgrd,bkgd->bgrqk', q.reshape(b, n, ATT_KV, rep, hd), k,
                   preferred_element_type=jnp.float32) * hd ** -0.5
    if sink is not None:
        snk = jnp.broadcast_to(sink.astype(jnp.float32).reshape(1, ATT_KV, rep, 1, 1), s.shape[:-1] + (1,))
        s = jnp.concatenate([s, snk], axis=-1)
    p = jax.nn.softmax(s, axis=-1)[..., :nk].astype(v.dtype)
    return jnp.einsum('bgrqk,bkgd->bqgrd', p, v).reshape(b, n, nh * hd)


def token_mixers(h, hc, w_in, w_out, ssd_conv_w, ssd_conv_b, ssd_a_log, ssd_dt_bias, ssd_d, ssd_norm,
                 hy_conv_w, hy_conv_b, hy_filter, hy_bias, attn_sink, q_norm, k_norm, cos, sin, need_ctx):
    n, n_ctx = h.shape[1], hc.shape[1]
    proj = h @ w_in
    projc = hc @ w_in
    ya, yac = ssd_branch(proj[..., :OFF_B], projc[..., :OFF_B], ssd_conv_w, ssd_conv_b, ssd_a_log,
                         ssd_dt_bias, ssd_d, ssd_norm, need_ctx)
    yb = hyena_seq(proj[..., OFF_B:OFF_C], hy_conv_w, hy_conv_b, hyena_spectra(n, *hy_filter), hy_bias)
    qw, kw, vw = split_qkv(proj[..., OFF_C:OFF_D])
    qwc, kwc, vwc = split_qkv(projc[..., OFF_C:OFF_D])
    yw = window_attention(apply_rope(qw, cos, sin), apply_rope(kw, cos, sin), vw, kwc, vwc, attn_sink)
    qd, kd, vd = split_qkv(proj[..., OFF_D:])
    qdc, kdc, vdc = split_qkv(projc[..., OFF_D:])
    kdc = rms_norm(kdc, k_norm)
    yd = dense_attention(apply_rope(rms_norm(qd, q_norm), cos, sin), apply_rope(rms_norm(kd, k_norm), cos, sin),
                         vd, kdc, vdc)
    dt = h.dtype
    y = jnp.concatenate([ya, yb.astype(dt), yw, yd], axis=-1) @ w_out
    if not need_ctx:
        return y, None
    ybc = hyena_seq(projc[..., OFF_B:OFF_C], hy_conv_w, hy_conv_b, hyena_spectra(n_ctx, *hy_filter), hy_bias)
    ywc = context_attention(qwc, kwc, vwc, attn_sink)
    ydc = context_attention(rms_norm(qdc, q_norm), kdc, vdc, None)
    yc = jnp.concatenate([yac, ybc.astype(dt), ywc, ydc], axis=-1) @ w_out
    return y, yc


def squared_relu_mlp(h, w1, w2):
    return jnp.square(jax.nn.relu(h @ w1)) @ w2


def setup_inputs(seed: int = 0) -> dict:
    key = jax.random.key(seed)
    ks = jax.random.split(key, 40)
    f32 = jnp.float32

    def nrm(k, shape, scale):
        return jax.random.normal(k, shape, f32) * scale

    def gain(k, shape):
        return 1.0 + 0.05 * jax.random.normal(k, shape, f32)

    dt_init = jnp.exp(jax.random.uniform(ks[15], (DEPTH, 2, SSD_HEADS), f32, math.log(1e-3), math.log(1e-1)))
    return {
        'x': nrm(ks[0], (BATCH, SEQ, D_MODEL), 1.0),
        'c': nrm(ks[1], (BATCH, D_MODEL), 1.0),
        'ctx': nrm(ks[2], (BATCH, CTX_LEN, D_MODEL), 1.0),
        'c_ctx': nrm(ks[3], (D_MODEL,), 1.0),
        'w_mod': nrm(ks[4], (DEPTH, D_MODEL, 6 * D_MODEL), 0.5 * D_MODEL ** -0.5),
        'b_mod': nrm(ks[5], (DEPTH, 6 * D_MODEL), 0.02),
        'norm_mix_pre': gain(ks[6], (DEPTH, D_MODEL)),
        'norm_mix_post': gain(ks[7], (DEPTH, D_MODEL)),
        'norm_mlp_pre': gain(ks[8], (DEPTH, D_MODEL)),
        'norm_mlp_post': gain(ks[9], (DEPTH, D_MODEL)),
        'w_in': nrm(ks[10], (DEPTH, D_MODEL, D_IN), D_MODEL ** -0.5),
        'w_out': nrm(ks[11], (DEPTH, D_MIX, D_MODEL), D_MIX ** -0.5),
        'ssd_conv_w': nrm(ks[12], (DEPTH, SSD_CONV, SSD_XBC), SSD_CONV ** -0.5),
        'ssd_conv_b': nrm(ks[13], (DEPTH, SSD_XBC), 0.02),
        'ssd_a_log': jnp.log(jax.random.uniform(ks[14], (DEPTH, 2, SSD_HEADS), f32, 1.0, 16.0)),
        'ssd_dt_bias': dt_init + jnp.log(-jnp.expm1(-dt_init)),
        'ssd_d': gain(ks[16], (DEPTH, SSD_HEADS)),
        'ssd_norm': gain(ks[17], (DEPTH, W_GROUP)),
        'hy_conv_w': nrm(ks[18], (DEPTH, HY_SHORT, COLS_B), HY_SHORT ** -0.5),
        'hy_conv_b': nrm(ks[19], (DEPTH, COLS_B), 0.02),
        'hy_w1': nrm(ks[20], (DEPTH, HY_EMB, HY_FILT), HY_EMB ** -0.5),
        'hy_b1': nrm(ks[21], (DEPTH, HY_FILT), 0.1),
        'hy_freq1': gain(ks[22], (DEPTH, HY_FILT)),
        'hy_w2': nrm(ks[23], (DEPTH, HY_FILT, HY_FILT), HY_FILT ** -0.5),
        'hy_b2': nrm(ks[24], (DEPTH, HY_FILT), 0.1),
        'hy_freq2': gain(ks[25], (DEPTH, HY_FILT)),
        'hy_w3': nrm(ks[26], (DEPTH, HY_FILT, 2 * HY_ORDER * HY_CH), HY_FILT ** -0.5),
        'hy_b3': nrm(ks[27], (DEPTH, 2 * HY_ORDER * HY_CH), 0.02),
        'hy_bias': nrm(ks[28], (DEPTH, HY_ORDER, HY_CH), 0.5),
        'attn_sink': nrm(ks[29], (DEPTH, ATT_HEADS), 0.5),
        'q_norm': gain(ks[30], (DEPTH, HEAD_DIM)),
        'k_norm': gain(ks[31], (DEPTH, HEAD_DIM)),
        'mlp_w1': nrm(ks[32], (DEPTH, D_MODEL, D_FF), D_MODEL ** -0.5),
        'mlp_w2': nrm(ks[33], (DEPTH, D_FF, D_MODEL), D_FF ** -0.5),
    }


def reference(x, c, ctx, c_ctx, w_mod, b_mod, norm_mix_pre, norm_mix_post, norm_mlp_pre, norm_mlp_post,
              w_in, w_out, ssd_conv_w, ssd_conv_b, ssd_a_log, ssd_dt_bias, ssd_d, ssd_norm,
              hy_conv_w, hy_conv_b, hy_w1, hy_b1, hy_freq1, hy_w2, hy_b2, hy_freq2, hy_w3, hy_b3, hy_bias,
              attn_sink, q_norm, k_norm, mlp_w1, mlp_w2):
    rows = x.shape[1] // GRID_W
    cos, sin = rope_tables(rows)
    for i in range(DEPTH):
        need_ctx = i < DEPTH - 1
        mod = jax.nn.silu(c) @ w_mod[i] + b_mod[i]
        mod_c = jax.nn.silu(c_ctx) @ w_mod[i] + b_mod[i]
        sh1, sc1, g1, sh2, sc2, g2 = jnp.split(mod[:, None, :], 6, axis=-1)
        csh1, csc1, cg1, csh2, csc2, cg2 = jnp.split(mod_c, 6, axis=-1)
        h = rms_norm(x, norm_mix_pre[i]) * (1.0 + sc1) + sh1
        hc = rms_norm(ctx, norm_mix_pre[i]) * (1.0 + csc1) + csh1
        hy_filter = (hy_w1[i], hy_b1[i], hy_freq1[i], hy_w2[i], hy_b2[i], hy_freq2[i], hy_w3[i], hy_b3[i])
        y, yc = token_mixers(h, hc, w_in[i], w_out[i], ssd_conv_w[i], ssd_conv_b[i], ssd_a_log[i],
                             ssd_dt_bias[i], ssd_d[i], ssd_norm[i], hy_conv_w[i], hy_conv_b[i], hy_filter,
                             hy_bias[i], attn_sink[i], q_norm[i], k_norm[i], cos, sin, need_ctx)
        x = x + g1 * rms_norm(y, norm_mix_post[i])
        h = rms_norm(x, norm_mlp_pre[i]) * (1.0 + sc2) + sh2
        x = x + g2 * rms_norm(squared_relu_mlp(h, mlp_w1[i], mlp_w2[i]), norm_mlp_post[i])
        if need_ctx:
            ctx = ctx + cg1 * rms_norm(yc, norm_mix_post[i])
            hc = rms_norm(ctx, norm_mlp_pre[i]) * (1.0 + csc2) + csh2
            ctx = ctx + cg2 * rms_norm(squared_relu_mlp(hc, mlp_w1[i], mlp_w2[i]), norm_mlp_post[i])
    return x
```

```python
import functools
import math

import numpy as np
import jax
import jax.numpy as jnp
from jax import lax
from jax.experimental import pallas as pl
from jax.experimental.pallas import tpu as pltpu

F32 = jnp.float32
MXU_DTYPE = jnp.bfloat16

EPS = 1e-6
HEAD_DIM = 64
GRID_W = 64
ROPE_THETA = 10000.0
N_HEADS = 4
N_KV = 2
W_GROUP = N_HEADS * HEAD_DIM
SSD_STATE = 64
SSD_CONV = 5
SSD_XBC = W_GROUP + 2 * N_KV * SSD_STATE
HY_SHORT = 3
HY_ORDER = 2
HY_BANDS = 16
HY_EMB = 2 * HY_BANDS + 1
HY_FILT = 64
HY_MAX_DECAY = math.log(1e-2) / 0.3
HY_MIN_DECAY = math.log(1e-2) / 1.5
CHUNK = 128
LANE = 128
SUBLANE = 8
ROW_TILE = 256
ATT_TQ = 768
FFT_COLS = 4096
VMEM_LIMIT = 56 * 1024 * 1024
NEG = -1e30

C_Z = (0, 256)
C_XBC = (256, 768)
C_HY = (768, 1536)
C_QKVW = (1536, 2048)
C_KD = (2304, 2432)
C_QD = (2048, 2304)
C_VD = (2432, 2560)
C_DT = (2560, 2688)
V_ROWS = HEAD_DIM + 16
ATT_QL = 256
ATT_KT = 256
LOG2E = math.log2(math.e)
D_IN_PAD = 2688


def _cparams(sem):
    return pltpu.CompilerParams(dimension_semantics=sem, vmem_limit_bytes=VMEM_LIMIT)


def _rms(x, g):
    return x * lax.rsqrt(jnp.mean(x * x, axis=-1, keepdims=True) + EPS) * g


def _silu(x):
    return x * (1.0 / (1.0 + jnp.exp(-x)))


def _softplus(x):
    return jnp.maximum(x, 0.0) + jnp.log(1.0 + jnp.exp(-jnp.abs(x)))


def _dot(a, b):
    return jnp.dot(a.astype(MXU_DTYPE), b.astype(MXU_DTYPE), preferred_element_type=F32)


def _dot_nt(a, b):
    return lax.dot_general(a.astype(MXU_DTYPE), b.astype(MXU_DTYPE), (((1,), (1,)), ((), ())),
                           preferred_element_type=F32)


def _dot_f32(a, b):
    return jnp.dot(a, b, preferred_element_type=F32, precision=lax.Precision.HIGHEST)


def _mod_kernel(c_ref, w_ref, b_ref, o_ref):
    o_ref[0] = _dot_f32(_silu(c_ref[...]), w_ref[0]) + b_ref[0]


def _modulation(cc, w_mod, b_mod):
    depth, d, d6 = w_mod.shape
    tn = d6 // 4
    return pl.pallas_call(
        _mod_kernel,
        grid=(depth, d6 // tn),
        in_specs=[pl.BlockSpec((SUBLANE, d), lambda l, j: (0, 0)),
                  pl.BlockSpec((1, d, tn), lambda l, j: (l, 0, j)),
                  pl.BlockSpec((1, 1, tn), lambda l, j: (l, 0, j))],
        out_specs=pl.BlockSpec((1, SUBLANE, tn), lambda l, j: (l, 0, j)),
        out_shape=jax.ShapeDtypeStruct((depth, SUBLANE, d6), F32),
        compiler_params=_cparams(("arbitrary", "arbitrary")),
        name="modulation",
    )(cc, w_mod, b_mod.reshape(depth, 1, d6))


def _rope(t, cs, sn):
    half = HEAD_DIM // 2
    rot = jnp.concatenate([t[:, half:], t[:, :half]], axis=1)
    return t * cs + rot * sn


def _inproj_kernel(x_ref, ctx_ref, mod_ref, g_ref, w_ref, wt_ref, cs_ref, sn_ref, cst_ref, snt_ref, qnt_ref, kn_ref,
                   z_ref, xbc_ref, hy_ref, dt_ref, qw_ref, kw_ref, vw_ref, qdt_ref, kd_ref, vdt_ref):
    i = pl.program_id(1)
    xv = jnp.where(i == 0, ctx_ref[0], x_ref[0])
    tm = xv.shape[0]
    sh = mod_ref[0, 0, 0:1, :]
    sc = mod_ref[0, 0, 1:2, :]
    hb = (_rms(xv, g_ref[...]) * (1.0 + sc) + sh).astype(MXU_DTYPE)

    def proj(cols):
        return jnp.dot(hb, w_ref[:, cols[0]:cols[1]], preferred_element_type=F32)

    z_ref[0] = proj(C_Z)
    xbc_ref[0] = proj(C_XBC)
    hy_ref[0] = proj(C_HY)
    dt_ref[0] = proj(C_DT)
    pw = proj(C_QKVW)
    pk = proj(C_KD)
    cs = cs_ref[...]
    sn = sn_ref[...]
    kn = kn_ref[...]
    scale = HEAD_DIM ** -0.5
    for h in range(N_HEADS):
        lo = h * HEAD_DIM
        qw_ref[0, h] = (_rope(pw[:, lo:lo + HEAD_DIM], cs, sn) * scale).astype(qw_ref.dtype)
    for g in range(N_KV):
        lo = W_GROUP + g * HEAD_DIM
        kw_ref[0, g] = _rope(pw[:, lo:lo + HEAD_DIM], cs, sn).astype(kw_ref.dtype)
        kd_ref[0, g] = _rope(_rms(pk[:, g * HEAD_DIM:(g + 1) * HEAD_DIM], kn), cs, sn).astype(kd_ref.dtype)
        lo = W_GROUP + (N_KV + g) * HEAD_DIM
        vw_ref[0, g] = pw[:, lo:lo + HEAD_DIM].astype(vw_ref.dtype)

    pt = _dot_nt(wt_ref[...], hb)
    cst = cst_ref[...]
    snt = snt_ref[...]
    qnt = jnp.concatenate([qnt_ref[...]] * (tm // LANE), axis=1)
    half = HEAD_DIM // 2
    for h in range(N_HEADS):
        t = pt[h * HEAD_DIM:(h + 1) * HEAD_DIM]
        t = t * lax.rsqrt(jnp.mean(t * t, axis=0, keepdims=True) + EPS) * qnt
        rot = jnp.concatenate([t[half:], t[:half]], axis=0)
        qdt_ref[0, h] = ((t * cst + rot * snt) * (scale * LOG2E)).astype(qdt_ref.dtype)
    ones = jnp.ones((vdt_ref.shape[3] - HEAD_DIM, LANE), vdt_ref.dtype)
    for g in range(N_KV):
        vt = pt[W_GROUP + g * HEAD_DIM:W_GROUP + (g + 1) * HEAD_DIM]
        for j in range(tm // LANE):
            vdt_ref[0, g, j, 0:HEAD_DIM, :] = vt[:, j * LANE:(j + 1) * LANE].astype(vdt_ref.dtype)
            vdt_ref[0, g, j, HEAD_DIM:, :] = ones


def _in_projection(x, ctx, mod6, g_pre, w_in_r, w_qv_t, cs, sn, qn, kn):
    b, _, d = x.shape
    tm = ROW_TILE
    if ctx is None:
        t = x.shape[1]
        x_spec = pl.BlockSpec((1, tm, d), lambda bb, i: (bb, i, 0))
        ctx_arr, ctx_spec = x, pl.BlockSpec((1, tm, d), lambda bb, i: (bb, 0, 0))
    else:
        assert ctx.shape[1] == tm
        t = x.shape[1] + tm
        x_spec = pl.BlockSpec((1, tm, d), lambda bb, i: (bb, jnp.maximum(i - 1, 0), 0))
        ctx_arr, ctx_spec = ctx, pl.BlockSpec((1, tm, d), lambda bb, i: (bb, 0, 0))
    nrow = t // tm
    per = tm // LANE
    row = lambda w: pl.BlockSpec((1, tm, w), lambda bb, i: (bb, i, 0))
    heads = lambda n: pl.BlockSpec((1, n, tm, HEAD_DIM), lambda bb, i: (bb, 0, i, 0))
    vec = lambda w: pl.BlockSpec((1, w), lambda bb, i: (0, 0))
    f32 = lambda w: jax.ShapeDtypeStruct((b, t, w), F32)
    hd = lambda n: jax.ShapeDtypeStruct((b, n, t, HEAD_DIM), MXU_DTYPE)
    qnt = jnp.broadcast_to(qn.reshape(HEAD_DIM, 1), (HEAD_DIM, LANE))
    return pl.pallas_call(
        _inproj_kernel,
        grid=(b, nrow),
        in_specs=[x_spec, ctx_spec,
                  pl.BlockSpec((1, 1, 6, d), lambda bb, i: (bb, jnp.minimum(i, 1), 0, 0)),
                  vec(d),
                  pl.BlockSpec((d, D_IN_PAD), lambda bb, i: (0, 0)),
                  pl.BlockSpec(w_qv_t.shape, lambda bb, i: (0, 0)),
                  pl.BlockSpec((tm, HEAD_DIM), lambda bb, i: (i, 0)),
                  pl.BlockSpec((tm, HEAD_DIM), lambda bb, i: (i, 0)),
                  pl.BlockSpec((HEAD_DIM, tm), lambda bb, i: (0, i)),
                  pl.BlockSpec((HEAD_DIM, tm), lambda bb, i: (0, i)),
                  pl.BlockSpec((HEAD_DIM, LANE), lambda bb, i: (0, 0)),
                  vec(HEAD_DIM)],
        out_specs=[row(W_GROUP), row(SSD_XBC), row(3 * W_GROUP), row(LANE),
                   heads(N_HEADS), heads(N_KV), heads(N_KV),
                   pl.BlockSpec((1, N_HEADS, HEAD_DIM, tm), lambda bb, i: (bb, 0, 0, i)),
                   heads(N_KV),
                   pl.BlockSpec((1, N_KV, per, V_ROWS, LANE), lambda bb, i: (bb, 0, i, 0, 0))],
        out_shape=[f32(W_GROUP), f32(SSD_XBC), f32(3 * W_GROUP), f32(LANE),
                   hd(N_HEADS), hd(N_KV), hd(N_KV),
                   jax.ShapeDtypeStruct((b, N_HEADS, HEAD_DIM, t), MXU_DTYPE),
                   hd(N_KV),
                   jax.ShapeDtypeStruct((b, N_KV, t // LANE, V_ROWS, LANE), MXU_DTYPE)],
        compiler_params=_cparams(("parallel", "arbitrary")),
        name="in_projection",
    )(x, ctx_arr, mod6, g_pre, w_in_r, w_qv_t, cs, sn, cs.T, sn.T, qnt, kn)


def _ssd_direction(d, c, nc, nt, par_ref, cw_ref, cb_ref, x_ref, xp_ref, xn_ref, dt_ref, st_ref):
    q = CHUNK
    prev_ok = jnp.logical_and(c != 0, c != nc)
    next_ok = jnp.logical_and(c != nc - 1, c != nt - 1)
    prev = jnp.where(prev_ok, xp_ref[0], 0.0)
    nxt = jnp.where(next_ok, xn_ref[0], 0.0)
    ext = jnp.concatenate([prev, x_ref[0], nxt], axis=0)
    u = cb_ref[...]
    for k in range(SSD_CONV):
        off = SUBLANE - SSD_CONV // 2 + k
        u = u + cw_ref[k:k + 1, :] * ext[off:off + q, :]
    u = _silu(u)
    xs = u[:, :W_GROUP]
    bm = u[:, W_GROUP:W_GROUP + N_KV * SSD_STATE]
    cm = u[:, W_GROUP + N_KV * SSD_STATE:]

    a_all = -jnp.exp(par_ref[0:1, :])
    dtv = _softplus(dt_ref[0] + par_ref[1:2, :])
    ri = lax.broadcasted_iota(jnp.int32, (q, q), 0)
    ci = lax.broadcasted_iota(jnp.int32, (q, q), 1)
    mask = (ri >= ci) if d == 0 else (ri <= ci)
    cum = _dot_f32(mask.astype(F32), dtv * a_all)
    cum_t = cum.T
    dt_t = dtv.T
    bm_t = bm.T
    end = q - 1 if d == 0 else 0
    outs = []
    s = None
    for h in range(N_HEADS):
        hl = N_HEADS * d + h
        g = h // (N_HEADS // N_KV)
        col = cum[:, hl:hl + 1]
        row = cum_t[hl:hl + 1, :]
        dt_row = dt_t[hl:hl + 1, :]
        last = cum[end:end + 1, hl:hl + 1]
        cg = cm[:, g * SSD_STATE:(g + 1) * SSD_STATE]
        bg_t = bm_t[g * SSD_STATE:(g + 1) * SSD_STATE, :]
        xh = xs[:, h * HEAD_DIM:(h + 1) * HEAD_DIM]
        if h % (N_HEADS // N_KV) == 0:
            s = _dot(cg, bg_t)
        decay = jnp.exp(jnp.where(mask, col - row, NEG))
        y = _dot(s * decay * dt_row, xh)
        st = st_ref[d, h]
        y = y + _dot(cg, st) * jnp.exp(col)
        st_ref[d, h] = jnp.exp(last) * st + _dot(bg_t * (jnp.exp(last - row) * dt_row), xh)
        if d == 0:
            y = y + par_ref[2:3, h:h + 1] * xh
        outs.append(y)
    return jnp.concatenate(outs, axis=1)


def _ssd_kernel(nc, nt, par_ref, cw_ref, cb_ref, xf_ref, xfp_ref, xfn_ref, dtf_ref,
                xb_ref, xbp_ref, xbn_ref, dtb_ref, yf_ref, yb_ref, st_ref):
    j = pl.program_id(1)

    @pl.when(j == 0)
    def _():
        st_ref[...] = jnp.zeros(st_ref.shape, F32)

    cf = j
    cb = jnp.where(j < nc, nc - 1 - j, nt + nc - 1 - j)
    yf_ref[0] = _ssd_direction(0, cf, nc, nt, par_ref, cw_ref, cb_ref, xf_ref, xfp_ref, xfn_ref, dtf_ref, st_ref)
    yb_ref[0] = _ssd_direction(1, cb, nc, nt, par_ref, cw_ref, cb_ref, xb_ref, xbp_ref, xbn_ref, dtb_ref, st_ref)


def _ssd(xbc, dt, par, conv_w, conv_b, n_ctx):
    b, t, w = xbc.shape
    q = CHUNK
    nc, nt = n_ctx // q, t // q
    per = q // SUBLANE
    fwd = lambda j: j
    bwd = lambda j: jnp.where(j < nc, nc - 1 - j, nt + nc - 1 - j)

    def specs(cmap):
        return [pl.BlockSpec((1, q, w), lambda bb, j: (bb, cmap(j), 0)),
                pl.BlockSpec((1, SUBLANE, w), lambda bb, j: (bb, jnp.maximum(cmap(j) * per - 1, 0), 0)),
                pl.BlockSpec((1, SUBLANE, w), lambda bb, j: (bb, jnp.minimum((cmap(j) + 1) * per, nt * per - 1), 0)),
                pl.BlockSpec((1, q, LANE), lambda bb, j: (bb, cmap(j), 0))]

    const = lambda s: pl.BlockSpec(s, lambda bb, j: (0, 0))
    return pl.pallas_call(
        functools.partial(_ssd_kernel, nc, nt),
        grid=(b, nt),
        in_specs=[const((SUBLANE, LANE)), const((SUBLANE, w)), const((1, w))] + specs(fwd) + specs(bwd),
        out_specs=[pl.BlockSpec((1, q, W_GROUP), lambda bb, j: (bb, fwd(j), 0)),
                   pl.BlockSpec((1, q, W_GROUP), lambda bb, j: (bb, bwd(j), 0))],
        out_shape=[jax.ShapeDtypeStruct((b, t, W_GROUP), F32)] * 2,
        scratch_shapes=[pltpu.VMEM((2, N_HEADS, SSD_STATE, HEAD_DIM), F32)],
        compiler_params=_cparams(("parallel", "arbitrary")),
        name="ssd_scan",
    )(par, conv_w, conv_b, xbc, xbc, xbc, dt, xbc, xbc, xbc, dt)


def _short_conv_kernel(nblk, w_ref, b_ref, x_ref, xp_ref, xn_ref, v_ref, x1_ref, x2_ref):
    i = pl.program_id(1)
    tm = x_ref.shape[1]
    prev = jnp.where(i != 0, xp_ref[0], 0.0)
    nxt = jnp.where(i != nblk - 1, xn_ref[0], 0.0)
    ext = jnp.concatenate([prev, x_ref[0], nxt], axis=0)
    u = b_ref[...]
    for k in range(HY_SHORT):
        off = SUBLANE - HY_SHORT // 2 + k
        u = u + w_ref[k:k + 1, :] * ext[off:off + tm, :]
    v_ref[0] = u[:, :W_GROUP]
    x1_ref[0] = u[:, W_GROUP:2 * W_GROUP]
    x2_ref[0] = u[:, 2 * W_GROUP:]


def _short_conv(hy, conv_w, conv_b, row0, nrows):
    b, t, w = hy.shape
    tm = ROW_TILE
    nblk = nrows // tm
    blk0 = row0 // tm
    per = tm // SUBLANE
    out = pl.BlockSpec((1, tm, W_GROUP), lambda bb, i: (bb, i, 0))
    const = lambda s: pl.BlockSpec(s, lambda bb, i: (0, 0))
    return pl.pallas_call(
        functools.partial(_short_conv_kernel, nblk),
        grid=(b, nblk),
        in_specs=[const((SUBLANE, w)), const((1, w)),
                  pl.BlockSpec((1, tm, w), lambda bb, i: (bb, blk0 + i, 0)),
                  pl.BlockSpec((1, SUBLANE, w), lambda bb, i: (bb, jnp.maximum((blk0 + i) * per - 1, 0), 0)),
                  pl.BlockSpec((1, SUBLANE, w),
                               lambda bb, i: (bb, jnp.minimum((blk0 + i + 1) * per, t // SUBLANE - 1), 0))],
        out_specs=[out, out, out],
        out_shape=[jax.ShapeDtypeStruct((b, nrows, W_GROUP), F32)] * 3,
        compiler_params=_cparams(("parallel", "arbitrary")),
        name="hyena_short_conv",
    )(conv_w, conv_b, hy, hy, hy)


def _filter_kernel(zf_ref, zb_ref, w1_ref, b1_ref, f1_ref, w2_ref, b2_ref, f2_ref,
                   w3f_ref, w3b_ref, b3f_ref, b3b_ref, dl_ref, o_ref, ss_ref):
    ph = pl.program_id(0)
    i = pl.program_id(1)
    tr = zf_ref.shape[0]

    def half(z_ref, w3_ref, b3_ref):
        z = z_ref[...]
        h = jnp.sin(f1_ref[...] * (_dot_f32(z, w1_ref[...]) + b1_ref[...]))
        h = jnp.sin(f2_ref[...] * (_dot_f32(h, w2_ref[...]) + b2_ref[...]))
        k = _dot_f32(h, w3_ref[...]) + b3_ref[...]
        return k * jnp.exp(-z[:, 0:1] * dl_ref[...])

    kf = half(zf_ref, w3f_ref, b3f_ref)
    kb = half(zb_ref, w3b_ref, b3b_ref)

    @pl.when(jnp.logical_and(ph == 0, i == 0))
    def _():
        ss_ref[...] = jnp.zeros(ss_ref.shape, F32)

    @pl.when(ph == 0)
    def _():
        ss_ref[...] += jnp.sum(kf * kf + kb * kb, axis=0, keepdims=True)

    @pl.when(ph == 1)
    def _():
        scale = lax.rsqrt(ss_ref[...] + EPS)
        rows = i * tr + lax.broadcasted_iota(jnp.int32, (tr, 1), 0)
        o_ref[0] = kf * scale
        o_ref[1] = jnp.where(rows == 0, 0.0, kb * scale)


def _hyena_filters(n, w1, b1, f1, w2, b2, f2, w3, b3):
    pos = jnp.arange(n, dtype=F32)
    t = jnp.linspace(0.0, 1.0, n, dtype=F32)
    f = jnp.linspace(1e-4, HY_BANDS - 1, HY_BANDS, dtype=F32)
    ang = 2.0 * math.pi * pos[:, None] * f[None, :] / n
    emb = jnp.concatenate([t[:, None], jnp.cos(ang), -jnp.sin(ang)], axis=-1)
    emb = jnp.pad(emb, ((0, 0), (0, LANE - HY_EMB)))
    emb_b = jnp.roll(jnp.flip(emb, axis=0), 1, axis=0)
    w1p = jnp.pad(w1, ((0, LANE - HY_EMB), (0, 0)))
    w3r = w3.reshape(HY_FILT, HY_ORDER, 2, W_GROUP)
    b3r = b3.reshape(HY_ORDER, 2, W_GROUP)
    wc = HY_ORDER * W_GROUP
    w3f, w3b = w3r[:, :, 0].reshape(HY_FILT, wc), w3r[:, :, 1].reshape(HY_FILT, wc)
    b3f, b3b = b3r[:, 0].reshape(1, wc), b3r[:, 1].reshape(1, wc)
    deltas = jnp.abs(jnp.linspace(HY_MIN_DECAY, HY_MAX_DECAY, W_GROUP, dtype=F32))
    deltas = jnp.tile(deltas, HY_ORDER).reshape(1, wc)
    tr = math.gcd(n, 512)
    const = lambda s: pl.BlockSpec(s, lambda p, i: (0, 0))
    rows = pl.BlockSpec((tr, LANE), lambda p, i: (i, 0))
    return pl.pallas_call(
        _filter_kernel,
        grid=(2, n // tr),
        in_specs=[rows, rows, const((LANE, HY_FILT)), const((1, HY_FILT)), const((1, HY_FILT)),
                  const((HY_FILT, HY_FILT)), const((1, HY_FILT)), const((1, HY_FILT)),
                  const((HY_FILT, wc)), const((HY_FILT, wc)), const((1, wc)), const((1, wc)), const((1, wc))],
        out_specs=pl.BlockSpec((2, tr, wc), lambda p, i: (0, p * i, 0)),
        out_shape=jax.ShapeDtypeStruct((2, n, wc), F32),
        scratch_shapes=[pltpu.VMEM((1, wc), F32)],
        compiler_params=_cparams(("arbitrary", "arbitrary")),
        name="hyena_filter",
    )(emb, emb_b, w1p, b1.reshape(1, -1), f1.reshape(1, -1), w2, b2.reshape(1, -1), f2.reshape(1, -1),
      w3f, w3b, b3f, b3b, deltas)


def _dft_tables(n):
    nn = 2 * n
    n1, n2 = nn // LANE, LANE
    a = n1 // 2
    k1 = np.arange(n1)[:, None]
    j1 = np.arange(n1)[None, :]
    ang1 = 2.0 * np.pi * (k1 * j1 % n1) / n1
    fr, fi = np.cos(ang1), -np.sin(ang1)
    m1 = np.block([[fr[:, :a], -fi[:, :a]], [fi[:, :a], fr[:, :a]]])
    m1_real = np.concatenate([fr, fi], axis=0)
    er, ei = fr.T[:a], -fi.T[:a]
    m3 = np.stack([np.concatenate([er, -ei], axis=1), np.concatenate([ei, er], axis=1)]) / nn
    k2 = np.arange(n2)[:, None]
    j2 = np.arange(n2)[None, :]
    ang2 = 2.0 * np.pi * (k2 * j2 % n2) / n2
    f2 = np.stack([np.cos(ang2), -np.sin(ang2)])
    angt = 2.0 * np.pi * (np.arange(n1)[:, None] * j2) / nn
    tw = np.stack([np.cos(angt), -np.sin(angt)], axis=1)
    return tuple(jnp.asarray(t, F32) for t in (m1, m1_real, m3, f2, tw))


def _fft_first_kernel(ur_ref, ui_ref, m_ref, o_ref):
    n1 = o_ref.shape[2]
    r = _dot(m_ref[...], jnp.concatenate([ur_ref[0], ui_ref[0]], axis=0))
    o_ref[0, 0] = r[:n1].astype(o_ref.dtype)
    o_ref[0, 1] = r[n1:].astype(o_ref.dtype)


def _fft_first(u, m1):
    b2, a, cols = u.shape
    p = b2 // 2
    n1 = m1.shape[0] // 2
    tc = min(cols, FFT_COLS)
    return pl.pallas_call(
        _fft_first_kernel,
        grid=(p, cols // tc),
        in_specs=[pl.BlockSpec((1, a, tc), lambda pp, j: (2 * pp, 0, j)),
                  pl.BlockSpec((1, a, tc), lambda pp, j: (2 * pp + 1, 0, j)),
                  pl.BlockSpec(m1.shape, lambda pp, j: (0, 0))],
        out_specs=pl.BlockSpec((1, 2, n1, tc), lambda pp, j: (pp, 0, 0, j)),
        out_shape=jax.ShapeDtypeStruct((p, 2, n1, cols), MXU_DTYPE),
        compiler_params=_cparams(("parallel", "arbitrary")),
        name="fft_first",
    )(u, u, m1)


def _twiddled_dft(f2_ref, tw_ref):
    fr, fi = f2_ref[0], f2_ref[1]
    tr, ti = tw_ref[0, 0:1, :], tw_ref[0, 1:2, :]
    gr = fr * tr - fi * ti
    gi = fr * ti + fi * tr
    return gr, gi


def _spectrum_kernel(a_ref, f2_ref, tw_ref, h_ref):
    gr, gi = _twiddled_dft(f2_ref, tw_ref)
    g = jnp.concatenate([jnp.concatenate([gr, -gi], axis=1), jnp.concatenate([gi, gr], axis=1)], axis=0)
    xin = jnp.concatenate([a_ref[0, 0, 0], a_ref[0, 1, 0]], axis=0)
    x = _dot(g, xin)
    h_ref[0, 0] = x[:LANE]
    h_ref[1, 0] = x[LANE:]


def _filter_spectrum(af, f2, tw):
    _, _, n1, cols = af.shape
    c = cols // LANE
    a5 = af.reshape(1, 2, n1, LANE, c)
    return pl.pallas_call(
        _spectrum_kernel,
        grid=(n1,),
        in_specs=[pl.BlockSpec((1, 2, 1, LANE, c), lambda k: (0, 0, k, 0, 0)),
                  pl.BlockSpec((2, LANE, LANE), lambda k: (0, 0, 0)),
                  pl.BlockSpec((1, 2, LANE), lambda k: (k, 0, 0))],
        out_specs=pl.BlockSpec((2, 1, LANE, c), lambda k: (0, k, 0, 0)),
        out_shape=jax.ShapeDtypeStruct((2, n1, LANE, c), F32),
        compiler_params=_cparams(("arbitrary",)),
        name="filter_spectrum",
    )(a5, f2, tw)


def _fft_mid_kernel(a_ref, h_ref, f2_ref, tw_ref, v_ref):
    npair = a_ref.shape[0]
    c = a_ref.shape[4]
    gr, gi = _twiddled_dft(f2_ref, tw_ref)
    g = jnp.concatenate([jnp.concatenate([gr, -gi], axis=1), jnp.concatenate([gi, gr], axis=1)], axis=0)
    grt, git = gr.T, gi.T
    ginv = jnp.concatenate([jnp.concatenate([grt, git], axis=1), jnp.concatenate([-git, grt], axis=1)], axis=0)
    xin = jnp.concatenate(
        [jnp.concatenate([a_ref[p, 0, 0], a_ref[p, 1, 0]], axis=0) for p in range(npair)], axis=1)
    x = _dot(g, xin)
    xr, xi = x[:LANE], x[LANE:]
    hr = jnp.concatenate([h_ref[0, 0]] * npair, axis=1)
    hi = jnp.concatenate([h_ref[1, 0]] * npair, axis=1)
    y = jnp.concatenate([xr * hr - xi * hi, xr * hi + xi * hr], axis=0)
    v = _dot(ginv, y)
    for p in range(npair):
        v_ref[p, 0, 0] = v[:LANE, p * c:(p + 1) * c].astype(v_ref.dtype)
        v_ref[p, 1, 0] = v[LANE:, p * c:(p + 1) * c].astype(v_ref.dtype)


def _fft_mid(a4, hspec, order, f2, tw):
    npair, _, n1, cols = a4.shape
    c = cols // LANE
    a5 = a4.reshape(npair, 2, n1, LANE, c)
    blk = pl.BlockSpec((npair, 2, 1, LANE, c), lambda k: (0, 0, k, 0, 0))
    out = pl.pallas_call(
        _fft_mid_kernel,
        grid=(n1,),
        in_specs=[blk,
                  pl.BlockSpec((2, 1, LANE, c), lambda k: (0, k, 0, order)),
                  pl.BlockSpec((2, LANE, LANE), lambda k: (0, 0, 0)),
                  pl.BlockSpec((1, 2, LANE), lambda k: (k, 0, 0))],
        out_specs=blk,
        out_shape=jax.ShapeDtypeStruct(a5.shape, MXU_DTYPE),
        compiler_params=_cparams(("arbitrary",)),
        name="fft_mid",
    )(a5, hspec, f2, tw)
    return out.reshape(npair, 2, n1, cols)


def _fft_last_kernel(v_ref, m_ref, z_ref, gate_ref, bias_ref, o_ref):
    vs = jnp.concatenate([v_ref[0, 0], v_ref[0, 1]], axis=0)
    zf = _dot(m_ref[0], vs)
    o_ref[0] = gate_ref[0] * (zf + z_ref[0] * bias_ref[...])


def _fft_last(v4, m3, z, gate, bias_cols):
    b, a, cols = z.shape
    n1 = v4.shape[2]
    tc = min(cols, FFT_COLS)
    row = pl.BlockSpec((1, a, tc), lambda j, bb: (bb, 0, j))
    return pl.pallas_call(
        _fft_last_kernel,
        grid=(cols // tc, b),
        in_specs=[pl.BlockSpec((1, 2, n1, tc), lambda j, bb: (bb // 2, 0, 0, j)),
                  pl.BlockSpec((1, a, 2 * n1), lambda j, bb: (bb % 2, 0, 0)),
                  row, row,
                  pl.BlockSpec((1, tc), lambda j, bb: (0, j))],
        out_specs=row,
        out_shape=jax.ShapeDtypeStruct((b, a, cols), F32),
        compiler_params=_cparams(("parallel", "arbitrary")),
        name="fft_last",
    )(v4, m3, z, gate, bias_cols)


def _hyena_long(v, x1, x2, filt, bias):
    b, n, c = v.shape
    a = n // LANE
    cols = LANE * c
    m1, m1_real, m3, f2, tw = _dft_tables(n)
    wc = filt.shape[2]
    af = _fft_first(filt.reshape(2, a, LANE * wc), m1_real)
    hspec = _filter_spectrum(af, f2, tw)
    z = v.reshape(b, a, cols)
    gates = (x1.reshape(b, a, cols), x2.reshape(b, a, cols))
    for o in range(HY_ORDER):
        a4 = _fft_first(z, m1)
        v4 = _fft_mid(a4, hspec, o, f2, tw)
        z = _fft_last(v4, m3, z, gates[o], jnp.tile(bias[o], LANE).reshape(1, cols))
    return z.reshape(b, n, c)


def _hyena_ctx_kernel(v_ref, x1_ref, x2_ref, filt_ref, bias_ref, ff_ref, fc_ref, fi_ref, o_ref):
    b, n, c = v_ref.shape
    npair = b // 2
    hs = _dot(ff_ref[...], jnp.concatenate([filt_ref[0], filt_ref[1]], axis=0))
    z = [v_ref[i] for i in range(b)]
    gates = (x1_ref, x2_ref)
    for o in range(HY_ORDER):
        hr = jnp.concatenate([hs[:2 * n, o * c:(o + 1) * c]] * npair, axis=1)
        hi = jnp.concatenate([hs[2 * n:, o * c:(o + 1) * c]] * npair, axis=1)
        xin = jnp.concatenate([jnp.concatenate([z[2 * p] for p in range(npair)], axis=1),
                               jnp.concatenate([z[2 * p + 1] for p in range(npair)], axis=1)], axis=0)
        x = _dot(fc_ref[...], xin)
        xr, xi = x[:2 * n], x[2 * n:]
        y = jnp.concatenate([xr * hr - xi * hi, xr * hi + xi * hr], axis=0)
        zf = _dot(fi_ref[...], y)
        bo = bias_ref[o:o + 1, :]
        for i in range(b):
            p, part = i // 2, i % 2
            conv = zf[part * n:(part + 1) * n, p * c:(p + 1) * c]
            z[i] = gates[o][i] * (conv + z[i] * bo)
    for i in range(b):
        o_ref[i] = z[i]


def _hyena_ctx(v, x1, x2, filt, bias):
    b, n, c = v.shape
    nn = 2 * n
    k = np.arange(nn)[:, None]
    j = np.arange(nn)[None, :]
    ang = 2.0 * np.pi * (k * j % nn) / nn
    fr, fi = np.cos(ang), -np.sin(ang)
    ff = np.concatenate([fr, fi], axis=0)
    fc = np.block([[fr[:, :n], -fi[:, :n]], [fi[:, :n], fr[:, :n]]])
    er, ei = fr[:n], -fi[:n]
    finv = np.block([[er, -ei], [ei, er]]) / nn
    return pl.pallas_call(
        _hyena_ctx_kernel,
        out_shape=jax.ShapeDtypeStruct((b, n, c), F32),
        compiler_params=pltpu.CompilerParams(vmem_limit_bytes=VMEM_LIMIT),
        name="hyena_ctx",
    )(v, x1, x2, filt, bias, jnp.asarray(ff, F32), jnp.asarray(fc, F32), jnp.asarray(finv, F32))


def _window_kernel(ncb, nt, sink_ref, q_ref, kp_ref, kc_ref, kn_ref, kx_ref, vp_ref, vc_ref, vn_ref, vx_ref, o_ref):
    g = pl.program_id(1)
    i = pl.program_id(2)
    blk = CHUNK
    rep = N_HEADS // N_KV
    q = q_ref[0].reshape(rep * blk, HEAD_DIM)
    latent = i >= ncb
    r = lax.broadcasted_iota(jnp.int32, (rep * blk, blk), 0) % blk
    c = lax.broadcasted_iota(jnp.int32, (rep * blk, blk), 1)
    ok_p = jnp.logical_and(latent, i - 1 >= ncb)
    ok_n = jnp.logical_and(latent, i + 1 <= nt - 1)
    s_p = jnp.where(jnp.logical_and(c >= r, ok_p), _dot_nt(q, kp_ref[0, 0]), NEG)
    s_c = jnp.where(latent, _dot_nt(q, kc_ref[0, 0]), NEG)
    s_n = jnp.where(jnp.logical_and(c <= r, ok_n), _dot_nt(q, kn_ref[0, 0]), NEG)
    s_x = _dot_nt(q, kx_ref[0, 0])
    hrow = lax.broadcasted_iota(jnp.int32, (rep * blk, 1), 0) // blk
    snk = sink_ref[0:1, 0:1]
    for h in range(rep):
        snk = jnp.where(jnp.logical_and(hrow == h, g == 0), sink_ref[0:1, h:h + 1], snk)
        snk = jnp.where(jnp.logical_and(hrow == h, g == 1), sink_ref[0:1, rep + h:rep + h + 1], snk)
    m = jnp.maximum(jnp.maximum(jnp.max(s_p, axis=1, keepdims=True), jnp.max(s_c, axis=1, keepdims=True)),
                    jnp.maximum(jnp.max(s_n, axis=1, keepdims=True), jnp.max(s_x, axis=1, keepdims=True)))
    m = jnp.maximum(m, snk)
    p_p, p_c, p_n, p_x = jnp.exp(s_p - m), jnp.exp(s_c - m), jnp.exp(s_n - m), jnp.exp(s_x - m)
    den = (jnp.sum(p_p, axis=1, keepdims=True) + jnp.sum(p_c, axis=1, keepdims=True)
           + jnp.sum(p_n, axis=1, keepdims=True) + jnp.sum(p_x, axis=1, keepdims=True) + jnp.exp(snk - m))
    inv = 1.0 / den
    o = (_dot(p_p * inv, vp_ref[0, 0]) + _dot(p_c * inv, vc_ref[0, 0])
         + _dot(p_n * inv, vn_ref[0, 0]) + _dot(p_x * inv, vx_ref[0, 0]))
    o_ref[0] = jnp.concatenate([o[h * blk:(h + 1) * blk] for h in range(rep)], axis=1)


def _window_attention(q, k, v, sink, n_ctx):
    b, _, t, hd = q.shape
    blk = CHUNK
    nt, ncb = t // blk, n_ctx // blk
    rep = N_HEADS // N_KV
    kv = lambda f: pl.BlockSpec((1, 1, blk, hd), lambda bb, g, i: (bb, g, f(i), 0))
    kvx = pl.BlockSpec((1, 1, n_ctx, hd), lambda bb, g, i: (bb, g, 0, 0))
    prev = lambda i: jnp.maximum(i - 1, 0)
    cur = lambda i: i
    nxt = lambda i: jnp.minimum(i + 1, nt - 1)
    return pl.pallas_call(
        functools.partial(_window_kernel, ncb, nt),
        grid=(b, N_KV, nt),
        in_specs=[pl.BlockSpec((SUBLANE, LANE), lambda bb, g, i: (0, 0)),
                  pl.BlockSpec((1, rep, blk, hd), lambda bb, g, i: (bb, g, i, 0)),
                  kv(prev), kv(cur), kv(nxt), kvx, kv(prev), kv(cur), kv(nxt), kvx],
        out_specs=pl.BlockSpec((1, blk, rep * hd), lambda bb, g, i: (bb, i, g)),
        out_shape=jax.ShapeDtypeStruct((b, t, N_HEADS * hd), F32),
        compiler_params=_cparams(("parallel", "parallel", "arbitrary")),
        name="window_attention",
    )(sink, q, k, k, k, k, v, v, v, v)


def _dense_kernel(n_ctx, q_ref, k_ref, v_ref, o_ref, m_ref, acc_ref):
    i = pl.program_id(2)
    rep, tq = q_ref.shape[1], q_ref.shape[3]
    kt, ql = ATT_KT, ATT_QL
    per = kt // LANE
    nc = tq // ql

    def run(chunks, nk):
        for c in chunks:
            for h in range(rep):
                m_ref[c * rep + h] = jnp.full((1, ql), NEG, F32)
                acc_ref[c * rep + h] = jnp.zeros(acc_ref.shape[1:], F32)

        def body(j, carry):
            kb = k_ref[0, 0, pl.ds(pl.multiple_of(j * kt, kt), kt), :]
            vb = jnp.concatenate([v_ref[0, 0, j * per + u] for u in range(per)], axis=1)
            ids = [(c, h) for c in chunks for h in range(rep)]
            ss = [jnp.dot(kb, q_ref[0, h, :, c * ql:(c + 1) * ql], preferred_element_type=F32) for c, h in ids]
            ps, alphas = [], []
            for (c, h), s in zip(ids, ss):
                n = c * rep + h
                m = m_ref[n]
                mn = jnp.maximum(m, jnp.max(s, axis=0, keepdims=True))
                ps.append(jnp.exp2(s - mn).astype(MXU_DTYPE))
                alphas.append(jnp.exp2(m - mn))
                m_ref[n] = mn
            pvs = [jnp.dot(vb, p, preferred_element_type=F32) for p in ps]
            for (c, h), alpha, pv in zip(ids, alphas, pvs):
                n = c * rep + h
                acc_ref[n] = alpha * acc_ref[n] + pv
            return carry

        lax.fori_loop(0, nk, body, 0)
        for c in chunks:
            o = [acc_ref[c * rep + h] for h in range(rep)]
            o = [a[:HEAD_DIM] * (1.0 / a[HEAD_DIM:HEAD_DIM + 1]) for a in o]
            o_ref[0, c * ql:(c + 1) * ql, :] = jnp.concatenate(o, axis=0).T

    nk_all = k_ref.shape[2] // kt

    @pl.when(i == 0)
    def _():
        run([0], n_ctx // kt)
        if nc > 1:
            run(list(range(1, nc)), nk_all)

    @pl.when(i != 0)
    def _():
        run(list(range(nc)), nk_all)


def _dense_attention(qt, k, vt, n_ctx):
    b, _, hd, t = qt.shape
    rep = N_HEADS // N_KV
    tq = ATT_TQ if t % ATT_TQ == 0 else ATT_QL
    assert n_ctx == ATT_QL and t % ATT_QL == 0 and rep * hd == LANE
    return pl.pallas_call(
        functools.partial(_dense_kernel, n_ctx),
        grid=(b, N_KV, t // tq),
        in_specs=[pl.BlockSpec((1, rep, hd, tq), lambda bb, g, i: (bb, g, 0, i)),
                  pl.BlockSpec((1, 1, t, hd), lambda bb, g, i: (bb, g, 0, 0)),
                  pl.BlockSpec((1, 1) + vt.shape[2:], lambda bb, g, i: (bb, g, 0, 0, 0))],
        out_specs=pl.BlockSpec((1, tq, rep * hd), lambda bb, g, i: (bb, i, g)),
        out_shape=jax.ShapeDtypeStruct((b, t, N_HEADS * hd), F32),
        scratch_shapes=[pltpu.VMEM((rep * tq // ATT_QL, 1, ATT_QL), F32),
                        pltpu.VMEM((rep * tq // ATT_QL, vt.shape[3], ATT_QL), F32)],
        compiler_params=_cparams(("parallel", "parallel", "arbitrary")),
        name="dense_attention",
    )(qt, k, vt)


def _mix_mlp_kernel(first, x_ref, ctx_ref, mod_ref, yf_ref, yb_ref, z_ref, gs_ref, hy_ref, hyc_ref,
                    yw_ref, yd_ref, gpost_ref, gpre_ref, gpost2_ref, wo_ref, w1_ref, w2_ref, o_ref):
    i = pl.program_id(1)
    if first:
        xv = jnp.where(i == 0, ctx_ref[0], x_ref[0])
        yh = jnp.where(i == 0, hyc_ref[0], hy_ref[0])
    else:
        xv = x_ref[0]
        yh = hy_ref[0]
    g1 = mod_ref[0, 0, 2:3, :]
    sh2 = mod_ref[0, 0, 3:4, :]
    sc2 = mod_ref[0, 0, 4:5, :]
    g2 = mod_ref[0, 0, 5:6, :]
    ya = _rms((yf_ref[0] + yb_ref[0]) * _silu(z_ref[0]), gs_ref[...])
    w = W_GROUP
    y = (_dot(ya, wo_ref[0:w, :]) + _dot(yh, wo_ref[w:2 * w, :])
         + _dot(yw_ref[0], wo_ref[2 * w:3 * w, :]) + _dot(yd_ref[0], wo_ref[3 * w:4 * w, :]))
    x1 = xv + g1 * _rms(y, gpost_ref[...])
    hb = (_rms(x1, gpre_ref[...]) * (1.0 + sc2) + sh2).astype(MXU_DTYPE)
    d = x1.shape[1]
    acc = jnp.zeros(x1.shape, F32)
    for c in range(w1_ref.shape[1] // d):
        a = jnp.maximum(jnp.dot(hb, w1_ref[:, c * d:(c + 1) * d], preferred_element_type=F32), 0.0)
        acc = acc + _dot(a * a, w2_ref[c * d:(c + 1) * d, :])
    o_ref[0] = x1 + g2 * _rms(acc, gpost2_ref[...])


def _mix_mlp(first, x, ctx, mod6, yf, yb, z, g_ssd, hy, hyc, yw, yd, g_post, g_pre2, g_post2, wo, w1, w2):
    b, t, _ = yf.shape
    d = wo.shape[1]
    tm = ROW_TILE
    off = 0 if first else 1
    nrow = t // tm - off
    if first:
        x_spec = pl.BlockSpec((1, tm, d), lambda bb, i: (bb, jnp.maximum(i - 1, 0), 0))
        hy_spec = pl.BlockSpec((1, tm, W_GROUP), lambda bb, i: (bb, jnp.maximum(i - 1, 0), 0))
        mod_spec = pl.BlockSpec((1, 1, 6, d), lambda bb, i: (bb, jnp.minimum(i, 1), 0, 0))
    else:
        x_spec = pl.BlockSpec((1, tm, d), lambda bb, i: (bb, i + 1, 0))
        hy_spec = pl.BlockSpec((1, tm, W_GROUP), lambda bb, i: (bb, i, 0))
        mod_spec = pl.BlockSpec((1, 1, 6, d), lambda bb, i: (bb, 1, 0, 0))
    first_blk = lambda w: pl.BlockSpec((1, tm, w), lambda bb, i: (bb, 0, 0))
    row = lambda w: pl.BlockSpec((1, tm, w), lambda bb, i: (bb, i + off, 0))
    vec = lambda w: pl.BlockSpec((1, w), lambda bb, i: (0, 0))
    full = lambda a: pl.BlockSpec(a.shape, lambda bb, i: (0, 0))
    return pl.pallas_call(
        functools.partial(_mix_mlp_kernel, first),
        grid=(b, nrow),
        in_specs=[x_spec, first_blk(d), mod_spec, row(W_GROUP), row(W_GROUP), row(W_GROUP), vec(W_GROUP),
                  hy_spec, first_blk(W_GROUP), row(W_GROUP), row(W_GROUP), vec(d), vec(d), vec(d),
                  full(wo), full(w1), full(w2)],
        out_specs=pl.BlockSpec((1, tm, d), lambda bb, i: (bb, i, 0)),
        out_shape=jax.ShapeDtypeStruct((b, nrow * tm, d), F32),
        compiler_params=_cparams(("parallel", "arbitrary")),
        name="mix_mlp",
    )(x, ctx, mod6, yf, yb, z, g_ssd, hy, hyc, yw, yd, g_post, g_pre2, g_post2, wo, w1, w2)


def _rope_tables(n, n_ctx):
    rows = n // GRID_W
    row = jnp.repeat(jnp.arange(rows, dtype=F32), GRID_W)
    col = jnp.tile(jnp.arange(GRID_W, dtype=F32), rows)
    n_freq = HEAD_DIM // 4
    inv = ROPE_THETA ** (-jnp.arange(n_freq, dtype=F32) / n_freq)
    ang = jnp.concatenate([row[:, None] * inv, col[:, None] * inv], axis=-1)
    cos, sin = jnp.cos(ang), jnp.sin(ang)
    cs = jnp.concatenate([cos, cos], axis=1)
    sn = jnp.concatenate([-sin, sin], axis=1)
    cs = jnp.concatenate([jnp.ones((n_ctx, HEAD_DIM), F32), cs], axis=0)
    sn = jnp.concatenate([jnp.zeros((n_ctx, HEAD_DIM), F32), sn], axis=0)
    return cs, sn


def _pad_rows(a, rows):
    return jnp.pad(a, ((0, rows - a.shape[0]), (0, 0)))


def kernel(x, c, ctx, c_ctx, w_mod, b_mod, norm_mix_pre, norm_mix_post, norm_mlp_pre, norm_mlp_post, w_in, w_out, ssd_conv_w, ssd_conv_b, ssd_a_log, ssd_dt_bias, ssd_d, ssd_norm, hy_conv_w, hy_conv_b, hy_w1, hy_b1, hy_freq1, hy_w2, hy_b2, hy_freq2, hy_w3, hy_b3, hy_bias, attn_sink, q_norm, k_norm, mlp_w1, mlp_w2):
    b, n, d = x.shape
    n_ctx = ctx.shape[1]
    depth = w_mod.shape[0]
    assert n_ctx == ROW_TILE and n % ROW_TILE == 0 and b % 2 == 0 and b + 1 <= SUBLANE

    cc = _pad_rows(jnp.concatenate([c, c_ctx[None, :]], axis=0), SUBLANE)
    mod = _modulation(cc, w_mod, b_mod)
    cs, sn = _rope_tables(n, n_ctx)

    off_b = W_GROUP + SSD_XBC + 2 * N_HEADS
    xall = None
    for l in range(depth):
        first = l == 0
        need_ctx = l < depth - 1
        mod_lat = mod[l, :b].reshape(b, 1, 6, d)
        mod_ctx = jnp.broadcast_to(mod[l, b].reshape(1, 1, 6, d), (b, 1, 6, d))
        mod6 = jnp.concatenate([mod_ctx, mod_lat], axis=1)

        wl = w_in[l]
        w_in_r = jnp.concatenate(
            [wl[:, :W_GROUP + SSD_XBC], wl[:, off_b:], wl[:, W_GROUP + SSD_XBC:off_b],
             jnp.zeros((d, D_IN_PAD - wl.shape[1]), F32)], axis=1).astype(MXU_DTYPE)
        w_qv_t = jnp.concatenate([w_in_r[:, C_QD[0]:C_QD[1]], w_in_r[:, C_VD[0]:C_VD[1]]], axis=1).T
        xin, cin = (x, ctx) if first else (xall, None)
        (z, xbc, hy, dt, qw, kw, vw, qdt, kd, vdt) = _in_projection(
            xin, cin, mod6, norm_mix_pre[l].reshape(1, d), w_in_r, w_qv_t, cs, sn,
            q_norm[l], k_norm[l].reshape(1, HEAD_DIM))

        par = jnp.zeros((SUBLANE, LANE), F32)
        par = par.at[0, :2 * N_HEADS].set(ssd_a_log[l].reshape(-1))
        par = par.at[1, :2 * N_HEADS].set(ssd_dt_bias[l].reshape(-1))
        par = par.at[2, :N_HEADS].set(ssd_d[l])
        yf, yb = _ssd(xbc, dt, par, _pad_rows(ssd_conv_w[l], SUBLANE), ssd_conv_b[l].reshape(1, -1), n_ctx)

        hy_w = _pad_rows(hy_conv_w[l], SUBLANE)
        hy_b = hy_conv_b[l].reshape(1, -1)
        filt_args = (hy_w1[l], hy_b1[l], hy_freq1[l], hy_w2[l], hy_b2[l], hy_freq2[l], hy_w3[l], hy_b3[l])
        v, x1, x2 = _short_conv(hy, hy_w, hy_b, n_ctx, n)
        yhy = _hyena_long(v, x1, x2, _hyena_filters(n, *filt_args), hy_bias[l])
        if need_ctx:
            vc, x1c, x2c = _short_conv(hy, hy_w, hy_b, 0, n_ctx)
            yhy_ctx = _hyena_ctx(vc, x1c, x2c, _hyena_filters(n_ctx, *filt_args), hy_bias[l])
        else:
            yhy_ctx = yhy

        sink = jnp.zeros((SUBLANE, LANE), F32).at[0, :N_HEADS].set(attn_sink[l])
        yw = _window_attention(qw, kw, vw, sink, n_ctx)
        yd = _dense_attention(qdt, kd, vdt, n_ctx)

        xres, cres = (x, ctx) if first else (xall, xall)
        xall = _mix_mlp(first, xres, cres, mod6, yf, yb, z, ssd_norm[l].reshape(1, -1), yhy, yhy_ctx, yw, yd,
                        norm_mix_post[l].reshape(1, d), norm_mlp_pre[l].reshape(1, d),
                        norm_mlp_post[l].reshape(1, d), w_out[l].astype(MXU_DTYPE),
                        mlp_w1[l].astype(MXU_DTYPE), mlp_w2[l].astype(MXU_DTYPE))
    return xall
```

```python
import functools
import math

import numpy as np
import jax
import jax.numpy as jnp
from jax import lax
from jax.experimental import pallas as pl
from jax.experimental.pallas import tpu as pltpu

F32 = jnp.float32
MXU_DTYPE = jnp.bfloat16

EPS = 1e-6
HEAD_DIM = 64
GRID_W = 64
ROPE_THETA = 10000.0
N_HEADS = 4
N_KV = 2
W_GROUP = N_HEADS * HEAD_DIM
SSD_STATE = 64
SSD_CONV = 5
SSD_XBC = W_GROUP + 2 * N_KV * SSD_STATE
HY_SHORT = 3
HY_ORDER = 2
HY_BANDS = 16
HY_EMB = 2 * HY_BANDS + 1
HY_FILT = 64
HY_MAX_DECAY = math.log(1e-2) / 0.3
HY_MIN_DECAY = math.log(1e-2) / 1.5
CHUNK = 128
LANE = 128
SUBLANE = 8
ROW_TILE = 256
ATT_TQ = 768
FFT_COLS = 4096
VMEM_LIMIT = 56 * 1024 * 1024
NEG = -1e30

C_Z = (0, 256)
C_XBC = (256, 768)
C_HY = (768, 1536)
C_DT = (1536, 1664)
D_ROW_PAD = 1664
A_QD, A_KD, A_VD = 0, 256, 384
A_QW, A_KW, A_VW = 512, 768, 896
V_ROWS = HEAD_DIM + 16
ATT_QL = 256
ATT_KT = 768
LOG2E = math.log2(math.e)


def _cparams(sem):
    return pltpu.CompilerParams(dimension_semantics=sem, vmem_limit_bytes=VMEM_LIMIT)


def _rms(x, g):
    return x * lax.rsqrt(jnp.mean(x * x, axis=-1, keepdims=True) + EPS) * g


def _silu(x):
    return x * (1.0 / (1.0 + jnp.exp(-x)))


def _softplus(x):
    return jnp.maximum(x, 0.0) + jnp.log(1.0 + jnp.exp(-jnp.abs(x)))


def _dot(a, b):
    return jnp.dot(a.astype(MXU_DTYPE), b.astype(MXU_DTYPE), preferred_element_type=F32)


def _dot_nt(a, b):
    return lax.dot_general(a.astype(MXU_DTYPE), b.astype(MXU_DTYPE), (((1,), (1,)), ((), ())),
                           preferred_element_type=F32)


def _dot_f32(a, b):
    return jnp.dot(a, b, preferred_element_type=F32, precision=lax.Precision.HIGHEST)


def _mod_kernel(c_ref, w_ref, b_ref, o_ref):
    o_ref[0] = _dot_f32(_silu(c_ref[...]), w_ref[0]) + b_ref[0]


def _modulation(cc, w_mod, b_mod):
    depth, d, d6 = w_mod.shape
    tn = d6 // 4
    return pl.pallas_call(
        _mod_kernel,
        grid=(depth, d6 // tn),
        in_specs=[pl.BlockSpec((SUBLANE, d), lambda l, j: (0, 0)),
                  pl.BlockSpec((1, d, tn), lambda l, j: (l, 0, j)),
                  pl.BlockSpec((1, 1, tn), lambda l, j: (l, 0, j))],
        out_specs=pl.BlockSpec((1, SUBLANE, tn), lambda l, j: (l, 0, j)),
        out_shape=jax.ShapeDtypeStruct((depth, SUBLANE, d6), F32),
        compiler_params=_cparams(("arbitrary", "arbitrary")),
        name="modulation",
    )(cc, w_mod, b_mod.reshape(depth, 1, d6))


def _inproj_kernel(x_ref, ctx_ref, mod_ref, g_ref, w_ref, wt_ref, cst_ref, snt_ref, qnt_ref, knt_ref,
                   z_ref, xbc_ref, hy_ref, dt_ref, qwt_ref, kw_ref, vwt_ref, qdt_ref, kd_ref, vdt_ref):
    i = pl.program_id(1)
    xv = jnp.where(i == 0, ctx_ref[0], x_ref[0])
    tm = xv.shape[0]
    sh = mod_ref[0, 0, 0:1, :]
    sc = mod_ref[0, 0, 1:2, :]
    hb = (_rms(xv, g_ref[...]) * (1.0 + sc) + sh).astype(MXU_DTYPE)

    def proj(cols):
        return jnp.dot(hb, w_ref[:, cols[0]:cols[1]], preferred_element_type=F32)

    z_ref[0] = proj(C_Z)
    xbc_ref[0] = proj(C_XBC)
    hy_ref[0] = proj(C_HY)
    dt_ref[0] = proj(C_DT)

    pt = _dot_nt(wt_ref[...], hb)
    cst = cst_ref[...]
    snt = snt_ref[...]
    tile = lambda r: jnp.concatenate([r[...]] * (tm // LANE), axis=1)
    qnt, knt = tile(qnt_ref), tile(knt_ref)
    half = HEAD_DIM // 2
    qscale = HEAD_DIM ** -0.5 * LOG2E

    def head(row0):
        return pt[row0:row0 + HEAD_DIM]

    def norm(t, gain):
        return t * lax.rsqrt(jnp.mean(t * t, axis=0, keepdims=True) + EPS) * gain

    def rope(t):
        return t * cst + jnp.concatenate([t[half:], t[:half]], axis=0) * snt

    def put_values(v_ref, row0):
        ones = jnp.ones((v_ref.shape[3] - HEAD_DIM, LANE), v_ref.dtype)
        for g in range(N_KV):
            vt = head(row0 + g * HEAD_DIM)
            for j in range(tm // LANE):
                v_ref[0, g, j, 0:HEAD_DIM, :] = vt[:, j * LANE:(j + 1) * LANE].astype(v_ref.dtype)
                v_ref[0, g, j, HEAD_DIM:, :] = ones

    def put_keys(k_ref, kt_pair):
        k_rows = jnp.concatenate(kt_pair, axis=0).T
        for g in range(N_KV):
            k_ref[0, g] = k_rows[:, g * HEAD_DIM:(g + 1) * HEAD_DIM].astype(k_ref.dtype)

    for h in range(N_HEADS):
        qdt_ref[0, h] = (rope(norm(head(A_QD + h * HEAD_DIM), qnt)) * qscale).astype(qdt_ref.dtype)
        qwt_ref[0, h] = (rope(head(A_QW + h * HEAD_DIM)) * qscale).astype(qwt_ref.dtype)
    put_keys(kd_ref, [rope(norm(head(A_KD + g * HEAD_DIM), knt)) for g in range(N_KV)])
    put_keys(kw_ref, [rope(head(A_KW + g * HEAD_DIM)) for g in range(N_KV)])
    put_values(vdt_ref, A_VD)
    put_values(vwt_ref, A_VW)


def _in_projection(x, ctx, mod6, g_pre, w_row, w_att_t, cs, sn, qn, kn):
    b, _, d = x.shape
    tm = ROW_TILE
    if ctx is None:
        t = x.shape[1]
        x_spec = pl.BlockSpec((1, tm, d), lambda bb, i: (bb, i, 0))
        ctx_arr, ctx_spec = x, pl.BlockSpec((1, tm, d), lambda bb, i: (bb, 0, 0))
    else:
        assert ctx.shape[1] == tm
        t = x.shape[1] + tm
        x_spec = pl.BlockSpec((1, tm, d), lambda bb, i: (bb, jnp.maximum(i - 1, 0), 0))
        ctx_arr, ctx_spec = ctx, pl.BlockSpec((1, tm, d), lambda bb, i: (bb, 0, 0))
    nrow = t // tm
    per = tm // LANE
    row = lambda w: pl.BlockSpec((1, tm, w), lambda bb, i: (bb, i, 0))
    full = lambda a: pl.BlockSpec(a.shape, lambda bb, i: (0, 0))
    f32 = lambda w: jax.ShapeDtypeStruct((b, t, w), F32)
    q_spec = pl.BlockSpec((1, N_HEADS, HEAD_DIM, tm), lambda bb, i: (bb, 0, 0, i))
    k_spec = pl.BlockSpec((1, N_KV, tm, HEAD_DIM), lambda bb, i: (bb, 0, i, 0))
    v_spec = pl.BlockSpec((1, N_KV, per, V_ROWS, LANE), lambda bb, i: (bb, 0, i, 0, 0))
    q_shape = jax.ShapeDtypeStruct((b, N_HEADS, HEAD_DIM, t), MXU_DTYPE)
    k_shape = jax.ShapeDtypeStruct((b, N_KV, t, HEAD_DIM), MXU_DTYPE)
    v_shape = jax.ShapeDtypeStruct((b, N_KV, t // LANE, V_ROWS, LANE), MXU_DTYPE)
    gain = lambda v: jnp.broadcast_to(v.reshape(HEAD_DIM, 1), (HEAD_DIM, LANE))
    qnt, knt = gain(qn), gain(kn)
    return pl.pallas_call(
        _inproj_kernel,
        grid=(b, nrow),
        in_specs=[x_spec, ctx_spec,
                  pl.BlockSpec((1, 1, 6, d), lambda bb, i: (bb, jnp.minimum(i, 1), 0, 0)),
                  pl.BlockSpec((1, d), lambda bb, i: (0, 0)),
                  full(w_row), full(w_att_t),
                  pl.BlockSpec((HEAD_DIM, tm), lambda bb, i: (0, i)),
                  pl.BlockSpec((HEAD_DIM, tm), lambda bb, i: (0, i)),
                  full(qnt), full(knt)],
        out_specs=[row(W_GROUP), row(SSD_XBC), row(3 * W_GROUP), row(LANE),
                   q_spec, k_spec, v_spec, q_spec, k_spec, v_spec],
        out_shape=[f32(W_GROUP), f32(SSD_XBC), f32(3 * W_GROUP), f32(LANE),
                   q_shape, k_shape, v_shape, q_shape, k_shape, v_shape],
        compiler_params=_cparams(("parallel", "arbitrary")),
        name="in_projection",
    )(x, ctx_arr, mod6, g_pre, w_row, w_att_t, cs.T, sn.T, qnt, knt)


def _ssd_direction(d, c, nc, nt, par_ref, cw_ref, cb_ref, x_ref, xp_ref, xn_ref, dt_ref, st_ref, ext_ref):
    q = CHUNK
    prev_ok = jnp.logical_and(c != 0, c != nc)
    next_ok = jnp.logical_and(c != nc - 1, c != nt - 1)
    ext_ref[d, 0:SUBLANE] = jnp.where(prev_ok, xp_ref[0], 0.0)
    ext_ref[d, SUBLANE:SUBLANE + q] = x_ref[0]
    ext_ref[d, SUBLANE + q:] = jnp.where(next_ok, xn_ref[0], 0.0)
    u = cb_ref[...]
    for k in range(SSD_CONV):
        off = SUBLANE - SSD_CONV // 2 + k
        u = u + cw_ref[k:k + 1, :] * ext_ref[d, off:off + q, :]
    u = _silu(u)
    xs = u[:, :W_GROUP]
    bm = u[:, W_GROUP:W_GROUP + N_KV * SSD_STATE]
    cm = u[:, W_GROUP + N_KV * SSD_STATE:]

    a_all = -jnp.exp(par_ref[0:1, :])
    dtv = _softplus(dt_ref[0] + par_ref[1:2, :])
    ri = lax.broadcasted_iota(jnp.int32, (q, q), 0)
    ci = lax.broadcasted_iota(jnp.int32, (q, q), 1)
    mask = (ri >= ci) if d == 0 else (ri <= ci)
    cum = _dot_f32(mask.astype(F32), dtv * a_all)
    cum_t = cum.T
    dt_t = dtv.T
    bm_t = bm.T
    end = q - 1 if d == 0 else 0
    outs = []
    s = None
    for h in range(N_HEADS):
        hl = N_HEADS * d + h
        g = h // (N_HEADS // N_KV)
        col = cum[:, hl:hl + 1]
        row = cum_t[hl:hl + 1, :]
        dt_row = dt_t[hl:hl + 1, :]
        last = cum[end:end + 1, hl:hl + 1]
        cg = cm[:, g * SSD_STATE:(g + 1) * SSD_STATE]
        bg_t = bm_t[g * SSD_STATE:(g + 1) * SSD_STATE, :]
        xh = xs[:, h * HEAD_DIM:(h + 1) * HEAD_DIM]
        if h % (N_HEADS // N_KV) == 0:
            s = _dot(cg, bg_t)
        decay = jnp.exp(jnp.where(mask, col - row, NEG))
        y = _dot(s * decay * dt_row, xh)
        st = st_ref[d, h]
        y = y + _dot(cg, st) * jnp.exp(col)
        st_ref[d, h] = jnp.exp(last) * st + _dot(bg_t * (jnp.exp(last - row) * dt_row), xh)
        if d == 0:
            y = y + par_ref[2:3, h:h + 1] * xh
        outs.append(y)
    return jnp.concatenate(outs, axis=1)


def _ssd_kernel(nc, nt, par_ref, cw_ref, cb_ref, xf_ref, xfp_ref, xfn_ref, dtf_ref,
                xb_ref, xbp_ref, xbn_ref, dtb_ref, yf_ref, yb_ref, st_ref, ext_ref):
    j = pl.program_id(1)

    @pl.when(j == 0)
    def _():
        st_ref[...] = jnp.zeros(st_ref.shape, F32)

    cf = j
    cb = jnp.where(j < nc, nc - 1 - j, nt + nc - 1 - j)
    yf_ref[0] = _ssd_direction(0, cf, nc, nt, par_ref, cw_ref, cb_ref, xf_ref, xfp_ref, xfn_ref, dtf_ref,
                               st_ref, ext_ref)
    yb_ref[0] = _ssd_direction(1, cb, nc, nt, par_ref, cw_ref, cb_ref, xb_ref, xbp_ref, xbn_ref, dtb_ref,
                               st_ref, ext_ref)


def _ssd(xbc, dt, par, conv_w, conv_b, n_ctx):
    b, t, w = xbc.shape
    q = CHUNK
    nc, nt = n_ctx // q, t // q
    per = q // SUBLANE
    fwd = lambda j: j
    bwd = lambda j: jnp.where(j < nc, nc - 1 - j, nt + nc - 1 - j)

    def specs(cmap):
        return [pl.BlockSpec((1, q, w), lambda bb, j: (bb, cmap(j), 0)),
                pl.BlockSpec((1, SUBLANE, w), lambda bb, j: (bb, jnp.maximum(cmap(j) * per - 1, 0), 0)),
                pl.BlockSpec((1, SUBLANE, w), lambda bb, j: (bb, jnp.minimum((cmap(j) + 1) * per, nt * per - 1), 0)),
                pl.BlockSpec((1, q, LANE), lambda bb, j: (bb, cmap(j), 0))]

    const = lambda s: pl.BlockSpec(s, lambda bb, j: (0, 0))
    return pl.pallas_call(
        functools.partial(_ssd_kernel, nc, nt),
        grid=(b, nt),
        in_specs=[const((SUBLANE, LANE)), const((SUBLANE, w)), const((1, w))] + specs(fwd) + specs(bwd),
        out_specs=[pl.BlockSpec((1, q, W_GROUP), lambda bb, j: (bb, fwd(j), 0)),
                   pl.BlockSpec((1, q, W_GROUP), lambda bb, j: (bb, bwd(j), 0))],
        out_shape=[jax.ShapeDtypeStruct((b, t, W_GROUP), F32)] * 2,
        scratch_shapes=[pltpu.VMEM((2, N_HEADS, SSD_STATE, HEAD_DIM), F32),
                        pltpu.VMEM((2, q + 2 * SUBLANE, w), F32)],
        compiler_params=_cparams(("parallel", "arbitrary")),
        name="ssd_scan",
    )(par, conv_w, conv_b, xbc, xbc, xbc, dt, xbc, xbc, xbc, dt)


def _short_conv_kernel(nblk, w_ref, b_ref, x_ref, xp_ref, xn_ref, v_ref, x1_ref, x2_ref, ext_ref):
    i = pl.program_id(1)
    tm = x_ref.shape[1]
    ext_ref[0:SUBLANE] = jnp.where(i != 0, xp_ref[0], 0.0)
    ext_ref[SUBLANE:SUBLANE + tm] = x_ref[0]
    ext_ref[SUBLANE + tm:] = jnp.where(i != nblk - 1, xn_ref[0], 0.0)
    u = b_ref[...]
    for k in range(HY_SHORT):
        off = SUBLANE - HY_SHORT // 2 + k
        u = u + w_ref[k:k + 1, :] * ext_ref[off:off + tm, :]
    v_ref[0] = u[:, :W_GROUP]
    x1_ref[0] = u[:, W_GROUP:2 * W_GROUP]
    x2_ref[0] = u[:, 2 * W_GROUP:]


def _short_conv(hy, conv_w, conv_b, row0, nrows):
    b, t, w = hy.shape
    tm = ROW_TILE
    nblk = nrows // tm
    blk0 = row0 // tm
    per = tm // SUBLANE
    out = pl.BlockSpec((1, tm, W_GROUP), lambda bb, i: (bb, i, 0))
    const = lambda s: pl.BlockSpec(s, lambda bb, i: (0, 0))
    return pl.pallas_call(
        functools.partial(_short_conv_kernel, nblk),
        grid=(b, nblk),
        in_specs=[const((SUBLANE, w)), const((1, w)),
                  pl.BlockSpec((1, tm, w), lambda bb, i: (bb, blk0 + i, 0)),
                  pl.BlockSpec((1, SUBLANE, w), lambda bb, i: (bb, jnp.maximum((blk0 + i) * per - 1, 0), 0)),
                  pl.BlockSpec((1, SUBLANE, w),
                               lambda bb, i: (bb, jnp.minimum((blk0 + i + 1) * per, t // SUBLANE - 1), 0))],
        out_specs=[out, out, out],
        out_shape=[jax.ShapeDtypeStruct((b, nrows, W_GROUP), F32)] * 3,
        scratch_shapes=[pltpu.VMEM((tm + 2 * SUBLANE, w), F32)],
        compiler_params=_cparams(("parallel", "arbitrary")),
        name="hyena_short_conv",
    )(conv_w, conv_b, hy, hy, hy)


def _filter_kernel(zf_ref, zb_ref, w1_ref, b1_ref, f1_ref, w2_ref, b2_ref, f2_ref,
                   w3f_ref, w3b_ref, b3f_ref, b3b_ref, dl_ref, o_ref, ss_ref):
    i = pl.program_id(0)
    tr = zf_ref.shape[0]

    def half(z_ref, w3_ref, b3_ref):
        z = z_ref[...]
        h = jnp.sin(f1_ref[...] * (_dot_f32(z, w1_ref[...]) + b1_ref[...]))
        h = jnp.sin(f2_ref[...] * (_dot_f32(h, w2_ref[...]) + b2_ref[...]))
        k = _dot_f32(h, w3_ref[...]) + b3_ref[...]
        return k * jnp.exp(-z[:, 0:1] * dl_ref[...])

    kf = half(zf_ref, w3f_ref, b3f_ref)
    kb = half(zb_ref, w3b_ref, b3b_ref)

    @pl.when(i == 0)
    def _():
        ss_ref[...] = jnp.zeros(ss_ref.shape, F32)

    ss_ref[...] += jnp.sum(kf * kf + kb * kb, axis=0, keepdims=True)
    rows = i * tr + lax.broadcasted_iota(jnp.int32, (tr, 1), 0)
    o_ref[0] = kf
    o_ref[1] = jnp.where(rows == 0, 0.0, kb)


def _hyena_filters(n, w1, b1, f1, w2, b2, f2, w3, b3):
    pos = jnp.arange(n, dtype=F32)
    t = jnp.linspace(0.0, 1.0, n, dtype=F32)
    f = jnp.linspace(1e-4, HY_BANDS - 1, HY_BANDS, dtype=F32)
    ang = 2.0 * math.pi * pos[:, None] * f[None, :] / n
    emb = jnp.concatenate([t[:, None], jnp.cos(ang), -jnp.sin(ang)], axis=-1)
    emb = jnp.pad(emb, ((0, 0), (0, LANE - HY_EMB)))
    emb_b = jnp.roll(jnp.flip(emb, axis=0), 1, axis=0)
    w1p = jnp.pad(w1, ((0, LANE - HY_EMB), (0, 0)))
    w3r = w3.reshape(HY_FILT, HY_ORDER, 2, W_GROUP)
    b3r = b3.reshape(HY_ORDER, 2, W_GROUP)
    wc = HY_ORDER * W_GROUP
    w3f, w3b = w3r[:, :, 0].reshape(HY_FILT, wc), w3r[:, :, 1].reshape(HY_FILT, wc)
    b3f, b3b = b3r[:, 0].reshape(1, wc), b3r[:, 1].reshape(1, wc)
    deltas = jnp.abs(jnp.linspace(HY_MIN_DECAY, HY_MAX_DECAY, W_GROUP, dtype=F32))
    deltas = jnp.tile(deltas, HY_ORDER).reshape(1, wc)
    tr = math.gcd(n, 512)
    const = lambda s: pl.BlockSpec(s, lambda i: (0, 0))
    rows = pl.BlockSpec((tr, LANE), lambda i: (i, 0))
    return pl.pallas_call(
        _filter_kernel,
        grid=(n // tr,),
        in_specs=[rows, rows, const((LANE, HY_FILT)), const((1, HY_FILT)), const((1, HY_FILT)),
                  const((HY_FILT, HY_FILT)), const((1, HY_FILT)), const((1, HY_FILT)),
                  const((HY_FILT, wc)), const((HY_FILT, wc)), const((1, wc)), const((1, wc)), const((1, wc))],
        out_specs=[pl.BlockSpec((2, tr, wc), lambda i: (0, i, 0)), const((1, wc))],
        out_shape=[jax.ShapeDtypeStruct((2, n, wc), F32), jax.ShapeDtypeStruct((1, wc), F32)],
        compiler_params=_cparams(("arbitrary",)),
        name="hyena_filter",
    )(emb, emb_b, w1p, b1.reshape(1, -1), f1.reshape(1, -1), w2, b2.reshape(1, -1), f2.reshape(1, -1),
      w3f, w3b, b3f, b3b, deltas)


def _dft_tables(n):
    nn = 2 * n
    n1, n2 = nn // LANE, LANE
    a = n1 // 2
    k1 = np.arange(n1)[:, None]
    j1 = np.arange(n1)[None, :]
    ang1 = 2.0 * np.pi * (k1 * j1 % n1) / n1
    fr, fi = np.cos(ang1), -np.sin(ang1)
    m1 = np.block([[fr[:, :a], -fi[:, :a]], [fi[:, :a], fr[:, :a]]])
    m1_real = np.concatenate([fr, fi], axis=0)
    er, ei = fr.T[:a], -fi.T[:a]
    m3 = np.stack([np.concatenate([er, -ei], axis=1), np.concatenate([ei, er], axis=1)]) / nn
    k2 = np.arange(n2)[:, None]
    j2 = np.arange(n2)[None, :]
    ang2 = 2.0 * np.pi * (k2 * j2 % n2) / n2
    f2 = np.stack([np.cos(ang2), -np.sin(ang2)])
    angt = 2.0 * np.pi * (np.arange(n1)[:, None] * j2) / nn
    tw = np.stack([np.cos(angt), -np.sin(angt)], axis=1)
    return tuple(jnp.asarray(t, F32) for t in (m1, m1_real, m3, f2, tw))


def _fft_first_kernel(ur_ref, ui_ref, m_ref, o_ref):
    n1 = o_ref.shape[2]
    r = _dot(m_ref[...], jnp.concatenate([ur_ref[0], ui_ref[0]], axis=0))
    o_ref[0, 0] = r[:n1].astype(o_ref.dtype)
    o_ref[0, 1] = r[n1:].astype(o_ref.dtype)


def _fft_first(u, m1):
    b2, a, cols = u.shape
    p = b2 // 2
    n1 = m1.shape[0] // 2
    tc = min(cols, FFT_COLS)
    return pl.pallas_call(
        _fft_first_kernel,
        grid=(p, cols // tc),
        in_specs=[pl.BlockSpec((1, a, tc), lambda pp, j: (2 * pp, 0, j)),
                  pl.BlockSpec((1, a, tc), lambda pp, j: (2 * pp + 1, 0, j)),
                  pl.BlockSpec(m1.shape, lambda pp, j: (0, 0))],
        out_specs=pl.BlockSpec((1, 2, n1, tc), lambda pp, j: (pp, 0, 0, j)),
        out_shape=jax.ShapeDtypeStruct((p, 2, n1, cols), MXU_DTYPE),
        compiler_params=_cparams(("parallel", "arbitrary")),
        name="fft_first",
    )(u, u, m1)


def _twiddled_dft(f2_ref, tw_ref):
    fr, fi = f2_ref[0], f2_ref[1]
    tr, ti = tw_ref[0, 0:1, :], tw_ref[0, 1:2, :]
    gr = fr * tr - fi * ti
    gi = fr * ti + fi * tr
    return gr, gi


def _spectrum_kernel(a_ref, ss_ref, f2_ref, tw_ref, h_ref):
    gr, gi = _twiddled_dft(f2_ref, tw_ref)
    g = jnp.concatenate([jnp.concatenate([gr, -gi], axis=1), jnp.concatenate([gi, gr], axis=1)], axis=0)
    xin = jnp.concatenate([a_ref[0, 0, 0], a_ref[0, 1, 0]], axis=0)
    x = _dot(g, xin) * lax.rsqrt(ss_ref[...] + EPS)
    h_ref[0, 0] = x[:LANE]
    h_ref[1, 0] = x[LANE:]


def _filter_spectrum(af, ss, f2, tw):
    _, _, n1, cols = af.shape
    c = cols // LANE
    a5 = af.reshape(1, 2, n1, LANE, c)
    return pl.pallas_call(
        _spectrum_kernel,
        grid=(n1,),
        in_specs=[pl.BlockSpec((1, 2, 1, LANE, c), lambda k: (0, 0, k, 0, 0)),
                  pl.BlockSpec((1, c), lambda k: (0, 0)),
                  pl.BlockSpec((2, LANE, LANE), lambda k: (0, 0, 0)),
                  pl.BlockSpec((1, 2, LANE), lambda k: (k, 0, 0))],
        out_specs=pl.BlockSpec((2, 1, LANE, c), lambda k: (0, k, 0, 0)),
        out_shape=jax.ShapeDtypeStruct((2, n1, LANE, c), F32),
        compiler_params=_cparams(("arbitrary",)),
        name="filter_spectrum",
    )(a5, ss, f2, tw)


def _fft_mid_kernel(a_ref, h_ref, f2_ref, tw_ref, v_ref):
    npair = a_ref.shape[0]
    c = a_ref.shape[4]
    gr, gi = _twiddled_dft(f2_ref, tw_ref)
    g = jnp.concatenate([jnp.concatenate([gr, -gi], axis=1), jnp.concatenate([gi, gr], axis=1)], axis=0)
    grt, git = gr.T, gi.T
    ginv = jnp.concatenate([jnp.concatenate([grt, git], axis=1), jnp.concatenate([-git, grt], axis=1)], axis=0)
    xin = jnp.concatenate(
        [jnp.concatenate([a_ref[p, 0, 0], a_ref[p, 1, 0]], axis=0) for p in range(npair)], axis=1)
    x = _dot(g, xin)
    xr, xi = x[:LANE], x[LANE:]
    hr = jnp.concatenate([h_ref[0, 0]] * npair, axis=1)
    hi = jnp.concatenate([h_ref[1, 0]] * npair, axis=1)
    y = jnp.concatenate([xr * hr - xi * hi, xr * hi + xi * hr], axis=0)
    v = _dot(ginv, y)
    for p in range(npair):
        v_ref[p, 0, 0] = v[:LANE, p * c:(p + 1) * c].astype(v_ref.dtype)
        v_ref[p, 1, 0] = v[LANE:, p * c:(p + 1) * c].astype(v_ref.dtype)


def _fft_mid(a4, hspec, order, f2, tw):
    npair, _, n1, cols = a4.shape
    c = cols // LANE
    a5 = a4.reshape(npair, 2, n1, LANE, c)
    blk = pl.BlockSpec((npair, 2, 1, LANE, c), lambda k: (0, 0, k, 0, 0))
    out = pl.pallas_call(
        _fft_mid_kernel,
        grid=(n1,),
        in_specs=[blk,
                  pl.BlockSpec((2, 1, LANE, c), lambda k: (0, k, 0, order)),
                  pl.BlockSpec((2, LANE, LANE), lambda k: (0, 0, 0)),
                  pl.BlockSpec((1, 2, LANE), lambda k: (k, 0, 0))],
        out_specs=blk,
        out_shape=jax.ShapeDtypeStruct(a5.shape, MXU_DTYPE),
        compiler_params=_cparams(("arbitrary",)),
        name="fft_mid",
    )(a5, hspec, f2, tw)
    return out.reshape(npair, 2, n1, cols)


def _fft_last_kernel(v_ref, m_ref, z_ref, gate_ref, bias_ref, o_ref):
    vs = jnp.concatenate([v_ref[0, 0], v_ref[0, 1]], axis=0)
    zf = _dot(m_ref[0], vs)
    o_ref[0] = gate_ref[0] * (zf + z_ref[0] * bias_ref[...])


def _fft_last(v4, m3, z, gate, bias_cols):
    b, a, cols = z.shape
    n1 = v4.shape[2]
    tc = min(cols, FFT_COLS)
    row = pl.BlockSpec((1, a, tc), lambda j, bb: (bb, 0, j))
    return pl.pallas_call(
        _fft_last_kernel,
        grid=(cols // tc, b),
        in_specs=[pl.BlockSpec((1, 2, n1, tc), lambda j, bb: (bb // 2, 0, 0, j)),
                  pl.BlockSpec((1, a, 2 * n1), lambda j, bb: (bb % 2, 0, 0)),
                  row, row,
                  pl.BlockSpec((1, tc), lambda j, bb: (0, j))],
        out_specs=row,
        out_shape=jax.ShapeDtypeStruct((b, a, cols), F32),
        compiler_params=_cparams(("parallel", "arbitrary")),
        name="fft_last",
    )(v4, m3, z, gate, bias_cols)


def _hyena_long(v, x1, x2, filt, ss, bias):
    b, n, c = v.shape
    a = n // LANE
    cols = LANE * c
    m1, m1_real, m3, f2, tw = _dft_tables(n)
    wc = filt.shape[2]
    af = _fft_first(filt.reshape(2, a, LANE * wc), m1_real)
    hspec = _filter_spectrum(af, ss, f2, tw)
    z = v.reshape(b, a, cols)
    gates = (x1.reshape(b, a, cols), x2.reshape(b, a, cols))
    for o in range(HY_ORDER):
        a4 = _fft_first(z, m1)
        v4 = _fft_mid(a4, hspec, o, f2, tw)
        z = _fft_last(v4, m3, z, gates[o], jnp.tile(bias[o], LANE).reshape(1, cols))
    return z.reshape(b, n, c)


def _hyena_ctx_kernel(v_ref, x1_ref, x2_ref, filt_ref, ss_ref, bias_ref, ff_ref, fc_ref, fi_ref, o_ref):
    b, n, c = v_ref.shape
    npair = b // 2
    hs = _dot(ff_ref[...], jnp.concatenate([filt_ref[0], filt_ref[1]], axis=0))
    hs = hs * lax.rsqrt(ss_ref[...] + EPS)
    z = [v_ref[i] for i in range(b)]
    gates = (x1_ref, x2_ref)
    for o in range(HY_ORDER):
        hr = jnp.concatenate([hs[:2 * n, o * c:(o + 1) * c]] * npair, axis=1)
        hi = jnp.concatenate([hs[2 * n:, o * c:(o + 1) * c]] * npair, axis=1)
        xin = jnp.concatenate([jnp.concatenate([z[2 * p] for p in range(npair)], axis=1),
                               jnp.concatenate([z[2 * p + 1] for p in range(npair)], axis=1)], axis=0)
        x = _dot(fc_ref[...], xin)
        xr, xi = x[:2 * n], x[2 * n:]
        y = jnp.concatenate([xr * hr - xi * hi, xr * hi + xi * hr], axis=0)
        zf = _dot(fi_ref[...], y)
        bo = bias_ref[o:o + 1, :]
        for i in range(b):
            p, part = i // 2, i % 2
            conv = zf[part * n:(part + 1) * n, p * c:(p + 1) * c]
            z[i] = gates[o][i] * (conv + z[i] * bo)
    for i in range(b):
        o_ref[i] = z[i]


def _hyena_ctx(v, x1, x2, filt, ss, bias):
    b, n, c = v.shape
    nn = 2 * n
    k = np.arange(nn)[:, None]
    j = np.arange(nn)[None, :]
    ang = 2.0 * np.pi * (k * j % nn) / nn
    fr, fi = np.cos(ang), -np.sin(ang)
    ff = np.concatenate([fr, fi], axis=0)
    fc = np.block([[fr[:, :n], -fi[:, :n]], [fi[:, :n], fr[:, :n]]])
    er, ei = fr[:n], -fi[:n]
    finv = np.block([[er, -ei], [ei, er]]) / nn
    return pl.pallas_call(
        _hyena_ctx_kernel,
        out_shape=jax.ShapeDtypeStruct((b, n, c), F32),
        compiler_params=pltpu.CompilerParams(vmem_limit_bytes=VMEM_LIMIT),
        name="hyena_ctx",
    )(v, x1, x2, filt, ss, bias, jnp.asarray(ff, F32), jnp.asarray(fc, F32), jnp.asarray(finv, F32))


def _window_kernel(n_ctx, sink_ref, q_ref, k_ref, v_ref, o_ref):
    g = pl.program_id(1)
    i = pl.program_id(2)
    rep, tq = q_ref.shape[1], q_ref.shape[3]
    t = k_ref.shape[2]
    ql, win = ATT_QL, CHUNK
    wk = ql + 2 * win
    ids = [(c, h) for c in range(tq // ql) for h in range(rep)]
    q0 = [i * tq + c * ql for c in range(tq // ql)]
    start = [pl.multiple_of(jnp.clip(q - win, 0, t - wk), LANE) for q in q0]

    kx = k_ref[0, 0, 0:n_ctx, :]
    vx = jnp.concatenate([v_ref[0, 0, u] for u in range(n_ctx // LANE)], axis=1)
    kl = [k_ref[0, 0, pl.ds(s, wk), :] for s in start]
    vl = [jnp.concatenate([v_ref[0, 0, s // LANE + u] for u in range(wk // LANE)], axis=1) for s in start]

    qs = [q_ref[0, h, :, c * ql:(c + 1) * ql] for c, h in ids]
    s_loc = [jnp.dot(kl[c], q, preferred_element_type=F32) for (c, h), q in zip(ids, qs)]
    s_ctx = [jnp.dot(kx, q, preferred_element_type=F32) for q in qs]

    diff = lax.broadcasted_iota(jnp.int32, (wk, ql), 0) - lax.broadcasted_iota(jnp.int32, (wk, ql), 1)
    krow = lax.broadcasted_iota(jnp.int32, (wk, 1), 0)
    p_loc, p_ctx, e_snk = [], [], []
    for n, (c, h) in enumerate(ids):
        d = diff + (start[c] - q0[c])
        ok = jnp.logical_and(jnp.abs(d) <= win, krow >= n_ctx - start[c])
        ok = jnp.logical_and(ok, q0[c] >= n_ctx)
        sl = jnp.where(ok, s_loc[n], NEG)
        snk = jnp.where(g == 0, sink_ref[0:1, h:h + 1], sink_ref[0:1, rep + h:rep + h + 1]) * LOG2E
        m = jnp.maximum(jnp.maximum(jnp.max(sl, axis=0, keepdims=True),
                                    jnp.max(s_ctx[n], axis=0, keepdims=True)), snk)
        p_loc.append(jnp.exp2(sl - m).astype(MXU_DTYPE))
        p_ctx.append(jnp.exp2(s_ctx[n] - m).astype(MXU_DTYPE))
        e_snk.append(jnp.exp2(snk - m))

    acc = [jnp.dot(vl[c], p_loc[n], preferred_element_type=F32)
           + jnp.dot(vx, p_ctx[n], preferred_element_type=F32) for n, (c, h) in enumerate(ids)]
    outs = [a[:HEAD_DIM] * (1.0 / (a[HEAD_DIM:HEAD_DIM + 1] + e)) for a, e in zip(acc, e_snk)]
    for c in range(tq // ql):
        o_ref[0, c * ql:(c + 1) * ql, :] = jnp.concatenate(outs[c * rep:(c + 1) * rep], axis=0).T


def _window_attention(qt, k, vt, sink, n_ctx):
    b, _, hd, t = qt.shape
    rep = N_HEADS // N_KV
    tq = ATT_TQ if t % ATT_TQ == 0 else ATT_QL
    assert n_ctx == ATT_QL and t % ATT_QL == 0 and rep * hd == LANE and t >= ATT_QL + 2 * CHUNK
    return pl.pallas_call(
        functools.partial(_window_kernel, n_ctx),
        grid=(b, N_KV, t // tq),
        in_specs=[pl.BlockSpec((SUBLANE, LANE), lambda bb, g, i: (0, 0)),
                  pl.BlockSpec((1, rep, hd, tq), lambda bb, g, i: (bb, g, 0, i)),
                  pl.BlockSpec((1, 1, t, hd), lambda bb, g, i: (bb, g, 0, 0)),
                  pl.BlockSpec((1, 1) + vt.shape[2:], lambda bb, g, i: (bb, g, 0, 0, 0))],
        out_specs=pl.BlockSpec((1, tq, rep * hd), lambda bb, g, i: (bb, i, g)),
        out_shape=jax.ShapeDtypeStruct((b, t, N_HEADS * hd), F32),
        compiler_params=_cparams(("parallel", "parallel", "arbitrary")),
        name="window_attention",
    )(sink, qt, k, vt)


def _dense_kernel(n_ctx, q_ref, k_ref, v_ref, o_ref, m_ref, alpha_ref, acc_ref, p_ref):
    i = pl.program_id(2)
    rep, tq = q_ref.shape[1], q_ref.shape[3]
    ql = ATT_QL
    nc = tq // ql

    def run(chunks, kt, nk):
        per = kt // LANE
        ids = [(c, h) for c in chunks for h in range(rep)]
        for c, h in ids:
            m_ref[c * rep + h] = jnp.full((1, ql), NEG, F32)
            alpha_ref[c * rep + h] = jnp.ones((1, ql), F32)
            acc_ref[c * rep + h] = jnp.zeros(acc_ref.shape[1:], F32)
            p_ref[c * rep + h, 0:kt] = jnp.zeros((kt, ql), p_ref.dtype)

        def scores(j):
            kb = k_ref[0, 0, pl.ds(pl.multiple_of(j * kt, kt), kt), :]
            return [jnp.dot(kb, q_ref[0, h, :, c * ql:(c + 1) * ql], preferred_element_type=F32) for c, h in ids]

        def values(j):
            vb = jnp.concatenate([v_ref[0, 0, j * per + u] for u in range(per)], axis=1)
            pvs = [jnp.dot(vb, p_ref[c * rep + h, 0:kt], preferred_element_type=F32) for c, h in ids]
            for (c, h), pv in zip(ids, pvs):
                n = c * rep + h
                acc_ref[n] = alpha_ref[n] * acc_ref[n] + pv

        def softmax(ss):
            for (c, h), s in zip(ids, ss):
                n = c * rep + h
                m = m_ref[n]
                mn = jnp.maximum(m, jnp.max(s, axis=0, keepdims=True))
                p_ref[n, 0:kt] = jnp.exp2(s - mn).astype(p_ref.dtype)
                alpha_ref[n] = jnp.exp2(m - mn)
                m_ref[n] = mn

        def body(j, carry):
            ss = scores(j)
            values(jnp.maximum(j - 1, 0))
            softmax(ss)
            return carry

        lax.fori_loop(0, nk, body, 0)
        values(nk - 1)
        for c in chunks:
            o = [acc_ref[c * rep + h] for h in range(rep)]
            o = [a[:HEAD_DIM] * (1.0 / a[HEAD_DIM:HEAD_DIM + 1]) for a in o]
            o_ref[0, c * ql:(c + 1) * ql, :] = jnp.concatenate(o, axis=0).T

    kt_all = p_ref.shape[1]
    nk_all = k_ref.shape[2] // kt_all

    @pl.when(i == 0)
    def _():
        run([0], n_ctx, 1)
        if nc > 1:
            run(list(range(1, nc)), kt_all, nk_all)

    @pl.when(i != 0)
    def _():
        run(list(range(nc)), kt_all, nk_all)


def _dense_attention(qt, k, vt, n_ctx):
    b, _, hd, t = qt.shape
    rep = N_HEADS // N_KV
    tq = ATT_TQ if t % ATT_TQ == 0 else ATT_QL
    kt = ATT_KT if t % ATT_KT == 0 else ATT_QL
    assert n_ctx == ATT_QL and t % ATT_QL == 0 and rep * hd == LANE
    return pl.pallas_call(
        functools.partial(_dense_kernel, n_ctx),
        grid=(b, N_KV, t // tq),
        in_specs=[pl.BlockSpec((1, rep, hd, tq), lambda bb, g, i: (bb, g, 0, i)),
                  pl.BlockSpec((1, 1, t, hd), lambda bb, g, i: (bb, g, 0, 0)),
                  pl.BlockSpec((1, 1) + vt.shape[2:], lambda bb, g, i: (bb, g, 0, 0, 0))],
        out_specs=pl.BlockSpec((1, tq, rep * hd), lambda bb, g, i: (bb, i, g)),
        out_shape=jax.ShapeDtypeStruct((b, t, N_HEADS * hd), F32),
        scratch_shapes=[pltpu.VMEM((rep * tq // ATT_QL, 1, ATT_QL), F32),
                        pltpu.VMEM((rep * tq // ATT_QL, 1, ATT_QL), F32),
                        pltpu.VMEM((rep * tq // ATT_QL, vt.shape[3], ATT_QL), F32),
                        pltpu.VMEM((rep * tq // ATT_QL, kt, ATT_QL), MXU_DTYPE)],
        compiler_params=_cparams(("parallel", "parallel", "arbitrary")),
        name="dense_attention",
    )(qt, k, vt)


def _mix_mlp_kernel(first, x_ref, ctx_ref, mod_ref, yf_ref, yb_ref, z_ref, gs_ref, hy_ref, hyc_ref,
                    yw_ref, yd_ref, gpost_ref, gpre_ref, gpost2_ref, wo_ref, w1_ref, w2_ref, o_ref):
    i = pl.program_id(1)
    if first:
        xv = jnp.where(i == 0, ctx_ref[0], x_ref[0])
        yh = jnp.where(i == 0, hyc_ref[0], hy_ref[0])
    else:
        xv = x_ref[0]
        yh = hy_ref[0]
    g1 = mod_ref[0, 0, 2:3, :]
    sh2 = mod_ref[0, 0, 3:4, :]
    sc2 = mod_ref[0, 0, 4:5, :]
    g2 = mod_ref[0, 0, 5:6, :]
    ya = _rms((yf_ref[0] + yb_ref[0]) * _silu(z_ref[0]), gs_ref[...])
    w = W_GROUP
    y = (_dot(ya, wo_ref[0:w, :]) + _dot(yh, wo_ref[w:2 * w, :])
         + _dot(yw_ref[0], wo_ref[2 * w:3 * w, :]) + _dot(yd_ref[0], wo_ref[3 * w:4 * w, :]))
    x1 = xv + g1 * _rms(y, gpost_ref[...])
    hb = (_rms(x1, gpre_ref[...]) * (1.0 + sc2) + sh2).astype(MXU_DTYPE)
    d = x1.shape[1]
    acc = jnp.zeros(x1.shape, F32)
    for c in range(w1_ref.shape[1] // d):
        a = jnp.maximum(jnp.dot(hb, w1_ref[:, c * d:(c + 1) * d], preferred_element_type=F32), 0.0)
        acc = acc + _dot(a * a, w2_ref[c * d:(c + 1) * d, :])
    o_ref[0] = x1 + g2 * _rms(acc, gpost2_ref[...])


def _mix_mlp(first, x, ctx, mod6, yf, yb, z, g_ssd, hy, hyc, yw, yd, g_post, g_pre2, g_post2, wo, w1, w2):
    b, t, _ = yf.shape
    d = wo.shape[1]
    tm = ROW_TILE
    off = 0 if first else 1
    nrow = t // tm - off
    if first:
        x_spec = pl.BlockSpec((1, tm, d), lambda bb, i: (bb, jnp.maximum(i - 1, 0), 0))
        hy_spec = pl.BlockSpec((1, tm, W_GROUP), lambda bb, i: (bb, jnp.maximum(i - 1, 0), 0))
        mod_spec = pl.BlockSpec((1, 1, 6, d), lambda bb, i: (bb, jnp.minimum(i, 1), 0, 0))
    else:
        x_spec = pl.BlockSpec((1, tm, d), lambda bb, i: (bb, i + 1, 0))
        hy_spec = pl.BlockSpec((1, tm, W_GROUP), lambda bb, i: (bb, i, 0))
        mod_spec = pl.BlockSpec((1, 1, 6, d), lambda bb, i: (bb, 1, 0, 0))
    first_blk = lambda w: pl.BlockSpec((1, tm, w), lambda bb, i: (bb, 0, 0))
    row = lambda w: pl.BlockSpec((1, tm, w), lambda bb, i: (bb, i + off, 0))
    vec = lambda w: pl.BlockSpec((1, w), lambda bb, i: (0, 0))
    full = lambda a: pl.BlockSpec(a.shape, lambda bb, i: (0, 0))
    return pl.pallas_call(
        functools.partial(_mix_mlp_kernel, first),
        grid=(b, nrow),
        in_specs=[x_spec, first_blk(d), mod_spec, row(W_GROUP), row(W_GROUP), row(W_GROUP), vec(W_GROUP),
                  hy_spec, first_blk(W_GROUP), row(W_GROUP), row(W_GROUP), vec(d), vec(d), vec(d),
                  full(wo), full(w1), full(w2)],
        out_specs=pl.BlockSpec((1, tm, d), lambda bb, i: (bb, i, 0)),
        out_shape=jax.ShapeDtypeStruct((b, nrow * tm, d), F32),
        compiler_params=_cparams(("parallel", "arbitrary")),
        name="mix_mlp",
    )(x, ctx, mod6, yf, yb, z, g_ssd, hy, hyc, yw, yd, g_post, g_pre2, g_post2, wo, w1, w2)


def _rope_tables(n, n_ctx):
    rows = n // GRID_W
    row = jnp.repeat(jnp.arange(rows, dtype=F32), GRID_W)
    col = jnp.tile(jnp.arange(GRID_W, dtype=F32), rows)
    n_freq = HEAD_DIM // 4
    inv = ROPE_THETA ** (-jnp.arange(n_freq, dtype=F32) / n_freq)
    ang = jnp.concatenate([row[:, None] * inv, col[:, None] * inv], axis=-1)
    cos, sin = jnp.cos(ang), jnp.sin(ang)
    cs = jnp.concatenate([cos, cos], axis=1)
    sn = jnp.concatenate([-sin, sin], axis=1)
    cs = jnp.concatenate([jnp.ones((n_ctx, HEAD_DIM), F32), cs], axis=0)
    sn = jnp.concatenate([jnp.zeros((n_ctx, HEAD_DIM), F32), sn], axis=0)
    return cs, sn


def _pad_rows(a, rows):
    return jnp.pad(a, ((0, rows - a.shape[0]), (0, 0)))


def kernel(x, c, ctx, c_ctx, w_mod, b_mod, norm_mix_pre, norm_mix_post, norm_mlp_pre, norm_mlp_post, w_in, w_out, ssd_conv_w, ssd_conv_b, ssd_a_log, ssd_dt_bias, ssd_d, ssd_norm, hy_conv_w, hy_conv_b, hy_w1, hy_b1, hy_freq1, hy_w2, hy_b2, hy_freq2, hy_w3, hy_b3, hy_bias, attn_sink, q_norm, k_norm, mlp_w1, mlp_w2):
    b, n, d = x.shape
    n_ctx = ctx.shape[1]
    depth = w_mod.shape[0]
    assert n_ctx == ROW_TILE and n % ROW_TILE == 0 and b % 2 == 0 and b + 1 <= SUBLANE

    cc = _pad_rows(jnp.concatenate([c, c_ctx[None, :]], axis=0), SUBLANE)
    mod = _modulation(cc, w_mod, b_mod)
    cs, sn = _rope_tables(n, n_ctx)

    off_b = W_GROUP + SSD_XBC + 2 * N_HEADS
    xall = None
    for l in range(depth):
        first = l == 0
        need_ctx = l < depth - 1
        mod_lat = mod[l, :b].reshape(b, 1, 6, d)
        mod_ctx = jnp.broadcast_to(mod[l, b].reshape(1, 1, 6, d), (b, 1, 6, d))
        mod6 = jnp.concatenate([mod_ctx, mod_lat], axis=1)

        wl = w_in[l]
        off_c = off_b + 3 * W_GROUP
        off_d = off_c + (N_HEADS + 2 * N_KV) * HEAD_DIM
        w_row = jnp.concatenate(
            [wl[:, :W_GROUP + SSD_XBC], wl[:, off_b:off_c], wl[:, W_GROUP + SSD_XBC:off_b],
             jnp.zeros((d, D_ROW_PAD - off_c), F32)], axis=1).astype(MXU_DTYPE)
        w_att_t = jnp.concatenate([wl[:, off_d:], wl[:, off_c:off_d]], axis=1).T.astype(MXU_DTYPE)
        xin, cin = (x, ctx) if first else (xall, None)
        (z, xbc, hy, dt, qwt, kw, vwt, qdt, kd, vdt) = _in_projection(
            xin, cin, mod6, norm_mix_pre[l].reshape(1, d), w_row, w_att_t, cs, sn, q_norm[l], k_norm[l])

        par = jnp.zeros((SUBLANE, LANE), F32)
        par = par.at[0, :2 * N_HEADS].set(ssd_a_log[l].reshape(-1))
        par = par.at[1, :2 * N_HEADS].set(ssd_dt_bias[l].reshape(-1))
        par = par.at[2, :N_HEADS].set(ssd_d[l])
        yf, yb = _ssd(xbc, dt, par, _pad_rows(ssd_conv_w[l], SUBLANE), ssd_conv_b[l].reshape(1, -1), n_ctx)

        hy_w = _pad_rows(hy_conv_w[l], SUBLANE)
        hy_b = hy_conv_b[l].reshape(1, -1)
        filt_args = (hy_w1[l], hy_b1[l], hy_freq1[l], hy_w2[l], hy_b2[l], hy_freq2[l], hy_w3[l], hy_b3[l])
        v, x1, x2 = _short_conv(hy, hy_w, hy_b, n_ctx, n)
        yhy = _hyena_long(v, x1, x2, *_hyena_filters(n, *filt_args), hy_bias[l])
        if need_ctx:
            vc, x1c, x2c = _short_conv(hy, hy_w, hy_b, 0, n_ctx)
            yhy_ctx = _hyena_ctx(vc, x1c, x2c, *_hyena_filters(n_ctx, *filt_args), hy_bias[l])
        else:
            yhy_ctx = yhy

        sink = jnp.zeros((SUBLANE, LANE), F32).at[0, :N_HEADS].set(attn_sink[l])
        yw = _window_attention(qwt, kw, vwt, sink, n_ctx)
        yd = _dense_attention(qdt, kd, vdt, n_ctx)

        xres, cres = (x, ctx) if first else (xall, xall)
        xall = _mix_mlp(first, xres, cres, mod6, yf, yb, z, ssd_norm[l].reshape(1, -1), yhy, yhy_ctx, yw, yd,
                        norm_mix_post[l].reshape(1, d), norm_mlp_pre[l].reshape(1, d),
                        norm_mlp_post[l].reshape(1, d), w_out[l].astype(MXU_DTYPE),
                        mlp_w1[l].astype(MXU_DTYPE), mlp_w2[l].astype(MXU_DTYPE))
    return xall
```

```python
import functools
import math

import numpy as np
import jax
import jax.numpy as jnp
from jax import lax
from jax.experimental import pallas as pl
from jax.experimental.pallas import tpu as pltpu

F32 = jnp.float32
MXU_DTYPE = jnp.bfloat16

EPS = 1e-6
HEAD_DIM = 64
GRID_W = 64
ROPE_THETA = 10000.0
N_HEADS = 4
N_KV = 2
W_GROUP = N_HEADS * HEAD_DIM
SSD_STATE = 64
SSD_XBC = W_GROUP + 2 * N_KV * SSD_STATE
HY_ORDER = 2
HY_BANDS = 16
HY_EMB = 2 * HY_BANDS + 1
HY_FILT = 64
HY_MAX_DECAY = math.log(1e-2) / 0.3
HY_MIN_DECAY = math.log(1e-2) / 1.5
CHUNK = 128
LANE = 128
SUBLANE = 8
ROW_TILE = 256
ATT_TQ = 768
FFT_COLS = 4096
FFT_KB = 4
VMEM_LIMIT = 56 * 1024 * 1024
NEG = -1e30

C_Z = (0, 256)
C_XBC = (256, 768)
C_HY = (768, 1536)
C_DT = (1536, 1664)
D_ROW_PAD = 1664
A_QD, A_KD, A_VD = 0, 256, 384
A_QW, A_KW, A_VW = 512, 768, 896
V_ROWS = HEAD_DIM + 16
ATT_QL = 256
ATT_KT = 768
LOG2E = math.log2(math.e)


def _cparams(sem):
    return pltpu.CompilerParams(dimension_semantics=sem, vmem_limit_bytes=VMEM_LIMIT)


def _rms(x, g):
    return x * lax.rsqrt(jnp.mean(x * x, axis=-1, keepdims=True) + EPS) * g


def _silu(x):
    return x * (1.0 / (1.0 + jnp.exp(-x)))


def _softplus(x):
    return jnp.maximum(x, 0.0) + jnp.log(1.0 + jnp.exp(-jnp.abs(x)))


def _dot(a, b):
    return jnp.dot(a.astype(MXU_DTYPE), b.astype(MXU_DTYPE), preferred_element_type=F32)


def _dot_nt(a, b):
    return lax.dot_general(a.astype(MXU_DTYPE), b.astype(MXU_DTYPE), (((1,), (1,)), ((), ())),
                           preferred_element_type=F32)


def _dot_f32(a, b):
    return jnp.dot(a, b, preferred_element_type=F32, precision=lax.Precision.HIGHEST)


def _mod_kernel(c_ref, w_ref, b_ref, o_ref):
    o_ref[0] = _dot_f32(_silu(c_ref[...]), w_ref[0]) + b_ref[0]


def _modulation(cc, w_mod, b_mod):
    depth, d, d6 = w_mod.shape
    tn = d6 // 4
    return pl.pallas_call(
        _mod_kernel,
        grid=(depth, d6 // tn),
        in_specs=[pl.BlockSpec((SUBLANE, d), lambda l, j: (0, 0)),
                  pl.BlockSpec((1, d, tn), lambda l, j: (l, 0, j)),
                  pl.BlockSpec((1, 1, tn), lambda l, j: (l, 0, j))],
        out_specs=pl.BlockSpec((1, SUBLANE, tn), lambda l, j: (l, 0, j)),
        out_shape=jax.ShapeDtypeStruct((depth, SUBLANE, d6), F32),
        compiler_params=_cparams(("arbitrary", "arbitrary")),
        name="modulation",
    )(cc, w_mod, b_mod.reshape(depth, 1, d6))


def _inproj_kernel(x_ref, xp_ref, xn_ref, ctx_ref, mod_ref, g_ref, w_ref, wt_ref, cst_ref, snt_ref, qnt_ref, knt_ref,
                   scw_ref, scb_ref, hcw_ref, hcb_ref,
                   z_ref, u_ref, dt_ref, v_ref, x1_ref, x2_ref, vc_ref, x1c_ref, x2c_ref,
                   qwt_ref, kw_ref, vwt_ref, qdt_ref, kd_ref, vdt_ref, ext_ref):
    i = pl.program_id(1)
    nrow = pl.num_programs(1)
    xv = jnp.where(i == 0, ctx_ref[0], x_ref[0])
    tm = xv.shape[0]
    sh = mod_ref[0, 0, 0:1, :]
    sc = mod_ref[0, 0, 1:2, :]

    def modulated(rows):
        return (_rms(rows, g_ref[...]) * (1.0 + sc) + sh).astype(MXU_DTYPE)

    hb = modulated(xv)

    def proj(cols):
        return jnp.dot(hb, w_ref[:, cols[0]:cols[1]], preferred_element_type=F32)

    z_ref[0] = proj(C_Z)
    dt_ref[0] = proj(C_DT)

    conv_cols = (C_XBC[0], C_HY[1])
    halo = jnp.dot(modulated(jnp.concatenate([xp_ref[0], xn_ref[0]], axis=0)),
                   w_ref[:, conv_cols[0]:conv_cols[1]], preferred_element_type=F32)
    prev_ok = i > 1
    next_ok = jnp.logical_and(i >= 1, i < nrow - 1)
    ext_ref[0:SUBLANE] = jnp.where(prev_ok, halo[0:SUBLANE], 0.0)
    ext_ref[SUBLANE:SUBLANE + tm] = proj(conv_cols)
    ext_ref[SUBLANE + tm:] = jnp.where(next_ok, halo[SUBLANE:], 0.0)

    def conv(w_r, b_r, lo, hi):
        taps = w_r.shape[0]
        acc = b_r[...]
        for k in range(taps):
            off = SUBLANE - taps // 2 + k
            acc = acc + w_r[k:k + 1, :] * ext_ref[off:off + tm, lo:hi]
        return acc

    nx = C_XBC[1] - C_XBC[0]
    u_ref[0] = _silu(conv(scw_ref, scb_ref, 0, nx))
    hyc = conv(hcw_ref, hcb_ref, nx, nx + C_HY[1] - C_HY[0])
    parts = [hyc[:, j * W_GROUP:(j + 1) * W_GROUP] for j in range(3)]
    for ref, val in zip((v_ref, x1_ref, x2_ref), parts):
        ref[0] = val

    @pl.when(i == 0)
    def _():
        for ref, val in zip((vc_ref, x1c_ref, x2c_ref), parts):
            ref[0] = val

    pt = _dot_nt(wt_ref[...], hb)
    cst = cst_ref[...]
    snt = snt_ref[...]
    tile = lambda r: jnp.concatenate([r[...]] * (tm // LANE), axis=1)
    qnt, knt = tile(qnt_ref), tile(knt_ref)
    half = HEAD_DIM // 2
    qscale = HEAD_DIM ** -0.5 * LOG2E

    def head(row0):
        return pt[row0:row0 + HEAD_DIM]

    def norm(t, gain):
        return t * lax.rsqrt(jnp.mean(t * t, axis=0, keepdims=True) + EPS) * gain

    def rope(t):
        return t * cst + jnp.concatenate([t[half:], t[:half]], axis=0) * snt

    def put_values(v_ref, row0):
        ones = jnp.ones((v_ref.shape[3] - HEAD_DIM, LANE), v_ref.dtype)
        for g in range(N_KV):
            vt = head(row0 + g * HEAD_DIM)
            for j in range(tm // LANE):
                v_ref[0, g, j, 0:HEAD_DIM, :] = vt[:, j * LANE:(j + 1) * LANE].astype(v_ref.dtype)
                v_ref[0, g, j, HEAD_DIM:, :] = ones

    def put_keys(k_ref, kt_pair):
        k_rows = jnp.concatenate(kt_pair, axis=0).T
        for g in range(N_KV):
            k_ref[0, g] = k_rows[:, g * HEAD_DIM:(g + 1) * HEAD_DIM].astype(k_ref.dtype)

    for h in range(N_HEADS):
        qdt_ref[0, h] = (rope(norm(head(A_QD + h * HEAD_DIM), qnt)) * qscale).astype(qdt_ref.dtype)
        qwt_ref[0, h] = (rope(head(A_QW + h * HEAD_DIM)) * qscale).astype(qwt_ref.dtype)
    put_keys(kd_ref, [rope(norm(head(A_KD + g * HEAD_DIM), knt)) for g in range(N_KV)])
    put_keys(kw_ref, [rope(head(A_KW + g * HEAD_DIM)) for g in range(N_KV)])
    put_values(vdt_ref, A_VD)
    put_values(vwt_ref, A_VW)


def _in_projection(x, ctx, mod6, g_pre, w_row, w_att_t, cs, sn, qn, kn, ssd_cw, ssd_cb, hy_cw, hy_cb):
    b, _, d = x.shape
    tm = ROW_TILE
    hp = tm // SUBLANE
    if ctx is None:
        t = x.shape[1]
        first_lat = 1
        x_spec = pl.BlockSpec((1, tm, d), lambda bb, i: (bb, i, 0))
        ctx_arr, ctx_spec = x, pl.BlockSpec((1, tm, d), lambda bb, i: (bb, 0, 0))
    else:
        assert ctx.shape[1] == tm
        t = x.shape[1] + tm
        first_lat = 0
        x_spec = pl.BlockSpec((1, tm, d), lambda bb, i: (bb, jnp.maximum(i - 1, 0), 0))
        ctx_arr, ctx_spec = ctx, pl.BlockSpec((1, tm, d), lambda bb, i: (bb, 0, 0))
    nrow = t // tm
    last8 = x.shape[1] // SUBLANE - 1
    xp_spec = pl.BlockSpec((1, SUBLANE, d), lambda bb, i: (bb, jnp.clip((i - 1 + first_lat) * hp - 1, 0, last8), 0))
    xn_spec = pl.BlockSpec((1, SUBLANE, d), lambda bb, i: (bb, jnp.clip((i + first_lat) * hp, 0, last8), 0))
    per = tm // LANE
    row = lambda w: pl.BlockSpec((1, tm, w), lambda bb, i: (bb, i, 0))
    lat = pl.BlockSpec((1, tm, W_GROUP), lambda bb, i: (bb, jnp.maximum(i - 1, 0), 0))
    cblk = pl.BlockSpec((1, tm, W_GROUP), lambda bb, i: (bb, 0, 0))
    full = lambda a: pl.BlockSpec(a.shape, lambda bb, i: (0, 0))
    f32 = lambda w: jax.ShapeDtypeStruct((b, t, w), F32)
    lat_shape = jax.ShapeDtypeStruct((b, t - tm, W_GROUP), F32)
    ctx_shape = jax.ShapeDtypeStruct((b, tm, W_GROUP), F32)
    q_spec = pl.BlockSpec((1, N_HEADS, HEAD_DIM, tm), lambda bb, i: (bb, 0, 0, i))
    k_spec = pl.BlockSpec((1, N_KV, tm, HEAD_DIM), lambda bb, i: (bb, 0, i, 0))
    v_spec = pl.BlockSpec((1, N_KV, per, V_ROWS, LANE), lambda bb, i: (bb, 0, i, 0, 0))
    q_shape = jax.ShapeDtypeStruct((b, N_HEADS, HEAD_DIM, t), MXU_DTYPE)
    k_shape = jax.ShapeDtypeStruct((b, N_KV, t, HEAD_DIM), MXU_DTYPE)
    v_shape = jax.ShapeDtypeStruct((b, N_KV, t // LANE, V_ROWS, LANE), MXU_DTYPE)
    gain = lambda v: jnp.broadcast_to(v.reshape(HEAD_DIM, 1), (HEAD_DIM, LANE))
    qnt, knt = gain(qn), gain(kn)
    return pl.pallas_call(
        _inproj_kernel,
        grid=(b, nrow),
        in_specs=[x_spec, xp_spec, xn_spec, ctx_spec,
                  pl.BlockSpec((1, 1, 6, d), lambda bb, i: (bb, jnp.minimum(i, 1), 0, 0)),
                  pl.BlockSpec((1, d), lambda bb, i: (0, 0)),
                  full(w_row), full(w_att_t),
                  pl.BlockSpec((HEAD_DIM, tm), lambda bb, i: (0, i)),
                  pl.BlockSpec((HEAD_DIM, tm), lambda bb, i: (0, i)),
                  full(qnt), full(knt), full(ssd_cw), full(ssd_cb), full(hy_cw), full(hy_cb)],
        out_specs=[row(W_GROUP), row(SSD_XBC), row(LANE), lat, lat, lat, cblk, cblk, cblk,
                   q_spec, k_spec, v_spec, q_spec, k_spec, v_spec],
        out_shape=[f32(W_GROUP), f32(SSD_XBC), f32(LANE), lat_shape, lat_shape, lat_shape,
                   ctx_shape, ctx_shape, ctx_shape,
                   q_shape, k_shape, v_shape, q_shape, k_shape, v_shape],
        scratch_shapes=[pltpu.VMEM((tm + 2 * SUBLANE, C_HY[1] - C_XBC[0]), F32)],
        compiler_params=_cparams(("parallel", "arbitrary")),
        name="in_projection",
    )(x, x, x, ctx_arr, mod6, g_pre, w_row, w_att_t, cs.T, sn.T, qnt, knt, ssd_cw, ssd_cb, hy_cw, hy_cb)


def _ssd_kernel(nc, nt, par_ref, uf_ref, dtf_ref, ub_ref, dtb_ref, yf_ref, yb_ref, st_ref):
    j = pl.program_id(1)
    q = CHUNK
    rep = N_HEADS // N_KV

    @pl.when(j == 0)
    def _():
        st_ref[...] = jnp.zeros(st_ref.shape, F32)

    a_all = -jnp.exp(par_ref[0:1, :])
    ri = lax.broadcasted_iota(jnp.int32, (q, q), 0)
    ci = lax.broadcasted_iota(jnp.int32, (q, q), 1)
    dirs = []
    for d, (u_ref, dt_ref) in enumerate(((uf_ref, dtf_ref), (ub_ref, dtb_ref))):
        u = u_ref[0]
        dtv = _softplus(dt_ref[0] + par_ref[1:2, :])
        mask = (ri >= ci) if d == 0 else (ri <= ci)
        cum = _dot_f32(mask.astype(F32), dtv * a_all)
        dirs.append(dict(
            d=d, mask=mask, cum=cum, cum_t=cum.T, dt_t=dtv.T, end=q - 1 if d == 0 else 0,
            xs=u[:, :W_GROUP], bm_t=u[:, W_GROUP:W_GROUP + N_KV * SSD_STATE].T,
            cm=u[:, W_GROUP + N_KV * SSD_STATE:]))

    def head_terms(v, h):
        hl = N_HEADS * v["d"] + h
        g = h // rep
        sl = slice(g * SSD_STATE, (g + 1) * SSD_STATE)
        return dict(col=v["cum"][:, hl:hl + 1], row=v["cum_t"][hl:hl + 1, :], dt_row=v["dt_t"][hl:hl + 1, :],
                    last=v["cum"][v["end"]:v["end"] + 1, hl:hl + 1], cg=v["cm"][:, sl], bg_t=v["bm_t"][sl, :],
                    xh=v["xs"][:, h * HEAD_DIM:(h + 1) * HEAD_DIM])

    terms = [[head_terms(v, h) for h in range(N_HEADS)] for v in dirs]
    scores = [[_dot(ts[g * rep]["cg"], ts[g * rep]["bg_t"]) for g in range(N_KV)] for ts in terms]
    carried = [[_dot(t["cg"], st_ref[v["d"], h]) for h, t in enumerate(ts)] for v, ts in zip(dirs, terms)]
    states = [[_dot(t["bg_t"] * (jnp.exp(t["last"] - t["row"]) * t["dt_row"]), t["xh"]) for t in ts]
              for ts in terms]
    diag = [[_dot(scores[v["d"]][h // rep] * jnp.exp(jnp.where(v["mask"], t["col"] - t["row"], NEG)) * t["dt_row"],
                  t["xh"]) for h, t in enumerate(ts)] for v, ts in zip(dirs, terms)]
    for v, ts, y_ref in zip(dirs, terms, (yf_ref, yb_ref)):
        d = v["d"]
        outs = []
        for h, t in enumerate(ts):
            y = diag[d][h] + carried[d][h] * jnp.exp(t["col"])
            st_ref[d, h] = jnp.exp(t["last"]) * st_ref[d, h] + states[d][h]
            if d == 0:
                y = y + par_ref[2:3, h:h + 1] * t["xh"]
            outs.append(y)
        y_ref[0] = jnp.concatenate(outs, axis=1)


def _ssd(u, dt, par, n_ctx):
    b, t, w = u.shape
    q = CHUNK
    nc, nt = n_ctx // q, t // q
    fwd = lambda j: j
    bwd = lambda j: jnp.where(j < nc, nc - 1 - j, nt + nc - 1 - j)

    def specs(cmap):
        return [pl.BlockSpec((1, q, w), lambda bb, j: (bb, cmap(j), 0)),
                pl.BlockSpec((1, q, LANE), lambda bb, j: (bb, cmap(j), 0))]

    return pl.pallas_call(
        functools.partial(_ssd_kernel, nc, nt),
        grid=(b, nt),
        in_specs=[pl.BlockSpec((SUBLANE, LANE), lambda bb, j: (0, 0))] + specs(fwd) + specs(bwd),
        out_specs=[pl.BlockSpec((1, q, W_GROUP), lambda bb, j: (bb, fwd(j), 0)),
                   pl.BlockSpec((1, q, W_GROUP), lambda bb, j: (bb, bwd(j), 0))],
        out_shape=[jax.ShapeDtypeStruct((b, t, W_GROUP), F32)] * 2,
        scratch_shapes=[pltpu.VMEM((2, N_HEADS, SSD_STATE, HEAD_DIM), F32)],
        compiler_params=_cparams(("parallel", "arbitrary")),
        name="ssd_scan",
    )(par, u, dt, u, dt)


def _filter_kernel(zf_ref, zb_ref, w1_ref, b1_ref, f1_ref, w2_ref, b2_ref, f2_ref,
                   w3f_ref, w3b_ref, b3f_ref, b3b_ref, dl_ref, o_ref, ss_ref):
    i = pl.program_id(0)
    tr = zf_ref.shape[0]

    def half(z_ref, w3_ref, b3_ref):
        z = z_ref[...]
        h = jnp.sin(f1_ref[...] * (_dot_f32(z, w1_ref[...]) + b1_ref[...]))
        h = jnp.sin(f2_ref[...] * (_dot_f32(h, w2_ref[...]) + b2_ref[...]))
        k = _dot_f32(h, w3_ref[...]) + b3_ref[...]
        return k * jnp.exp(-z[:, 0:1] * dl_ref[...])

    kf = half(zf_ref, w3f_ref, b3f_ref)
    kb = half(zb_ref, w3b_ref, b3b_ref)

    @pl.when(i == 0)
    def _():
        ss_ref[...] = jnp.zeros(ss_ref.shape, F32)

    ss_ref[...] += jnp.sum(kf * kf + kb * kb, axis=0, keepdims=True)
    rows = i * tr + lax.broadcasted_iota(jnp.int32, (tr, 1), 0)
    o_ref[0] = kf
    o_ref[1] = jnp.where(rows == 0, 0.0, kb)


def _hyena_filters(n, w1, b1, f1, w2, b2, f2, w3, b3):
    pos = jnp.arange(n, dtype=F32)
    t = jnp.linspace(0.0, 1.0, n, dtype=F32)
    f = jnp.linspace(1e-4, HY_BANDS - 1, HY_BANDS, dtype=F32)
    ang = 2.0 * math.pi * pos[:, None] * f[None, :] / n
    emb = jnp.concatenate([t[:, None], jnp.cos(ang), -jnp.sin(ang)], axis=-1)
    emb = jnp.pad(emb, ((0, 0), (0, LANE - HY_EMB)))
    emb_b = jnp.roll(jnp.flip(emb, axis=0), 1, axis=0)
    w1p = jnp.pad(w1, ((0, LANE - HY_EMB), (0, 0)))
    w3r = w3.reshape(HY_FILT, HY_ORDER, 2, W_GROUP)
    b3r = b3.reshape(HY_ORDER, 2, W_GROUP)
    wc = HY_ORDER * W_GROUP
    w3f, w3b = w3r[:, :, 0].reshape(HY_FILT, wc), w3r[:, :, 1].reshape(HY_FILT, wc)
    b3f, b3b = b3r[:, 0].reshape(1, wc), b3r[:, 1].reshape(1, wc)
    deltas = jnp.abs(jnp.linspace(HY_MIN_DECAY, HY_MAX_DECAY, W_GROUP, dtype=F32))
    deltas = jnp.tile(deltas, HY_ORDER).reshape(1, wc)
    tr = math.gcd(n, 512)
    const = lambda s: pl.BlockSpec(s, lambda i: (0, 0))
    rows = pl.BlockSpec((tr, LANE), lambda i: (i, 0))
    return pl.pallas_call(
        _filter_kernel,
        grid=(n // tr,),
        in_specs=[rows, rows, const((LANE, HY_FILT)), const((1, HY_FILT)), const((1, HY_FILT)),
                  const((HY_FILT, HY_FILT)), const((1, HY_FILT)), const((1, HY_FILT)),
                  const((HY_FILT, wc)), const((HY_FILT, wc)), const((1, wc)), const((1, wc)), const((1, wc))],
        out_specs=[pl.BlockSpec((2, tr, wc), lambda i: (0, i, 0)), const((1, wc))],
        out_shape=[jax.ShapeDtypeStruct((2, n, wc), F32), jax.ShapeDtypeStruct((1, wc), F32)],
        compiler_params=_cparams(("arbitrary",)),
        name="hyena_filter",
    )(emb, emb_b, w1p, b1.reshape(1, -1), f1.reshape(1, -1), w2, b2.reshape(1, -1), f2.reshape(1, -1),
      w3f, w3b, b3f, b3b, deltas)


def _dft_tables(n):
    nn = 2 * n
    n1, n2 = nn // LANE, LANE
    a = n1 // 2
    k1 = np.arange(n1)[:, None]
    j1 = np.arange(n1)[None, :]
    ang1 = 2.0 * np.pi * (k1 * j1 % n1) / n1
    fr, fi = np.cos(ang1), -np.sin(ang1)
    m1 = np.block([[fr[:, :a], -fi[:, :a]], [fi[:, :a], fr[:, :a]]])
    m1_real = np.concatenate([fr, fi], axis=0)
    er, ei = fr.T[:a], -fi.T[:a]
    m3 = np.stack([np.concatenate([er, -ei], axis=1), np.concatenate([ei, er], axis=1)]) / nn
    k2 = np.arange(n2)[:, None]
    j2 = np.arange(n2)[None, :]
    ang2 = 2.0 * np.pi * (k2 * j2 % n2) / n2
    f2 = np.stack([np.cos(ang2), -np.sin(ang2)])
    angt = 2.0 * np.pi * (np.arange(n1)[:, None] * j2) / nn
    tw = np.stack([np.cos(angt), -np.sin(angt)], axis=1)
    return tuple(jnp.asarray(t, F32) for t in (m1, m1_real, m3, f2, tw))


def _fft_first_kernel(ur_ref, ui_ref, m_ref, o_ref):
    n1 = o_ref.shape[2]
    r = _dot(m_ref[...], jnp.concatenate([ur_ref[0], ui_ref[0]], axis=0))
    o_ref[0, 0] = r[:n1].astype(o_ref.dtype)
    o_ref[0, 1] = r[n1:].astype(o_ref.dtype)


def _fft_first(u, m1):
    b2, a, cols = u.shape
    p = b2 // 2
    n1 = m1.shape[0] // 2
    tc = min(cols, FFT_COLS)
    return pl.pallas_call(
        _fft_first_kernel,
        grid=(p, cols // tc),
        in_specs=[pl.BlockSpec((1, a, tc), lambda pp, j: (2 * pp, 0, j)),
                  pl.BlockSpec((1, a, tc), lambda pp, j: (2 * pp + 1, 0, j)),
                  pl.BlockSpec(m1.shape, lambda pp, j: (0, 0))],
        out_specs=pl.BlockSpec((1, 2, n1, tc), lambda pp, j: (pp, 0, 0, j)),
        out_shape=jax.ShapeDtypeStruct((p, 2, n1, cols), MXU_DTYPE),
        compiler_params=_cparams(("parallel", "arbitrary")),
        name="fft_first",
    )(u, u, m1)


def _twiddled_dft(f2_ref, tw_ref, u):
    fr, fi = f2_ref[0], f2_ref[1]
    tr, ti = tw_ref[u, 0:1, :], tw_ref[u, 1:2, :]
    return fr * tr - fi * ti, fr * ti + fi * tr


def _real_form(gr, gi):
    return jnp.concatenate([jnp.concatenate([gr, -gi], axis=1), jnp.concatenate([gi, gr], axis=1)], axis=0)


def _spectrum_kernel(a_ref, ss_ref, f2_ref, tw_ref, h_ref):
    scale = lax.rsqrt(ss_ref[...] + EPS)
    for u in range(a_ref.shape[2]):
        xin = jnp.concatenate([a_ref[0, 0, u], a_ref[0, 1, u]], axis=0)
        x = _dot(_real_form(*_twiddled_dft(f2_ref, tw_ref, u)), xin) * scale
        h_ref[0, u] = x[:LANE]
        h_ref[1, u] = x[LANE:]


def _filter_spectrum(af, ss, f2, tw):
    _, _, n1, cols = af.shape
    c = cols // LANE
    kb = math.gcd(n1, FFT_KB)
    a5 = af.reshape(1, 2, n1, LANE, c)
    return pl.pallas_call(
        _spectrum_kernel,
        grid=(n1 // kb,),
        in_specs=[pl.BlockSpec((1, 2, kb, LANE, c), lambda k: (0, 0, k, 0, 0)),
                  pl.BlockSpec((1, c), lambda k: (0, 0)),
                  pl.BlockSpec((2, LANE, LANE), lambda k: (0, 0, 0)),
                  pl.BlockSpec((kb, 2, LANE), lambda k: (k, 0, 0))],
        out_specs=pl.BlockSpec((2, kb, LANE, c), lambda k: (0, k, 0, 0)),
        out_shape=jax.ShapeDtypeStruct((2, n1, LANE, c), F32),
        compiler_params=_cparams(("arbitrary",)),
        name="filter_spectrum",
    )(a5, ss, f2, tw)


def _fft_mid_kernel(a_ref, h_ref, f2_ref, tw_ref, v_ref):
    npair = a_ref.shape[0]
    c = a_ref.shape[4]
    kb = a_ref.shape[2]
    gs = [_twiddled_dft(f2_ref, tw_ref, u) for u in range(kb)]
    xs = []
    for u, (gr, gi) in enumerate(gs):
        xin = jnp.concatenate(
            [jnp.concatenate([a_ref[p, 0, u], a_ref[p, 1, u]], axis=0) for p in range(npair)], axis=1)
        xs.append(_dot(_real_form(gr, gi), xin))
    vs = []
    for u, ((gr, gi), x) in enumerate(zip(gs, xs)):
        xr, xi = x[:LANE], x[LANE:]
        hr = jnp.concatenate([h_ref[0, u]] * npair, axis=1)
        hi = jnp.concatenate([h_ref[1, u]] * npair, axis=1)
        y = jnp.concatenate([xr * hr - xi * hi, xr * hi + xi * hr], axis=0)
        vs.append(_dot(_real_form(gr.T, -gi.T), y))
    for u, v in enumerate(vs):
        for p in range(npair):
            v_ref[p, 0, u] = v[:LANE, p * c:(p + 1) * c].astype(v_ref.dtype)
            v_ref[p, 1, u] = v[LANE:, p * c:(p + 1) * c].astype(v_ref.dtype)


def _fft_mid(a4, hspec, order, f2, tw):
    npair, _, n1, cols = a4.shape
    c = cols // LANE
    kb = math.gcd(n1, FFT_KB)
    a5 = a4.reshape(npair, 2, n1, LANE, c)
    blk = pl.BlockSpec((npair, 2, kb, LANE, c), lambda k: (0, 0, k, 0, 0))
    out = pl.pallas_call(
        _fft_mid_kernel,
        grid=(n1 // kb,),
        in_specs=[blk,
                  pl.BlockSpec((2, kb, LANE, c), lambda k: (0, k, 0, order)),
                  pl.BlockSpec((2, LANE, LANE), lambda k: (0, 0, 0)),
                  pl.BlockSpec((kb, 2, LANE), lambda k: (k, 0, 0))],
        out_specs=blk,
        out_shape=jax.ShapeDtypeStruct(a5.shape, MXU_DTYPE),
        compiler_params=_cparams(("arbitrary",)),
        name="fft_mid",
    )(a5, hspec, f2, tw)
    return out.reshape(npair, 2, n1, cols)


def _fft_last_kernel(v_ref, m_ref, z_ref, gate_ref, bias_ref, o_ref):
    vs = jnp.concatenate([v_ref[0, 0], v_ref[0, 1]], axis=0)
    zf = _dot(m_ref[0], vs)
    o_ref[0] = gate_ref[0] * (zf + z_ref[0] * bias_ref[...])


def _fft_last(v4, m3, z, gate, bias_cols):
    b, a, cols = z.shape
    n1 = v4.shape[2]
    tc = min(cols, FFT_COLS)
    row = pl.BlockSpec((1, a, tc), lambda j, bb: (bb, 0, j))
    return pl.pallas_call(
        _fft_last_kernel,
        grid=(cols // tc, b),
        in_specs=[pl.BlockSpec((1, 2, n1, tc), lambda j, bb: (bb // 2, 0, 0, j)),
                  pl.BlockSpec((1, a, 2 * n1), lambda j, bb: (bb % 2, 0, 0)),
                  row, row,
                  pl.BlockSpec((1, tc), lambda j, bb: (0, j))],
        out_specs=row,
        out_shape=jax.ShapeDtypeStruct((b, a, cols), F32),
        compiler_params=_cparams(("parallel", "arbitrary")),
        name="fft_last",
    )(v4, m3, z, gate, bias_cols)


def _hyena_long(v, x1, x2, filt, ss, bias):
    b, n, c = v.shape
    a = n // LANE
    cols = LANE * c
    m1, m1_real, m3, f2, tw = _dft_tables(n)
    wc = filt.shape[2]
    af = _fft_first(filt.reshape(2, a, LANE * wc), m1_real)
    hspec = _filter_spectrum(af, ss, f2, tw)
    z = v.reshape(b, a, cols)
    gates = (x1.reshape(b, a, cols), x2.reshape(b, a, cols))
    for o in range(HY_ORDER):
        a4 = _fft_first(z, m1)
        v4 = _fft_mid(a4, hspec, o, f2, tw)
        z = _fft_last(v4, m3, z, gates[o], jnp.tile(bias[o], LANE).reshape(1, cols))
    return z.reshape(b, n, c)


def _hyena_ctx_kernel(v_ref, x1_ref, x2_ref, filt_ref, ss_ref, bias_ref, ff_ref, fc_ref, fi_ref, o_ref):
    b, n, c = v_ref.shape
    npair = b // 2
    hs = _dot(ff_ref[...], jnp.concatenate([filt_ref[0], filt_ref[1]], axis=0))
    hs = hs * lax.rsqrt(ss_ref[...] + EPS)
    z = [v_ref[i] for i in range(b)]
    gates = (x1_ref, x2_ref)
    for o in range(HY_ORDER):
        hr = jnp.concatenate([hs[:2 * n, o * c:(o + 1) * c]] * npair, axis=1)
        hi = jnp.concatenate([hs[2 * n:, o * c:(o + 1) * c]] * npair, axis=1)
        xin = jnp.concatenate([jnp.concatenate([z[2 * p] for p in range(npair)], axis=1),
                               jnp.concatenate([z[2 * p + 1] for p in range(npair)], axis=1)], axis=0)
        x = _dot(fc_ref[...], xin)
        xr, xi = x[:2 * n], x[2 * n:]
        y = jnp.concatenate([xr * hr - xi * hi, xr * hi + xi * hr], axis=0)
        zf = _dot(fi_ref[...], y)
        bo = bias_ref[o:o + 1, :]
        for i in range(b):
            p, part = i // 2, i % 2
            conv = zf[part * n:(part + 1) * n, p * c:(p + 1) * c]
            z[i] = gates[o][i] * (conv + z[i] * bo)
    for i in range(b):
        o_ref[i] = z[i]


def _hyena_ctx(v, x1, x2, filt, ss, bias):
    b, n, c = v.shape
    nn = 2 * n
    k = np.arange(nn)[:, None]
    j = np.arange(nn)[None, :]
    ang = 2.0 * np.pi * (k * j % nn) / nn
    fr, fi = np.cos(ang), -np.sin(ang)
    ff = np.concatenate([fr, fi], axis=0)
    fc = np.block([[fr[:, :n], -fi[:, :n]], [fi[:, :n], fr[:, :n]]])
    er, ei = fr[:n], -fi[:n]
    finv = np.block([[er, -ei], [ei, er]]) / nn
    return pl.pallas_call(
        _hyena_ctx_kernel,
        out_shape=jax.ShapeDtypeStruct((b, n, c), F32),
        compiler_params=pltpu.CompilerParams(vmem_limit_bytes=VMEM_LIMIT),
        name="hyena_ctx",
    )(v, x1, x2, filt, ss, bias, jnp.asarray(ff, F32), jnp.asarray(fc, F32), jnp.asarray(finv, F32))


def _window_kernel(n_ctx, sink_ref, q_ref, k_ref, v_ref, o_ref):
    g = pl.program_id(1)
    i = pl.program_id(2)
    rep, tq = q_ref.shape[1], q_ref.shape[3]
    t = k_ref.shape[2]
    ql, win = ATT_QL, CHUNK
    wk = ql + 2 * win
    ids = [(c, h) for c in range(tq // ql) for h in range(rep)]
    q0 = [i * tq + c * ql for c in range(tq // ql)]
    start = [pl.multiple_of(jnp.clip(q - win, 0, t - wk), LANE) for q in q0]

    kx = k_ref[0, 0, 0:n_ctx, :]
    vx = jnp.concatenate([v_ref[0, 0, u] for u in range(n_ctx // LANE)], axis=1)
    kl = [k_ref[0, 0, pl.ds(s, wk), :] for s in start]
    vl = [jnp.concatenate([v_ref[0, 0, s // LANE + u] for u in range(wk // LANE)], axis=1) for s in start]

    qs = [q_ref[0, h, :, c * ql:(c + 1) * ql] for c, h in ids]
    s_loc = [jnp.dot(kl[c], q, preferred_element_type=F32) for (c, h), q in zip(ids, qs)]
    s_ctx = [jnp.dot(kx, q, preferred_element_type=F32) for q in qs]

    diff = lax.broadcasted_iota(jnp.int32, (wk, ql), 0) - lax.broadcasted_iota(jnp.int32, (wk, ql), 1)
    krow = lax.broadcasted_iota(jnp.int32, (wk, 1), 0)
    p_loc, p_ctx, e_snk = [], [], []
    for n, (c, h) in enumerate(ids):
        d = diff + (start[c] - q0[c])
        ok = jnp.logical_and(jnp.abs(d) <= win, krow >= n_ctx - start[c])
        ok = jnp.logical_and(ok, q0[c] >= n_ctx)
        sl = jnp.where(ok, s_loc[n], NEG)
        snk = jnp.where(g == 0, sink_ref[0:1, h:h + 1], sink_ref[0:1, rep + h:rep + h + 1]) * LOG2E
        m = jnp.maximum(jnp.maximum(jnp.max(sl, axis=0, keepdims=True),
                                    jnp.max(s_ctx[n], axis=0, keepdims=True)), snk)
        p_loc.append(jnp.exp2(sl - m).astype(MXU_DTYPE))
        p_ctx.append(jnp.exp2(s_ctx[n] - m).astype(MXU_DTYPE))
        e_snk.append(jnp.exp2(snk - m))

    acc = [jnp.dot(vl[c], p_loc[n], preferred_element_type=F32)
           + jnp.dot(vx, p_ctx[n], preferred_element_type=F32) for n, (c, h) in enumerate(ids)]
    outs = [a[:HEAD_DIM] * (1.0 / (a[HEAD_DIM:HEAD_DIM + 1] + e)) for a, e in zip(acc, e_snk)]
    for c in range(tq // ql):
        o_ref[0, c * ql:(c + 1) * ql, :] = jnp.concatenate(outs[c * rep:(c + 1) * rep], axis=0).T


def _window_attention(qt, k, vt, sink, n_ctx):
    b, _, hd, t = qt.shape
    rep = N_HEADS // N_KV
    tq = ATT_TQ if t % ATT_TQ == 0 else ATT_QL
    assert n_ctx == ATT_QL and t % ATT_QL == 0 and rep * hd == LANE and t >= ATT_QL + 2 * CHUNK
    return pl.pallas_call(
        functools.partial(_window_kernel, n_ctx),
        grid=(b, N_KV, t // tq),
        in_specs=[pl.BlockSpec((SUBLANE, LANE), lambda bb, g, i: (0, 0)),
                  pl.BlockSpec((1, rep, hd, tq), lambda bb, g, i: (bb, g, 0, i)),
                  pl.BlockSpec((1, 1, t, hd), lambda bb, g, i: (bb, g, 0, 0)),
                  pl.BlockSpec((1, 1) + vt.shape[2:], lambda bb, g, i: (bb, g, 0, 0, 0))],
        out_specs=pl.BlockSpec((1, tq, rep * hd), lambda bb, g, i: (bb, i, g)),
        out_shape=jax.ShapeDtypeStruct((b, t, N_HEADS * hd), F32),
        compiler_params=_cparams(("parallel", "parallel", "arbitrary")),
        name="window_attention",
    )(sink, qt, k, vt)


def _dense_kernel(n_ctx, q_ref, k_ref, v_ref, o_ref, m_ref, alpha_ref, acc_ref, p_ref):
    i = pl.program_id(2)
    rep, tq = q_ref.shape[1], q_ref.shape[3]
    ql = ATT_QL
    nc = tq // ql

    def run(chunks, kt, nk):
        per = kt // LANE
        ids = [(c, h) for c in chunks for h in range(rep)]
        for c, h in ids:
            m_ref[c * rep + h] = jnp.full((1, ql), NEG, F32)
            alpha_ref[c * rep + h] = jnp.ones((1, ql), F32)
            acc_ref[c * rep + h] = jnp.zeros(acc_ref.shape[1:], F32)
            p_ref[c * rep + h, 0:kt] = jnp.zeros((kt, ql), p_ref.dtype)

        def scores(j):
            kb = k_ref[0, 0, pl.ds(pl.multiple_of(j * kt, kt), kt), :]
            return [jnp.dot(kb, q_ref[0, h, :, c * ql:(c + 1) * ql], preferred_element_type=F32) for c, h in ids]

        def values(j):
            vb = jnp.concatenate([v_ref[0, 0, j * per + u] for u in range(per)], axis=1)
            pvs = [jnp.dot(vb, p_ref[c * rep + h, 0:kt], preferred_element_type=F32) for c, h in ids]
            for (c, h), pv in zip(ids, pvs):
                n = c * rep + h
                acc_ref[n] = alpha_ref[n] * acc_ref[n] + pv

        def softmax(ss):
            for (c, h), s in zip(ids, ss):
                n = c * rep + h
                m = m_ref[n]
                mn = jnp.maximum(m, jnp.max(s, axis=0, keepdims=True))
                p_ref[n, 0:kt] = jnp.exp2(s - mn).astype(p_ref.dtype)
                alpha_ref[n] = jnp.exp2(m - mn)
                m_ref[n] = mn

        def body(j, carry):
            ss = scores(j)
            values(jnp.maximum(j - 1, 0))
            softmax(ss)
            return carry

        lax.fori_loop(0, nk, body, 0)
        values(nk - 1)
        for c in chunks:
            o = [acc_ref[c * rep + h] for h in range(rep)]
            o = [a[:HEAD_DIM] * (1.0 / a[HEAD_DIM:HEAD_DIM + 1]) for a in o]
            o_ref[0, c * ql:(c + 1) * ql, :] = jnp.concatenate(o, axis=0).T

    kt_all = p_ref.shape[1]
    nk_all = k_ref.shape[2] // kt_all

    @pl.when(i == 0)
    def _():
        run([0], n_ctx, 1)
        if nc > 1:
            run(list(range(1, nc)), kt_all, nk_all)

    @pl.when(i != 0)
    def _():
        run(list(range(nc)), kt_all, nk_all)


def _dense_attention(qt, k, vt, n_ctx):
    b, _, hd, t = qt.shape
    rep = N_HEADS // N_KV
    tq = ATT_TQ if t % ATT_TQ == 0 else ATT_QL
    kt = ATT_KT if t % ATT_KT == 0 else ATT_QL
    assert n_ctx == ATT_QL and t % ATT_QL == 0 and rep * hd == LANE
    return pl.pallas_call(
        functools.partial(_dense_kernel, n_ctx),
        grid=(b, N_KV, t // tq),
        in_specs=[pl.BlockSpec((1, rep, hd, tq), lambda bb, g, i: (bb, g, 0, i)),
                  pl.BlockSpec((1, 1, t, hd), lambda bb, g, i: (bb, g, 0, 0)),
                  pl.BlockSpec((1, 1) + vt.shape[2:], lambda bb, g, i: (bb, g, 0, 0, 0))],
        out_specs=pl.BlockSpec((1, tq, rep * hd), lambda bb, g, i: (bb, i, g)),
        out_shape=jax.ShapeDtypeStruct((b, t, N_HEADS * hd), F32),
        scratch_shapes=[pltpu.VMEM((rep * tq // ATT_QL, 1, ATT_QL), F32),
                        pltpu.VMEM((rep * tq // ATT_QL, 1, ATT_QL), F32),
                        pltpu.VMEM((rep * tq // ATT_QL, vt.shape[3], ATT_QL), F32),
                        pltpu.VMEM((rep * tq // ATT_QL, kt, ATT_QL), MXU_DTYPE)],
        compiler_params=_cparams(("parallel", "parallel", "arbitrary")),
        name="dense_attention",
    )(qt, k, vt)


def _mix_mlp_kernel(first, x_ref, ctx_ref, mod_ref, yf_ref, yb_ref, z_ref, gs_ref, hy_ref, hyc_ref,
                    yw_ref, yd_ref, gpost_ref, gpre_ref, gpost2_ref, wo_ref, w1_ref, w2_ref, o_ref):
    i = pl.program_id(1)
    if first:
        xv = jnp.where(i == 0, ctx_ref[0], x_ref[0])
        yh = jnp.where(i == 0, hyc_ref[0], hy_ref[0])
    else:
        xv = x_ref[0]
        yh = hy_ref[0]
    g1 = mod_ref[0, 0, 2:3, :]
    sh2 = mod_ref[0, 0, 3:4, :]
    sc2 = mod_ref[0, 0, 4:5, :]
    g2 = mod_ref[0, 0, 5:6, :]
    ya = _rms((yf_ref[0] + yb_ref[0]) * _silu(z_ref[0]), gs_ref[...])
    w = W_GROUP
    y = (_dot(ya, wo_ref[0:w, :]) + _dot(yh, wo_ref[w:2 * w, :])
         + _dot(yw_ref[0], wo_ref[2 * w:3 * w, :]) + _dot(yd_ref[0], wo_ref[3 * w:4 * w, :]))
    x1 = xv + g1 * _rms(y, gpost_ref[...])
    hb = (_rms(x1, gpre_ref[...]) * (1.0 + sc2) + sh2).astype(MXU_DTYPE)
    d = x1.shape[1]
    acc = jnp.zeros(x1.shape, F32)
    for c in range(w1_ref.shape[1] // d):
        a = jnp.maximum(jnp.dot(hb, w1_ref[:, c * d:(c + 1) * d], preferred_element_type=F32), 0.0)
        acc = acc + _dot(a * a, w2_ref[c * d:(c + 1) * d, :])
    o_ref[0] = x1 + g2 * _rms(acc, gpost2_ref[...])


def _mix_mlp(first, x, ctx, mod6, yf, yb, z, g_ssd, hy, hyc, yw, yd, g_post, g_pre2, g_post2, wo, w1, w2):
    b, t, _ = yf.shape
    d = wo.shape[1]
    tm = ROW_TILE
    off = 0 if first else 1
    nrow = t // tm - off
    if first:
        x_spec = pl.BlockSpec((1, tm, d), lambda bb, i: (bb, jnp.maximum(i - 1, 0), 0))
        hy_spec = pl.BlockSpec((1, tm, W_GROUP), lambda bb, i: (bb, jnp.maximum(i - 1, 0), 0))
        mod_spec = pl.BlockSpec((1, 1, 6, d), lambda bb, i: (bb, jnp.minimum(i, 1), 0, 0))
    else:
        x_spec = pl.BlockSpec((1, tm, d), lambda bb, i: (bb, i + 1, 0))
        hy_spec = pl.BlockSpec((1, tm, W_GROUP), lambda bb, i: (bb, i, 0))
        mod_spec = pl.BlockSpec((1, 1, 6, d), lambda bb, i: (bb, 1, 0, 0))
    first_blk = lambda w: pl.BlockSpec((1, tm, w), lambda bb, i: (bb, 0, 0))
    row = lambda w: pl.BlockSpec((1, tm, w), lambda bb, i: (bb, i + off, 0))
    vec = lambda w: pl.BlockSpec((1, w), lambda bb, i: (0, 0))
    full = lambda a: pl.BlockSpec(a.shape, lambda bb, i: (0, 0))
    return pl.pallas_call(
        functools.partial(_mix_mlp_kernel, first),
        grid=(b, nrow),
        in_specs=[x_spec, first_blk(d), mod_spec, row(W_GROUP), row(W_GROUP), row(W_GROUP), vec(W_GROUP),
                  hy_spec, first_blk(W_GROUP), row(W_GROUP), row(W_GROUP), vec(d), vec(d), vec(d),
                  full(wo), full(w1), full(w2)],
        out_specs=pl.BlockSpec((1, tm, d), lambda bb, i: (bb, i, 0)),
        out_shape=jax.ShapeDtypeStruct((b, nrow * tm, d), F32),
        compiler_params=_cparams(("parallel", "arbitrary")),
        name="mix_mlp",
    )(x, ctx, mod6, yf, yb, z, g_ssd, hy, hyc, yw, yd, g_post, g_pre2, g_post2, wo, w1, w2)


def _rope_tables(n, n_ctx):
    rows = n // GRID_W
    row = jnp.repeat(jnp.arange(rows, dtype=F32), GRID_W)
    col = jnp.tile(jnp.arange(GRID_W, dtype=F32), rows)
    n_freq = HEAD_DIM // 4
    inv = ROPE_THETA ** (-jnp.arange(n_freq, dtype=F32) / n_freq)
    ang = jnp.concatenate([row[:, None] * inv, col[:, None] * inv], axis=-1)
    cos, sin = jnp.cos(ang), jnp.sin(ang)
    cs = jnp.concatenate([cos, cos], axis=1)
    sn = jnp.concatenate([-sin, sin], axis=1)
    cs = jnp.concatenate([jnp.ones((n_ctx, HEAD_DIM), F32), cs], axis=0)
    sn = jnp.concatenate([jnp.zeros((n_ctx, HEAD_DIM), F32), sn], axis=0)
    return cs, sn


def _pad_rows(a, rows):
    return jnp.pad(a, ((0, rows - a.shape[0]), (0, 0)))


def kernel(x, c, ctx, c_ctx, w_mod, b_mod, norm_mix_pre, norm_mix_post, norm_mlp_pre, norm_mlp_post, w_in, w_out, ssd_conv_w, ssd_conv_b, ssd_a_log, ssd_dt_bias, ssd_d, ssd_norm, hy_conv_w, hy_conv_b, hy_w1, hy_b1, hy_freq1, hy_w2, hy_b2, hy_freq2, hy_w3, hy_b3, hy_bias, attn_sink, q_norm, k_norm, mlp_w1, mlp_w2):
    b, n, d = x.shape
    n_ctx = ctx.shape[1]
    depth = w_mod.shape[0]
    assert n_ctx == ROW_TILE and n % ROW_TILE == 0 and b % 2 == 0 and b + 1 <= SUBLANE

    cc = _pad_rows(jnp.concatenate([c, c_ctx[None, :]], axis=0), SUBLANE)
    mod = _modulation(cc, w_mod, b_mod)
    cs, sn = _rope_tables(n, n_ctx)

    off_b = W_GROUP + SSD_XBC + 2 * N_HEADS
    xall = None
    for l in range(depth):
        first = l == 0
        need_ctx = l < depth - 1
        mod_lat = mod[l, :b].reshape(b, 1, 6, d)
        mod_ctx = jnp.broadcast_to(mod[l, b].reshape(1, 1, 6, d), (b, 1, 6, d))
        mod6 = jnp.concatenate([mod_ctx, mod_lat], axis=1)

        wl = w_in[l]
        off_c = off_b + 3 * W_GROUP
        off_d = off_c + (N_HEADS + 2 * N_KV) * HEAD_DIM
        w_row = jnp.concatenate(
            [wl[:, :W_GROUP + SSD_XBC], wl[:, off_b:off_c], wl[:, W_GROUP + SSD_XBC:off_b],
             jnp.zeros((d, D_ROW_PAD - off_c), F32)], axis=1).astype(MXU_DTYPE)
        w_att_t = jnp.concatenate([wl[:, off_d:], wl[:, off_c:off_d]], axis=1).T.astype(MXU_DTYPE)
        xin, cin = (x, ctx) if first else (xall, None)
        (z, u, dt, v, x1, x2, vc, x1c, x2c, qwt, kw, vwt, qdt, kd, vdt) = _in_projection(
            xin, cin, mod6, norm_mix_pre[l].reshape(1, d), w_row, w_att_t, cs, sn, q_norm[l], k_norm[l],
            ssd_conv_w[l], ssd_conv_b[l].reshape(1, -1), hy_conv_w[l], hy_conv_b[l].reshape(1, -1))

        par = jnp.zeros((SUBLANE, LANE), F32)
        par = par.at[0, :2 * N_HEADS].set(ssd_a_log[l].reshape(-1))
        par = par.at[1, :2 * N_HEADS].set(ssd_dt_bias[l].reshape(-1))
        par = par.at[2, :N_HEADS].set(ssd_d[l])
        yf, yb = _ssd(u, dt, par, n_ctx)

        filt_args = (hy_w1[l], hy_b1[l], hy_freq1[l], hy_w2[l], hy_b2[l], hy_freq2[l], hy_w3[l], hy_b3[l])
        yhy = _hyena_long(v, x1, x2, *_hyena_filters(n, *filt_args), hy_bias[l])
        if need_ctx:
            yhy_ctx = _hyena_ctx(vc, x1c, x2c, *_hyena_filters(n_ctx, *filt_args), hy_bias[l])
        else:
            yhy_ctx = yhy

        sink = jnp.zeros((SUBLANE, LANE), F32).at[0, :N_HEADS].set(attn_sink[l])
        yw = _window_attention(qwt, kw, vwt, sink, n_ctx)
        yd = _dense_attention(qdt, kd, vdt, n_ctx)

        xres, cres = (x, ctx) if first else (xall, xall)
        xall = _mix_mlp(first, xres, cres, mod6, yf, yb, z, ssd_norm[l].reshape(1, -1), yhy, yhy_ctx, yw, yd,
                        norm_mix_post[l].reshape(1, d), norm_mlp_pre[l].reshape(1, d),
                        norm_mlp_post[l].reshape(1, d), w_out[l].astype(MXU_DTYPE),
                        mlp_w1[l].astype(MXU_DTYPE), mlp_w2[l].astype(MXU_DTYPE))
    return xall
```

```python
import functools
import math

import numpy as np
import jax
import jax.numpy as jnp
from jax import lax
from jax.experimental import pallas as pl
from jax.experimental.pallas import tpu as pltpu

F32 = jnp.float32
MXU_DTYPE = jnp.bfloat16

EPS = 1e-6
HEAD_DIM = 64
GRID_W = 64
ROPE_THETA = 10000.0
N_HEADS = 4
N_KV = 2
W_GROUP = N_HEADS * HEAD_DIM
SSD_STATE = 64
SSD_XBC = W_GROUP + 2 * N_KV * SSD_STATE
HY_ORDER = 2
HY_BANDS = 16
HY_EMB = 2 * HY_BANDS + 1
HY_FILT = 64
HY_MAX_DECAY = math.log(1e-2) / 0.3
HY_MIN_DECAY = math.log(1e-2) / 1.5
CHUNK = 128
LANE = 128
SUBLANE = 8
ROW_TILE = 256
ATT_TQ = 768
FFT_JS = 16
FFT_KB = 4
MLP_SPLIT = 2
VMEM_LIMIT = 56 * 1024 * 1024
NEG = -1e30

C_Z = (0, 256)
C_XBC = (256, 768)
C_HY = (768, 1536)
C_DT = (1536, 1664)
D_ROW_PAD = 1664
A_QD, A_KD, A_VD = 0, 256, 384
A_QW, A_KW, A_VW = 512, 768, 896
V_ROWS = HEAD_DIM + 16
ATT_QL = 256
ATT_KT = 768
LOG2E = math.log2(math.e)


def _cparams(sem):
    return pltpu.CompilerParams(dimension_semantics=sem, vmem_limit_bytes=VMEM_LIMIT)


def _rms(x, g):
    return x * lax.rsqrt(jnp.mean(x * x, axis=-1, keepdims=True) + EPS) * g


def _silu(x):
    return x * (1.0 / (1.0 + jnp.exp(-x)))


def _softplus(x):
    return jnp.maximum(x, 0.0) + jnp.log(1.0 + jnp.exp(-jnp.abs(x)))


def _dot(a, b):
    return jnp.dot(a.astype(MXU_DTYPE), b.astype(MXU_DTYPE), preferred_element_type=F32)


def _dot_nt(a, b):
    return lax.dot_general(a.astype(MXU_DTYPE), b.astype(MXU_DTYPE), (((1,), (1,)), ((), ())),
                           preferred_element_type=F32)


def _dot_f32(a, b):
    return jnp.dot(a, b, preferred_element_type=F32, precision=lax.Precision.HIGHEST)


def _mod_kernel(c_ref, w_ref, b_ref, o_ref):
    o_ref[0] = _dot_f32(_silu(c_ref[...]), w_ref[0]) + b_ref[0]


def _modulation(cc, w_mod, b_mod):
    depth, d, d6 = w_mod.shape
    tn = d6 // 4
    return pl.pallas_call(
        _mod_kernel,
        grid=(depth, d6 // tn),
        in_specs=[pl.BlockSpec((SUBLANE, d), lambda l, j: (0, 0)),
                  pl.BlockSpec((1, d, tn), lambda l, j: (l, 0, j)),
                  pl.BlockSpec((1, 1, tn), lambda l, j: (l, 0, j))],
        out_specs=pl.BlockSpec((1, SUBLANE, tn), lambda l, j: (l, 0, j)),
        out_shape=jax.ShapeDtypeStruct((depth, SUBLANE, d6), F32),
        compiler_params=_cparams(("arbitrary", "arbitrary")),
        name="modulation",
    )(cc, w_mod, b_mod.reshape(depth, 1, d6))


def _inproj_kernel(x_ref, xp_ref, xn_ref, ctx_ref, mod_ref, g_ref, w_ref, wt_ref, cst_ref, snt_ref, qnt_ref, knt_ref,
                   scw_ref, scb_ref, hcw_ref, hcb_ref,
                   z_ref, u_ref, dt_ref, v_ref, x1_ref, x2_ref, vc_ref, x1c_ref, x2c_ref,
                   qwt_ref, kw_ref, vwt_ref, qdt_ref, kd_ref, vdt_ref, ext_ref):
    i = pl.program_id(1)
    nrow = pl.num_programs(1)
    xv = jnp.where(i == 0, ctx_ref[0], x_ref[0])
    tm = xv.shape[0]
    sh = mod_ref[0, 0, 0:1, :]
    sc = mod_ref[0, 0, 1:2, :]

    def modulated(rows):
        return (_rms(rows, g_ref[...]) * (1.0 + sc) + sh).astype(MXU_DTYPE)

    hb = modulated(xv)

    def proj(cols):
        return jnp.dot(hb, w_ref[:, cols[0]:cols[1]], preferred_element_type=F32)

    z_ref[0] = proj(C_Z)
    dt_ref[0] = proj(C_DT)

    conv_cols = (C_XBC[0], C_HY[1])
    halo = jnp.dot(modulated(jnp.concatenate([xp_ref[0], xn_ref[0]], axis=0)),
                   w_ref[:, conv_cols[0]:conv_cols[1]], preferred_element_type=F32)
    prev_ok = i > 1
    next_ok = jnp.logical_and(i >= 1, i < nrow - 1)
    ext_ref[0:SUBLANE] = jnp.where(prev_ok, halo[0:SUBLANE], 0.0)
    ext_ref[SUBLANE:SUBLANE + tm] = proj(conv_cols)
    ext_ref[SUBLANE + tm:] = jnp.where(next_ok, halo[SUBLANE:], 0.0)

    def conv(w_r, b_r, lo, hi):
        taps = w_r.shape[0]
        acc = b_r[...]
        for k in range(taps):
            off = SUBLANE - taps // 2 + k
            acc = acc + w_r[k:k + 1, :] * ext_ref[off:off + tm, lo:hi]
        return acc

    nx = C_XBC[1] - C_XBC[0]
    u_ref[0] = _silu(conv(scw_ref, scb_ref, 0, nx))
    hyc = conv(hcw_ref, hcb_ref, nx, nx + C_HY[1] - C_HY[0])
    parts = [hyc[:, j * W_GROUP:(j + 1) * W_GROUP] for j in range(3)]
    for ref, val in zip((v_ref, x1_ref, x2_ref), parts):
        ref[0] = val

    @pl.when(i == 0)
    def _():
        for ref, val in zip((vc_ref, x1c_ref, x2c_ref), parts):
            ref[0] = val

    pt = _dot_nt(wt_ref[...], hb)
    cst = cst_ref[...]
    snt = snt_ref[...]
    tile = lambda r: jnp.concatenate([r[...]] * (tm // LANE), axis=1)
    qnt, knt = tile(qnt_ref), tile(knt_ref)
    half = HEAD_DIM // 2
    qscale = HEAD_DIM ** -0.5 * LOG2E

    def head(row0):
        return pt[row0:row0 + HEAD_DIM]

    def norm(t, gain):
        return t * lax.rsqrt(jnp.mean(t * t, axis=0, keepdims=True) + EPS) * gain

    def rope(t):
        return t * cst + jnp.concatenate([t[half:], t[:half]], axis=0) * snt

    def put_values(v_ref, row0):
        ones = jnp.ones((v_ref.shape[3] - HEAD_DIM, LANE), v_ref.dtype)
        for g in range(N_KV):
            vt = head(row0 + g * HEAD_DIM)
            for j in range(tm // LANE):
                v_ref[0, g, j, 0:HEAD_DIM, :] = vt[:, j * LANE:(j + 1) * LANE].astype(v_ref.dtype)
                v_ref[0, g, j, HEAD_DIM:, :] = ones

    def put_keys(k_ref, kt_pair):
        k_rows = jnp.concatenate(kt_pair, axis=0).T
        for g in range(N_KV):
            k_ref[0, g] = k_rows[:, g * HEAD_DIM:(g + 1) * HEAD_DIM].astype(k_ref.dtype)

    for h in range(N_HEADS):
        qdt_ref[0, h] = (rope(norm(head(A_QD + h * HEAD_DIM), qnt)) * qscale).astype(qdt_ref.dtype)
        qwt_ref[0, h] = (rope(head(A_QW + h * HEAD_DIM)) * qscale).astype(qwt_ref.dtype)
    put_keys(kd_ref, [rope(norm(head(A_KD + g * HEAD_DIM), knt)) for g in range(N_KV)])
    put_keys(kw_ref, [rope(head(A_KW + g * HEAD_DIM)) for g in range(N_KV)])
    put_values(vdt_ref, A_VD)
    put_values(vwt_ref, A_VW)


def _in_projection(x, ctx, mod6, g_pre, w_row, w_att_t, cst, snt, qn, kn, ssd_cw, ssd_cb, hy_cw, hy_cb):
    b, _, d = x.shape
    tm = ROW_TILE
    hp = tm // SUBLANE
    if ctx is None:
        t = x.shape[1]
        first_lat = 1
        x_spec = pl.BlockSpec((1, tm, d), lambda bb, i: (bb, i, 0))
        ctx_arr, ctx_spec = x, pl.BlockSpec((1, tm, d), lambda bb, i: (bb, 0, 0))
    else:
        assert ctx.shape[1] == tm
        t = x.shape[1] + tm
        first_lat = 0
        x_spec = pl.BlockSpec((1, tm, d), lambda bb, i: (bb, jnp.maximum(i - 1, 0), 0))
        ctx_arr, ctx_spec = ctx, pl.BlockSpec((1, tm, d), lambda bb, i: (bb, 0, 0))
    nrow = t // tm
    last8 = x.shape[1] // SUBLANE - 1
    xp_spec = pl.BlockSpec((1, SUBLANE, d), lambda bb, i: (bb, jnp.clip((i - 1 + first_lat) * hp - 1, 0, last8), 0))
    xn_spec = pl.BlockSpec((1, SUBLANE, d), lambda bb, i: (bb, jnp.clip((i + first_lat) * hp, 0, last8), 0))
    per = tm // LANE
    row = lambda w: pl.BlockSpec((1, tm, w), lambda bb, i: (bb, i, 0))
    lat = pl.BlockSpec((1, tm, W_GROUP), lambda bb, i: (bb, jnp.maximum(i - 1, 0), 0))
    cblk = pl.BlockSpec((1, tm, W_GROUP), lambda bb, i: (bb, 0, 0))
    full = lambda a: pl.BlockSpec(a.shape, lambda bb, i: (0, 0))
    f32 = lambda w: jax.ShapeDtypeStruct((b, t, w), F32)
    lat_shape = jax.ShapeDtypeStruct((b, t - tm, W_GROUP), F32)
    ctx_shape = jax.ShapeDtypeStruct((b, tm, W_GROUP), F32)
    q_spec = pl.BlockSpec((1, N_HEADS, HEAD_DIM, tm), lambda bb, i: (bb, 0, 0, i))
    k_spec = pl.BlockSpec((1, N_KV, tm, HEAD_DIM), lambda bb, i: (bb, 0, i, 0))
    v_spec = pl.BlockSpec((1, N_KV, per, V_ROWS, LANE), lambda bb, i: (bb, 0, i, 0, 0))
    q_shape = jax.ShapeDtypeStruct((b, N_HEADS, HEAD_DIM, t), MXU_DTYPE)
    k_shape = jax.ShapeDtypeStruct((b, N_KV, t, HEAD_DIM), MXU_DTYPE)
    v_shape = jax.ShapeDtypeStruct((b, N_KV, t // LANE, V_ROWS, LANE), MXU_DTYPE)
    gain = lambda v: jnp.broadcast_to(v.reshape(HEAD_DIM, 1), (HEAD_DIM, LANE))
    qnt, knt = gain(qn), gain(kn)
    return pl.pallas_call(
        _inproj_kernel,
        grid=(b, nrow),
        in_specs=[x_spec, xp_spec, xn_spec, ctx_spec,
                  pl.BlockSpec((1, 1, 6, d), lambda bb, i: (bb, jnp.minimum(i, 1), 0, 0)),
                  pl.BlockSpec((1, d), lambda bb, i: (0, 0)),
                  full(w_row), full(w_att_t),
                  pl.BlockSpec((HEAD_DIM, tm), lambda bb, i: (0, i)),
                  pl.BlockSpec((HEAD_DIM, tm), lambda bb, i: (0, i)),
                  full(qnt), full(knt), full(ssd_cw), full(ssd_cb), full(hy_cw), full(hy_cb)],
        out_specs=[row(W_GROUP), row(SSD_XBC), row(LANE), lat, lat, lat, cblk, cblk, cblk,
                   q_spec, k_spec, v_spec, q_spec, k_spec, v_spec],
        out_shape=[f32(W_GROUP), f32(SSD_XBC), f32(LANE), lat_shape, lat_shape, lat_shape,
                   ctx_shape, ctx_shape, ctx_shape,
                   q_shape, k_shape, v_shape, q_shape, k_shape, v_shape],
        scratch_shapes=[pltpu.VMEM((tm + 2 * SUBLANE, C_HY[1] - C_XBC[0]), F32)],
        compiler_params=_cparams(("parallel", "arbitrary")),
        name="in_projection",
    )(x, x, x, ctx_arr, mod6, g_pre, w_row, w_att_t, cst, snt, qnt, knt, ssd_cw, ssd_cb, hy_cw, hy_cb)


def _ssd_kernel(nc, nt, par_ref, uf_ref, dtf_ref, ub_ref, dtb_ref, yf_ref, yb_ref, st_ref):
    j = pl.program_id(1)
    q = CHUNK
    rep = N_HEADS // N_KV

    @pl.when(j == 0)
    def _():
        st_ref[...] = jnp.zeros(st_ref.shape, F32)

    a_all = -jnp.exp(par_ref[0:1, :])
    ri = lax.broadcasted_iota(jnp.int32, (q, q), 0)
    ci = lax.broadcasted_iota(jnp.int32, (q, q), 1)
    dirs = []
    for d, (u_ref, dt_ref) in enumerate(((uf_ref, dtf_ref), (ub_ref, dtb_ref))):
        u = u_ref[0]
        dtv = _softplus(dt_ref[0] + par_ref[1:2, :])
        mask = (ri >= ci) if d == 0 else (ri <= ci)
        cum = _dot_f32(mask.astype(F32), dtv * a_all)
        dirs.append(dict(
            d=d, mask=mask, cum=cum, cum_t=cum.T, dt_t=dtv.T, end=q - 1 if d == 0 else 0,
            xs=u[:, :W_GROUP], bm_t=u[:, W_GROUP:W_GROUP + N_KV * SSD_STATE].T,
            cm=u[:, W_GROUP + N_KV * SSD_STATE:]))

    def head_terms(v, h):
        hl = N_HEADS * v["d"] + h
        g = h // rep
        sl = slice(g * SSD_STATE, (g + 1) * SSD_STATE)
        return dict(col=v["cum"][:, hl:hl + 1], row=v["cum_t"][hl:hl + 1, :], dt_row=v["dt_t"][hl:hl + 1, :],
                    last=v["cum"][v["end"]:v["end"] + 1, hl:hl + 1], cg=v["cm"][:, sl], bg_t=v["bm_t"][sl, :],
                    xh=v["xs"][:, h * HEAD_DIM:(h + 1) * HEAD_DIM])

    terms = [[head_terms(v, h) for h in range(N_HEADS)] for v in dirs]
    scores = [[_dot(ts[g * rep]["cg"], ts[g * rep]["bg_t"]) for g in range(N_KV)] for ts in terms]
    carried = [[_dot(t["cg"], st_ref[v["d"], h]) for h, t in enumerate(ts)] for v, ts in zip(dirs, terms)]
    states = [[_dot(t["bg_t"] * (jnp.exp(t["last"] - t["row"]) * t["dt_row"]), t["xh"]) for t in ts]
              for ts in terms]
    diag = [[_dot(scores[v["d"]][h // rep] * jnp.exp(jnp.where(v["mask"], t["col"] - t["row"], NEG)) * t["dt_row"],
                  t["xh"]) for h, t in enumerate(ts)] for v, ts in zip(dirs, terms)]
    for v, ts, y_ref in zip(dirs, terms, (yf_ref, yb_ref)):
        d = v["d"]
        outs = []
        for h, t in enumerate(ts):
            y = diag[d][h] + carried[d][h] * jnp.exp(t["col"])
            st_ref[d, h] = jnp.exp(t["last"]) * st_ref[d, h] + states[d][h]
            if d == 0:
                y = y + par_ref[2:3, h:h + 1] * t["xh"]
            outs.append(y)
        y_ref[0] = jnp.concatenate(outs, axis=1)


def _ssd(u, dt, par, n_ctx):
    b, t, w = u.shape
    q = CHUNK
    nc, nt = n_ctx // q, t // q
    fwd = lambda j: j
    bwd = lambda j: jnp.where(j < nc, nc - 1 - j, nt + nc - 1 - j)

    def specs(cmap):
        return [pl.BlockSpec((1, q, w), lambda bb, j: (bb, cmap(j), 0)),
                pl.BlockSpec((1, q, LANE), lambda bb, j: (bb, cmap(j), 0))]

    return pl.pallas_call(
        functools.partial(_ssd_kernel, nc, nt),
        grid=(b, nt),
        in_specs=[pl.BlockSpec((SUBLANE, LANE), lambda bb, j: (0, 0))] + specs(fwd) + specs(bwd),
        out_specs=[pl.BlockSpec((1, q, W_GROUP), lambda bb, j: (bb, fwd(j), 0)),
                   pl.BlockSpec((1, q, W_GROUP), lambda bb, j: (bb, bwd(j), 0))],
        out_shape=[jax.ShapeDtypeStruct((b, t, W_GROUP), F32)] * 2,
        scratch_shapes=[pltpu.VMEM((2, N_HEADS, SSD_STATE, HEAD_DIM), F32)],
        compiler_params=_cparams(("parallel", "arbitrary")),
        name="ssd_scan",
    )(par, u, dt, u, dt)


def _filter_kernel(zf_ref, zb_ref, w1_ref, b1_ref, f1_ref, w2_ref, b2_ref, f2_ref,
                   w3f_ref, w3b_ref, b3f_ref, b3b_ref, dl_ref, o_ref, ss_ref):
    i = pl.program_id(0)
    tr = zf_ref.shape[0]

    def half(z_ref, w3_ref, b3_ref):
        z = z_ref[...]
        h = jnp.sin(f1_ref[...] * (_dot_f32(z, w1_ref[...]) + b1_ref[...]))
        h = jnp.sin(f2_ref[...] * (_dot_f32(h, w2_ref[...]) + b2_ref[...]))
        k = _dot_f32(h, w3_ref[...]) + b3_ref[...]
        return k * jnp.exp(-z[:, 0:1] * dl_ref[...])

    kf = half(zf_ref, w3f_ref, b3f_ref)
    kb = half(zb_ref, w3b_ref, b3b_ref)

    @pl.when(i == 0)
    def _():
        ss_ref[...] = jnp.zeros(ss_ref.shape, F32)

    ss_ref[...] += jnp.sum(kf * kf + kb * kb, axis=0, keepdims=True)
    rows = i * tr + lax.broadcasted_iota(jnp.int32, (tr, 1), 0)
    o_ref[0] = kf
    o_ref[1] = jnp.where(rows == 0, 0.0, kb)


def _hyena_filters(n, w1, b1, f1, w2, b2, f2, w3, b3):
    pos = np.arange(n, dtype=np.float64)
    t = np.linspace(0.0, 1.0, n)
    f = np.linspace(1e-4, HY_BANDS - 1, HY_BANDS)
    ang = 2.0 * math.pi * pos[:, None] * f[None, :] / n
    emb = np.concatenate([t[:, None], np.cos(ang), -np.sin(ang)], axis=-1)
    emb = np.pad(emb, ((0, 0), (0, LANE - HY_EMB)))
    emb_b = np.roll(np.flip(emb, axis=0), 1, axis=0)
    emb, emb_b = jnp.asarray(emb, F32), jnp.asarray(emb_b, F32)
    w1p = jnp.pad(w1, ((0, LANE - HY_EMB), (0, 0)))
    w3r = w3.reshape(HY_FILT, HY_ORDER, 2, W_GROUP)
    b3r = b3.reshape(HY_ORDER, 2, W_GROUP)
    wc = HY_ORDER * W_GROUP
    w3f, w3b = w3r[:, :, 0].reshape(HY_FILT, wc), w3r[:, :, 1].reshape(HY_FILT, wc)
    b3f, b3b = b3r[:, 0].reshape(1, wc), b3r[:, 1].reshape(1, wc)
    deltas = np.abs(np.linspace(HY_MIN_DECAY, HY_MAX_DECAY, W_GROUP))
    deltas = jnp.asarray(np.tile(deltas, HY_ORDER).reshape(1, wc), F32)
    tr = math.gcd(n, 512)
    const = lambda s: pl.BlockSpec(s, lambda i: (0, 0))
    rows = pl.BlockSpec((tr, LANE), lambda i: (i, 0))
    return pl.pallas_call(
        _filter_kernel,
        grid=(n // tr,),
        in_specs=[rows, rows, const((LANE, HY_FILT)), const((1, HY_FILT)), const((1, HY_FILT)),
                  const((HY_FILT, HY_FILT)), const((1, HY_FILT)), const((1, HY_FILT)),
                  const((HY_FILT, wc)), const((HY_FILT, wc)), const((1, wc)), const((1, wc)), const((1, wc))],
        out_specs=[pl.BlockSpec((2, tr, wc), lambda i: (0, i, 0)), const((1, wc))],
        out_shape=[jax.ShapeDtypeStruct((2, n, wc), F32), jax.ShapeDtypeStruct((1, wc), F32)],
        compiler_params=_cparams(("arbitrary",)),
        name="hyena_filter",
    )(emb, emb_b, w1p, b1.reshape(1, -1), f1.reshape(1, -1), w2, b2.reshape(1, -1), f2.reshape(1, -1),
      w3f, w3b, b3f, b3b, deltas)


def _dft_tables(n):
    nn = 2 * n
    n1, n2 = nn // LANE, LANE
    a = n1 // 2
    k1 = np.arange(n1)[:, None]
    j1 = np.arange(n1)[None, :]
    ang1 = 2.0 * np.pi * (k1 * j1 % n1) / n1
    fr, fi = np.cos(ang1), -np.sin(ang1)
    m1 = np.block([[fr[:, :a], -fi[:, :a]], [fi[:, :a], fr[:, :a]]])
    m1_real = np.concatenate([fr, fi], axis=0)
    er, ei = fr.T[:a], -fi.T[:a]
    m3 = np.stack([np.concatenate([er, -ei], axis=1), np.concatenate([ei, er], axis=1)]) / nn
    k2 = np.arange(n2)[:, None]
    j2 = np.arange(n2)[None, :]
    ang2 = 2.0 * np.pi * (k2 * j2 % n2) / n2
    f2 = np.stack([np.cos(ang2), -np.sin(ang2)])
    angt = 2.0 * np.pi * (np.arange(n1)[:, None] * j2) / nn
    tw = np.stack([np.cos(angt), -np.sin(angt)], axis=1)
    return tuple(jnp.asarray(t, F32) for t in (m1, m1_real, m3, f2, tw))


def _fft_first_kernel(ur_ref, ui_ref, m_ref, o_ref, acc_ref):
    n1 = o_ref.shape[2]
    for s in range(ur_ref.shape[2]):
        r = _dot(m_ref[...], jnp.concatenate([ur_ref[0, :, s, :], ui_ref[0, :, s, :]], axis=0))
        acc_ref[0, :, s, :] = r[:n1]
        acc_ref[1, :, s, :] = r[n1:]
    o_ref[0] = acc_ref[...].astype(o_ref.dtype)


def _fft_first(u, m1):
    b2, a, _, c = u.shape
    p = b2 // 2
    n1 = m1.shape[0] // 2
    js = FFT_JS
    return pl.pallas_call(
        _fft_first_kernel,
        grid=(p, LANE // js),
        in_specs=[pl.BlockSpec((1, a, js, c), lambda pp, j: (2 * pp, 0, j, 0)),
                  pl.BlockSpec((1, a, js, c), lambda pp, j: (2 * pp + 1, 0, j, 0)),
                  pl.BlockSpec(m1.shape, lambda pp, j: (0, 0))],
        out_specs=pl.BlockSpec((1, 2, n1, js, c), lambda pp, j: (pp, 0, 0, j, 0)),
        out_shape=jax.ShapeDtypeStruct((p, 2, n1, LANE, c), MXU_DTYPE),
        scratch_shapes=[pltpu.VMEM((2, n1, js, c), F32)],
        compiler_params=_cparams(("parallel", "arbitrary")),
        name="fft_first",
    )(u, u, m1)


def _twiddled_dft(f2_ref, tw_ref, u):
    fr, fi = f2_ref[0], f2_ref[1]
    tr, ti = tw_ref[u, 0:1, :], tw_ref[u, 1:2, :]
    return fr * tr - fi * ti, fr * ti + fi * tr


def _real_form(gr, gi):
    return jnp.concatenate([jnp.concatenate([gr, -gi], axis=1), jnp.concatenate([gi, gr], axis=1)], axis=0)


def _spectrum_kernel(a_ref, ss_ref, f2_ref, tw_ref, h_ref):
    scale = lax.rsqrt(ss_ref[...] + EPS)
    for u in range(a_ref.shape[2]):
        xin = jnp.concatenate([a_ref[0, 0, u], a_ref[0, 1, u]], axis=0)
        x = _dot(_real_form(*_twiddled_dft(f2_ref, tw_ref, u)), xin) * scale
        h_ref[0, u] = x[:LANE]
        h_ref[1, u] = x[LANE:]


def _filter_spectrum(a5, ss, f2, tw):
    _, _, n1, _, c = a5.shape
    kb = math.gcd(n1, FFT_KB)
    return pl.pallas_call(
        _spectrum_kernel,
        grid=(n1 // kb,),
        in_specs=[pl.BlockSpec((1, 2, kb, LANE, c), lambda k: (0, 0, k, 0, 0)),
                  pl.BlockSpec((1, c), lambda k: (0, 0)),
                  pl.BlockSpec((2, LANE, LANE), lambda k: (0, 0, 0)),
                  pl.BlockSpec((kb, 2, LANE), lambda k: (k, 0, 0))],
        out_specs=pl.BlockSpec((2, kb, LANE, c), lambda k: (0, k, 0, 0)),
        out_shape=jax.ShapeDtypeStruct((2, n1, LANE, c), F32),
        compiler_params=_cparams(("arbitrary",)),
        name="filter_spectrum",
    )(a5, ss, f2, tw)


def _fft_mid_kernel(a_ref, h_ref, f2_ref, tw_ref, v_ref):
    npair = a_ref.shape[0]
    c = a_ref.shape[4]
    kb = a_ref.shape[2]
    gs = [_twiddled_dft(f2_ref, tw_ref, u) for u in range(kb)]
    xs = []
    for u, (gr, gi) in enumerate(gs):
        xin = jnp.concatenate(
            [jnp.concatenate([a_ref[p, 0, u], a_ref[p, 1, u]], axis=0) for p in range(npair)], axis=1)
        xs.append(_dot(_real_form(gr, gi), xin))
    vs = []
    for u, ((gr, gi), x) in enumerate(zip(gs, xs)):
        xr, xi = x[:LANE], x[LANE:]
        hr = jnp.concatenate([h_ref[0, u]] * npair, axis=1)
        hi = jnp.concatenate([h_ref[1, u]] * npair, axis=1)
        y = jnp.concatenate([xr * hr - xi * hi, xr * hi + xi * hr], axis=0)
        vs.append(_dot(_real_form(gr.T, -gi.T), y))
    for u, v in enumerate(vs):
        for p in range(npair):
            v_ref[p, 0, u] = v[:LANE, p * c:(p + 1) * c].astype(v_ref.dtype)
            v_ref[p, 1, u] = v[LANE:, p * c:(p + 1) * c].astype(v_ref.dtype)


def _fft_mid(a5, hspec, order, f2, tw):
    npair, _, n1, _, c = a5.shape
    kb = math.gcd(n1, FFT_KB)
    blk = pl.BlockSpec((npair, 2, kb, LANE, c), lambda k: (0, 0, k, 0, 0))
    return pl.pallas_call(
        _fft_mid_kernel,
        grid=(n1 // kb,),
        in_specs=[blk,
                  pl.BlockSpec((2, kb, LANE, c), lambda k: (0, k, 0, order)),
                  pl.BlockSpec((2, LANE, LANE), lambda k: (0, 0, 0)),
                  pl.BlockSpec((kb, 2, LANE), lambda k: (k, 0, 0))],
        out_specs=blk,
        out_shape=jax.ShapeDtypeStruct(a5.shape, MXU_DTYPE),
        compiler_params=_cparams(("arbitrary",)),
        name="fft_mid",
    )(a5, hspec, f2, tw)


def _fft_last_kernel(v_ref, m_ref, z_ref, gate_ref, bias_ref, o_ref, vf_ref):
    vf_ref[...] = v_ref[0].astype(F32)
    for s in range(z_ref.shape[2]):
        vs = jnp.concatenate([vf_ref[0, :, s, :], vf_ref[1, :, s, :]], axis=0)
        zf = _dot(m_ref[0], vs)
        o_ref[0, :, s, :] = gate_ref[0, :, s, :] * (zf + z_ref[0, :, s, :] * bias_ref[...])


def _fft_last(v5, m3, z, gate, bias):
    b, a, _, c = z.shape
    n1 = v5.shape[2]
    js = FFT_JS
    row = pl.BlockSpec((1, a, js, c), lambda j, bb: (bb, 0, j, 0))
    return pl.pallas_call(
        _fft_last_kernel,
        grid=(LANE // js, b),
        in_specs=[pl.BlockSpec((1, 2, n1, js, c), lambda j, bb: (bb // 2, 0, 0, j, 0)),
                  pl.BlockSpec((1, a, 2 * n1), lambda j, bb: (bb % 2, 0, 0)),
                  row, row,
                  pl.BlockSpec((1, c), lambda j, bb: (0, 0))],
        out_specs=row,
        out_shape=jax.ShapeDtypeStruct(z.shape, F32),
        scratch_shapes=[pltpu.VMEM((2, n1, js, c), F32)],
        compiler_params=_cparams(("parallel", "arbitrary")),
        name="fft_last",
    )(v5, m3, z, gate, bias)


def _hyena_long(v, x1, x2, filt, ss, bias):
    b, n, c = v.shape
    a = n // LANE
    m1, m1_real, m3, f2, tw = _dft_tables(n)
    hspec = _filter_spectrum(_fft_first(filt.reshape(2, a, LANE, filt.shape[2]), m1_real), ss, f2, tw)
    z = v.reshape(b, a, LANE, c)
    gates = (x1.reshape(b, a, LANE, c), x2.reshape(b, a, LANE, c))
    for o in range(HY_ORDER):
        z = _fft_last(_fft_mid(_fft_first(z, m1), hspec, o, f2, tw), m3, z, gates[o], bias[o].reshape(1, c))
    return z.reshape(b, n, c)


def _hyena_ctx_kernel(v_ref, x1_ref, x2_ref, filt_ref, ss_ref, bias_ref, ff_ref, fc_ref, fi_ref, o_ref):
    b, n, c = v_ref.shape
    npair = b // 2
    hs = _dot(ff_ref[...], jnp.concatenate([filt_ref[0], filt_ref[1]], axis=0))
    hs = hs * lax.rsqrt(ss_ref[...] + EPS)
    z = [v_ref[i] for i in range(b)]
    gates = (x1_ref, x2_ref)
    for o in range(HY_ORDER):
        hr = jnp.concatenate([hs[:2 * n, o * c:(o + 1) * c]] * npair, axis=1)
        hi = jnp.concatenate([hs[2 * n:, o * c:(o + 1) * c]] * npair, axis=1)
        xin = jnp.concatenate([jnp.concatenate([z[2 * p] for p in range(npair)], axis=1),
                               jnp.concatenate([z[2 * p + 1] for p in range(npair)], axis=1)], axis=0)
        x = _dot(fc_ref[...], xin)
        xr, xi = x[:2 * n], x[2 * n:]
        y = jnp.concatenate([xr * hr - xi * hi, xr * hi + xi * hr], axis=0)
        zf = _dot(fi_ref[...], y)
        bo = bias_ref[o:o + 1, :]
        for i in range(b):
            p, part = i // 2, i % 2
            conv = zf[part * n:(part + 1) * n, p * c:(p + 1) * c]
            z[i] = gates[o][i] * (conv + z[i] * bo)
    for i in range(b):
        o_ref[i] = z[i]


def _hyena_ctx(v, x1, x2, filt, ss, bias):
    b, n, c = v.shape
    nn = 2 * n
    k = np.arange(nn)[:, None]
    j = np.arange(nn)[None, :]
    ang = 2.0 * np.pi * (k * j % nn) / nn
    fr, fi = np.cos(ang), -np.sin(ang)
    ff = np.concatenate([fr, fi], axis=0)
    fc = np.block([[fr[:, :n], -fi[:, :n]], [fi[:, :n], fr[:, :n]]])
    er, ei = fr[:n], -fi[:n]
    finv = np.block([[er, -ei], [ei, er]]) / nn
    return pl.pallas_call(
        _hyena_ctx_kernel,
        out_shape=jax.ShapeDtypeStruct((b, n, c), F32),
        compiler_params=pltpu.CompilerParams(vmem_limit_bytes=VMEM_LIMIT),
        name="hyena_ctx",
    )(v, x1, x2, filt, ss, bias, jnp.asarray(ff, F32), jnp.asarray(fc, F32), jnp.asarray(finv, F32))


def _window_kernel(n_ctx, sink_ref, q_ref, k_ref, v_ref, o_ref):
    g = pl.program_id(1)
    i = pl.program_id(2)
    rep, tq = q_ref.shape[1], q_ref.shape[3]
    t = k_ref.shape[2]
    ql, win = ATT_QL, CHUNK
    wk = ql + 2 * win
    ids = [(c, h) for c in range(tq // ql) for h in range(rep)]
    q0 = [i * tq + c * ql for c in range(tq // ql)]
    start = [pl.multiple_of(jnp.clip(q - win, 0, t - wk), LANE) for q in q0]

    kx = k_ref[0, 0, 0:n_ctx, :]
    vx = jnp.concatenate([v_ref[0, 0, u] for u in range(n_ctx // LANE)], axis=1)
    kl = [k_ref[0, 0, pl.ds(s, wk), :] for s in start]
    vl = [jnp.concatenate([v_ref[0, 0, s // LANE + u] for u in range(wk // LANE)], axis=1) for s in start]

    qs = [q_ref[0, h, :, c * ql:(c + 1) * ql] for c, h in ids]
    s_loc = [jnp.dot(kl[c], q, preferred_element_type=F32) for (c, h), q in zip(ids, qs)]
    s_ctx = [jnp.dot(kx, q, preferred_element_type=F32) for q in qs]

    diff = lax.broadcasted_iota(jnp.int32, (wk, ql), 0) - lax.broadcasted_iota(jnp.int32, (wk, ql), 1)
    krow = lax.broadcasted_iota(jnp.int32, (wk, 1), 0)
    p_loc, p_ctx, e_snk = [], [], []
    for n, (c, h) in enumerate(ids):
        d = diff + (start[c] - q0[c])
        ok = jnp.logical_and(jnp.abs(d) <= win, krow >= n_ctx - start[c])
        ok = jnp.logical_and(ok, q0[c] >= n_ctx)
        sl = jnp.where(ok, s_loc[n], NEG)
        snk = jnp.where(g == 0, sink_ref[0:1, h:h + 1], sink_ref[0:1, rep + h:rep + h + 1]) * LOG2E
        m = jnp.maximum(jnp.maximum(jnp.max(sl, axis=0, keepdims=True),
                                    jnp.max(s_ctx[n], axis=0, keepdims=True)), snk)
        p_loc.append(jnp.exp2(sl - m).astype(MXU_DTYPE))
        p_ctx.append(jnp.exp2(s_ctx[n] - m).astype(MXU_DTYPE))
        e_snk.append(jnp.exp2(snk - m))

    acc = [jnp.dot(vl[c], p_loc[n], preferred_element_type=F32)
           + jnp.dot(vx, p_ctx[n], preferred_element_type=F32) for n, (c, h) in enumerate(ids)]
    outs = [a[:HEAD_DIM] * (1.0 / (a[HEAD_DIM:HEAD_DIM + 1] + e)) for a, e in zip(acc, e_snk)]
    for c in range(tq // ql):
        o_ref[0, c * ql:(c + 1) * ql, :] = jnp.concatenate(outs[c * rep:(c + 1) * rep], axis=0).T


def _window_attention(qt, k, vt, sink, n_ctx):
    b, _, hd, t = qt.shape
    rep = N_HEADS // N_KV
    tq = ATT_TQ if t % ATT_TQ == 0 else ATT_QL
    assert n_ctx == ATT_QL and t % ATT_QL == 0 and rep * hd == LANE and t >= ATT_QL + 2 * CHUNK
    return pl.pallas_call(
        functools.partial(_window_kernel, n_ctx),
        grid=(b, N_KV, t // tq),
        in_specs=[pl.BlockSpec((SUBLANE, LANE), lambda bb, g, i: (0, 0)),
                  pl.BlockSpec((1, rep, hd, tq), lambda bb, g, i: (bb, g, 0, i)),
                  pl.BlockSpec((1, 1, t, hd), lambda bb, g, i: (bb, g, 0, 0)),
                  pl.BlockSpec((1, 1) + vt.shape[2:], lambda bb, g, i: (bb, g, 0, 0, 0))],
        out_specs=pl.BlockSpec((1, tq, rep * hd), lambda bb, g, i: (bb, i, g)),
        out_shape=jax.ShapeDtypeStruct((b, t, N_HEADS * hd), F32),
        compiler_params=_cparams(("parallel", "parallel", "arbitrary")),
        name="window_attention",
    )(sink, qt, k, vt)


def _dense_kernel(n_ctx, q_ref, k_ref, v_ref, o_ref, m_ref, alpha_ref, acc_ref, p_ref):
    i = pl.program_id(2)
    rep, tq = q_ref.shape[1], q_ref.shape[3]
    ql = ATT_QL
    nc = tq // ql

    def run(chunks, kt, nk):
        per = kt // LANE
        ids = [(c, h) for c in chunks for h in range(rep)]
        for c, h in ids:
            m_ref[c * rep + h] = jnp.full((1, ql), NEG, F32)
            alpha_ref[c * rep + h] = jnp.ones((1, ql), F32)
            acc_ref[c * rep + h] = jnp.zeros(acc_ref.shape[1:], F32)
            p_ref[c * rep + h, 0:kt] = jnp.zeros((kt, ql), p_ref.dtype)

        def scores(j):
            kb = k_ref[0, 0, pl.ds(pl.multiple_of(j * kt, kt), kt), :]
            return [jnp.dot(kb, q_ref[0, h, :, c * ql:(c + 1) * ql], preferred_element_type=F32) for c, h in ids]

        def values(j):
            vb = jnp.concatenate([v_ref[0, 0, j * per + u] for u in range(per)], axis=1)
            pvs = [jnp.dot(vb, p_ref[c * rep + h, 0:kt], preferred_element_type=F32) for c, h in ids]
            for (c, h), pv in zip(ids, pvs):
                n = c * rep + h
                acc_ref[n] = alpha_ref[n] * acc_ref[n] + pv

        def softmax(ss):
            for (c, h), s in zip(ids, ss):
                n = c * rep + h
                m = m_ref[n]
                mn = jnp.maximum(m, jnp.max(s, axis=0, keepdims=True))
                p_ref[n, 0:kt] = jnp.exp2(s - mn).astype(p_ref.dtype)
                alpha_ref[n] = jnp.exp2(m - mn)
                m_ref[n] = mn

        def body(j, carry):
            ss = scores(j)
            values(jnp.maximum(j - 1, 0))
            softmax(ss)
            return carry

        lax.fori_loop(0, nk, body, 0)
        values(nk - 1)
        for c in chunks:
            o = [acc_ref[c * rep + h] for h in range(rep)]
            o = [a[:HEAD_DIM] * (1.0 / a[HEAD_DIM:HEAD_DIM + 1]) for a in o]
            o_ref[0, c * ql:(c + 1) * ql, :] = jnp.concatenate(o, axis=0).T

    kt_all = p_ref.shape[1]
    nk_all = k_ref.shape[2] // kt_all

    @pl.when(i == 0)
    def _():
        run([0], n_ctx, 1)
        if nc > 1:
            run(list(range(1, nc)), kt_all, nk_all)

    @pl.when(i != 0)
    def _():
        run(list(range(nc)), kt_all, nk_all)


def _dense_attention(qt, k, vt, n_ctx):
    b, _, hd, t = qt.shape
    rep = N_HEADS // N_KV
    tq = ATT_TQ if t % ATT_TQ == 0 else ATT_QL
    kt = ATT_KT if t % ATT_KT == 0 else ATT_QL
    assert n_ctx == ATT_QL and t % ATT_QL == 0 and rep * hd == LANE
    return pl.pallas_call(
        functools.partial(_dense_kernel, n_ctx),
        grid=(b, N_KV, t // tq),
        in_specs=[pl.BlockSpec((1, rep, hd, tq), lambda bb, g, i: (bb, g, 0, i)),
                  pl.BlockSpec((1, 1, t, hd), lambda bb, g, i: (bb, g, 0, 0)),
                  pl.BlockSpec((1, 1) + vt.shape[2:], lambda bb, g, i: (bb, g, 0, 0, 0))],
        out_specs=pl.BlockSpec((1, tq, rep * hd), lambda bb, g, i: (bb, i, g)),
        out_shape=jax.ShapeDtypeStruct((b, t, N_HEADS * hd), F32),
        scratch_shapes=[pltpu.VMEM((rep * tq // ATT_QL, 1, ATT_QL), F32),
                        pltpu.VMEM((rep * tq // ATT_QL, 1, ATT_QL), F32),
                        pltpu.VMEM((rep * tq // ATT_QL, vt.shape[3], ATT_QL), F32),
                        pltpu.VMEM((rep * tq // ATT_QL, kt, ATT_QL), MXU_DTYPE)],
        compiler_params=_cparams(("parallel", "parallel", "arbitrary")),
        name="dense_attention",
    )(qt, k, vt)


def _mix_mlp_kernel(first, x_ref, ctx_ref, mod_ref, yf_ref, yb_ref, z_ref, gs_ref, hy_ref, hyc_ref,
                    yw_ref, yd_ref, gpost_ref, gpre_ref, gpost2_ref, wo_ref, w1_ref, w2_ref, o_ref):
    i = pl.program_id(1)
    if first:
        xv = jnp.where(i == 0, ctx_ref[0], x_ref[0])
        yh = jnp.where(i == 0, hyc_ref[0], hy_ref[0])
    else:
        xv = x_ref[0]
        yh = hy_ref[0]
    g1 = mod_ref[0, 0, 2:3, :]
    sh2 = mod_ref[0, 0, 3:4, :]
    sc2 = mod_ref[0, 0, 4:5, :]
    g2 = mod_ref[0, 0, 5:6, :]
    ya = _rms((yf_ref[0] + yb_ref[0]) * _silu(z_ref[0]), gs_ref[...])
    w = W_GROUP
    tm, d = xv.shape
    halves = [slice(r, r + tm // MLP_SPLIT) for r in range(0, tm, tm // MLP_SPLIT)]
    branches = (ya, yh, yw_ref[0], yd_ref[0])
    ys = [sum(_dot(br[rows], wo_ref[j * w:(j + 1) * w, :]) for j, br in enumerate(branches)) for rows in halves]
    x1s = [xv[rows] + g1 * _rms(y, gpost_ref[...]) for rows, y in zip(halves, ys)]
    hbs = [(_rms(x1, gpre_ref[...]) * (1.0 + sc2) + sh2).astype(MXU_DTYPE) for x1 in x1s]
    accs = [jnp.zeros((tm // MLP_SPLIT, d), F32) for _ in halves]
    for c in range(w1_ref.shape[1] // d):
        hidden = [jnp.maximum(jnp.dot(hb, w1_ref[:, c * d:(c + 1) * d], preferred_element_type=F32), 0.0)
                  for hb in hbs]
        accs = [acc + _dot(a * a, w2_ref[c * d:(c + 1) * d, :]) for acc, a in zip(accs, hidden)]
    for rows, x1, acc in zip(halves, x1s, accs):
        o_ref[0, rows] = x1 + g2 * _rms(acc, gpost2_ref[...])


def _mix_mlp(first, x, ctx, mod6, yf, yb, z, g_ssd, hy, hyc, yw, yd, g_post, g_pre2, g_post2, wo, w1, w2):
    b, t, _ = yf.shape
    d = wo.shape[1]
    tm = ROW_TILE
    off = 0 if first else 1
    nrow = t // tm - off
    if first:
        x_spec = pl.BlockSpec((1, tm, d), lambda bb, i: (bb, jnp.maximum(i - 1, 0), 0))
        hy_spec = pl.BlockSpec((1, tm, W_GROUP), lambda bb, i: (bb, jnp.maximum(i - 1, 0), 0))
        mod_spec = pl.BlockSpec((1, 1, 6, d), lambda bb, i: (bb, jnp.minimum(i, 1), 0, 0))
    else:
        x_spec = pl.BlockSpec((1, tm, d), lambda bb, i: (bb, i + 1, 0))
        hy_spec = pl.BlockSpec((1, tm, W_GROUP), lambda bb, i: (bb, i, 0))
        mod_spec = pl.BlockSpec((1, 1, 6, d), lambda bb, i: (bb, 1, 0, 0))
    first_blk = lambda w: pl.BlockSpec((1, tm, w), lambda bb, i: (bb, 0, 0))
    row = lambda w: pl.BlockSpec((1, tm, w), lambda bb, i: (bb, i + off, 0))
    vec = lambda w: pl.BlockSpec((1, w), lambda bb, i: (0, 0))
    full = lambda a: pl.BlockSpec(a.shape, lambda bb, i: (0, 0))
    return pl.pallas_call(
        functools.partial(_mix_mlp_kernel, first),
        grid=(b, nrow),
        in_specs=[x_spec, first_blk(d), mod_spec, row(W_GROUP), row(W_GROUP), row(W_GROUP), vec(W_GROUP),
                  hy_spec, first_blk(W_GROUP), row(W_GROUP), row(W_GROUP), vec(d), vec(d), vec(d),
                  full(wo), full(w1), full(w2)],
        out_specs=pl.BlockSpec((1, tm, d), lambda bb, i: (bb, i, 0)),
        out_shape=jax.ShapeDtypeStruct((b, nrow * tm, d), F32),
        compiler_params=_cparams(("parallel", "arbitrary")),
        name="mix_mlp",
    )(x, ctx, mod6, yf, yb, z, g_ssd, hy, hyc, yw, yd, g_post, g_pre2, g_post2, wo, w1, w2)


def _rope_tables(n, n_ctx):
    rows = n // GRID_W
    row = np.repeat(np.arange(rows, dtype=np.float64), GRID_W)
    col = np.tile(np.arange(GRID_W, dtype=np.float64), rows)
    n_freq = HEAD_DIM // 4
    inv = ROPE_THETA ** (-np.arange(n_freq, dtype=np.float64) / n_freq)
    ang = np.concatenate([row[:, None] * inv, col[:, None] * inv], axis=-1)
    cos, sin = np.cos(ang), np.sin(ang)
    cs = np.concatenate([np.ones((n_ctx, HEAD_DIM)), np.concatenate([cos, cos], axis=1)], axis=0)
    sn = np.concatenate([np.zeros((n_ctx, HEAD_DIM)), np.concatenate([-sin, sin], axis=1)], axis=0)
    return jnp.asarray(cs.T, F32), jnp.asarray(sn.T, F32)


def _pad_rows(a, rows):
    return jnp.pad(a, ((0, rows - a.shape[0]), (0, 0)))


def kernel(x, c, ctx, c_ctx, w_mod, b_mod, norm_mix_pre, norm_mix_post, norm_mlp_pre, norm_mlp_post, w_in, w_out, ssd_conv_w, ssd_conv_b, ssd_a_log, ssd_dt_bias, ssd_d, ssd_norm, hy_conv_w, hy_conv_b, hy_w1, hy_b1, hy_freq1, hy_w2, hy_b2, hy_freq2, hy_w3, hy_b3, hy_bias, attn_sink, q_norm, k_norm, mlp_w1, mlp_w2):
    b, n, d = x.shape
    n_ctx = ctx.shape[1]
    depth = w_mod.shape[0]
    assert n_ctx == ROW_TILE and n % ROW_TILE == 0 and b % 2 == 0 and b + 1 <= SUBLANE

    cc = _pad_rows(jnp.concatenate([c, c_ctx[None, :]], axis=0), SUBLANE)
    mod = _modulation(cc, w_mod, b_mod)
    cs, sn = _rope_tables(n, n_ctx)

    off_b = W_GROUP + SSD_XBC + 2 * N_HEADS
    xall = None
    for l in range(depth):
        first = l == 0
        need_ctx = l < depth - 1
        mod_lat = mod[l, :b].reshape(b, 1, 6, d)
        mod_ctx = jnp.broadcast_to(mod[l, b].reshape(1, 1, 6, d), (b, 1, 6, d))
        mod6 = jnp.concatenate([mod_ctx, mod_lat], axis=1)

        wl = w_in[l]
        off_c = off_b + 3 * W_GROUP
        off_d = off_c + (N_HEADS + 2 * N_KV) * HEAD_DIM
        w_row = jnp.concatenate(
            [wl[:, :W_GROUP + SSD_XBC], wl[:, off_b:off_c], wl[:, W_GROUP + SSD_XBC:off_b],
             jnp.zeros((d, D_ROW_PAD - off_c), F32)], axis=1).astype(MXU_DTYPE)
        w_att_t = jnp.concatenate([wl[:, off_d:], wl[:, off_c:off_d]], axis=1).T.astype(MXU_DTYPE)
        xin, cin = (x, ctx) if first else (xall, None)
        (z, u, dt, v, x1, x2, vc, x1c, x2c, qwt, kw, vwt, qdt, kd, vdt) = _in_projection(
            xin, cin, mod6, norm_mix_pre[l].reshape(1, d), w_row, w_att_t, cs, sn, q_norm[l], k_norm[l],
            ssd_conv_w[l], ssd_conv_b[l].reshape(1, -1), hy_conv_w[l], hy_conv_b[l].reshape(1, -1))

        par = jnp.zeros((SUBLANE, LANE), F32)
        par = par.at[0, :2 * N_HEADS].set(ssd_a_log[l].reshape(-1))
        par = par.at[1, :2 * N_HEADS].set(ssd_dt_bias[l].reshape(-1))
        par = par.at[2, :N_HEADS].set(ssd_d[l])
        yf, yb = _ssd(u, dt, par, n_ctx)

        filt_args = (hy_w1[l], hy_b1[l], hy_freq1[l], hy_w2[l], hy_b2[l], hy_freq2[l], hy_w3[l], hy_b3[l])
        yhy = _hyena_long(v, x1, x2, *_hyena_filters(n, *filt_args), hy_bias[l])
        if need_ctx:
            yhy_ctx = _hyena_ctx(vc, x1c, x2c, *_hyena_filters(n_ctx, *filt_args), hy_bias[l])
        else:
            yhy_ctx = yhy

        sink = jnp.zeros((SUBLANE, LANE), F32).at[0, :N_HEADS].set(attn_sink[l])
        yw = _window_attention(qwt, kw, vwt, sink, n_ctx)
        yd = _dense_attention(qdt, kd, vdt, n_ctx)

        xres, cres = (x, ctx) if first else (xall, xall)
        xall = _mix_mlp(first, xres, cres, mod6, yf, yb, z, ssd_norm[l].reshape(1, -1), yhy, yhy_ctx, yw, yd,
                        norm_mix_post[l].reshape(1, d), norm_mlp_pre[l].reshape(1, d),
                        norm_mlp_post[l].reshape(1, d), w_out[l].astype(MXU_DTYPE),
                        mlp_w1[l].astype(MXU_DTYPE), mlp_w2[l].astype(MXU_DTYPE))
    return xall
```

```python
import functools
import math

import numpy as np
import jax
import jax.numpy as jnp
from jax import lax
from jax.experimental import pallas as pl
from jax.experimental.pallas import tpu as pltpu

F32 = jnp.float32
MXU_DTYPE = jnp.bfloat16

EPS = 1e-6
HEAD_DIM = 64
GRID_W = 64
ROPE_THETA = 10000.0
N_HEADS = 4
N_KV = 2
W_GROUP = N_HEADS * HEAD_DIM
SSD_STATE = 64
SSD_XBC = W_GROUP + 2 * N_KV * SSD_STATE
HY_ORDER = 2
HY_BANDS = 16
HY_EMB = 2 * HY_BANDS + 1
HY_FILT = 64
HY_MAX_DECAY = math.log(1e-2) / 0.3
HY_MIN_DECAY = math.log(1e-2) / 1.5
CHUNK = 128
LANE = 128
SUBLANE = 8
ROW_TILE = 256
ATT_TQ = 768
FFT_JS = 16
FFT_KB = 4
MLP_SPLIT = 2
VMEM_LIMIT = 56 * 1024 * 1024
NEG = -1e30

C_Z = (0, 256)
C_XBC = (256, 768)
C_HY = (768, 1536)
C_DT = (1536, 1664)
D_ROW_PAD = 1664
A_QD, A_KD, A_VD = 0, 256, 384
A_QW, A_KW, A_VW = 512, 768, 896
V_ROWS = HEAD_DIM + 16
ATT_QL = 256
ATT_KT = 768
LOG2E = math.log2(math.e)


def _cparams(sem):
    return pltpu.CompilerParams(dimension_semantics=sem, vmem_limit_bytes=VMEM_LIMIT)


def _rms(x, g):
    return x * lax.rsqrt(jnp.mean(x * x, axis=-1, keepdims=True) + EPS) * g


def _silu(x):
    return x * (1.0 / (1.0 + jnp.exp(-x)))


def _softplus(x):
    return jnp.maximum(x, 0.0) + jnp.log(1.0 + jnp.exp(-jnp.abs(x)))


def _dot(a, b):
    return jnp.dot(a.astype(MXU_DTYPE), b.astype(MXU_DTYPE), preferred_element_type=F32)


def _dot_nt(a, b):
    return lax.dot_general(a.astype(MXU_DTYPE), b.astype(MXU_DTYPE), (((1,), (1,)), ((), ())),
                           preferred_element_type=F32)


def _dot_f32(a, b):
    return jnp.dot(a, b, preferred_element_type=F32, precision=lax.Precision.HIGHEST)


def _dot_split(a, b):
    ah, bh = a.astype(MXU_DTYPE), b.astype(MXU_DTYPE)
    al = (a - ah.astype(F32)).astype(MXU_DTYPE)
    bl = (b - bh.astype(F32)).astype(MXU_DTYPE)
    dot = functools.partial(jnp.dot, preferred_element_type=F32)
    return dot(ah, bh) + dot(ah, bl) + dot(al, bh)


def _mod_kernel(c_ref, w_ref, b_ref, o_ref):
    o_ref[0] = _dot_f32(_silu(c_ref[...]), w_ref[0]) + b_ref[0]


def _modulation(cc, w_mod, b_mod):
    depth, d, d6 = w_mod.shape
    tn = d6 // 4
    return pl.pallas_call(
        _mod_kernel,
        grid=(depth, d6 // tn),
        in_specs=[pl.BlockSpec((SUBLANE, d), lambda l, j: (0, 0)),
                  pl.BlockSpec((1, d, tn), lambda l, j: (l, 0, j)),
                  pl.BlockSpec((1, 1, tn), lambda l, j: (l, 0, j))],
        out_specs=pl.BlockSpec((1, SUBLANE, tn), lambda l, j: (l, 0, j)),
        out_shape=jax.ShapeDtypeStruct((depth, SUBLANE, d6), F32),
        compiler_params=_cparams(("arbitrary", "arbitrary")),
        name="modulation",
    )(cc, w_mod, b_mod.reshape(depth, 1, d6))


def _inproj_kernel(x_ref, xp_ref, xn_ref, ctx_ref, mod_ref, g_ref, w_ref, wt_ref, cst_ref, snt_ref, qnt_ref, knt_ref,
                   scw_ref, scb_ref, hcw_ref, hcb_ref,
                   z_ref, u_ref, dt_ref, v_ref, x1_ref, x2_ref, vc_ref, x1c_ref, x2c_ref,
                   qwt_ref, kw_ref, vwt_ref, qdt_ref, kd_ref, vdt_ref, ext_ref):
    i = pl.program_id(1)
    nrow = pl.num_programs(1)
    xv = jnp.where(i == 0, ctx_ref[0], x_ref[0])
    tm = xv.shape[0]
    sh = mod_ref[0, 0, 0:1, :]
    sc = mod_ref[0, 0, 1:2, :]

    def modulated(rows):
        return (_rms(rows, g_ref[...]) * (1.0 + sc) + sh).astype(MXU_DTYPE)

    hb = modulated(xv)

    def proj(cols):
        return jnp.dot(hb, w_ref[:, cols[0]:cols[1]], preferred_element_type=F32)

    z_ref[0] = proj(C_Z)
    dt_ref[0] = proj(C_DT)

    conv_cols = (C_XBC[0], C_HY[1])
    halo = jnp.dot(modulated(jnp.concatenate([xp_ref[0], xn_ref[0]], axis=0)),
                   w_ref[:, conv_cols[0]:conv_cols[1]], preferred_element_type=F32)
    prev_ok = i > 1
    next_ok = jnp.logical_and(i >= 1, i < nrow - 1)
    ext_ref[0:SUBLANE] = jnp.where(prev_ok, halo[0:SUBLANE], 0.0)
    ext_ref[SUBLANE:SUBLANE + tm] = proj(conv_cols)
    ext_ref[SUBLANE + tm:] = jnp.where(next_ok, halo[SUBLANE:], 0.0)

    def conv(w_r, b_r, lo, hi):
        taps = w_r.shape[0]
        acc = b_r[...]
        for k in range(taps):
            off = SUBLANE - taps // 2 + k
            acc = acc + w_r[k:k + 1, :] * ext_ref[off:off + tm, lo:hi]
        return acc

    nx = C_XBC[1] - C_XBC[0]
    u_ref[0] = _silu(conv(scw_ref, scb_ref, 0, nx))
    hyc = conv(hcw_ref, hcb_ref, nx, nx + C_HY[1] - C_HY[0])
    parts = [hyc[:, j * W_GROUP:(j + 1) * W_GROUP] for j in range(3)]
    for ref, val in zip((v_ref, x1_ref, x2_ref), parts):
        ref[0] = val

    @pl.when(i == 0)
    def _():
        for ref, val in zip((vc_ref, x1c_ref, x2c_ref), parts):
            ref[0] = val

    pt = _dot_nt(wt_ref[...], hb)
    cst = cst_ref[...]
    snt = snt_ref[...]
    tile = lambda r: jnp.concatenate([r[...]] * (tm // LANE), axis=1)
    qnt, knt = tile(qnt_ref), tile(knt_ref)
    half = HEAD_DIM // 2
    qscale = HEAD_DIM ** -0.5 * LOG2E

    def head(row0):
        return pt[row0:row0 + HEAD_DIM]

    def norm(t, gain):
        return t * lax.rsqrt(jnp.mean(t * t, axis=0, keepdims=True) + EPS) * gain

    def rope(t):
        return t * cst + jnp.concatenate([t[half:], t[:half]], axis=0) * snt

    def put_values(v_ref, row0):
        ones = jnp.ones((v_ref.shape[3] - HEAD_DIM, LANE), v_ref.dtype)
        for g in range(N_KV):
            vt = head(row0 + g * HEAD_DIM)
            for j in range(tm // LANE):
                v_ref[0, g, j, 0:HEAD_DIM, :] = vt[:, j * LANE:(j + 1) * LANE].astype(v_ref.dtype)
                v_ref[0, g, j, HEAD_DIM:, :] = ones

    def put_keys(k_ref, kt_pair):
        k_rows = jnp.concatenate(kt_pair, axis=0).T
        for g in range(N_KV):
            k_ref[0, g] = k_rows[:, g * HEAD_DIM:(g + 1) * HEAD_DIM].astype(k_ref.dtype)

    for h in range(N_HEADS):
        qdt_ref[0, h] = (rope(norm(head(A_QD + h * HEAD_DIM), qnt)) * qscale).astype(qdt_ref.dtype)
        qwt_ref[0, h] = (rope(head(A_QW + h * HEAD_DIM)) * qscale).astype(qwt_ref.dtype)
    put_keys(kd_ref, [rope(norm(head(A_KD + g * HEAD_DIM), knt)) for g in range(N_KV)])
    put_keys(kw_ref, [rope(head(A_KW + g * HEAD_DIM)) for g in range(N_KV)])
    put_values(vdt_ref, A_VD)
    put_values(vwt_ref, A_VW)


def _in_projection(x, ctx, mod6, g_pre, w_row, w_att_t, cst, snt, qn, kn, ssd_cw, ssd_cb, hy_cw, hy_cb):
    b, _, d = x.shape
    tm = ROW_TILE
    hp = tm // SUBLANE
    if ctx is None:
        t = x.shape[1]
        first_lat = 1
        x_spec = pl.BlockSpec((1, tm, d), lambda bb, i: (bb, i, 0))
        ctx_arr, ctx_spec = x, pl.BlockSpec((1, tm, d), lambda bb, i: (bb, 0, 0))
    else:
        assert ctx.shape[1] == tm
        t = x.shape[1] + tm
        first_lat = 0
        x_spec = pl.BlockSpec((1, tm, d), lambda bb, i: (bb, jnp.maximum(i - 1, 0), 0))
        ctx_arr, ctx_spec = ctx, pl.BlockSpec((1, tm, d), lambda bb, i: (bb, 0, 0))
    nrow = t // tm
    last8 = x.shape[1] // SUBLANE - 1
    xp_spec = pl.BlockSpec((1, SUBLANE, d), lambda bb, i: (bb, jnp.clip((i - 1 + first_lat) * hp - 1, 0, last8), 0))
    xn_spec = pl.BlockSpec((1, SUBLANE, d), lambda bb, i: (bb, jnp.clip((i + first_lat) * hp, 0, last8), 0))
    per = tm // LANE
    row = lambda w: pl.BlockSpec((1, tm, w), lambda bb, i: (bb, i, 0))
    lat = pl.BlockSpec((1, tm, W_GROUP), lambda bb, i: (bb, jnp.maximum(i - 1, 0), 0))
    cblk = pl.BlockSpec((1, tm, W_GROUP), lambda bb, i: (bb, 0, 0))
    full = lambda a: pl.BlockSpec(a.shape, lambda bb, i: (0, 0))
    f32 = lambda w: jax.ShapeDtypeStruct((b, t, w), F32)
    lat_shape = jax.ShapeDtypeStruct((b, t - tm, W_GROUP), F32)
    ctx_shape = jax.ShapeDtypeStruct((b, tm, W_GROUP), F32)
    q_spec = pl.BlockSpec((1, N_HEADS, HEAD_DIM, tm), lambda bb, i: (bb, 0, 0, i))
    k_spec = pl.BlockSpec((1, N_KV, tm, HEAD_DIM), lambda bb, i: (bb, 0, i, 0))
    v_spec = pl.BlockSpec((1, N_KV, per, V_ROWS, LANE), lambda bb, i: (bb, 0, i, 0, 0))
    q_shape = jax.ShapeDtypeStruct((b, N_HEADS, HEAD_DIM, t), MXU_DTYPE)
    k_shape = jax.ShapeDtypeStruct((b, N_KV, t, HEAD_DIM), MXU_DTYPE)
    v_shape = jax.ShapeDtypeStruct((b, N_KV, t // LANE, V_ROWS, LANE), MXU_DTYPE)
    gain = lambda v: jnp.broadcast_to(v.reshape(HEAD_DIM, 1), (HEAD_DIM, LANE))
    qnt, knt = gain(qn), gain(kn)
    return pl.pallas_call(
        _inproj_kernel,
        grid=(b, nrow),
        in_specs=[x_spec, xp_spec, xn_spec, ctx_spec,
                  pl.BlockSpec((1, 1, 6, d), lambda bb, i: (bb, jnp.minimum(i, 1), 0, 0)),
                  pl.BlockSpec((1, d), lambda bb, i: (0, 0)),
                  full(w_row), full(w_att_t),
                  pl.BlockSpec((HEAD_DIM, tm), lambda bb, i: (0, i)),
                  pl.BlockSpec((HEAD_DIM, tm), lambda bb, i: (0, i)),
                  full(qnt), full(knt), full(ssd_cw), full(ssd_cb), full(hy_cw), full(hy_cb)],
        out_specs=[row(W_GROUP), row(SSD_XBC), row(LANE), lat, lat, lat, cblk, cblk, cblk,
                   q_spec, k_spec, v_spec, q_spec, k_spec, v_spec],
        out_shape=[f32(W_GROUP), f32(SSD_XBC), f32(LANE), lat_shape, lat_shape, lat_shape,
                   ctx_shape, ctx_shape, ctx_shape,
                   q_shape, k_shape, v_shape, q_shape, k_shape, v_shape],
        scratch_shapes=[pltpu.VMEM((tm + 2 * SUBLANE, C_HY[1] - C_XBC[0]), F32)],
        compiler_params=_cparams(("parallel", "arbitrary")),
        name="in_projection",
    )(x, x, x, ctx_arr, mod6, g_pre, w_row, w_att_t, cst, snt, qnt, knt, ssd_cw, ssd_cb, hy_cw, hy_cb)


def _ssd_kernel(nc, nt, par_ref, uf_ref, dtf_ref, ub_ref, dtb_ref, yf_ref, yb_ref, st_ref):
    j = pl.program_id(1)
    q = CHUNK
    rep = N_HEADS // N_KV

    @pl.when(j == 0)
    def _():
        st_ref[...] = jnp.zeros(st_ref.shape, F32)

    a_all = -jnp.exp(par_ref[0:1, :])
    ri = lax.broadcasted_iota(jnp.int32, (q, q), 0)
    ci = lax.broadcasted_iota(jnp.int32, (q, q), 1)
    dirs = []
    for d, (u_ref, dt_ref) in enumerate(((uf_ref, dtf_ref), (ub_ref, dtb_ref))):
        u = u_ref[0]
        dtv = _softplus(dt_ref[0] + par_ref[1:2, :])
        mask = (ri >= ci) if d == 0 else (ri <= ci)
        cum = _dot_f32(mask.astype(F32), dtv * a_all)
        dirs.append(dict(
            d=d, mask=mask, cum=cum, cum_t=cum.T, dt_t=dtv.T, end=q - 1 if d == 0 else 0,
            xs=u[:, :W_GROUP], bm_t=u[:, W_GROUP:W_GROUP + N_KV * SSD_STATE].T,
            cm=u[:, W_GROUP + N_KV * SSD_STATE:]))

    def head_terms(v, h):
        hl = N_HEADS * v["d"] + h
        g = h // rep
        sl = slice(g * SSD_STATE, (g + 1) * SSD_STATE)
        return dict(col=v["cum"][:, hl:hl + 1], row=v["cum_t"][hl:hl + 1, :], dt_row=v["dt_t"][hl:hl + 1, :],
                    last=v["cum"][v["end"]:v["end"] + 1, hl:hl + 1], cg=v["cm"][:, sl], bg_t=v["bm_t"][sl, :],
                    xh=v["xs"][:, h * HEAD_DIM:(h + 1) * HEAD_DIM])

    terms = [[head_terms(v, h) for h in range(N_HEADS)] for v in dirs]
    scores = [[_dot(ts[g * rep]["cg"], ts[g * rep]["bg_t"]) for g in range(N_KV)] for ts in terms]
    carried = [[_dot(t["cg"], st_ref[v["d"], h]) for h, t in enumerate(ts)] for v, ts in zip(dirs, terms)]
    states = [[_dot(t["bg_t"] * (jnp.exp(t["last"] - t["row"]) * t["dt_row"]), t["xh"]) for t in ts]
              for ts in terms]
    diag = [[_dot(scores[v["d"]][h // rep] * jnp.exp(jnp.where(v["mask"], t["col"] - t["row"], NEG)) * t["dt_row"],
                  t["xh"]) for h, t in enumerate(ts)] for v, ts in zip(dirs, terms)]
    for v, ts, y_ref in zip(dirs, terms, (yf_ref, yb_ref)):
        d = v["d"]
        outs = []
        for h, t in enumerate(ts):
            y = diag[d][h] + carried[d][h] * jnp.exp(t["col"])
            st_ref[d, h] = jnp.exp(t["last"]) * st_ref[d, h] + states[d][h]
            if d == 0:
                y = y + par_ref[2:3, h:h + 1] * t["xh"]
            outs.append(y)
        y_ref[0] = jnp.concatenate(outs, axis=1)


def _ssd(u, dt, par, n_ctx):
    b, t, w = u.shape
    q = CHUNK
    nc, nt = n_ctx // q, t // q
    fwd = lambda j: j
    bwd = lambda j: jnp.where(j < nc, nc - 1 - j, nt + nc - 1 - j)

    def specs(cmap):
        return [pl.BlockSpec((1, q, w), lambda bb, j: (bb, cmap(j), 0)),
                pl.BlockSpec((1, q, LANE), lambda bb, j: (bb, cmap(j), 0))]

    return pl.pallas_call(
        functools.partial(_ssd_kernel, nc, nt),
        grid=(b, nt),
        in_specs=[pl.BlockSpec((SUBLANE, LANE), lambda bb, j: (0, 0))] + specs(fwd) + specs(bwd),
        out_specs=[pl.BlockSpec((1, q, W_GROUP), lambda bb, j: (bb, fwd(j), 0)),
                   pl.BlockSpec((1, q, W_GROUP), lambda bb, j: (bb, bwd(j), 0))],
        out_shape=[jax.ShapeDtypeStruct((b, t, W_GROUP), F32)] * 2,
        scratch_shapes=[pltpu.VMEM((2, N_HEADS, SSD_STATE, HEAD_DIM), F32)],
        compiler_params=_cparams(("parallel", "arbitrary")),
        name="ssd_scan",
    )(par, u, dt, u, dt)


def _filter_kernel(zf_ref, zb_ref, w1_ref, b1_ref, f1_ref, w2_ref, b2_ref, f2_ref,
                   w3f_ref, w3b_ref, b3f_ref, b3b_ref, dl_ref, o_ref, ss_ref):
    i = pl.program_id(0)
    tr = zf_ref.shape[0]

    def half(z_ref, w3_ref, b3_ref):
        z = z_ref[...]
        h = jnp.sin(f1_ref[...] * (_dot_split(z, w1_ref[...]) + b1_ref[...]))
        h = jnp.sin(f2_ref[...] * (_dot_split(h, w2_ref[...]) + b2_ref[...]))
        k = _dot_split(h, w3_ref[...]) + b3_ref[...]
        return k * jnp.exp(-z[:, 0:1] * dl_ref[...])

    kf = half(zf_ref, w3f_ref, b3f_ref)
    kb = half(zb_ref, w3b_ref, b3b_ref)

    @pl.when(i == 0)
    def _():
        ss_ref[...] = jnp.zeros(ss_ref.shape, F32)

    ss_ref[...] += jnp.sum(kf * kf + kb * kb, axis=0, keepdims=True)
    rows = i * tr + lax.broadcasted_iota(jnp.int32, (tr, 1), 0)
    o_ref[0] = kf
    o_ref[1] = jnp.where(rows == 0, 0.0, kb)


def _hyena_filters(n, w1, b1, f1, w2, b2, f2, w3, b3):
    pos = np.arange(n, dtype=np.float64)
    t = np.linspace(0.0, 1.0, n)
    f = np.linspace(1e-4, HY_BANDS - 1, HY_BANDS)
    ang = 2.0 * math.pi * pos[:, None] * f[None, :] / n
    emb = np.concatenate([t[:, None], np.cos(ang), -np.sin(ang)], axis=-1)
    emb = np.pad(emb, ((0, 0), (0, LANE - HY_EMB)))
    emb_b = np.roll(np.flip(emb, axis=0), 1, axis=0)
    emb, emb_b = jnp.asarray(emb, F32), jnp.asarray(emb_b, F32)
    w1p = jnp.pad(w1, ((0, LANE - HY_EMB), (0, 0)))
    w3r = w3.reshape(HY_FILT, HY_ORDER, 2, W_GROUP)
    b3r = b3.reshape(HY_ORDER, 2, W_GROUP)
    wc = HY_ORDER * W_GROUP
    w3f, w3b = w3r[:, :, 0].reshape(HY_FILT, wc), w3r[:, :, 1].reshape(HY_FILT, wc)
    b3f, b3b = b3r[:, 0].reshape(1, wc), b3r[:, 1].reshape(1, wc)
    deltas = np.abs(np.linspace(HY_MIN_DECAY, HY_MAX_DECAY, W_GROUP))
    deltas = jnp.asarray(np.tile(deltas, HY_ORDER).reshape(1, wc), F32)
    tr = math.gcd(n, 512)
    const = lambda s: pl.BlockSpec(s, lambda i: (0, 0))
    rows = pl.BlockSpec((tr, LANE), lambda i: (i, 0))
    return pl.pallas_call(
        _filter_kernel,
        grid=(n // tr,),
        in_specs=[rows, rows, const((LANE, HY_FILT)), const((1, HY_FILT)), const((1, HY_FILT)),
                  const((HY_FILT, HY_FILT)), const((1, HY_FILT)), const((1, HY_FILT)),
                  const((HY_FILT, wc)), const((HY_FILT, wc)), const((1, wc)), const((1, wc)), const((1, wc))],
        out_specs=[pl.BlockSpec((2, tr, wc), lambda i: (0, i, 0)), const((1, wc))],
        out_shape=[jax.ShapeDtypeStruct((2, n, wc), F32), jax.ShapeDtypeStruct((1, wc), F32)],
        compiler_params=_cparams(("arbitrary",)),
        name="hyena_filter",
    )(emb, emb_b, w1p, b1.reshape(1, -1), f1.reshape(1, -1), w2, b2.reshape(1, -1), f2.reshape(1, -1),
      w3f, w3b, b3f, b3b, deltas)


def _dft_tables(n):
    nn = 2 * n
    n1, n2 = nn // LANE, LANE
    a = n1 // 2
    k1 = np.arange(n1)[:, None]
    j1 = np.arange(n1)[None, :]
    ang1 = 2.0 * np.pi * (k1 * j1 % n1) / n1
    fr, fi = np.cos(ang1), -np.sin(ang1)
    m1 = np.block([[fr[:, :a], -fi[:, :a]], [fi[:, :a], fr[:, :a]]])
    m1_real = np.concatenate([fr, fi], axis=0)
    er, ei = fr.T[:a], -fi.T[:a]
    m3 = np.stack([np.concatenate([er, -ei], axis=1), np.concatenate([ei, er], axis=1)]) / nn
    k2 = np.arange(n2)[:, None]
    j2 = np.arange(n2)[None, :]
    ang2 = 2.0 * np.pi * (k2 * j2 % n2) / n2
    f2 = np.stack([np.cos(ang2), -np.sin(ang2)])
    angt = 2.0 * np.pi * (np.arange(n1)[:, None] * j2) / nn
    tw = np.stack([np.cos(angt), -np.sin(angt)], axis=1)
    return tuple(jnp.asarray(t, F32) for t in (m1, m1_real, m3, f2, tw))


def _fft_first_kernel(ur_ref, ui_ref, m_ref, o_ref, acc_ref):
    n1 = o_ref.shape[2]
    for s in range(ur_ref.shape[2]):
        r = _dot(m_ref[...], jnp.concatenate([ur_ref[0, :, s, :], ui_ref[0, :, s, :]], axis=0))
        acc_ref[0, :, s, :] = r[:n1]
        acc_ref[1, :, s, :] = r[n1:]
    o_ref[0] = acc_ref[...].astype(o_ref.dtype)


def _fft_first(u, m1):
    b2, a, _, c = u.shape
    p = b2 // 2
    n1 = m1.shape[0] // 2
    js = FFT_JS
    return pl.pallas_call(
        _fft_first_kernel,
        grid=(p, LANE // js),
        in_specs=[pl.BlockSpec((1, a, js, c), lambda pp, j: (2 * pp, 0, j, 0)),
                  pl.BlockSpec((1, a, js, c), lambda pp, j: (2 * pp + 1, 0, j, 0)),
                  pl.BlockSpec(m1.shape, lambda pp, j: (0, 0))],
        out_specs=pl.BlockSpec((1, 2, n1, js, c), lambda pp, j: (pp, 0, 0, j, 0)),
        out_shape=jax.ShapeDtypeStruct((p, 2, n1, LANE, c), MXU_DTYPE),
        scratch_shapes=[pltpu.VMEM((2, n1, js, c), F32)],
        compiler_params=_cparams(("parallel", "arbitrary")),
        name="fft_first",
    )(u, u, m1)


def _twiddled_dft(f2_ref, tw_ref, u):
    fr, fi = f2_ref[0], f2_ref[1]
    tr, ti = tw_ref[u, 0:1, :], tw_ref[u, 1:2, :]
    return fr * tr - fi * ti, fr * ti + fi * tr


def _real_form(gr, gi):
    return jnp.concatenate([jnp.concatenate([gr, -gi], axis=1), jnp.concatenate([gi, gr], axis=1)], axis=0)


def _spectrum_kernel(a_ref, ss_ref, f2_ref, tw_ref, h_ref):
    scale = lax.rsqrt(ss_ref[...] + EPS)
    for u in range(a_ref.shape[2]):
        xin = jnp.concatenate([a_ref[0, 0, u], a_ref[0, 1, u]], axis=0)
        x = _dot(_real_form(*_twiddled_dft(f2_ref, tw_ref, u)), xin) * scale
        h_ref[0, u] = x[:LANE]
        h_ref[1, u] = x[LANE:]


def _filter_spectrum(a5, ss, f2, tw):
    _, _, n1, _, c = a5.shape
    kb = math.gcd(n1, FFT_KB)
    return pl.pallas_call(
        _spectrum_kernel,
        grid=(n1 // kb,),
        in_specs=[pl.BlockSpec((1, 2, kb, LANE, c), lambda k: (0, 0, k, 0, 0)),
                  pl.BlockSpec((1, c), lambda k: (0, 0)),
                  pl.BlockSpec((2, LANE, LANE), lambda k: (0, 0, 0)),
                  pl.BlockSpec((kb, 2, LANE), lambda k: (k, 0, 0))],
        out_specs=pl.BlockSpec((2, kb, LANE, c), lambda k: (0, k, 0, 0)),
        out_shape=jax.ShapeDtypeStruct((2, n1, LANE, c), F32),
        compiler_params=_cparams(("arbitrary",)),
        name="filter_spectrum",
    )(a5, ss, f2, tw)


def _fft_mid_kernel(a_ref, h_ref, f2_ref, tw_ref, v_ref):
    npair = a_ref.shape[0]
    c = a_ref.shape[4]
    kb = a_ref.shape[2]
    gs = [_twiddled_dft(f2_ref, tw_ref, u) for u in range(kb)]
    xs = []
    for u, (gr, gi) in enumerate(gs):
        xin = jnp.concatenate(
            [jnp.concatenate([a_ref[p, 0, u], a_ref[p, 1, u]], axis=0) for p in range(npair)], axis=1)
        xs.append(_dot(_real_form(gr, gi), xin))
    vs = []
    for u, ((gr, gi), x) in enumerate(zip(gs, xs)):
        xr, xi = x[:LANE], x[LANE:]
        hr = jnp.concatenate([h_ref[0, u]] * npair, axis=1)
        hi = jnp.concatenate([h_ref[1, u]] * npair, axis=1)
        y = jnp.concatenate([xr * hr - xi * hi, xr * hi + xi * hr], axis=0)
        vs.append(_dot(_real_form(gr.T, -gi.T), y))
    for u, v in enumerate(vs):
        for p in range(npair):
            v_ref[p, 0, u] = v[:LANE, p * c:(p + 1) * c].astype(v_ref.dtype)
            v_ref[p, 1, u] = v[LANE:, p * c:(p + 1) * c].astype(v_ref.dtype)


def _fft_mid(a5, hspec, order, f2, tw):
    npair, _, n1, _, c = a5.shape
    kb = math.gcd(n1, FFT_KB)
    blk = pl.BlockSpec((npair, 2, kb, LANE, c), lambda k: (0, 0, k, 0, 0))
    return pl.pallas_call(
        _fft_mid_kernel,
        grid=(n1 // kb,),
        in_specs=[blk,
                  pl.BlockSpec((2, kb, LANE, c), lambda k: (0, k, 0, order)),
                  pl.BlockSpec((2, LANE, LANE), lambda k: (0, 0, 0)),
                  pl.BlockSpec((kb, 2, LANE), lambda k: (k, 0, 0))],
        out_specs=blk,
        out_shape=jax.ShapeDtypeStruct(a5.shape, MXU_DTYPE),
        compiler_params=_cparams(("arbitrary",)),
        name="fft_mid",
    )(a5, hspec, f2, tw)


def _fft_last_kernel(v_ref, m_ref, z_ref, gate_ref, bias_ref, o_ref, vf_ref):
    vf_ref[...] = v_ref[0].astype(F32)
    for s in range(z_ref.shape[2]):
        vs = jnp.concatenate([vf_ref[0, :, s, :], vf_ref[1, :, s, :]], axis=0)
        zf = _dot(m_ref[0], vs)
        o_ref[0, :, s, :] = gate_ref[0, :, s, :] * (zf + z_ref[0, :, s, :] * bias_ref[...])


def _fft_last(v5, m3, z, gate, bias):
    b, a, _, c = z.shape
    n1 = v5.shape[2]
    js = FFT_JS
    row = pl.BlockSpec((1, a, js, c), lambda j, bb: (bb, 0, j, 0))
    return pl.pallas_call(
        _fft_last_kernel,
        grid=(LANE // js, b),
        in_specs=[pl.BlockSpec((1, 2, n1, js, c), lambda j, bb: (bb // 2, 0, 0, j, 0)),
                  pl.BlockSpec((1, a, 2 * n1), lambda j, bb: (bb % 2, 0, 0)),
                  row, row,
                  pl.BlockSpec((1, c), lambda j, bb: (0, 0))],
        out_specs=row,
        out_shape=jax.ShapeDtypeStruct(z.shape, F32),
        scratch_shapes=[pltpu.VMEM((2, n1, js, c), F32)],
        compiler_params=_cparams(("parallel", "arbitrary")),
        name="fft_last",
    )(v5, m3, z, gate, bias)


def _hyena_long(v, x1, x2, filt, ss, bias):
    b, n, c = v.shape
    a = n // LANE
    m1, m1_real, m3, f2, tw = _dft_tables(n)
    hspec = _filter_spectrum(_fft_first(filt.reshape(2, a, LANE, filt.shape[2]), m1_real), ss, f2, tw)
    z = v.reshape(b, a, LANE, c)
    gates = (x1.reshape(b, a, LANE, c), x2.reshape(b, a, LANE, c))
    for o in range(HY_ORDER):
        z = _fft_last(_fft_mid(_fft_first(z, m1), hspec, o, f2, tw), m3, z, gates[o], bias[o].reshape(1, c))
    return z.reshape(b, n, c)


def _hyena_ctx_kernel(v_ref, x1_ref, x2_ref, filt_ref, ss_ref, bias_ref, ff_ref, fc_ref, fi_ref, o_ref):
    b, n, c = v_ref.shape
    npair = b // 2
    hs = _dot(ff_ref[...], jnp.concatenate([filt_ref[0], filt_ref[1]], axis=0))
    hs = hs * lax.rsqrt(ss_ref[...] + EPS)
    z = [v_ref[i] for i in range(b)]
    gates = (x1_ref, x2_ref)
    for o in range(HY_ORDER):
        hr = jnp.concatenate([hs[:2 * n, o * c:(o + 1) * c]] * npair, axis=1)
        hi = jnp.concatenate([hs[2 * n:, o * c:(o + 1) * c]] * npair, axis=1)
        xin = jnp.concatenate([jnp.concatenate([z[2 * p] for p in range(npair)], axis=1),
                               jnp.concatenate([z[2 * p + 1] for p in range(npair)], axis=1)], axis=0)
        x = _dot(fc_ref[...], xin)
        xr, xi = x[:2 * n], x[2 * n:]
        y = jnp.concatenate([xr * hr - xi * hi, xr * hi + xi * hr], axis=0)
        zf = _dot(fi_ref[...], y)
        bo = bias_ref[o:o + 1, :]
        for i in range(b):
            p, part = i // 2, i % 2
            conv = zf[part * n:(part + 1) * n, p * c:(p + 1) * c]
            z[i] = gates[o][i] * (conv + z[i] * bo)
    for i in range(b):
        o_ref[i] = z[i]


def _hyena_ctx(v, x1, x2, filt, ss, bias):
    b, n, c = v.shape
    nn = 2 * n
    k = np.arange(nn)[:, None]
    j = np.arange(nn)[None, :]
    ang = 2.0 * np.pi * (k * j % nn) / nn
    fr, fi = np.cos(ang), -np.sin(ang)
    ff = np.concatenate([fr, fi], axis=0)
    fc = np.block([[fr[:, :n], -fi[:, :n]], [fi[:, :n], fr[:, :n]]])
    er, ei = fr[:n], -fi[:n]
    finv = np.block([[er, -ei], [ei, er]]) / nn
    return pl.pallas_call(
        _hyena_ctx_kernel,
        out_shape=jax.ShapeDtypeStruct((b, n, c), F32),
        compiler_params=pltpu.CompilerParams(vmem_limit_bytes=VMEM_LIMIT),
        name="hyena_ctx",
    )(v, x1, x2, filt, ss, bias, jnp.asarray(ff, F32), jnp.asarray(fc, F32), jnp.asarray(finv, F32))


def _window_kernel(n_ctx, sink_ref, q_ref, k_ref, v_ref, o_ref):
    g = pl.program_id(1)
    i = pl.program_id(2)
    rep, tq = q_ref.shape[1], q_ref.shape[3]
    t = k_ref.shape[2]
    ql, win = ATT_QL, CHUNK
    wk = ql + 2 * win
    ids = [(c, h) for c in range(tq // ql) for h in range(rep)]
    q0 = [i * tq + c * ql for c in range(tq // ql)]
    start = [pl.multiple_of(jnp.clip(q - win, 0, t - wk), LANE) for q in q0]

    kx = k_ref[0, 0, 0:n_ctx, :]
    vx = jnp.concatenate([v_ref[0, 0, u] for u in range(n_ctx // LANE)], axis=1)
    kl = [k_ref[0, 0, pl.ds(s, wk), :] for s in start]
    vl = [jnp.concatenate([v_ref[0, 0, s // LANE + u] for u in range(wk // LANE)], axis=1) for s in start]

    qs = [q_ref[0, h, :, c * ql:(c + 1) * ql] for c, h in ids]
    s_loc = [jnp.dot(kl[c], q, preferred_element_type=F32) for (c, h), q in zip(ids, qs)]
    s_ctx = [jnp.dot(kx, q, preferred_element_type=F32) for q in qs]

    diff = lax.broadcasted_iota(jnp.int32, (wk, ql), 0) - lax.broadcasted_iota(jnp.int32, (wk, ql), 1)
    krow = lax.broadcasted_iota(jnp.int32, (wk, 1), 0)
    p_loc, p_ctx, e_snk = [], [], []
    for n, (c, h) in enumerate(ids):
        d = diff + (start[c] - q0[c])
        ok = jnp.logical_and(jnp.abs(d) <= win, krow >= n_ctx - start[c])
        ok = jnp.logical_and(ok, q0[c] >= n_ctx)
        sl = jnp.where(ok, s_loc[n], NEG)
        snk = jnp.where(g == 0, sink_ref[0:1, h:h + 1], sink_ref[0:1, rep + h:rep + h + 1]) * LOG2E
        m = jnp.maximum(jnp.maximum(jnp.max(sl, axis=0, keepdims=True),
                                    jnp.max(s_ctx[n], axis=0, keepdims=True)), snk)
        p_loc.append(jnp.exp2(sl - m).astype(MXU_DTYPE))
        p_ctx.append(jnp.exp2(s_ctx[n] - m).astype(MXU_DTYPE))
        e_snk.append(jnp.exp2(snk - m))

    acc = [jnp.dot(vl[c], p_loc[n], preferred_element_type=F32)
           + jnp.dot(vx, p_ctx[n], preferred_element_type=F32) for n, (c, h) in enumerate(ids)]
    outs = [a[:HEAD_DIM] * (1.0 / (a[HEAD_DIM:HEAD_DIM + 1] + e)) for a, e in zip(acc, e_snk)]
    for c in range(tq // ql):
        o_ref[0, c * ql:(c + 1) * ql, :] = jnp.concatenate(outs[c * rep:(c + 1) * rep], axis=0).T


def _window_attention(qt, k, vt, sink, n_ctx):
    b, _, hd, t = qt.shape
    rep = N_HEADS // N_KV
    tq = ATT_TQ if t % ATT_TQ == 0 else ATT_QL
    assert n_ctx == ATT_QL and t % ATT_QL == 0 and rep * hd == LANE and t >= ATT_QL + 2 * CHUNK
    return pl.pallas_call(
        functools.partial(_window_kernel, n_ctx),
        grid=(b, N_KV, t // tq),
        in_specs=[pl.BlockSpec((SUBLANE, LANE), lambda bb, g, i: (0, 0)),
                  pl.BlockSpec((1, rep, hd, tq), lambda bb, g, i: (bb, g, 0, i)),
                  pl.BlockSpec((1, 1, t, hd), lambda bb, g, i: (bb, g, 0, 0)),
                  pl.BlockSpec((1, 1) + vt.shape[2:], lambda bb, g, i: (bb, g, 0, 0, 0))],
        out_specs=pl.BlockSpec((1, tq, rep * hd), lambda bb, g, i: (bb, i, g)),
        out_shape=jax.ShapeDtypeStruct((b, t, N_HEADS * hd), F32),
        compiler_params=_cparams(("parallel", "parallel", "arbitrary")),
        name="window_attention",
    )(sink, qt, k, vt)


def _dense_kernel(n_ctx, q_ref, k_ref, v_ref, o_ref, m_ref, alpha_ref, acc_ref, p_ref):
    i = pl.program_id(2)
    rep, tq = q_ref.shape[1], q_ref.shape[3]
    ql = ATT_QL
    nc = tq // ql

    def run(chunks, kt, nk):
        per = kt // LANE
        ids = [(c, h) for c in chunks for h in range(rep)]
        for c, h in ids:
            m_ref[c * rep + h] = jnp.full((1, ql), NEG, F32)
            alpha_ref[c * rep + h] = jnp.ones((1, ql), F32)
            acc_ref[c * rep + h] = jnp.zeros(acc_ref.shape[1:], F32)
            p_ref[c * rep + h, 0:kt] = jnp.zeros((kt, ql), p_ref.dtype)

        def scores(j):
            kb = k_ref[0, 0, pl.ds(pl.multiple_of(j * kt, kt), kt), :]
            return [jnp.dot(kb, q_ref[0, h, :, c * ql:(c + 1) * ql], preferred_element_type=F32) for c, h in ids]

        def values(j):
            vb = jnp.concatenate([v_ref[0, 0, j * per + u] for u in range(per)], axis=1)
            pvs = [jnp.dot(vb, p_ref[c * rep + h, 0:kt], preferred_element_type=F32) for c, h in ids]
            for (c, h), pv in zip(ids, pvs):
                n = c * rep + h
                acc_ref[n] = alpha_ref[n] * acc_ref[n] + pv

        def softmax(ss):
            for (c, h), s in zip(ids, ss):
                n = c * rep + h
                m = m_ref[n]
                mn = jnp.maximum(m, jnp.max(s, axis=0, keepdims=True))
                p_ref[n, 0:kt] = jnp.exp2(s - mn).astype(p_ref.dtype)
                alpha_ref[n] = jnp.exp2(m - mn)
                m_ref[n] = mn

        def body(j, carry):
            ss = scores(j)
            values(jnp.maximum(j - 1, 0))
            softmax(ss)
            return carry

        lax.fori_loop(0, nk, body, 0)
        values(nk - 1)
        for c in chunks:
            o = [acc_ref[c * rep + h] for h in range(rep)]
            o = [a[:HEAD_DIM] * (1.0 / a[HEAD_DIM:HEAD_DIM + 1]) for a in o]
            o_ref[0, c * ql:(c + 1) * ql, :] = jnp.concatenate(o, axis=0).T

    kt_all = p_ref.shape[1]
    nk_all = k_ref.shape[2] // kt_all

    @pl.when(i == 0)
    def _():
        run([0], n_ctx, 1)
        if nc > 1:
            run(list(range(1, nc)), kt_all, nk_all)

    @pl.when(i != 0)
    def _():
        run(list(range(nc)), kt_all, nk_all)


def _dense_attention(qt, k, vt, n_ctx):
    b, _, hd, t = qt.shape
    rep = N_HEADS // N_KV
    tq = ATT_TQ if t % ATT_TQ == 0 else ATT_QL
    kt = ATT_KT if t % ATT_KT == 0 else ATT_QL
    assert n_ctx == ATT_QL and t % ATT_QL == 0 and rep * hd == LANE
    return pl.pallas_call(
        functools.partial(_dense_kernel, n_ctx),
        grid=(b, N_KV, t // tq),
        in_specs=[pl.BlockSpec((1, rep, hd, tq), lambda bb, g, i: (bb, g, 0, i)),
                  pl.BlockSpec((1, 1, t, hd), lambda bb, g, i: (bb, g, 0, 0)),
                  pl.BlockSpec((1, 1) + vt.shape[2:], lambda bb, g, i: (bb, g, 0, 0, 0))],
        out_specs=pl.BlockSpec((1, tq, rep * hd), lambda bb, g, i: (bb, i, g)),
        out_shape=jax.ShapeDtypeStruct((b, t, N_HEADS * hd), F32),
        scratch_shapes=[pltpu.VMEM((rep * tq // ATT_QL, 1, ATT_QL), F32),
                        pltpu.VMEM((rep * tq // ATT_QL, 1, ATT_QL), F32),
                        pltpu.VMEM((rep * tq // ATT_QL, vt.shape[3], ATT_QL), F32),
                        pltpu.VMEM((rep * tq // ATT_QL, kt, ATT_QL), MXU_DTYPE)],
        compiler_params=_cparams(("parallel", "parallel", "arbitrary")),
        name="dense_attention",
    )(qt, k, vt)


def _mix_mlp_kernel(first, x_ref, ctx_ref, mod_ref, yf_ref, yb_ref, z_ref, gs_ref, hy_ref, hyc_ref,
                    yw_ref, yd_ref, gpost_ref, gpre_ref, gpost2_ref, wo_ref, w1_ref, w2_ref, o_ref):
    i = pl.program_id(1)
    if first:
        xv = jnp.where(i == 0, ctx_ref[0], x_ref[0])
        yh = jnp.where(i == 0, hyc_ref[0], hy_ref[0])
    else:
        xv = x_ref[0]
        yh = hy_ref[0]
    g1 = mod_ref[0, 0, 2:3, :]
    sh2 = mod_ref[0, 0, 3:4, :]
    sc2 = mod_ref[0, 0, 4:5, :]
    g2 = mod_ref[0, 0, 5:6, :]
    ya = _rms((yf_ref[0] + yb_ref[0]) * _silu(z_ref[0]), gs_ref[...])
    w = W_GROUP
    tm, d = xv.shape
    halves = [slice(r, r + tm // MLP_SPLIT) for r in range(0, tm, tm // MLP_SPLIT)]
    branches = (ya, yh, yw_ref[0], yd_ref[0])
    ys = [sum(_dot(br[rows], wo_ref[j * w:(j + 1) * w, :]) for j, br in enumerate(branches)) for rows in halves]
    x1s = [xv[rows] + g1 * _rms(y, gpost_ref[...]) for rows, y in zip(halves, ys)]
    hbs = [(_rms(x1, gpre_ref[...]) * (1.0 + sc2) + sh2).astype(MXU_DTYPE) for x1 in x1s]
    accs = [jnp.zeros((tm // MLP_SPLIT, d), F32) for _ in halves]
    for c in range(w1_ref.shape[1] // d):
        hidden = [jnp.maximum(jnp.dot(hb, w1_ref[:, c * d:(c + 1) * d], preferred_element_type=F32), 0.0)
                  for hb in hbs]
        accs = [acc + _dot(a * a, w2_ref[c * d:(c + 1) * d, :]) for acc, a in zip(accs, hidden)]
    for rows, x1, acc in zip(halves, x1s, accs):
        o_ref[0, rows] = x1 + g2 * _rms(acc, gpost2_ref[...])


def _mix_mlp(first, x, ctx, mod6, yf, yb, z, g_ssd, hy, hyc, yw, yd, g_post, g_pre2, g_post2, wo, w1, w2):
    b, t, _ = yf.shape
    d = wo.shape[1]
    tm = ROW_TILE
    off = 0 if first else 1
    nrow = t // tm - off
    if first:
        x_spec = pl.BlockSpec((1, tm, d), lambda bb, i: (bb, jnp.maximum(i - 1, 0), 0))
        hy_spec = pl.BlockSpec((1, tm, W_GROUP), lambda bb, i: (bb, jnp.maximum(i - 1, 0), 0))
        mod_spec = pl.BlockSpec((1, 1, 6, d), lambda bb, i: (bb, jnp.minimum(i, 1), 0, 0))
    else:
        x_spec = pl.BlockSpec((1, tm, d), lambda bb, i: (bb, i + 1, 0))
        hy_spec = pl.BlockSpec((1, tm, W_GROUP), lambda bb, i: (bb, i, 0))
        mod_spec = pl.BlockSpec((1, 1, 6, d), lambda bb, i: (bb, 1, 0, 0))
    first_blk = lambda w: pl.BlockSpec((1, tm, w), lambda bb, i: (bb, 0, 0))
    row = lambda w: pl.BlockSpec((1, tm, w), lambda bb, i: (bb, i + off, 0))
    vec = lambda w: pl.BlockSpec((1, w), lambda bb, i: (0, 0))
    full = lambda a: pl.BlockSpec(a.shape, lambda bb, i: (0, 0))
    return pl.pallas_call(
        functools.partial(_mix_mlp_kernel, first),
        grid=(b, nrow),
        in_specs=[x_spec, first_blk(d), mod_spec, row(W_GROUP), row(W_GROUP), row(W_GROUP), vec(W_GROUP),
                  hy_spec, first_blk(W_GROUP), row(W_GROUP), row(W_GROUP), vec(d), vec(d), vec(d),
                  full(wo), full(w1), full(w2)],
        out_specs=pl.BlockSpec((1, tm, d), lambda bb, i: (bb, i, 0)),
        out_shape=jax.ShapeDtypeStruct((b, nrow * tm, d), F32),
        compiler_params=_cparams(("parallel", "arbitrary")),
        name="mix_mlp",
    )(x, ctx, mod6, yf, yb, z, g_ssd, hy, hyc, yw, yd, g_post, g_pre2, g_post2, wo, w1, w2)


def _rope_tables(n, n_ctx):
    rows = n // GRID_W
    row = np.repeat(np.arange(rows, dtype=np.float64), GRID_W)
    col = np.tile(np.arange(GRID_W, dtype=np.float64), rows)
    n_freq = HEAD_DIM // 4
    inv = ROPE_THETA ** (-np.arange(n_freq, dtype=np.float64) / n_freq)
    ang = np.concatenate([row[:, None] * inv, col[:, None] * inv], axis=-1)
    cos, sin = np.cos(ang), np.sin(ang)
    cs = np.concatenate([np.ones((n_ctx, HEAD_DIM)), np.concatenate([cos, cos], axis=1)], axis=0)
    sn = np.concatenate([np.zeros((n_ctx, HEAD_DIM)), np.concatenate([-sin, sin], axis=1)], axis=0)
    return jnp.asarray(cs.T, F32), jnp.asarray(sn.T, F32)


def _pad_rows(a, rows):
    return jnp.pad(a, ((0, rows - a.shape[0]), (0, 0)))


def kernel(x, c, ctx, c_ctx, w_mod, b_mod, norm_mix_pre, norm_mix_post, norm_mlp_pre, norm_mlp_post, w_in, w_out, ssd_conv_w, ssd_conv_b, ssd_a_log, ssd_dt_bias, ssd_d, ssd_norm, hy_conv_w, hy_conv_b, hy_w1, hy_b1, hy_freq1, hy_w2, hy_b2, hy_freq2, hy_w3, hy_b3, hy_bias, attn_sink, q_norm, k_norm, mlp_w1, mlp_w2):
    b, n, d = x.shape
    n_ctx = ctx.shape[1]
    depth = w_mod.shape[0]
    assert n_ctx == ROW_TILE and n % ROW_TILE == 0 and b % 2 == 0 and b + 1 <= SUBLANE

    cc = _pad_rows(jnp.concatenate([c, c_ctx[None, :]], axis=0), SUBLANE)
    mod = _modulation(cc, w_mod, b_mod)
    cs, sn = _rope_tables(n, n_ctx)

    off_b = W_GROUP + SSD_XBC + 2 * N_HEADS
    xall = None
    for l in range(depth):
        first = l == 0
        need_ctx = l < depth - 1
        mod_lat = mod[l, :b].reshape(b, 1, 6, d)
        mod_ctx = jnp.broadcast_to(mod[l, b].reshape(1, 1, 6, d), (b, 1, 6, d))
        mod6 = jnp.concatenate([mod_ctx, mod_lat], axis=1)

        wl = w_in[l]
        off_c = off_b + 3 * W_GROUP
        off_d = off_c + (N_HEADS + 2 * N_KV) * HEAD_DIM
        w_row = jnp.concatenate(
            [wl[:, :W_GROUP + SSD_XBC], wl[:, off_b:off_c], wl[:, W_GROUP + SSD_XBC:off_b],
             jnp.zeros((d, D_ROW_PAD - off_c), F32)], axis=1).astype(MXU_DTYPE)
        w_att_t = jnp.concatenate([wl[:, off_d:], wl[:, off_c:off_d]], axis=1).T.astype(MXU_DTYPE)
        xin, cin = (x, ctx) if first else (xall, None)
        (z, u, dt, v, x1, x2, vc, x1c, x2c, qwt, kw, vwt, qdt, kd, vdt) = _in_projection(
            xin, cin, mod6, norm_mix_pre[l].reshape(1, d), w_row, w_att_t, cs, sn, q_norm[l], k_norm[l],
            ssd_conv_w[l], ssd_conv_b[l].reshape(1, -1), hy_conv_w[l], hy_conv_b[l].reshape(1, -1))

        par = jnp.zeros((SUBLANE, LANE), F32)
        par = par.at[0, :2 * N_HEADS].set(ssd_a_log[l].reshape(-1))
        par = par.at[1, :2 * N_HEADS].set(ssd_dt_bias[l].reshape(-1))
        par = par.at[2, :N_HEADS].set(ssd_d[l])
        yf, yb = _ssd(u, dt, par, n_ctx)

        filt_args = (hy_w1[l], hy_b1[l], hy_freq1[l], hy_w2[l], hy_b2[l], hy_freq2[l], hy_w3[l], hy_b3[l])
        yhy = _hyena_long(v, x1, x2, *_hyena_filters(n, *filt_args), hy_bias[l])
        if need_ctx:
            yhy_ctx = _hyena_ctx(vc, x1c, x2c, *_hyena_filters(n_ctx, *filt_args), hy_bias[l])
        else:
            yhy_ctx = yhy

        sink = jnp.zeros((SUBLANE, LANE), F32).at[0, :N_HEADS].set(attn_sink[l])
        yw = _window_attention(qwt, kw, vwt, sink, n_ctx)
        yd = _dense_attention(qdt, kd, vdt, n_ctx)

        xres, cres = (x, ctx) if first else (xall, xall)
        xall = _mix_mlp(first, xres, cres, mod6, yf, yb, z, ssd_norm[l].reshape(1, -1), yhy, yhy_ctx, yw, yd,
                        norm_mix_post[l].reshape(1, d), norm_mlp_pre[l].reshape(1, d),
                        norm_mlp_post[l].reshape(1, d), w_out[l].astype(MXU_DTYPE),
                        mlp_w1[l].astype(MXU_DTYPE), mlp_w2[l].astype(MXU_DTYPE))
    return xall
```

```python
import functools
import math

import numpy as np
import jax
import jax.numpy as jnp
from jax import lax
from jax.experimental import pallas as pl
from jax.experimental.pallas import tpu as pltpu

F32 = jnp.float32
MXU_DTYPE = jnp.bfloat16

EPS = 1e-6
HEAD_DIM = 64
GRID_W = 64
ROPE_THETA = 10000.0
N_HEADS = 4
N_KV = 2
W_GROUP = N_HEADS * HEAD_DIM
SSD_STATE = 64
SSD_XBC = W_GROUP + 2 * N_KV * SSD_STATE
HY_ORDER = 2
HY_BANDS = 16
HY_EMB = 2 * HY_BANDS + 1
HY_FILT = 64
HY_MAX_DECAY = math.log(1e-2) / 0.3
HY_MIN_DECAY = math.log(1e-2) / 1.5
CHUNK = 128
LANE = 128
SUBLANE = 8
ROW_TILE = 256
ATT_TQ = 768
FFT_JS = 16
FFT_KB = 4
MLP_SPLIT = 2
VMEM_LIMIT = 56 * 1024 * 1024
NEG = -1e30

C_Z = (0, 256)
C_XBC = (256, 768)
C_HY = (768, 1536)
C_DT = (1536, 1664)
D_ROW_PAD = 1664
A_QD, A_KD, A_VD = 0, 256, 384
A_QW, A_KW, A_VW = 512, 768, 896
V_ROWS = HEAD_DIM + 16
ATT_QL = 256
ATT_KT = 768
LOG2E = math.log2(math.e)


def _cparams(sem):
    return pltpu.CompilerParams(dimension_semantics=sem, vmem_limit_bytes=VMEM_LIMIT)


def _rms(x, g):
    return x * lax.rsqrt(jnp.mean(x * x, axis=-1, keepdims=True) + EPS) * g


def _silu(x):
    return x * (1.0 / (1.0 + jnp.exp(-x)))


def _softplus(x):
    return jnp.maximum(x, 0.0) + jnp.log(1.0 + jnp.exp(-jnp.abs(x)))


def _dot(a, b):
    return jnp.dot(a.astype(MXU_DTYPE), b.astype(MXU_DTYPE), preferred_element_type=F32)


def _dot_nt(a, b):
    return lax.dot_general(a.astype(MXU_DTYPE), b.astype(MXU_DTYPE), (((1,), (1,)), ((), ())),
                           preferred_element_type=F32)


def _dot_f32(a, b):
    return jnp.dot(a, b, preferred_element_type=F32, precision=lax.Precision.HIGHEST)


def _dot_split(a, b):
    ah, bh = a.astype(MXU_DTYPE), b.astype(MXU_DTYPE)
    al = (a - ah.astype(F32)).astype(MXU_DTYPE)
    bl = (b - bh.astype(F32)).astype(MXU_DTYPE)
    dot = functools.partial(jnp.dot, preferred_element_type=F32)
    return dot(ah, bh) + dot(ah, bl) + dot(al, bh)


def _mod_kernel(c_ref, w_ref, b_ref, o_ref):
    o_ref[0] = _dot_f32(_silu(c_ref[...]), w_ref[0]) + b_ref[0]


def _modulation(cc, w_mod, b_mod):
    depth, d, d6 = w_mod.shape
    tn = d6 // 4
    return pl.pallas_call(
        _mod_kernel,
        grid=(depth, d6 // tn),
        in_specs=[pl.BlockSpec((SUBLANE, d), lambda l, j: (0, 0)),
                  pl.BlockSpec((1, d, tn), lambda l, j: (l, 0, j)),
                  pl.BlockSpec((1, 1, tn), lambda l, j: (l, 0, j))],
        out_specs=pl.BlockSpec((1, SUBLANE, tn), lambda l, j: (l, 0, j)),
        out_shape=jax.ShapeDtypeStruct((depth, SUBLANE, d6), F32),
        compiler_params=_cparams(("arbitrary", "arbitrary")),
        name="modulation",
    )(cc, w_mod, b_mod.reshape(depth, 1, d6))


def _inproj_kernel(x_ref, xp_ref, xn_ref, ctx_ref, mod_ref, g_ref, w_ref, wt_ref, cst_ref, snt_ref, qnt_ref, knt_ref,
                   scw_ref, scb_ref, hcw_ref, hcb_ref,
                   z_ref, u_ref, dt_ref, v_ref, x1_ref, x2_ref, vc_ref, x1c_ref, x2c_ref,
                   qwt_ref, kw_ref, vwt_ref, qdt_ref, kd_ref, vdt_ref, ext_ref):
    i = pl.program_id(1)
    nrow = pl.num_programs(1)
    xv = jnp.where(i == 0, ctx_ref[0], x_ref[0])
    tm = xv.shape[0]
    sh = mod_ref[0, 0, 0:1, :]
    sc = mod_ref[0, 0, 1:2, :]

    def modulated(rows):
        return (_rms(rows, g_ref[...]) * (1.0 + sc) + sh).astype(MXU_DTYPE)

    hb = modulated(xv)

    def proj(cols):
        return jnp.dot(hb, w_ref[:, cols[0]:cols[1]], preferred_element_type=F32)

    z_ref[0] = proj(C_Z)
    dt_ref[0] = proj(C_DT)

    conv_cols = (C_XBC[0], C_HY[1])
    rows = jnp.concatenate([hb, modulated(jnp.concatenate([xp_ref[0], xn_ref[0]], axis=0))], axis=0)
    pc = jnp.dot(rows, w_ref[:, conv_cols[0]:conv_cols[1]], preferred_element_type=F32)
    prev_ok = i > 1
    next_ok = jnp.logical_and(i >= 1, i < nrow - 1)
    ext_ref[0:SUBLANE] = jnp.where(prev_ok, pc[tm:tm + SUBLANE], 0.0)
    ext_ref[SUBLANE:SUBLANE + tm] = pc[:tm]
    ext_ref[SUBLANE + tm:] = jnp.where(next_ok, pc[tm + SUBLANE:], 0.0)

    def conv(w_r, b_r, lo, hi):
        taps = w_r.shape[0]
        acc = b_r[...]
        for k in range(taps):
            off = SUBLANE - taps // 2 + k
            acc = acc + w_r[k:k + 1, :] * ext_ref[off:off + tm, lo:hi]
        return acc

    nx = C_XBC[1] - C_XBC[0]
    u_ref[0] = _silu(conv(scw_ref, scb_ref, 0, nx))
    hyc = conv(hcw_ref, hcb_ref, nx, nx + C_HY[1] - C_HY[0])
    parts = [hyc[:, j * W_GROUP:(j + 1) * W_GROUP] for j in range(3)]
    for ref, val in zip((v_ref, x1_ref, x2_ref), parts):
        ref[0] = val

    @pl.when(i == 0)
    def _():
        for ref, val in zip((vc_ref, x1c_ref, x2c_ref), parts):
            ref[0] = val

    pt = _dot_nt(wt_ref[...], hb)
    cst = cst_ref[...]
    snt = snt_ref[...]
    tile = lambda r: jnp.concatenate([r[...]] * (tm // LANE), axis=1)
    qnt, knt = tile(qnt_ref), tile(knt_ref)
    half = HEAD_DIM // 2
    qscale = HEAD_DIM ** -0.5 * LOG2E

    def head(row0):
        return pt[row0:row0 + HEAD_DIM]

    def norm(t, gain):
        return t * lax.rsqrt(jnp.mean(t * t, axis=0, keepdims=True) + EPS) * gain

    def rope(t):
        return t * cst + jnp.concatenate([t[half:], t[:half]], axis=0) * snt

    def put_values(v_ref, row0):
        ones = jnp.ones((v_ref.shape[3] - HEAD_DIM, LANE), v_ref.dtype)
        for g in range(N_KV):
            vt = head(row0 + g * HEAD_DIM)
            for j in range(tm // LANE):
                v_ref[0, g, j, 0:HEAD_DIM, :] = vt[:, j * LANE:(j + 1) * LANE].astype(v_ref.dtype)
                v_ref[0, g, j, HEAD_DIM:, :] = ones

    def put_keys(k_ref, kt_pair):
        k_rows = jnp.concatenate(kt_pair, axis=0).T
        for g in range(N_KV):
            k_ref[0, g] = k_rows[:, g * HEAD_DIM:(g + 1) * HEAD_DIM].astype(k_ref.dtype)

    for h in range(N_HEADS):
        qdt_ref[0, h] = (rope(norm(head(A_QD + h * HEAD_DIM), qnt)) * qscale).astype(qdt_ref.dtype)
        qwt_ref[0, h] = (rope(head(A_QW + h * HEAD_DIM)) * qscale).astype(qwt_ref.dtype)
    put_keys(kd_ref, [rope(norm(head(A_KD + g * HEAD_DIM), knt)) for g in range(N_KV)])
    put_keys(kw_ref, [rope(head(A_KW + g * HEAD_DIM)) for g in range(N_KV)])
    put_values(vdt_ref, A_VD)
    put_values(vwt_ref, A_VW)


def _in_projection(x, ctx, mod6, g_pre, w_row, w_att_t, cst, snt, qn, kn, ssd_cw, ssd_cb, hy_cw, hy_cb):
    b, _, d = x.shape
    tm = ROW_TILE
    hp = tm // SUBLANE
    if ctx is None:
        t = x.shape[1]
        first_lat = 1
        x_spec = pl.BlockSpec((1, tm, d), lambda bb, i: (bb, i, 0))
        ctx_arr, ctx_spec = x, pl.BlockSpec((1, tm, d), lambda bb, i: (bb, 0, 0))
    else:
        assert ctx.shape[1] == tm
        t = x.shape[1] + tm
        first_lat = 0
        x_spec = pl.BlockSpec((1, tm, d), lambda bb, i: (bb, jnp.maximum(i - 1, 0), 0))
        ctx_arr, ctx_spec = ctx, pl.BlockSpec((1, tm, d), lambda bb, i: (bb, 0, 0))
    nrow = t // tm
    last8 = x.shape[1] // SUBLANE - 1
    xp_spec = pl.BlockSpec((1, SUBLANE, d), lambda bb, i: (bb, jnp.clip((i - 1 + first_lat) * hp - 1, 0, last8), 0))
    xn_spec = pl.BlockSpec((1, SUBLANE, d), lambda bb, i: (bb, jnp.clip((i + first_lat) * hp, 0, last8), 0))
    per = tm // LANE
    row = lambda w: pl.BlockSpec((1, tm, w), lambda bb, i: (bb, i, 0))
    lat = pl.BlockSpec((1, tm, W_GROUP), lambda bb, i: (bb, jnp.maximum(i - 1, 0), 0))
    cblk = pl.BlockSpec((1, tm, W_GROUP), lambda bb, i: (bb, 0, 0))
    full = lambda a: pl.BlockSpec(a.shape, lambda bb, i: (0, 0))
    f32 = lambda w: jax.ShapeDtypeStruct((b, t, w), F32)
    lat_shape = jax.ShapeDtypeStruct((b, t - tm, W_GROUP), F32)
    ctx_shape = jax.ShapeDtypeStruct((b, tm, W_GROUP), F32)
    q_spec = pl.BlockSpec((1, N_HEADS, HEAD_DIM, tm), lambda bb, i: (bb, 0, 0, i))
    k_spec = pl.BlockSpec((1, N_KV, tm, HEAD_DIM), lambda bb, i: (bb, 0, i, 0))
    v_spec = pl.BlockSpec((1, N_KV, per, V_ROWS, LANE), lambda bb, i: (bb, 0, i, 0, 0))
    q_shape = jax.ShapeDtypeStruct((b, N_HEADS, HEAD_DIM, t), MXU_DTYPE)
    k_shape = jax.ShapeDtypeStruct((b, N_KV, t, HEAD_DIM), MXU_DTYPE)
    v_shape = jax.ShapeDtypeStruct((b, N_KV, t // LANE, V_ROWS, LANE), MXU_DTYPE)
    gain = lambda v: jnp.broadcast_to(v.reshape(HEAD_DIM, 1), (HEAD_DIM, LANE))
    qnt, knt = gain(qn), gain(kn)
    return pl.pallas_call(
        _inproj_kernel,
        grid=(b, nrow),
        in_specs=[x_spec, xp_spec, xn_spec, ctx_spec,
                  pl.BlockSpec((1, 1, 6, d), lambda bb, i: (bb, jnp.minimum(i, 1), 0, 0)),
                  pl.BlockSpec((1, d), lambda bb, i: (0, 0)),
                  full(w_row), full(w_att_t),
                  pl.BlockSpec((HEAD_DIM, tm), lambda bb, i: (0, i)),
                  pl.BlockSpec((HEAD_DIM, tm), lambda bb, i: (0, i)),
                  full(qnt), full(knt), full(ssd_cw), full(ssd_cb), full(hy_cw), full(hy_cb)],
        out_specs=[row(W_GROUP), row(SSD_XBC), row(LANE), lat, lat, lat, cblk, cblk, cblk,
                   q_spec, k_spec, v_spec, q_spec, k_spec, v_spec],
        out_shape=[f32(W_GROUP), f32(SSD_XBC), f32(LANE), lat_shape, lat_shape, lat_shape,
                   ctx_shape, ctx_shape, ctx_shape,
                   q_shape, k_shape, v_shape, q_shape, k_shape, v_shape],
        scratch_shapes=[pltpu.VMEM((tm + 2 * SUBLANE, C_HY[1] - C_XBC[0]), F32)],
        compiler_params=_cparams(("parallel", "arbitrary")),
        name="in_projection",
    )(x, x, x, ctx_arr, mod6, g_pre, w_row, w_att_t, cst, snt, qnt, knt, ssd_cw, ssd_cb, hy_cw, hy_cb)


def _ssd_kernel(nc, nt, par_ref, uf_ref, dtf_ref, ub_ref, dtb_ref, yf_ref, yb_ref, st_ref):
    j = pl.program_id(1)
    q = CHUNK
    rep = N_HEADS // N_KV

    @pl.when(j == 0)
    def _():
        st_ref[...] = jnp.zeros(st_ref.shape, F32)

    a_all = -jnp.exp(par_ref[0:1, :])
    ri = lax.broadcasted_iota(jnp.int32, (q, q), 0)
    ci = lax.broadcasted_iota(jnp.int32, (q, q), 1)
    dirs = []
    for d, (u_ref, dt_ref) in enumerate(((uf_ref, dtf_ref), (ub_ref, dtb_ref))):
        u = u_ref[0]
        dtv = _softplus(dt_ref[0] + par_ref[1:2, :])
        mask = (ri >= ci) if d == 0 else (ri <= ci)
        cum = _dot_f32(mask.astype(F32), dtv * a_all)
        dirs.append(dict(
            d=d, mask=mask, cum=cum, cum_t=cum.T, dt_t=dtv.T, end=q - 1 if d == 0 else 0,
            xs=u[:, :W_GROUP], bm_t=u[:, W_GROUP:W_GROUP + N_KV * SSD_STATE].T,
            cm=u[:, W_GROUP + N_KV * SSD_STATE:]))

    def head_terms(v, h):
        hl = N_HEADS * v["d"] + h
        g = h // rep
        sl = slice(g * SSD_STATE, (g + 1) * SSD_STATE)
        return dict(col=v["cum"][:, hl:hl + 1], row=v["cum_t"][hl:hl + 1, :], dt_row=v["dt_t"][hl:hl + 1, :],
                    last=v["cum"][v["end"]:v["end"] + 1, hl:hl + 1], cg=v["cm"][:, sl], bg_t=v["bm_t"][sl, :],
                    xh=v["xs"][:, h * HEAD_DIM:(h + 1) * HEAD_DIM])

    terms = [[head_terms(v, h) for h in range(N_HEADS)] for v in dirs]
    scores = [[_dot(ts[g * rep]["cg"], ts[g * rep]["bg_t"]) for g in range(N_KV)] for ts in terms]
    carried = [[_dot(t["cg"], st_ref[v["d"], h]) for h, t in enumerate(ts)] for v, ts in zip(dirs, terms)]
    states = [[_dot(t["bg_t"] * (jnp.exp(t["last"] - t["row"]) * t["dt_row"]), t["xh"]) for t in ts]
              for ts in terms]
    diag = [[_dot(scores[v["d"]][h // rep] * jnp.exp(jnp.where(v["mask"], t["col"] - t["row"], NEG)) * t["dt_row"],
                  t["xh"]) for h, t in enumerate(ts)] for v, ts in zip(dirs, terms)]
    for v, ts, y_ref in zip(dirs, terms, (yf_ref, yb_ref)):
        d = v["d"]
        outs = []
        for h, t in enumerate(ts):
            y = diag[d][h] + carried[d][h] * jnp.exp(t["col"])
            st_ref[d, h] = jnp.exp(t["last"]) * st_ref[d, h] + states[d][h]
            if d == 0:
                y = y + par_ref[2:3, h:h + 1] * t["xh"]
            outs.append(y)
        y_ref[0] = jnp.concatenate(outs, axis=1)


def _ssd(u, dt, par, n_ctx):
    b, t, w = u.shape
    q = CHUNK
    nc, nt = n_ctx // q, t // q
    fwd = lambda j: j
    bwd = lambda j: jnp.where(j < nc, nc - 1 - j, nt + nc - 1 - j)

    def specs(cmap):
        return [pl.BlockSpec((1, q, w), lambda bb, j: (bb, cmap(j), 0)),
                pl.BlockSpec((1, q, LANE), lambda bb, j: (bb, cmap(j), 0))]

    return pl.pallas_call(
        functools.partial(_ssd_kernel, nc, nt),
        grid=(b, nt),
        in_specs=[pl.BlockSpec((SUBLANE, LANE), lambda bb, j: (0, 0))] + specs(fwd) + specs(bwd),
        out_specs=[pl.BlockSpec((1, q, W_GROUP), lambda bb, j: (bb, fwd(j), 0)),
                   pl.BlockSpec((1, q, W_GROUP), lambda bb, j: (bb, bwd(j), 0))],
        out_shape=[jax.ShapeDtypeStruct((b, t, W_GROUP), F32)] * 2,
        scratch_shapes=[pltpu.VMEM((2, N_HEADS, SSD_STATE, HEAD_DIM), F32)],
        compiler_params=_cparams(("parallel", "arbitrary")),
        name="ssd_scan",
    )(par, u, dt, u, dt)


def _filter_kernel(zf_ref, zb_ref, w1_ref, b1_ref, f1_ref, w2_ref, b2_ref, f2_ref,
                   w3f_ref, w3b_ref, b3f_ref, b3b_ref, dl_ref, o_ref, ss_ref):
    i = pl.program_id(0)
    tr = zf_ref.shape[0]

    def half(z_ref, w3_ref, b3_ref):
        z = z_ref[...]
        h = jnp.sin(f1_ref[...] * (_dot_split(z, w1_ref[...]) + b1_ref[...]))
        h = jnp.sin(f2_ref[...] * (_dot_split(h, w2_ref[...]) + b2_ref[...]))
        k = _dot_split(h, w3_ref[...]) + b3_ref[...]
        return k * jnp.exp(-z[:, 0:1] * dl_ref[...])

    kf = half(zf_ref, w3f_ref, b3f_ref)
    kb = half(zb_ref, w3b_ref, b3b_ref)

    @pl.when(i == 0)
    def _():
        ss_ref[...] = jnp.zeros(ss_ref.shape, F32)

    ss_ref[...] += jnp.sum(kf * kf + kb * kb, axis=0, keepdims=True)
    rows = i * tr + lax.broadcasted_iota(jnp.int32, (tr, 1), 0)
    o_ref[0] = kf
    o_ref[1] = jnp.where(rows == 0, 0.0, kb)


def _hyena_filters(n, w1, b1, f1, w2, b2, f2, w3, b3):
    pos = np.arange(n, dtype=np.float64)
    t = np.linspace(0.0, 1.0, n)
    f = np.linspace(1e-4, HY_BANDS - 1, HY_BANDS)
    ang = 2.0 * math.pi * pos[:, None] * f[None, :] / n
    emb = np.concatenate([t[:, None], np.cos(ang), -np.sin(ang)], axis=-1)
    emb = np.pad(emb, ((0, 0), (0, LANE - HY_EMB)))
    emb_b = np.roll(np.flip(emb, axis=0), 1, axis=0)
    emb, emb_b = jnp.asarray(emb, F32), jnp.asarray(emb_b, F32)
    w1p = jnp.pad(w1, ((0, LANE - HY_EMB), (0, 0)))
    w3r = w3.reshape(HY_FILT, HY_ORDER, 2, W_GROUP)
    b3r = b3.reshape(HY_ORDER, 2, W_GROUP)
    wc = HY_ORDER * W_GROUP
    w3f, w3b = w3r[:, :, 0].reshape(HY_FILT, wc), w3r[:, :, 1].reshape(HY_FILT, wc)
    b3f, b3b = b3r[:, 0].reshape(1, wc), b3r[:, 1].reshape(1, wc)
    deltas = np.abs(np.linspace(HY_MIN_DECAY, HY_MAX_DECAY, W_GROUP))
    deltas = jnp.asarray(np.tile(deltas, HY_ORDER).reshape(1, wc), F32)
    tr = math.gcd(n, 512)
    const = lambda s: pl.BlockSpec(s, lambda i: (0, 0))
    rows = pl.BlockSpec((tr, LANE), lambda i: (i, 0))
    return pl.pallas_call(
        _filter_kernel,
        grid=(n // tr,),
        in_specs=[rows, rows, const((LANE, HY_FILT)), const((1, HY_FILT)), const((1, HY_FILT)),
                  const((HY_FILT, HY_FILT)), const((1, HY_FILT)), const((1, HY_FILT)),
                  const((HY_FILT, wc)), const((HY_FILT, wc)), const((1, wc)), const((1, wc)), const((1, wc))],
        out_specs=[pl.BlockSpec((2, tr, wc), lambda i: (0, i, 0)), const((1, wc))],
        out_shape=[jax.ShapeDtypeStruct((2, n, wc), F32), jax.ShapeDtypeStruct((1, wc), F32)],
        compiler_params=_cparams(("arbitrary",)),
        name="hyena_filter",
    )(emb, emb_b, w1p, b1.reshape(1, -1), f1.reshape(1, -1), w2, b2.reshape(1, -1), f2.reshape(1, -1),
      w3f, w3b, b3f, b3b, deltas)


def _dft_tables(n):
    nn = 2 * n
    n1, n2 = nn // LANE, LANE
    a = n1 // 2
    k1 = np.arange(n1)[:, None]
    j1 = np.arange(n1)[None, :]
    ang1 = 2.0 * np.pi * (k1 * j1 % n1) / n1
    fr, fi = np.cos(ang1), -np.sin(ang1)
    m1 = np.block([[fr[:, :a], -fi[:, :a]], [fi[:, :a], fr[:, :a]]])
    m1_real = np.concatenate([fr, fi], axis=0)
    er, ei = fr.T[:a], -fi.T[:a]
    m3 = np.stack([np.concatenate([er, -ei], axis=1), np.concatenate([ei, er], axis=1)]) / nn
    k2 = np.arange(n2)[:, None]
    j2 = np.arange(n2)[None, :]
    ang2 = 2.0 * np.pi * (k2 * j2 % n2) / n2
    f2 = np.stack([np.cos(ang2), -np.sin(ang2)])
    angt = 2.0 * np.pi * (np.arange(n1)[:, None] * j2) / nn
    tw = np.stack([np.cos(angt), -np.sin(angt)], axis=1)
    return tuple(jnp.asarray(t, F32) for t in (m1, m1_real, m3, f2, tw))


def _fft_first_kernel(ur_ref, ui_ref, m_ref, o_ref, acc_ref):
    n1 = o_ref.shape[2]
    for s in range(ur_ref.shape[2]):
        r = _dot(m_ref[...], jnp.concatenate([ur_ref[0, :, s, :], ui_ref[0, :, s, :]], axis=0))
        acc_ref[0, :, s, :] = r[:n1]
        acc_ref[1, :, s, :] = r[n1:]
    o_ref[0] = acc_ref[...].astype(o_ref.dtype)


def _fft_first(u, m1):
    b2, a, _, c = u.shape
    p = b2 // 2
    n1 = m1.shape[0] // 2
    js = FFT_JS
    return pl.pallas_call(
        _fft_first_kernel,
        grid=(p, LANE // js),
        in_specs=[pl.BlockSpec((1, a, js, c), lambda pp, j: (2 * pp, 0, j, 0)),
                  pl.BlockSpec((1, a, js, c), lambda pp, j: (2 * pp + 1, 0, j, 0)),
                  pl.BlockSpec(m1.shape, lambda pp, j: (0, 0))],
        out_specs=pl.BlockSpec((1, 2, n1, js, c), lambda pp, j: (pp, 0, 0, j, 0)),
        out_shape=jax.ShapeDtypeStruct((p, 2, n1, LANE, c), MXU_DTYPE),
        scratch_shapes=[pltpu.VMEM((2, n1, js, c), F32)],
        compiler_params=_cparams(("parallel", "arbitrary")),
        name="fft_first",
    )(u, u, m1)


def _twiddled_dft(f2_ref, tw_ref, u):
    fr, fi = f2_ref[0], f2_ref[1]
    tr, ti = tw_ref[u, 0:1, :], tw_ref[u, 1:2, :]
    return fr * tr - fi * ti, fr * ti + fi * tr


def _real_form(gr, gi):
    return jnp.concatenate([jnp.concatenate([gr, -gi], axis=1), jnp.concatenate([gi, gr], axis=1)], axis=0)


def _spectrum_kernel(a_ref, ss_ref, f2_ref, tw_ref, h_ref):
    scale = lax.rsqrt(ss_ref[...] + EPS)
    for u in range(a_ref.shape[2]):
        xin = jnp.concatenate([a_ref[0, 0, u], a_ref[0, 1, u]], axis=0)
        x = _dot(_real_form(*_twiddled_dft(f2_ref, tw_ref, u)), xin) * scale
        h_ref[0, u] = x[:LANE]
        h_ref[1, u] = x[LANE:]


def _filter_spectrum(a5, ss, f2, tw):
    _, _, n1, _, c = a5.shape
    kb = math.gcd(n1, FFT_KB)
    return pl.pallas_call(
        _spectrum_kernel,
        grid=(n1 // kb,),
        in_specs=[pl.BlockSpec((1, 2, kb, LANE, c), lambda k: (0, 0, k, 0, 0)),
                  pl.BlockSpec((1, c), lambda k: (0, 0)),
                  pl.BlockSpec((2, LANE, LANE), lambda k: (0, 0, 0)),
                  pl.BlockSpec((kb, 2, LANE), lambda k: (k, 0, 0))],
        out_specs=pl.BlockSpec((2, kb, LANE, c), lambda k: (0, k, 0, 0)),
        out_shape=jax.ShapeDtypeStruct((2, n1, LANE, c), F32),
        compiler_params=_cparams(("arbitrary",)),
        name="filter_spectrum",
    )(a5, ss, f2, tw)


def _fft_mid_kernel(a_ref, h_ref, f2_ref, tw_ref, v_ref):
    npair = a_ref.shape[0]
    c = a_ref.shape[4]
    kb = a_ref.shape[2]
    gs = [_twiddled_dft(f2_ref, tw_ref, u) for u in range(kb)]
    xs = []
    for u, (gr, gi) in enumerate(gs):
        xin = jnp.concatenate(
            [jnp.concatenate([a_ref[p, 0, u], a_ref[p, 1, u]], axis=0) for p in range(npair)], axis=1)
        xs.append(_dot(_real_form(gr, gi), xin))
    vs = []
    for u, ((gr, gi), x) in enumerate(zip(gs, xs)):
        xr, xi = x[:LANE], x[LANE:]
        hr = jnp.concatenate([h_ref[0, u]] * npair, axis=1)
        hi = jnp.concatenate([h_ref[1, u]] * npair, axis=1)
        y = jnp.concatenate([xr * hr - xi * hi, xr * hi + xi * hr], axis=0)
        vs.append(_dot(_real_form(gr.T, -gi.T), y))
    for u, v in enumerate(vs):
        for p in range(npair):
            v_ref[p, 0, u] = v[:LANE, p * c:(p + 1) * c].astype(v_ref.dtype)
            v_ref[p, 1, u] = v[LANE:, p * c:(p + 1) * c].astype(v_ref.dtype)


def _fft_mid(a5, hspec, order, f2, tw):
    npair, _, n1, _, c = a5.shape
    kb = math.gcd(n1, FFT_KB)
    blk = pl.BlockSpec((npair, 2, kb, LANE, c), lambda k: (0, 0, k, 0, 0))
    return pl.pallas_call(
        _fft_mid_kernel,
        grid=(n1 // kb,),
        in_specs=[blk,
                  pl.BlockSpec((2, kb, LANE, c), lambda k: (0, k, 0, order)),
                  pl.BlockSpec((2, LANE, LANE), lambda k: (0, 0, 0)),
                  pl.BlockSpec((kb, 2, LANE), lambda k: (k, 0, 0))],
        out_specs=blk,
        out_shape=jax.ShapeDtypeStruct(a5.shape, MXU_DTYPE),
        compiler_params=_cparams(("arbitrary",)),
        name="fft_mid",
    )(a5, hspec, f2, tw)


def _fft_last_kernel(v_ref, m_ref, z_ref, gate_ref, bias_ref, o_ref, vf_ref):
    vf_ref[...] = v_ref[0].astype(F32)
    for s in range(z_ref.shape[2]):
        vs = jnp.concatenate([vf_ref[0, :, s, :], vf_ref[1, :, s, :]], axis=0)
        zf = _dot(m_ref[0], vs)
        o_ref[0, :, s, :] = gate_ref[0, :, s, :] * (zf + z_ref[0, :, s, :] * bias_ref[...])


def _fft_last(v5, m3, z, gate, bias):
    b, a, _, c = z.shape
    n1 = v5.shape[2]
    js = FFT_JS
    row = pl.BlockSpec((1, a, js, c), lambda j, bb: (bb, 0, j, 0))
    return pl.pallas_call(
        _fft_last_kernel,
        grid=(LANE // js, b),
        in_specs=[pl.BlockSpec((1, 2, n1, js, c), lambda j, bb: (bb // 2, 0, 0, j, 0)),
                  pl.BlockSpec((1, a, 2 * n1), lambda j, bb: (bb % 2, 0, 0)),
                  row, row,
                  pl.BlockSpec((1, c), lambda j, bb: (0, 0))],
        out_specs=row,
        out_shape=jax.ShapeDtypeStruct(z.shape, F32),
        scratch_shapes=[pltpu.VMEM((2, n1, js, c), F32)],
        compiler_params=_cparams(("parallel", "arbitrary")),
        name="fft_last",
    )(v5, m3, z, gate, bias)


def _hyena_long(v, x1, x2, filt, ss, bias):
    b, n, c = v.shape
    a = n // LANE
    m1, m1_real, m3, f2, tw = _dft_tables(n)
    hspec = _filter_spectrum(_fft_first(filt.reshape(2, a, LANE, filt.shape[2]), m1_real), ss, f2, tw)
    z = v.reshape(b, a, LANE, c)
    gates = (x1.reshape(b, a, LANE, c), x2.reshape(b, a, LANE, c))
    for o in range(HY_ORDER):
        z = _fft_last(_fft_mid(_fft_first(z, m1), hspec, o, f2, tw), m3, z, gates[o], bias[o].reshape(1, c))
    return z.reshape(b, n, c)


def _hyena_ctx_kernel(v_ref, x1_ref, x2_ref, filt_ref, ss_ref, bias_ref, ff_ref, fc_ref, fi_ref, o_ref):
    b, n, c = v_ref.shape
    npair = b // 2
    hs = _dot(ff_ref[...], jnp.concatenate([filt_ref[0], filt_ref[1]], axis=0))
    hs = hs * lax.rsqrt(ss_ref[...] + EPS)
    z = [v_ref[i] for i in range(b)]
    gates = (x1_ref, x2_ref)
    for o in range(HY_ORDER):
        hr = jnp.concatenate([hs[:2 * n, o * c:(o + 1) * c]] * npair, axis=1)
        hi = jnp.concatenate([hs[2 * n:, o * c:(o + 1) * c]] * npair, axis=1)
        xin = jnp.concatenate([jnp.concatenate([z[2 * p] for p in range(npair)], axis=1),
                               jnp.concatenate([z[2 * p + 1] for p in range(npair)], axis=1)], axis=0)
        x = _dot(fc_ref[...], xin)
        xr, xi = x[:2 * n], x[2 * n:]
        y = jnp.concatenate([xr * hr - xi * hi, xr * hi + xi * hr], axis=0)
        zf = _dot(fi_ref[...], y)
        bo = bias_ref[o:o + 1, :]
        for i in range(b):
            p, part = i // 2, i % 2
            conv = zf[part * n:(part + 1) * n, p * c:(p + 1) * c]
            z[i] = gates[o][i] * (conv + z[i] * bo)
    for i in range(b):
        o_ref[i] = z[i]


def _hyena_ctx(v, x1, x2, filt, ss, bias):
    b, n, c = v.shape
    nn = 2 * n
    k = np.arange(nn)[:, None]
    j = np.arange(nn)[None, :]
    ang = 2.0 * np.pi * (k * j % nn) / nn
    fr, fi = np.cos(ang), -np.sin(ang)
    ff = np.concatenate([fr, fi], axis=0)
    fc = np.block([[fr[:, :n], -fi[:, :n]], [fi[:, :n], fr[:, :n]]])
    er, ei = fr[:n], -fi[:n]
    finv = np.block([[er, -ei], [ei, er]]) / nn
    return pl.pallas_call(
        _hyena_ctx_kernel,
        out_shape=jax.ShapeDtypeStruct((b, n, c), F32),
        compiler_params=pltpu.CompilerParams(vmem_limit_bytes=VMEM_LIMIT),
        name="hyena_ctx",
    )(v, x1, x2, filt, ss, bias, jnp.asarray(ff, F32), jnp.asarray(fc, F32), jnp.asarray(finv, F32))


def _window_kernel(n_ctx, sink_ref, q_ref, k_ref, v_ref, o_ref):
    g = pl.program_id(1)
    i = pl.program_id(2)
    rep, tq = q_ref.shape[1], q_ref.shape[3]
    t = k_ref.shape[2]
    ql, win = ATT_QL, CHUNK
    wk = ql + 2 * win
    ids = [(c, h) for c in range(tq // ql) for h in range(rep)]
    q0 = [i * tq + c * ql for c in range(tq // ql)]
    start = [pl.multiple_of(jnp.clip(q - win, 0, t - wk), LANE) for q in q0]

    kx = k_ref[0, 0, 0:n_ctx, :]
    vx = jnp.concatenate([v_ref[0, 0, u] for u in range(n_ctx // LANE)], axis=1)
    kl = [k_ref[0, 0, pl.ds(s, wk), :] for s in start]
    vl = [jnp.concatenate([v_ref[0, 0, s // LANE + u] for u in range(wk // LANE)], axis=1) for s in start]

    qs = [q_ref[0, h, :, c * ql:(c + 1) * ql] for c, h in ids]
    s_loc = [jnp.dot(kl[c], q, preferred_element_type=F32) for (c, h), q in zip(ids, qs)]
    s_ctx = [jnp.dot(kx, q, preferred_element_type=F32) for q in qs]

    diff = lax.broadcasted_iota(jnp.int32, (wk, ql), 0) - lax.broadcasted_iota(jnp.int32, (wk, ql), 1)
    krow = lax.broadcasted_iota(jnp.int32, (wk, 1), 0)
    p_loc, p_ctx, e_snk = [], [], []
    for n, (c, h) in enumerate(ids):
        d = diff + (start[c] - q0[c])
        ok = jnp.logical_and(jnp.abs(d) <= win, krow >= n_ctx - start[c])
        ok = jnp.logical_and(ok, q0[c] >= n_ctx)
        sl = jnp.where(ok, s_loc[n], NEG)
        snk = jnp.where(g == 0, sink_ref[0:1, h:h + 1], sink_ref[0:1, rep + h:rep + h + 1]) * LOG2E
        m = jnp.maximum(jnp.maximum(jnp.max(sl, axis=0, keepdims=True),
                                    jnp.max(s_ctx[n], axis=0, keepdims=True)), snk)
        p_loc.append(jnp.exp2(sl - m).astype(MXU_DTYPE))
        p_ctx.append(jnp.exp2(s_ctx[n] - m).astype(MXU_DTYPE))
        e_snk.append(jnp.exp2(snk - m))

    acc = [jnp.dot(vl[c], p_loc[n], preferred_element_type=F32)
           + jnp.dot(vx, p_ctx[n], preferred_element_type=F32) for n, (c, h) in enumerate(ids)]
    outs = [a[:HEAD_DIM] * (1.0 / (a[HEAD_DIM:HEAD_DIM + 1] + e)) for a, e in zip(acc, e_snk)]
    for c in range(tq // ql):
        o_ref[0, c * ql:(c + 1) * ql, :] = jnp.concatenate(outs[c * rep:(c + 1) * rep], axis=0).T


def _window_attention(qt, k, vt, sink, n_ctx):
    b, _, hd, t = qt.shape
    rep = N_HEADS // N_KV
    tq = ATT_TQ if t % ATT_TQ == 0 else ATT_QL
    assert n_ctx == ATT_QL and t % ATT_QL == 0 and rep * hd == LANE and t >= ATT_QL + 2 * CHUNK
    return pl.pallas_call(
        functools.partial(_window_kernel, n_ctx),
        grid=(b, N_KV, t // tq),
        in_specs=[pl.BlockSpec((SUBLANE, LANE), lambda bb, g, i: (0, 0)),
                  pl.BlockSpec((1, rep, hd, tq), lambda bb, g, i: (bb, g, 0, i)),
                  pl.BlockSpec((1, 1, t, hd), lambda bb, g, i: (bb, g, 0, 0)),
                  pl.BlockSpec((1, 1) + vt.shape[2:], lambda bb, g, i: (bb, g, 0, 0, 0))],
        out_specs=pl.BlockSpec((1, tq, rep * hd), lambda bb, g, i: (bb, i, g)),
        out_shape=jax.ShapeDtypeStruct((b, t, N_HEADS * hd), F32),
        compiler_params=_cparams(("parallel", "parallel", "arbitrary")),
        name="window_attention",
    )(sink, qt, k, vt)


def _dense_kernel(n_ctx, q_ref, k_ref, v_ref, o_ref, m_ref, alpha_ref, acc_ref, p_ref):
    i = pl.program_id(2)
    rep, tq = q_ref.shape[1], q_ref.shape[3]
    ql = ATT_QL
    nc = tq // ql

    def run(chunks, kt, nk):
        per = kt // LANE
        ids = [(c, h) for c in chunks for h in range(rep)]
        for c, h in ids:
            m_ref[c * rep + h] = jnp.full((1, ql), NEG, F32)
            alpha_ref[c * rep + h] = jnp.ones((1, ql), F32)
            acc_ref[c * rep + h] = jnp.zeros(acc_ref.shape[1:], F32)
            p_ref[c * rep + h, 0:kt] = jnp.zeros((kt, ql), p_ref.dtype)

        def scores(j):
            kb = k_ref[0, 0, pl.ds(pl.multiple_of(j * kt, kt), kt), :]
            return [jnp.dot(kb, q_ref[0, h, :, c * ql:(c + 1) * ql], preferred_element_type=F32) for c, h in ids]

        def values(j):
            vb = jnp.concatenate([v_ref[0, 0, j * per + u] for u in range(per)], axis=1)
            pvs = [jnp.dot(vb, p_ref[c * rep + h, 0:kt], preferred_element_type=F32) for c, h in ids]
            for (c, h), pv in zip(ids, pvs):
                n = c * rep + h
                acc_ref[n] = alpha_ref[n] * acc_ref[n] + pv

        def softmax(ss):
            for (c, h), s in zip(ids, ss):
                n = c * rep + h
                m = m_ref[n]
                mn = jnp.maximum(m, jnp.max(s, axis=0, keepdims=True))
                p_ref[n, 0:kt] = jnp.exp2(s - mn).astype(p_ref.dtype)
                alpha_ref[n] = jnp.exp2(m - mn)
                m_ref[n] = mn

        def body(j, carry):
            ss = scores(j)
            values(jnp.maximum(j - 1, 0))
            softmax(ss)
            return carry

        lax.fori_loop(0, nk, body, 0)
        values(nk - 1)
        for c in chunks:
            o = [acc_ref[c * rep + h] for h in range(rep)]
            o = [a[:HEAD_DIM] * (1.0 / a[HEAD_DIM:HEAD_DIM + 1]) for a in o]
            o_ref[0, c * ql:(c + 1) * ql, :] = jnp.concatenate(o, axis=0).T

    kt_all = p_ref.shape[1]
    nk_all = k_ref.shape[2] // kt_all

    @pl.when(i == 0)
    def _():
        run([0], n_ctx, 1)
        if nc > 1:
            run(list(range(1, nc)), kt_all, nk_all)

    @pl.when(i != 0)
    def _():
        run(list(range(nc)), kt_all, nk_all)


def _dense_attention(qt, k, vt, n_ctx):
    b, _, hd, t = qt.shape
    rep = N_HEADS // N_KV
    tq = ATT_TQ if t % ATT_TQ == 0 else ATT_QL
    kt = ATT_KT if t % ATT_KT == 0 else ATT_QL
    assert n_ctx == ATT_QL and t % ATT_QL == 0 and rep * hd == LANE
    return pl.pallas_call(
        functools.partial(_dense_kernel, n_ctx),
        grid=(b, N_KV, t // tq),
        in_specs=[pl.BlockSpec((1, rep, hd, tq), lambda bb, g, i: (bb, g, 0, i)),
                  pl.BlockSpec((1, 1, t, hd), lambda bb, g, i: (bb, g, 0, 0)),
                  pl.BlockSpec((1, 1) + vt.shape[2:], lambda bb, g, i: (bb, g, 0, 0, 0))],
        out_specs=pl.BlockSpec((1, tq, rep * hd), lambda bb, g, i: (bb, i, g)),
        out_shape=jax.ShapeDtypeStruct((b, t, N_HEADS * hd), F32),
        scratch_shapes=[pltpu.VMEM((rep * tq // ATT_QL, 1, ATT_QL), F32),
                        pltpu.VMEM((rep * tq // ATT_QL, 1, ATT_QL), F32),
                        pltpu.VMEM((rep * tq // ATT_QL, vt.shape[3], ATT_QL), F32),
                        pltpu.VMEM((rep * tq // ATT_QL, kt, ATT_QL), MXU_DTYPE)],
        compiler_params=_cparams(("parallel", "parallel", "arbitrary")),
        name="dense_attention",
    )(qt, k, vt)


def _mix_mlp_kernel(first, x_ref, ctx_ref, mod_ref, yf_ref, yb_ref, z_ref, gs_ref, hy_ref, hyc_ref,
                    yw_ref, yd_ref, gpost_ref, gpre_ref, gpost2_ref, wo_ref, w1_ref, w2_ref, o_ref):
    i = pl.program_id(1)
    if first:
        xv = jnp.where(i == 0, ctx_ref[0], x_ref[0])
        yh = jnp.where(i == 0, hyc_ref[0], hy_ref[0])
    else:
        xv = x_ref[0]
        yh = hy_ref[0]
    g1 = mod_ref[0, 0, 2:3, :]
    sh2 = mod_ref[0, 0, 3:4, :]
    sc2 = mod_ref[0, 0, 4:5, :]
    g2 = mod_ref[0, 0, 5:6, :]
    ya = _rms((yf_ref[0] + yb_ref[0]) * _silu(z_ref[0]), gs_ref[...])
    w = W_GROUP
    tm, d = xv.shape
    halves = [slice(r, r + tm // MLP_SPLIT) for r in range(0, tm, tm // MLP_SPLIT)]
    branches = (ya, yh, yw_ref[0], yd_ref[0])
    ys = [sum(_dot(br[rows], wo_ref[j * w:(j + 1) * w, :]) for j, br in enumerate(branches)) for rows in halves]
    x1s = [xv[rows] + g1 * _rms(y, gpost_ref[...]) for rows, y in zip(halves, ys)]
    hbs = [(_rms(x1, gpre_ref[...]) * (1.0 + sc2) + sh2).astype(MXU_DTYPE) for x1 in x1s]
    accs = [jnp.zeros((tm // MLP_SPLIT, d), F32) for _ in halves]
    for c in range(w1_ref.shape[1] // d):
        hidden = [jnp.maximum(jnp.dot(hb, w1_ref[:, c * d:(c + 1) * d], preferred_element_type=F32), 0.0)
                  for hb in hbs]
        accs = [acc + _dot(a * a, w2_ref[c * d:(c + 1) * d, :]) for acc, a in zip(accs, hidden)]
    for rows, x1, acc in zip(halves, x1s, accs):
        o_ref[0, rows] = x1 + g2 * _rms(acc, gpost2_ref[...])


def _mix_mlp(first, x, ctx, mod6, yf, yb, z, g_ssd, hy, hyc, yw, yd, g_post, g_pre2, g_post2, wo, w1, w2):
    b, t, _ = yf.shape
    d = wo.shape[1]
    tm = ROW_TILE
    off = 0 if first else 1
    nrow = t // tm - off
    if first:
        x_spec = pl.BlockSpec((1, tm, d), lambda bb, i: (bb, jnp.maximum(i - 1, 0), 0))
        hy_spec = pl.BlockSpec((1, tm, W_GROUP), lambda bb, i: (bb, jnp.maximum(i - 1, 0), 0))
        mod_spec = pl.BlockSpec((1, 1, 6, d), lambda bb, i: (bb, jnp.minimum(i, 1), 0, 0))
    else:
        x_spec = pl.BlockSpec((1, tm, d), lambda bb, i: (bb, i + 1, 0))
        hy_spec = pl.BlockSpec((1, tm, W_GROUP), lambda bb, i: (bb, i, 0))
        mod_spec = pl.BlockSpec((1, 1, 6, d), lambda bb, i: (bb, 1, 0, 0))
    first_blk = lambda w: pl.BlockSpec((1, tm, w), lambda bb, i: (bb, 0, 0))
    row = lambda w: pl.BlockSpec((1, tm, w), lambda bb, i: (bb, i + off, 0))
    vec = lambda w: pl.BlockSpec((1, w), lambda bb, i: (0, 0))
    full = lambda a: pl.BlockSpec(a.shape, lambda bb, i: (0, 0))
    return pl.pallas_call(
        functools.partial(_mix_mlp_kernel, first),
        grid=(b, nrow),
        in_specs=[x_spec, first_blk(d), mod_spec, row(W_GROUP), row(W_GROUP), row(W_GROUP), vec(W_GROUP),
                  hy_spec, first_blk(W_GROUP), row(W_GROUP), row(W_GROUP), vec(d), vec(d), vec(d),
                  full(wo), full(w1), full(w2)],
        out_specs=pl.BlockSpec((1, tm, d), lambda bb, i: (bb, i, 0)),
        out_shape=jax.ShapeDtypeStruct((b, nrow * tm, d), F32),
        compiler_params=_cparams(("parallel", "arbitrary")),
        name="mix_mlp",
    )(x, ctx, mod6, yf, yb, z, g_ssd, hy, hyc, yw, yd, g_post, g_pre2, g_post2, wo, w1, w2)


def _rope_tables(n, n_ctx):
    rows = n // GRID_W
    row = np.repeat(np.arange(rows, dtype=np.float64), GRID_W)
    col = np.tile(np.arange(GRID_W, dtype=np.float64), rows)
    n_freq = HEAD_DIM // 4
    inv = ROPE_THETA ** (-np.arange(n_freq, dtype=np.float64) / n_freq)
    ang = np.concatenate([row[:, None] * inv, col[:, None] * inv], axis=-1)
    cos, sin = np.cos(ang), np.sin(ang)
    cs = np.concatenate([np.ones((n_ctx, HEAD_DIM)), np.concatenate([cos, cos], axis=1)], axis=0)
    sn = np.concatenate([np.zeros((n_ctx, HEAD_DIM)), np.concatenate([-sin, sin], axis=1)], axis=0)
    return jnp.asarray(cs.T, F32), jnp.asarray(sn.T, F32)


def _pad_rows(a, rows):
    return jnp.pad(a, ((0, rows - a.shape[0]), (0, 0)))


def kernel(x, c, ctx, c_ctx, w_mod, b_mod, norm_mix_pre, norm_mix_post, norm_mlp_pre, norm_mlp_post, w_in, w_out, ssd_conv_w, ssd_conv_b, ssd_a_log, ssd_dt_bias, ssd_d, ssd_norm, hy_conv_w, hy_conv_b, hy_w1, hy_b1, hy_freq1, hy_w2, hy_b2, hy_freq2, hy_w3, hy_b3, hy_bias, attn_sink, q_norm, k_norm, mlp_w1, mlp_w2):
    b, n, d = x.shape
    n_ctx = ctx.shape[1]
    depth = w_mod.shape[0]
    assert n_ctx == ROW_TILE and n % ROW_TILE == 0 and b % 2 == 0 and b + 1 <= SUBLANE

    cc = _pad_rows(jnp.concatenate([c, c_ctx[None, :]], axis=0), SUBLANE)
    mod = _modulation(cc, w_mod, b_mod)
    cs, sn = _rope_tables(n, n_ctx)

    off_b = W_GROUP + SSD_XBC + 2 * N_HEADS
    xall = None
    for l in range(depth):
        first = l == 0
        need_ctx = l < depth - 1
        mod_lat = mod[l, :b].reshape(b, 1, 6, d)
        mod_ctx = jnp.broadcast_to(mod[l, b].reshape(1, 1, 6, d), (b, 1, 6, d))
        mod6 = jnp.concatenate([mod_ctx, mod_lat], axis=1)

        wl = w_in[l]
        off_c = off_b + 3 * W_GROUP
        off_d = off_c + (N_HEADS + 2 * N_KV) * HEAD_DIM
        w_row = jnp.concatenate(
            [wl[:, :W_GROUP + SSD_XBC], wl[:, off_b:off_c], wl[:, W_GROUP + SSD_XBC:off_b],
             jnp.zeros((d, D_ROW_PAD - off_c), F32)], axis=1).astype(MXU_DTYPE)
        w_att_t = jnp.concatenate([wl[:, off_d:], wl[:, off_c:off_d]], axis=1).T.astype(MXU_DTYPE)
        xin, cin = (x, ctx) if first else (xall, None)
        (z, u, dt, v, x1, x2, vc, x1c, x2c, qwt, kw, vwt, qdt, kd, vdt) = _in_projection(
            xin, cin, mod6, norm_mix_pre[l].reshape(1, d), w_row, w_att_t, cs, sn, q_norm[l], k_norm[l],
            ssd_conv_w[l], ssd_conv_b[l].reshape(1, -1), hy_conv_w[l], hy_conv_b[l].reshape(1, -1))

        par = jnp.zeros((SUBLANE, LANE), F32)
        par = par.at[0, :2 * N_HEADS].set(ssd_a_log[l].reshape(-1))
        par = par.at[1, :2 * N_HEADS].set(ssd_dt_bias[l].reshape(-1))
        par = par.at[2, :N_HEADS].set(ssd_d[l])
        yf, yb = _ssd(u, dt, par, n_ctx)

        filt_args = (hy_w1[l], hy_b1[l], hy_freq1[l], hy_w2[l], hy_b2[l], hy_freq2[l], hy_w3[l], hy_b3[l])
        yhy = _hyena_long(v, x1, x2, *_hyena_filters(n, *filt_args), hy_bias[l])
        if need_ctx:
            yhy_ctx = _hyena_ctx(vc, x1c, x2c, *_hyena_filters(n_ctx, *filt_args), hy_bias[l])
        else:
            yhy_ctx = yhy

        sink = jnp.zeros((SUBLANE, LANE), F32).at[0, :N_HEADS].set(attn_sink[l])
        yw = _window_attention(qwt, kw, vwt, sink, n_ctx)
        yd = _dense_attention(qdt, kd, vdt, n_ctx)

        xres, cres = (x, ctx) if first else (xall, xall)
        xall = _mix_mlp(first, xres, cres, mod6, yf, yb, z, ssd_norm[l].reshape(1, -1), yhy, yhy_ctx, yw, yd,
                        norm_mix_post[l].reshape(1, d), norm_mlp_pre[l].reshape(1, d),
                        norm_mlp_post[l].reshape(1, d), w_out[l].astype(MXU_DTYPE),
                        mlp_w1[l].astype(MXU_DTYPE), mlp_w2[l].astype(MXU_DTYPE))
    return xall
```

```python
import functools
import math

import numpy as np
import jax
import jax.numpy as jnp
from jax import lax
from jax.experimental import pallas as pl
from jax.experimental.pallas import tpu as pltpu

F32 = jnp.float32
MXU_DTYPE = jnp.bfloat16

EPS = 1e-6
HEAD_DIM = 64
GRID_W = 64
ROPE_THETA = 10000.0
N_HEADS = 4
N_KV = 2
W_GROUP = N_HEADS * HEAD_DIM
SSD_STATE = 64
SSD_XBC = W_GROUP + 2 * N_KV * SSD_STATE
HY_ORDER = 2
HY_BANDS = 16
HY_EMB = 2 * HY_BANDS + 1
HY_FILT = 64
HY_MAX_DECAY = math.log(1e-2) / 0.3
HY_MIN_DECAY = math.log(1e-2) / 1.5
CHUNK = 128
LANE = 128
SUBLANE = 8
ROW_TILE = 256
ATT_TQ = 768
DENSE_TQ = 2816
FFT_JS = 16
FFT_KB = 4
MLP_SPLIT = 2
VMEM_LIMIT = 56 * 1024 * 1024
NEG = -1e30

C_Z = (0, 256)
C_XBC = (256, 768)
C_HY = (768, 1536)
C_DT = (1536, 1664)
D_ROW_PAD = 1664
A_QD, A_KD, A_VD = 0, 256, 384
A_QW, A_KW, A_VW = 512, 768, 896
V_ROWS = HEAD_DIM + 16
ATT_QL = 256
ATT_KT = 768
LOG2E = math.log2(math.e)


def _cparams(sem):
    return pltpu.CompilerParams(dimension_semantics=sem, vmem_limit_bytes=VMEM_LIMIT)


def _rms(x, g):
    return x * lax.rsqrt(jnp.mean(x * x, axis=-1, keepdims=True) + EPS) * g


def _silu(x):
    return x * (1.0 / (1.0 + jnp.exp(-x)))


def _softplus(x):
    return jnp.maximum(x, 0.0) + jnp.log(1.0 + jnp.exp(-jnp.abs(x)))


def _dot(a, b):
    return jnp.dot(a.astype(MXU_DTYPE), b.astype(MXU_DTYPE), preferred_element_type=F32)


def _dot_nt(a, b):
    return lax.dot_general(a.astype(MXU_DTYPE), b.astype(MXU_DTYPE), (((1,), (1,)), ((), ())),
                           preferred_element_type=F32)


def _dot_f32(a, b):
    return jnp.dot(a, b, preferred_element_type=F32, precision=lax.Precision.HIGHEST)


def _dot_split(a, b):
    ah, bh = a.astype(MXU_DTYPE), b.astype(MXU_DTYPE)
    al = (a - ah.astype(F32)).astype(MXU_DTYPE)
    bl = (b - bh.astype(F32)).astype(MXU_DTYPE)
    dot = functools.partial(jnp.dot, preferred_element_type=F32)
    return dot(ah, bh) + dot(ah, bl) + dot(al, bh)


def _mod_kernel(c_ref, w_ref, b_ref, o_ref):
    o_ref[0] = _dot_f32(_silu(c_ref[...]), w_ref[0]) + b_ref[0]


def _modulation(cc, w_mod, b_mod):
    depth, d, d6 = w_mod.shape
    tn = d6 // 4
    return pl.pallas_call(
        _mod_kernel,
        grid=(depth, d6 // tn),
        in_specs=[pl.BlockSpec((SUBLANE, d), lambda l, j: (0, 0)),
                  pl.BlockSpec((1, d, tn), lambda l, j: (l, 0, j)),
                  pl.BlockSpec((1, 1, tn), lambda l, j: (l, 0, j))],
        out_specs=pl.BlockSpec((1, SUBLANE, tn), lambda l, j: (l, 0, j)),
        out_shape=jax.ShapeDtypeStruct((depth, SUBLANE, d6), F32),
        compiler_params=_cparams(("arbitrary", "arbitrary")),
        name="modulation",
    )(cc, w_mod, b_mod.reshape(depth, 1, d6))


def _inproj_kernel(x_ref, xp_ref, xn_ref, ctx_ref, mod_ref, g_ref, w_ref, wt_ref, cst_ref, snt_ref, qnt_ref, knt_ref,
                   scw_ref, scb_ref, hcw_ref, hcb_ref,
                   z_ref, u_ref, dt_ref, v_ref, x1_ref, x2_ref, vc_ref, x1c_ref, x2c_ref,
                   qwt_ref, kw_ref, vwt_ref, qdt_ref, kd_ref, vdt_ref, ext_ref):
    i = pl.program_id(1)
    nrow = pl.num_programs(1)
    xv = jnp.where(i == 0, ctx_ref[0], x_ref[0])
    tm = xv.shape[0]
    sh = mod_ref[0, 0, 0:1, :]
    sc = mod_ref[0, 0, 1:2, :]

    def modulated(rows):
        return (_rms(rows, g_ref[...]) * (1.0 + sc) + sh).astype(MXU_DTYPE)

    hb = modulated(xv)

    def proj(cols):
        return jnp.dot(hb, w_ref[:, cols[0]:cols[1]], preferred_element_type=F32)

    z_ref[0] = proj(C_Z)
    dt_ref[0] = proj(C_DT)

    conv_cols = (C_XBC[0], C_HY[1])
    halo = jnp.dot(modulated(jnp.concatenate([xp_ref[0], xn_ref[0]], axis=0)),
                   w_ref[:, conv_cols[0]:conv_cols[1]], preferred_element_type=F32)
    prev_ok = i > 1
    next_ok = jnp.logical_and(i >= 1, i < nrow - 1)
    ext_ref[0:SUBLANE] = jnp.where(prev_ok, halo[0:SUBLANE], 0.0)
    ext_ref[SUBLANE:SUBLANE + tm] = proj(conv_cols)
    ext_ref[SUBLANE + tm:] = jnp.where(next_ok, halo[SUBLANE:], 0.0)

    def conv(w_r, b_r, lo, hi):
        taps = w_r.shape[0]
        acc = b_r[...]
        for k in range(taps):
            off = SUBLANE - taps // 2 + k
            acc = acc + w_r[k:k + 1, :] * ext_ref[off:off + tm, lo:hi]
        return acc

    nx = C_XBC[1] - C_XBC[0]
    u_ref[0] = _silu(conv(scw_ref, scb_ref, 0, nx))
    hyc = conv(hcw_ref, hcb_ref, nx, nx + C_HY[1] - C_HY[0])
    parts = [hyc[:, j * W_GROUP:(j + 1) * W_GROUP] for j in range(3)]
    for ref, val in zip((v_ref, x1_ref, x2_ref), parts):
        ref[0] = val

    @pl.when(i == 0)
    def _():
        for ref, val in zip((vc_ref, x1c_ref, x2c_ref), parts):
            ref[0] = val

    pt = _dot_nt(wt_ref[...], hb)
    cst = cst_ref[...]
    snt = snt_ref[...]
    tile = lambda r: jnp.concatenate([r[...]] * (tm // LANE), axis=1)
    qnt, knt = tile(qnt_ref), tile(knt_ref)
    half = HEAD_DIM // 2
    qscale = HEAD_DIM ** -0.5 * LOG2E

    def head(row0):
        return pt[row0:row0 + HEAD_DIM]

    def norm(t, gain):
        return t * lax.rsqrt(jnp.mean(t * t, axis=0, keepdims=True) + EPS) * gain

    def rope(t):
        return t * cst + jnp.concatenate([t[half:], t[:half]], axis=0) * snt

    def put_values(v_ref, row0):
        ones = jnp.ones((v_ref.shape[3] - HEAD_DIM, LANE), v_ref.dtype)
        for g in range(N_KV):
            vt = head(row0 + g * HEAD_DIM)
            for j in range(tm // LANE):
                v_ref[0, g, j, 0:HEAD_DIM, :] = vt[:, j * LANE:(j + 1) * LANE].astype(v_ref.dtype)
                v_ref[0, g, j, HEAD_DIM:, :] = ones

    def put_keys(k_ref, kt_pair):
        k_rows = jnp.concatenate(kt_pair, axis=0).T
        for g in range(N_KV):
            k_ref[0, g] = k_rows[:, g * HEAD_DIM:(g + 1) * HEAD_DIM].astype(k_ref.dtype)

    for h in range(N_HEADS):
        qdt_ref[0, h] = (rope(norm(head(A_QD + h * HEAD_DIM), qnt)) * qscale).astype(qdt_ref.dtype)
        qwt_ref[0, h] = (rope(head(A_QW + h * HEAD_DIM)) * qscale).astype(qwt_ref.dtype)
    put_keys(kd_ref, [rope(norm(head(A_KD + g * HEAD_DIM), knt)) for g in range(N_KV)])
    put_keys(kw_ref, [rope(head(A_KW + g * HEAD_DIM)) for g in range(N_KV)])
    put_values(vdt_ref, A_VD)
    put_values(vwt_ref, A_VW)


def _in_projection(x, ctx, mod6, g_pre, w_row, w_att_t, cst, snt, qn, kn, ssd_cw, ssd_cb, hy_cw, hy_cb):
    b, _, d = x.shape
    tm = ROW_TILE
    hp = tm // SUBLANE
    if ctx is None:
        t = x.shape[1]
        first_lat = 1
        x_spec = pl.BlockSpec((1, tm, d), lambda bb, i: (bb, i, 0))
        ctx_arr, ctx_spec = x, pl.BlockSpec((1, tm, d), lambda bb, i: (bb, 0, 0))
    else:
        assert ctx.shape[1] == tm
        t = x.shape[1] + tm
        first_lat = 0
        x_spec = pl.BlockSpec((1, tm, d), lambda bb, i: (bb, jnp.maximum(i - 1, 0), 0))
        ctx_arr, ctx_spec = ctx, pl.BlockSpec((1, tm, d), lambda bb, i: (bb, 0, 0))
    nrow = t // tm
    last8 = x.shape[1] // SUBLANE - 1
    xp_spec = pl.BlockSpec((1, SUBLANE, d), lambda bb, i: (bb, jnp.clip((i - 1 + first_lat) * hp - 1, 0, last8), 0))
    xn_spec = pl.BlockSpec((1, SUBLANE, d), lambda bb, i: (bb, jnp.clip((i + first_lat) * hp, 0, last8), 0))
    per = tm // LANE
    row = lambda w: pl.BlockSpec((1, tm, w), lambda bb, i: (bb, i, 0))
    lat = pl.BlockSpec((1, tm, W_GROUP), lambda bb, i: (bb, jnp.maximum(i - 1, 0), 0))
    cblk = pl.BlockSpec((1, tm, W_GROUP), lambda bb, i: (bb, 0, 0))
    full = lambda a: pl.BlockSpec(a.shape, lambda bb, i: (0, 0))
    f32 = lambda w: jax.ShapeDtypeStruct((b, t, w), F32)
    lat_shape = jax.ShapeDtypeStruct((b, t - tm, W_GROUP), F32)
    ctx_shape = jax.ShapeDtypeStruct((b, tm, W_GROUP), F32)
    q_spec = pl.BlockSpec((1, N_HEADS, HEAD_DIM, tm), lambda bb, i: (bb, 0, 0, i))
    k_spec = pl.BlockSpec((1, N_KV, tm, HEAD_DIM), lambda bb, i: (bb, 0, i, 0))
    v_spec = pl.BlockSpec((1, N_KV, per, V_ROWS, LANE), lambda bb, i: (bb, 0, i, 0, 0))
    q_shape = jax.ShapeDtypeStruct((b, N_HEADS, HEAD_DIM, t), MXU_DTYPE)
    k_shape = jax.ShapeDtypeStruct((b, N_KV, t, HEAD_DIM), MXU_DTYPE)
    v_shape = jax.ShapeDtypeStruct((b, N_KV, t // LANE, V_ROWS, LANE), MXU_DTYPE)
    gain = lambda v: jnp.broadcast_to(v.reshape(HEAD_DIM, 1), (HEAD_DIM, LANE))
    qnt, knt = gain(qn), gain(kn)
    return pl.pallas_call(
        _inproj_kernel,
        grid=(b, nrow),
        in_specs=[x_spec, xp_spec, xn_spec, ctx_spec,
                  pl.BlockSpec((1, 1, 6, d), lambda bb, i: (bb, jnp.minimum(i, 1), 0, 0)),
                  pl.BlockSpec((1, d), lambda bb, i: (0, 0)),
                  full(w_row), full(w_att_t),
                  pl.BlockSpec((HEAD_DIM, tm), lambda bb, i: (0, i)),
                  pl.BlockSpec((HEAD_DIM, tm), lambda bb, i: (0, i)),
                  full(qnt), full(knt), full(ssd_cw), full(ssd_cb), full(hy_cw), full(hy_cb)],
        out_specs=[row(W_GROUP), row(SSD_XBC), row(LANE), lat, lat, lat, cblk, cblk, cblk,
                   q_spec, k_spec, v_spec, q_spec, k_spec, v_spec],
        out_shape=[f32(W_GROUP), f32(SSD_XBC), f32(LANE), lat_shape, lat_shape, lat_shape,
                   ctx_shape, ctx_shape, ctx_shape,
                   q_shape, k_shape, v_shape, q_shape, k_shape, v_shape],
        scratch_shapes=[pltpu.VMEM((tm + 2 * SUBLANE, C_HY[1] - C_XBC[0]), F32)],
        compiler_params=_cparams(("parallel", "arbitrary")),
        name="in_projection",
    )(x, x, x, ctx_arr, mod6, g_pre, w_row, w_att_t, cst, snt, qnt, knt, ssd_cw, ssd_cb, hy_cw, hy_cb)


def _ssd_kernel(nc, nt, par_ref, uf_ref, dtf_ref, ub_ref, dtb_ref, yf_ref, yb_ref, st_ref):
    j = pl.program_id(1)
    q = CHUNK
    rep = N_HEADS // N_KV

    @pl.when(j == 0)
    def _():
        st_ref[...] = jnp.zeros(st_ref.shape, F32)

    a_all = -jnp.exp(par_ref[0:1, :])
    ri = lax.broadcasted_iota(jnp.int32, (q, q), 0)
    ci = lax.broadcasted_iota(jnp.int32, (q, q), 1)
    dirs = []
    for d, (u_ref, dt_ref) in enumerate(((uf_ref, dtf_ref), (ub_ref, dtb_ref))):
        u = u_ref[0]
        dtv = _softplus(dt_ref[0] + par_ref[1:2, :])
        mask = (ri >= ci) if d == 0 else (ri <= ci)
        cum = _dot_f32(mask.astype(F32), dtv * a_all)
        dirs.append(dict(
            d=d, mask=mask, cum=cum, cum_t=cum.T, dt_t=dtv.T, end=q - 1 if d == 0 else 0,
            xs=u[:, :W_GROUP], bm_t=u[:, W_GROUP:W_GROUP + N_KV * SSD_STATE].T,
            cm=u[:, W_GROUP + N_KV * SSD_STATE:]))

    def head_terms(v, h):
        hl = N_HEADS * v["d"] + h
        g = h // rep
        sl = slice(g * SSD_STATE, (g + 1) * SSD_STATE)
        return dict(col=v["cum"][:, hl:hl + 1], row=v["cum_t"][hl:hl + 1, :], dt_row=v["dt_t"][hl:hl + 1, :],
                    last=v["cum"][v["end"]:v["end"] + 1, hl:hl + 1], cg=v["cm"][:, sl], bg_t=v["bm_t"][sl, :],
                    xh=v["xs"][:, h * HEAD_DIM:(h + 1) * HEAD_DIM])

    terms = [[head_terms(v, h) for h in range(N_HEADS)] for v in dirs]
    scores = [[_dot(ts[g * rep]["cg"], ts[g * rep]["bg_t"]) for g in range(N_KV)] for ts in terms]
    carried = [[_dot(t["cg"], st_ref[v["d"], h]) for h, t in enumerate(ts)] for v, ts in zip(dirs, terms)]
    states = [[_dot(t["bg_t"] * (jnp.exp(t["last"] - t["row"]) * t["dt_row"]), t["xh"]) for t in ts]
              for ts in terms]
    diag = [[_dot(scores[v["d"]][h // rep] * jnp.exp(jnp.where(v["mask"], t["col"] - t["row"], NEG)) * t["dt_row"],
                  t["xh"]) for h, t in enumerate(ts)] for v, ts in zip(dirs, terms)]
    for v, ts, y_ref in zip(dirs, terms, (yf_ref, yb_ref)):
        d = v["d"]
        outs = []
        for h, t in enumerate(ts):
            y = diag[d][h] + carried[d][h] * jnp.exp(t["col"])
            st_ref[d, h] = jnp.exp(t["last"]) * st_ref[d, h] + states[d][h]
            if d == 0:
                y = y + par_ref[2:3, h:h + 1] * t["xh"]
            outs.append(y)
        y_ref[0] = jnp.concatenate(outs, axis=1)


def _ssd(u, dt, par, n_ctx):
    b, t, w = u.shape
    q = CHUNK
    nc, nt = n_ctx // q, t // q
    fwd = lambda j: j
    bwd = lambda j: jnp.where(j < nc, nc - 1 - j, nt + nc - 1 - j)

    def specs(cmap):
        return [pl.BlockSpec((1, q, w), lambda bb, j: (bb, cmap(j), 0)),
                pl.BlockSpec((1, q, LANE), lambda bb, j: (bb, cmap(j), 0))]

    return pl.pallas_call(
        functools.partial(_ssd_kernel, nc, nt),
        grid=(b, nt),
        in_specs=[pl.BlockSpec((SUBLANE, LANE), lambda bb, j: (0, 0))] + specs(fwd) + specs(bwd),
        out_specs=[pl.BlockSpec((1, q, W_GROUP), lambda bb, j: (bb, fwd(j), 0)),
                   pl.BlockSpec((1, q, W_GROUP), lambda bb, j: (bb, bwd(j), 0))],
        out_shape=[jax.ShapeDtypeStruct((b, t, W_GROUP), F32)] * 2,
        scratch_shapes=[pltpu.VMEM((2, N_HEADS, SSD_STATE, HEAD_DIM), F32)],
        compiler_params=_cparams(("parallel", "arbitrary")),
        name="ssd_scan",
    )(par, u, dt, u, dt)


def _filter_kernel(zf_ref, zb_ref, w1_ref, b1_ref, f1_ref, w2_ref, b2_ref, f2_ref,
                   w3f_ref, w3b_ref, b3f_ref, b3b_ref, dl_ref, o_ref, ss_ref):
    i = pl.program_id(0)
    tr = zf_ref.shape[0]

    def half(z_ref, w3_ref, b3_ref):
        z = z_ref[...]
        h = jnp.sin(f1_ref[...] * (_dot_split(z, w1_ref[...]) + b1_ref[...]))
        h = jnp.sin(f2_ref[...] * (_dot_split(h, w2_ref[...]) + b2_ref[...]))
        k = _dot_split(h, w3_ref[...]) + b3_ref[...]
        return k * jnp.exp(-z[:, 0:1] * dl_ref[...])

    kf = half(zf_ref, w3f_ref, b3f_ref)
    kb = half(zb_ref, w3b_ref, b3b_ref)

    @pl.when(i == 0)
    def _():
        ss_ref[...] = jnp.zeros(ss_ref.shape, F32)

    ss_ref[...] += jnp.sum(kf * kf + kb * kb, axis=0, keepdims=True)
    rows = i * tr + lax.broadcasted_iota(jnp.int32, (tr, 1), 0)
    o_ref[0] = kf
    o_ref[1] = jnp.where(rows == 0, 0.0, kb)


def _hyena_filters(n, w1, b1, f1, w2, b2, f2, w3, b3):
    pos = np.arange(n, dtype=np.float64)
    t = np.linspace(0.0, 1.0, n)
    f = np.linspace(1e-4, HY_BANDS - 1, HY_BANDS)
    ang = 2.0 * math.pi * pos[:, None] * f[None, :] / n
    emb = np.concatenate([t[:, None], np.cos(ang), -np.sin(ang)], axis=-1)
    emb = np.pad(emb, ((0, 0), (0, LANE - HY_EMB)))
    emb_b = np.roll(np.flip(emb, axis=0), 1, axis=0)
    emb, emb_b = jnp.asarray(emb, F32), jnp.asarray(emb_b, F32)
    w1p = jnp.pad(w1, ((0, LANE - HY_EMB), (0, 0)))
    w3r = w3.reshape(HY_FILT, HY_ORDER, 2, W_GROUP)
    b3r = b3.reshape(HY_ORDER, 2, W_GROUP)
    wc = HY_ORDER * W_GROUP
    w3f, w3b = w3r[:, :, 0].reshape(HY_FILT, wc), w3r[:, :, 1].reshape(HY_FILT, wc)
    b3f, b3b = b3r[:, 0].reshape(1, wc), b3r[:, 1].reshape(1, wc)
    deltas = np.abs(np.linspace(HY_MIN_DECAY, HY_MAX_DECAY, W_GROUP))
    deltas = jnp.asarray(np.tile(deltas, HY_ORDER).reshape(1, wc), F32)
    tr = math.gcd(n, 512)
    const = lambda s: pl.BlockSpec(s, lambda i: (0, 0))
    rows = pl.BlockSpec((tr, LANE), lambda i: (i, 0))
    return pl.pallas_call(
        _filter_kernel,
        grid=(n // tr,),
        in_specs=[rows, rows, const((LANE, HY_FILT)), const((1, HY_FILT)), const((1, HY_FILT)),
                  const((HY_FILT, HY_FILT)), const((1, HY_FILT)), const((1, HY_FILT)),
                  const((HY_FILT, wc)), const((HY_FILT, wc)), const((1, wc)), const((1, wc)), const((1, wc))],
        out_specs=[pl.BlockSpec((2, tr, wc), lambda i: (0, i, 0)), const((1, wc))],
        out_shape=[jax.ShapeDtypeStruct((2, n, wc), F32), jax.ShapeDtypeStruct((1, wc), F32)],
        compiler_params=_cparams(("arbitrary",)),
        name="hyena_filter",
    )(emb, emb_b, w1p, b1.reshape(1, -1), f1.reshape(1, -1), w2, b2.reshape(1, -1), f2.reshape(1, -1),
      w3f, w3b, b3f, b3b, deltas)


def _dft_tables(n):
    nn = 2 * n
    n1, n2 = nn // LANE, LANE
    a = n1 // 2
    k1 = np.arange(n1)[:, None]
    j1 = np.arange(n1)[None, :]
    ang1 = 2.0 * np.pi * (k1 * j1 % n1) / n1
    fr, fi = np.cos(ang1), -np.sin(ang1)
    m1 = np.block([[fr[:, :a], -fi[:, :a]], [fi[:, :a], fr[:, :a]]])
    m1_real = np.concatenate([fr, fi], axis=0)
    er, ei = fr.T[:a], -fi.T[:a]
    m3 = np.stack([np.concatenate([er, -ei], axis=1), np.concatenate([ei, er], axis=1)]) / nn
    k2 = np.arange(n2)[:, None]
    j2 = np.arange(n2)[None, :]
    ang2 = 2.0 * np.pi * (k2 * j2 % n2) / n2
    f2 = np.stack([np.cos(ang2), -np.sin(ang2)])
    angt = 2.0 * np.pi * (np.arange(n1)[:, None] * j2) / nn
    tw = np.stack([np.cos(angt), -np.sin(angt)], axis=1)
    return tuple(jnp.asarray(t, F32) for t in (m1, m1_real, m3, f2, tw))


def _fft_first_kernel(ur_ref, ui_ref, m_ref, o_ref, acc_ref):
    n1 = o_ref.shape[2]
    for s in range(ur_ref.shape[2]):
        r = _dot(m_ref[...], jnp.concatenate([ur_ref[0, :, s, :], ui_ref[0, :, s, :]], axis=0))
        acc_ref[0, :, s, :] = r[:n1]
        acc_ref[1, :, s, :] = r[n1:]
    o_ref[0] = acc_ref[...].astype(o_ref.dtype)


def _fft_first(u, m1):
    b2, a, _, c = u.shape
    p = b2 // 2
    n1 = m1.shape[0] // 2
    js = FFT_JS
    return pl.pallas_call(
        _fft_first_kernel,
        grid=(p, LANE // js),
        in_specs=[pl.BlockSpec((1, a, js, c), lambda pp, j: (2 * pp, 0, j, 0)),
                  pl.BlockSpec((1, a, js, c), lambda pp, j: (2 * pp + 1, 0, j, 0)),
                  pl.BlockSpec(m1.shape, lambda pp, j: (0, 0))],
        out_specs=pl.BlockSpec((1, 2, n1, js, c), lambda pp, j: (pp, 0, 0, j, 0)),
        out_shape=jax.ShapeDtypeStruct((p, 2, n1, LANE, c), MXU_DTYPE),
        scratch_shapes=[pltpu.VMEM((2, n1, js, c), F32)],
        compiler_params=_cparams(("parallel", "arbitrary")),
        name="fft_first",
    )(u, u, m1)


def _twiddled_dft(f2_ref, tw_ref, u):
    fr, fi = f2_ref[0], f2_ref[1]
    tr, ti = tw_ref[u, 0:1, :], tw_ref[u, 1:2, :]
    return fr * tr - fi * ti, fr * ti + fi * tr


def _real_form(gr, gi):
    return jnp.concatenate([jnp.concatenate([gr, -gi], axis=1), jnp.concatenate([gi, gr], axis=1)], axis=0)


def _spectrum_kernel(a_ref, ss_ref, f2_ref, tw_ref, h_ref):
    scale = lax.rsqrt(ss_ref[...] + EPS)
    for u in range(a_ref.shape[2]):
        xin = jnp.concatenate([a_ref[0, 0, u], a_ref[0, 1, u]], axis=0)
        x = _dot(_real_form(*_twiddled_dft(f2_ref, tw_ref, u)), xin) * scale
        h_ref[0, u] = x[:LANE]
        h_ref[1, u] = x[LANE:]


def _filter_spectrum(a5, ss, f2, tw):
    _, _, n1, _, c = a5.shape
    kb = math.gcd(n1, FFT_KB)
    return pl.pallas_call(
        _spectrum_kernel,
        grid=(n1 // kb,),
        in_specs=[pl.BlockSpec((1, 2, kb, LANE, c), lambda k: (0, 0, k, 0, 0)),
                  pl.BlockSpec((1, c), lambda k: (0, 0)),
                  pl.BlockSpec((2, LANE, LANE), lambda k: (0, 0, 0)),
                  pl.BlockSpec((kb, 2, LANE), lambda k: (k, 0, 0))],
        out_specs=pl.BlockSpec((2, kb, LANE, c), lambda k: (0, k, 0, 0)),
        out_shape=jax.ShapeDtypeStruct((2, n1, LANE, c), F32),
        compiler_params=_cparams(("arbitrary",)),
        name="filter_spectrum",
    )(a5, ss, f2, tw)


def _fft_mid_kernel(a_ref, h_ref, f2_ref, tw_ref, v_ref):
    npair = a_ref.shape[0]
    c = a_ref.shape[4]
    kb = a_ref.shape[2]
    gs = [_twiddled_dft(f2_ref, tw_ref, u) for u in range(kb)]
    xs = []
    for u, (gr, gi) in enumerate(gs):
        xin = jnp.concatenate(
            [jnp.concatenate([a_ref[p, 0, u], a_ref[p, 1, u]], axis=0) for p in range(npair)], axis=1)
        xs.append(_dot(_real_form(gr, gi), xin))
    vs = []
    for u, ((gr, gi), x) in enumerate(zip(gs, xs)):
        xr, xi = x[:LANE], x[LANE:]
        hr = jnp.concatenate([h_ref[0, u]] * npair, axis=1)
        hi = jnp.concatenate([h_ref[1, u]] * npair, axis=1)
        y = jnp.concatenate([xr * hr - xi * hi, xr * hi + xi * hr], axis=0)
        vs.append(_dot(_real_form(gr.T, -gi.T), y))
    for u, v in enumerate(vs):
        for p in range(npair):
            v_ref[p, 0, u] = v[:LANE, p * c:(p + 1) * c].astype(v_ref.dtype)
            v_ref[p, 1, u] = v[LANE:, p * c:(p + 1) * c].astype(v_ref.dtype)


def _fft_mid(a5, hspec, order, f2, tw):
    npair, _, n1, _, c = a5.shape
    kb = math.gcd(n1, FFT_KB)
    blk = pl.BlockSpec((npair, 2, kb, LANE, c), lambda k: (0, 0, k, 0, 0))
    return pl.pallas_call(
        _fft_mid_kernel,
        grid=(n1 // kb,),
        in_specs=[blk,
                  pl.BlockSpec((2, kb, LANE, c), lambda k: (0, k, 0, order)),
                  pl.BlockSpec((2, LANE, LANE), lambda k: (0, 0, 0)),
                  pl.BlockSpec((kb, 2, LANE), lambda k: (k, 0, 0))],
        out_specs=blk,
        out_shape=jax.ShapeDtypeStruct(a5.shape, MXU_DTYPE),
        compiler_params=_cparams(("arbitrary",)),
        name="fft_mid",
    )(a5, hspec, f2, tw)


def _fft_last_kernel(v_ref, m_ref, z_ref, gate_ref, bias_ref, o_ref, vf_ref):
    vf_ref[...] = v_ref[0].astype(F32)
    for s in range(z_ref.shape[2]):
        vs = jnp.concatenate([vf_ref[0, :, s, :], vf_ref[1, :, s, :]], axis=0)
        zf = _dot(m_ref[0], vs)
        o_ref[0, :, s, :] = gate_ref[0, :, s, :] * (zf + z_ref[0, :, s, :] * bias_ref[...])


def _fft_last(v5, m3, z, gate, bias):
    b, a, _, c = z.shape
    n1 = v5.shape[2]
    js = FFT_JS
    row = pl.BlockSpec((1, a, js, c), lambda j, bb: (bb, 0, j, 0))
    return pl.pallas_call(
        _fft_last_kernel,
        grid=(LANE // js, b),
        in_specs=[pl.BlockSpec((1, 2, n1, js, c), lambda j, bb: (bb // 2, 0, 0, j, 0)),
                  pl.BlockSpec((1, a, 2 * n1), lambda j, bb: (bb % 2, 0, 0)),
                  row, row,
                  pl.BlockSpec((1, c), lambda j, bb: (0, 0))],
        out_specs=row,
        out_shape=jax.ShapeDtypeStruct(z.shape, F32),
        scratch_shapes=[pltpu.VMEM((2, n1, js, c), F32)],
        compiler_params=_cparams(("parallel", "arbitrary")),
        name="fft_last",
    )(v5, m3, z, gate, bias)


def _hyena_long(v, x1, x2, filt, ss, bias):
    b, n, c = v.shape
    a = n // LANE
    m1, m1_real, m3, f2, tw = _dft_tables(n)
    hspec = _filter_spectrum(_fft_first(filt.reshape(2, a, LANE, filt.shape[2]), m1_real), ss, f2, tw)
    z = v.reshape(b, a, LANE, c)
    gates = (x1.reshape(b, a, LANE, c), x2.reshape(b, a, LANE, c))
    for o in range(HY_ORDER):
        z = _fft_last(_fft_mid(_fft_first(z, m1), hspec, o, f2, tw), m3, z, gates[o], bias[o].reshape(1, c))
    return z.reshape(b, n, c)


def _hyena_ctx_kernel(v_ref, x1_ref, x2_ref, filt_ref, ss_ref, bias_ref, ff_ref, fc_ref, fi_ref, o_ref):
    b, n, c = v_ref.shape
    npair = b // 2
    hs = _dot(ff_ref[...], jnp.concatenate([filt_ref[0], filt_ref[1]], axis=0))
    hs = hs * lax.rsqrt(ss_ref[...] + EPS)
    z = [v_ref[i] for i in range(b)]
    gates = (x1_ref, x2_ref)
    for o in range(HY_ORDER):
        hr = jnp.concatenate([hs[:2 * n, o * c:(o + 1) * c]] * npair, axis=1)
        hi = jnp.concatenate([hs[2 * n:, o * c:(o + 1) * c]] * npair, axis=1)
        xin = jnp.concatenate([jnp.concatenate([z[2 * p] for p in range(npair)], axis=1),
                               jnp.concatenate([z[2 * p + 1] for p in range(npair)], axis=1)], axis=0)
        x = _dot(fc_ref[...], xin)
        xr, xi = x[:2 * n], x[2 * n:]
        y = jnp.concatenate([xr * hr - xi * hi, xr * hi + xi * hr], axis=0)
        zf = _dot(fi_ref[...], y)
        bo = bias_ref[o:o + 1, :]
        for i in range(b):
            p, part = i // 2, i % 2
            conv = zf[part * n:(part + 1) * n, p * c:(p + 1) * c]
            z[i] = gates[o][i] * (conv + z[i] * bo)
    for i in range(b):
        o_ref[i] = z[i]


def _hyena_ctx(v, x1, x2, filt, ss, bias):
    b, n, c = v.shape
    nn = 2 * n
    k = np.arange(nn)[:, None]
    j = np.arange(nn)[None, :]
    ang = 2.0 * np.pi * (k * j % nn) / nn
    fr, fi = np.cos(ang), -np.sin(ang)
    ff = np.concatenate([fr, fi], axis=0)
    fc = np.block([[fr[:, :n], -fi[:, :n]], [fi[:, :n], fr[:, :n]]])
    er, ei = fr[:n], -fi[:n]
    finv = np.block([[er, -ei], [ei, er]]) / nn
    return pl.pallas_call(
        _hyena_ctx_kernel,
        out_shape=jax.ShapeDtypeStruct((b, n, c), F32),
        compiler_params=pltpu.CompilerParams(vmem_limit_bytes=VMEM_LIMIT),
        name="hyena_ctx",
    )(v, x1, x2, filt, ss, bias, jnp.asarray(ff, F32), jnp.asarray(fc, F32), jnp.asarray(finv, F32))


def _window_kernel(n_ctx, sink_ref, q_ref, k_ref, v_ref, o_ref):
    g = pl.program_id(1)
    i = pl.program_id(2)
    rep, tq = q_ref.shape[1], q_ref.shape[3]
    t = k_ref.shape[2]
    ql, win = ATT_QL, CHUNK
    wk = ql + 2 * win
    ids = [(c, h) for c in range(tq // ql) for h in range(rep)]
    q0 = [i * tq + c * ql for c in range(tq // ql)]
    start = [pl.multiple_of(jnp.clip(q - win, 0, t - wk), LANE) for q in q0]

    kx = k_ref[0, 0, 0:n_ctx, :]
    vx = jnp.concatenate([v_ref[0, 0, u] for u in range(n_ctx // LANE)], axis=1)
    kl = [k_ref[0, 0, pl.ds(s, wk), :] for s in start]
    vl = [jnp.concatenate([v_ref[0, 0, s // LANE + u] for u in range(wk // LANE)], axis=1) for s in start]

    qs = [q_ref[0, h, :, c * ql:(c + 1) * ql] for c, h in ids]
    s_loc = [jnp.dot(kl[c], q, preferred_element_type=F32) for (c, h), q in zip(ids, qs)]
    s_ctx = [jnp.dot(kx, q, preferred_element_type=F32) for q in qs]

    diff = lax.broadcasted_iota(jnp.int32, (wk, ql), 0) - lax.broadcasted_iota(jnp.int32, (wk, ql), 1)
    krow = lax.broadcasted_iota(jnp.int32, (wk, 1), 0)
    p_loc, p_ctx, e_snk = [], [], []
    for n, (c, h) in enumerate(ids):
        d = diff + (start[c] - q0[c])
        ok = jnp.logical_and(jnp.abs(d) <= win, krow >= n_ctx - start[c])
        ok = jnp.logical_and(ok, q0[c] >= n_ctx)
        sl = jnp.where(ok, s_loc[n], NEG)
        snk = jnp.where(g == 0, sink_ref[0:1, h:h + 1], sink_ref[0:1, rep + h:rep + h + 1]) * LOG2E
        m = jnp.maximum(jnp.maximum(jnp.max(sl, axis=0, keepdims=True),
                                    jnp.max(s_ctx[n], axis=0, keepdims=True)), snk)
        p_loc.append(jnp.exp2(sl - m).astype(MXU_DTYPE))
        p_ctx.append(jnp.exp2(s_ctx[n] - m).astype(MXU_DTYPE))
        e_snk.append(jnp.exp2(snk - m))

    acc = [jnp.dot(vl[c], p_loc[n], preferred_element_type=F32)
           + jnp.dot(vx, p_ctx[n], preferred_element_type=F32) for n, (c, h) in enumerate(ids)]
    outs = [a[:HEAD_DIM] * (1.0 / (a[HEAD_DIM:HEAD_DIM + 1] + e)) for a, e in zip(acc, e_snk)]
    for c in range(tq // ql):
        o_ref[0, c * ql:(c + 1) * ql, :] = jnp.concatenate(outs[c * rep:(c + 1) * rep], axis=0).T


def _window_attention(qt, k, vt, sink, n_ctx):
    b, _, hd, t = qt.shape
    rep = N_HEADS // N_KV
    tq = ATT_TQ if t % ATT_TQ == 0 else ATT_QL
    assert n_ctx == ATT_QL and t % ATT_QL == 0 and rep * hd == LANE and t >= ATT_QL + 2 * CHUNK
    return pl.pallas_call(
        functools.partial(_window_kernel, n_ctx),
        grid=(b, N_KV, t // tq),
        in_specs=[pl.BlockSpec((SUBLANE, LANE), lambda bb, g, i: (0, 0)),
                  pl.BlockSpec((1, rep, hd, tq), lambda bb, g, i: (bb, g, 0, i)),
                  pl.BlockSpec((1, 1, t, hd), lambda bb, g, i: (bb, g, 0, 0)),
                  pl.BlockSpec((1, 1) + vt.shape[2:], lambda bb, g, i: (bb, g, 0, 0, 0))],
        out_specs=pl.BlockSpec((1, tq, rep * hd), lambda bb, g, i: (bb, i, g)),
        out_shape=jax.ShapeDtypeStruct((b, t, N_HEADS * hd), F32),
        compiler_params=_cparams(("parallel", "parallel", "arbitrary")),
        name="window_attention",
    )(sink, qt, k, vt)


def _dense_kernel(n_ctx, q_ref, k_ref, v_ref, o_ref, m_ref, alpha_ref, acc_ref, p_ref):
    i = pl.program_id(2)
    rep, tq = q_ref.shape[1], q_ref.shape[3]
    ql = ATT_QL
    nc = tq // ql

    def run(chunks, kt, nk):
        per = kt // LANE
        ids = [(c, h) for c in chunks for h in range(rep)]
        for c, h in ids:
            m_ref[c * rep + h] = jnp.full((1, ql), NEG, F32)
            alpha_ref[c * rep + h] = jnp.ones((1, ql), F32)
            acc_ref[c * rep + h] = jnp.zeros(acc_ref.shape[1:], F32)
            p_ref[c * rep + h, 0:kt] = jnp.zeros((kt, ql), p_ref.dtype)

        def scores(j):
            kb = k_ref[0, 0, pl.ds(pl.multiple_of(j * kt, kt), kt), :]
            return [jnp.dot(kb, q_ref[0, h, :, c * ql:(c + 1) * ql], preferred_element_type=F32) for c, h in ids]

        def values(j):
            vb = jnp.concatenate([v_ref[0, 0, j * per + u] for u in range(per)], axis=1)
            pvs = [jnp.dot(vb, p_ref[c * rep + h, 0:kt], preferred_element_type=F32) for c, h in ids]
            for (c, h), pv in zip(ids, pvs):
                n = c * rep + h
                acc_ref[n] = alpha_ref[n] * acc_ref[n] + pv

        def softmax(ss):
            for (c, h), s in zip(ids, ss):
                n = c * rep + h
                m = m_ref[n]
                mn = jnp.maximum(m, jnp.max(s, axis=0, keepdims=True))
                p_ref[n, 0:kt] = jnp.exp2(s - mn).astype(p_ref.dtype)
                alpha_ref[n] = jnp.exp2(m - mn)
                m_ref[n] = mn

        def body(j, carry):
            ss = scores(j)
            values(jnp.maximum(j - 1, 0))
            softmax(ss)
            return carry

        lax.fori_loop(0, nk, body, 0)
        values(nk - 1)
        for c in chunks:
            o = [acc_ref[c * rep + h] for h in range(rep)]
            o = [a[:HEAD_DIM] * (1.0 / a[HEAD_DIM:HEAD_DIM + 1]) for a in o]
            o_ref[0, c * ql:(c + 1) * ql, :] = jnp.concatenate(o, axis=0).T

    kt_all = p_ref.shape[1]
    nk_all = k_ref.shape[2] // kt_all

    @pl.when(i == 0)
    def _():
        run([0], n_ctx, 1)
        if nc > 1:
            run(list(range(1, nc)), kt_all, nk_all)

    @pl.when(i != 0)
    def _():
        run(list(range(nc)), kt_all, nk_all)


def _dense_attention(qt, k, vt, n_ctx):
    b, _, hd, t = qt.shape
    rep = N_HEADS // N_KV
    tq = DENSE_TQ if t % DENSE_TQ == 0 else ATT_QL
    kt = ATT_KT if t % ATT_KT == 0 else ATT_QL
    assert n_ctx == ATT_QL and t % ATT_QL == 0 and rep * hd == LANE
    return pl.pallas_call(
        functools.partial(_dense_kernel, n_ctx),
        grid=(b, N_KV, t // tq),
        in_specs=[pl.BlockSpec((1, rep, hd, tq), lambda bb, g, i: (bb, g, 0, i)),
                  pl.BlockSpec((1, 1, t, hd), lambda bb, g, i: (bb, g, 0, 0)),
                  pl.BlockSpec((1, 1) + vt.shape[2:], lambda bb, g, i: (bb, g, 0, 0, 0))],
        out_specs=pl.BlockSpec((1, tq, rep * hd), lambda bb, g, i: (bb, i, g)),
        out_shape=jax.ShapeDtypeStruct((b, t, N_HEADS * hd), F32),
        scratch_shapes=[pltpu.VMEM((rep * tq // ATT_QL, 1, ATT_QL), F32),
                        pltpu.VMEM((rep * tq // ATT_QL, 1, ATT_QL), F32),
                        pltpu.VMEM((rep * tq // ATT_QL, vt.shape[3], ATT_QL), F32),
                        pltpu.VMEM((rep * tq // ATT_QL, kt, ATT_QL), MXU_DTYPE)],
        compiler_params=_cparams(("parallel", "parallel", "arbitrary")),
        name="dense_attention",
    )(qt, k, vt)


def _mix_mlp_kernel(first, x_ref, ctx_ref, mod_ref, yf_ref, yb_ref, z_ref, gs_ref, hy_ref, hyc_ref,
                    yw_ref, yd_ref, gpost_ref, gpre_ref, gpost2_ref, wo_ref, w1_ref, w2_ref, o_ref):
    i = pl.program_id(1)
    if first:
        xv = jnp.where(i == 0, ctx_ref[0], x_ref[0])
        yh = jnp.where(i == 0, hyc_ref[0], hy_ref[0])
    else:
        xv = x_ref[0]
        yh = hy_ref[0]
    g1 = mod_ref[0, 0, 2:3, :]
    sh2 = mod_ref[0, 0, 3:4, :]
    sc2 = mod_ref[0, 0, 4:5, :]
    g2 = mod_ref[0, 0, 5:6, :]
    ya = _rms((yf_ref[0] + yb_ref[0]) * _silu(z_ref[0]), gs_ref[...])
    w = W_GROUP
    tm, d = xv.shape
    halves = [slice(r, r + tm // MLP_SPLIT) for r in range(0, tm, tm // MLP_SPLIT)]
    branches = (ya, yh, yw_ref[0], yd_ref[0])
    ys = [sum(_dot(br[rows], wo_ref[j * w:(j + 1) * w, :]) for j, br in enumerate(branches)) for rows in halves]
    x1s = [xv[rows] + g1 * _rms(y, gpost_ref[...]) for rows, y in zip(halves, ys)]
    hbs = [(_rms(x1, gpre_ref[...]) * (1.0 + sc2) + sh2).astype(MXU_DTYPE) for x1 in x1s]
    accs = [jnp.zeros((tm // MLP_SPLIT, d), F32) for _ in halves]
    for c in range(w1_ref.shape[1] // d):
        hidden = [jnp.maximum(jnp.dot(hb, w1_ref[:, c * d:(c + 1) * d], preferred_element_type=F32), 0.0)
                  for hb in hbs]
        accs = [acc + _dot(a * a, w2_ref[c * d:(c + 1) * d, :]) for acc, a in zip(accs, hidden)]
    for rows, x1, acc in zip(halves, x1s, accs):
        o_ref[0, rows] = x1 + g2 * _rms(acc, gpost2_ref[...])


def _mix_mlp(first, x, ctx, mod6, yf, yb, z, g_ssd, hy, hyc, yw, yd, g_post, g_pre2, g_post2, wo, w1, w2):
    b, t, _ = yf.shape
    d = wo.shape[1]
    tm = ROW_TILE
    off = 0 if first else 1
    nrow = t // tm - off
    if first:
        x_spec = pl.BlockSpec((1, tm, d), lambda bb, i: (bb, jnp.maximum(i - 1, 0), 0))
        hy_spec = pl.BlockSpec((1, tm, W_GROUP), lambda bb, i: (bb, jnp.maximum(i - 1, 0), 0))
        mod_spec = pl.BlockSpec((1, 1, 6, d), lambda bb, i: (bb, jnp.minimum(i, 1), 0, 0))
    else:
        x_spec = pl.BlockSpec((1, tm, d), lambda bb, i: (bb, i + 1, 0))
        hy_spec = pl.BlockSpec((1, tm, W_GROUP), lambda bb, i: (bb, i, 0))
        mod_spec = pl.BlockSpec((1, 1, 6, d), lambda bb, i: (bb, 1, 0, 0))
    first_blk = lambda w: pl.BlockSpec((1, tm, w), lambda bb, i: (bb, 0, 0))
    row = lambda w: pl.BlockSpec((1, tm, w), lambda bb, i: (bb, i + off, 0))
    vec = lambda w: pl.BlockSpec((1, w), lambda bb, i: (0, 0))
    full = lambda a: pl.BlockSpec(a.shape, lambda bb, i: (0, 0))
    return pl.pallas_call(
        functools.partial(_mix_mlp_kernel, first),
        grid=(b, nrow),
        in_specs=[x_spec, first_blk(d), mod_spec, row(W_GROUP), row(W_GROUP), row(W_GROUP), vec(W_GROUP),
                  hy_spec, first_blk(W_GROUP), row(W_GROUP), row(W_GROUP), vec(d), vec(d), vec(d),
                  full(wo), full(w1), full(w2)],
        out_specs=pl.BlockSpec((1, tm, d), lambda bb, i: (bb, i, 0)),
        out_shape=jax.ShapeDtypeStruct((b, nrow * tm, d), F32),
        compiler_params=_cparams(("parallel", "arbitrary")),
        name="mix_mlp",
    )(x, ctx, mod6, yf, yb, z, g_ssd, hy, hyc, yw, yd, g_post, g_pre2, g_post2, wo, w1, w2)


def _rope_tables(n, n_ctx):
    rows = n // GRID_W
    row = np.repeat(np.arange(rows, dtype=np.float64), GRID_W)
    col = np.tile(np.arange(GRID_W, dtype=np.float64), rows)
    n_freq = HEAD_DIM // 4
    inv = ROPE_THETA ** (-np.arange(n_freq, dtype=np.float64) / n_freq)
    ang = np.concatenate([row[:, None] * inv, col[:, None] * inv], axis=-1)
    cos, sin = np.cos(ang), np.sin(ang)
    cs = np.concatenate([np.ones((n_ctx, HEAD_DIM)), np.concatenate([cos, cos], axis=1)], axis=0)
    sn = np.concatenate([np.zeros((n_ctx, HEAD_DIM)), np.concatenate([-sin, sin], axis=1)], axis=0)
    return jnp.asarray(cs.T, F32), jnp.asarray(sn.T, F32)


def _pad_rows(a, rows):
    return jnp.pad(a, ((0, rows - a.shape[0]), (0, 0)))


def kernel(x, c, ctx, c_ctx, w_mod, b_mod, norm_mix_pre, norm_mix_post, norm_mlp_pre, norm_mlp_post, w_in, w_out, ssd_conv_w, ssd_conv_b, ssd_a_log, ssd_dt_bias, ssd_d, ssd_norm, hy_conv_w, hy_conv_b, hy_w1, hy_b1, hy_freq1, hy_w2, hy_b2, hy_freq2, hy_w3, hy_b3, hy_bias, attn_sink, q_norm, k_norm, mlp_w1, mlp_w2):
    b, n, d = x.shape
    n_ctx = ctx.shape[1]
    depth = w_mod.shape[0]
    assert n_ctx == ROW_TILE and n % ROW_TILE == 0 and b % 2 == 0 and b + 1 <= SUBLANE

    cc = _pad_rows(jnp.concatenate([c, c_ctx[None, :]], axis=0), SUBLANE)
    mod = _modulation(cc, w_mod, b_mod)
    cs, sn = _rope_tables(n, n_ctx)

    off_b = W_GROUP + SSD_XBC + 2 * N_HEADS
    xall = None
    for l in range(depth):
        first = l == 0
        need_ctx = l < depth - 1
        mod_lat = mod[l, :b].reshape(b, 1, 6, d)
        mod_ctx = jnp.broadcast_to(mod[l, b].reshape(1, 1, 6, d), (b, 1, 6, d))
        mod6 = jnp.concatenate([mod_ctx, mod_lat], axis=1)

        wl = w_in[l]
        off_c = off_b + 3 * W_GROUP
        off_d = off_c + (N_HEADS + 2 * N_KV) * HEAD_DIM
        w_row = jnp.concatenate(
            [wl[:, :W_GROUP + SSD_XBC], wl[:, off_b:off_c], wl[:, W_GROUP + SSD_XBC:off_b],
             jnp.zeros((d, D_ROW_PAD - off_c), F32)], axis=1).astype(MXU_DTYPE)
        w_att_t = jnp.concatenate([wl[:, off_d:], wl[:, off_c:off_d]], axis=1).T.astype(MXU_DTYPE)
        xin, cin = (x, ctx) if first else (xall, None)
        (z, u, dt, v, x1, x2, vc, x1c, x2c, qwt, kw, vwt, qdt, kd, vdt) = _in_projection(
            xin, cin, mod6, norm_mix_pre[l].reshape(1, d), w_row, w_att_t, cs, sn, q_norm[l], k_norm[l],
            ssd_conv_w[l], ssd_conv_b[l].reshape(1, -1), hy_conv_w[l], hy_conv_b[l].reshape(1, -1))

        par = jnp.zeros((SUBLANE, LANE), F32)
        par = par.at[0, :2 * N_HEADS].set(ssd_a_log[l].reshape(-1))
        par = par.at[1, :2 * N_HEADS].set(ssd_dt_bias[l].reshape(-1))
        par = par.at[2, :N_HEADS].set(ssd_d[l])
        yf, yb = _ssd(u, dt, par, n_ctx)

        filt_args = (hy_w1[l], hy_b1[l], hy_freq1[l], hy_w2[l], hy_b2[l], hy_freq2[l], hy_w3[l], hy_b3[l])
        yhy = _hyena_long(v, x1, x2, *_hyena_filters(n, *filt_args), hy_bias[l])
        if need_ctx:
            yhy_ctx = _hyena_ctx(vc, x1c, x2c, *_hyena_filters(n_ctx, *filt_args), hy_bias[l])
        else:
            yhy_ctx = yhy

        sink = jnp.zeros((SUBLANE, LANE), F32).at[0, :N_HEADS].set(attn_sink[l])
        yw = _window_attention(qwt, kw, vwt, sink, n_ctx)
        yd = _dense_attention(qdt, kd, vdt, n_ctx)

        xres, cres = (x, ctx) if first else (xall, xall)
        xall = _mix_mlp(first, xres, cres, mod6, yf, yb, z, ssd_norm[l].reshape(1, -1), yhy, yhy_ctx, yw, yd,
                        norm_mix_post[l].reshape(1, d), norm_mlp_pre[l].reshape(1, d),
                        norm_mlp_post[l].reshape(1, d), w_out[l].astype(MXU_DTYPE),
                        mlp_w1[l].astype(MXU_DTYPE), mlp_w2[l].astype(MXU_DTYPE))
    return xall
```

```python
import functools
import math

import numpy as np
import jax
import jax.numpy as jnp
from jax import lax
from jax.experimental import pallas as pl
from jax.experimental.pallas import tpu as pltpu

F32 = jnp.float32
MXU_DTYPE = jnp.bfloat16

EPS = 1e-6
HEAD_DIM = 64
GRID_W = 64
ROPE_THETA = 10000.0
N_HEADS = 4
N_KV = 2
W_GROUP = N_HEADS * HEAD_DIM
SSD_STATE = 64
SSD_XBC = W_GROUP + 2 * N_KV * SSD_STATE
HY_ORDER = 2
HY_BANDS = 16
HY_EMB = 2 * HY_BANDS + 1
HY_FILT = 64
HY_MAX_DECAY = math.log(1e-2) / 0.3
HY_MIN_DECAY = math.log(1e-2) / 1.5
CHUNK = 128
LANE = 128
SUBLANE = 8
ROW_TILE = 256
ATT_TQ = 2816
DENSE_TQ = 8448
FFT_JS = 16
FFT_KB = 4
MLP_SPLIT = 2
VMEM_LIMIT = 56 * 1024 * 1024
NEG = -1e30

C_Z = (0, 256)
C_XBC = (256, 768)
C_HY = (768, 1536)
C_DT = (1536, 1664)
D_ROW_PAD = 1664
A_QD, A_KD, A_VD = 0, 256, 384
A_QW, A_KW, A_VW = 512, 768, 896
V_ROWS = HEAD_DIM + 16
ATT_QL = 256
ATT_KT = 256
LOG2E = math.log2(math.e)


def _cparams(sem):
    return pltpu.CompilerParams(dimension_semantics=sem, vmem_limit_bytes=VMEM_LIMIT)


def _rms(x, g):
    return x * lax.rsqrt(jnp.mean(x * x, axis=-1, keepdims=True) + EPS) * g


def _silu(x):
    return x * (1.0 / (1.0 + jnp.exp(-x)))


def _softplus(x):
    return jnp.maximum(x, 0.0) + jnp.log(1.0 + jnp.exp(-jnp.abs(x)))


def _dot(a, b):
    return jnp.dot(a.astype(MXU_DTYPE), b.astype(MXU_DTYPE), preferred_element_type=F32)


def _dot_nt(a, b):
    return lax.dot_general(a.astype(MXU_DTYPE), b.astype(MXU_DTYPE), (((1,), (1,)), ((), ())),
                           preferred_element_type=F32)


def _dot_f32(a, b):
    return jnp.dot(a, b, preferred_element_type=F32, precision=lax.Precision.HIGHEST)


def _dot_split(a, b):
    ah, bh = a.astype(MXU_DTYPE), b.astype(MXU_DTYPE)
    al = (a - ah.astype(F32)).astype(MXU_DTYPE)
    bl = (b - bh.astype(F32)).astype(MXU_DTYPE)
    dot = functools.partial(jnp.dot, preferred_element_type=F32)
    return dot(ah, bh) + dot(ah, bl) + dot(al, bh)


def _mod_kernel(c_ref, w_ref, b_ref, o_ref):
    o_ref[0] = _dot_f32(_silu(c_ref[...]), w_ref[0]) + b_ref[0]


def _modulation(cc, w_mod, b_mod):
    depth, d, d6 = w_mod.shape
    tn = d6 // 4
    return pl.pallas_call(
        _mod_kernel,
        grid=(depth, d6 // tn),
        in_specs=[pl.BlockSpec((SUBLANE, d), lambda l, j: (0, 0)),
                  pl.BlockSpec((1, d, tn), lambda l, j: (l, 0, j)),
                  pl.BlockSpec((1, 1, tn), lambda l, j: (l, 0, j))],
        out_specs=pl.BlockSpec((1, SUBLANE, tn), lambda l, j: (l, 0, j)),
        out_shape=jax.ShapeDtypeStruct((depth, SUBLANE, d6), F32),
        compiler_params=_cparams(("arbitrary", "arbitrary")),
        name="modulation",
    )(cc, w_mod, b_mod.reshape(depth, 1, d6))


def _inproj_kernel(x_ref, xp_ref, xn_ref, ctx_ref, mod_ref, g_ref, w_ref, wt_ref, cst_ref, snt_ref, qnt_ref, knt_ref,
                   scw_ref, scb_ref, hcw_ref, hcb_ref,
                   z_ref, u_ref, dt_ref, v_ref, x1_ref, x2_ref, vc_ref, x1c_ref, x2c_ref,
                   qwt_ref, kw_ref, vwt_ref, qdt_ref, kd_ref, vdt_ref, ext_ref):
    i = pl.program_id(1)
    nrow = pl.num_programs(1)
    xv = jnp.where(i == 0, ctx_ref[0], x_ref[0])
    tm = xv.shape[0]
    sh = mod_ref[0, 0, 0:1, :]
    sc = mod_ref[0, 0, 1:2, :]

    def modulated(rows):
        return (_rms(rows, g_ref[...]) * (1.0 + sc) + sh).astype(MXU_DTYPE)

    hb = modulated(xv)

    def proj(cols):
        return jnp.dot(hb, w_ref[:, cols[0]:cols[1]], preferred_element_type=F32)

    z_ref[0] = proj(C_Z)
    dt_ref[0] = proj(C_DT)

    conv_cols = (C_XBC[0], C_HY[1])
    halo = jnp.dot(modulated(jnp.concatenate([xp_ref[0], xn_ref[0]], axis=0)),
                   w_ref[:, conv_cols[0]:conv_cols[1]], preferred_element_type=F32)
    prev_ok = i > 1
    next_ok = jnp.logical_and(i >= 1, i < nrow - 1)
    ext_ref[0:SUBLANE] = jnp.where(prev_ok, halo[0:SUBLANE], 0.0)
    ext_ref[SUBLANE:SUBLANE + tm] = proj(conv_cols)
    ext_ref[SUBLANE + tm:] = jnp.where(next_ok, halo[SUBLANE:], 0.0)

    def conv(w_r, b_r, lo, hi):
        taps = w_r.shape[0]
        acc = b_r[...]
        for k in range(taps):
            off = SUBLANE - taps // 2 + k
            acc = acc + w_r[k:k + 1, :] * ext_ref[off:off + tm, lo:hi]
        return acc

    nx = C_XBC[1] - C_XBC[0]
    u_ref[0] = _silu(conv(scw_ref, scb_ref, 0, nx))
    hyc = conv(hcw_ref, hcb_ref, nx, nx + C_HY[1] - C_HY[0])
    parts = [hyc[:, j * W_GROUP:(j + 1) * W_GROUP] for j in range(3)]
    for ref, val in zip((v_ref, x1_ref, x2_ref), parts):
        ref[0] = val

    @pl.when(i == 0)
    def _():
        for ref, val in zip((vc_ref, x1c_ref, x2c_ref), parts):
            ref[0] = val

    pt = _dot_nt(wt_ref[...], hb)
    cst = cst_ref[...]
    snt = snt_ref[...]
    tile = lambda r: jnp.concatenate([r[...]] * (tm // LANE), axis=1)
    qnt, knt = tile(qnt_ref), tile(knt_ref)
    half = HEAD_DIM // 2
    qscale = HEAD_DIM ** -0.5 * LOG2E

    def head(row0):
        return pt[row0:row0 + HEAD_DIM]

    def norm(t, gain):
        return t * lax.rsqrt(jnp.mean(t * t, axis=0, keepdims=True) + EPS) * gain

    def rope(t):
        return t * cst + jnp.concatenate([t[half:], t[:half]], axis=0) * snt

    def put_values(v_ref, row0):
        ones = jnp.ones((v_ref.shape[3] - HEAD_DIM, LANE), v_ref.dtype)
        for g in range(N_KV):
            vt = head(row0 + g * HEAD_DIM)
            for j in range(tm // LANE):
                v_ref[0, g, j, 0:HEAD_DIM, :] = vt[:, j * LANE:(j + 1) * LANE].astype(v_ref.dtype)
                v_ref[0, g, j, HEAD_DIM:, :] = ones

    def put_keys(k_ref, kt_pair):
        k_rows = jnp.concatenate(kt_pair, axis=0).T
        for g in range(N_KV):
            k_ref[0, g] = k_rows[:, g * HEAD_DIM:(g + 1) * HEAD_DIM].astype(k_ref.dtype)

    for h in range(N_HEADS):
        qdt_ref[0, h] = (rope(norm(head(A_QD + h * HEAD_DIM), qnt)) * qscale).astype(qdt_ref.dtype)
        qwt_ref[0, h] = (rope(head(A_QW + h * HEAD_DIM)) * qscale).astype(qwt_ref.dtype)
    put_keys(kd_ref, [rope(norm(head(A_KD + g * HEAD_DIM), knt)) for g in range(N_KV)])
    put_keys(kw_ref, [rope(head(A_KW + g * HEAD_DIM)) for g in range(N_KV)])
    put_values(vdt_ref, A_VD)
    put_values(vwt_ref, A_VW)


def _in_projection(x, ctx, mod6, g_pre, w_row, w_att_t, cst, snt, qn, kn, ssd_cw, ssd_cb, hy_cw, hy_cb):
    b, _, d = x.shape
    tm = ROW_TILE
    hp = tm // SUBLANE
    if ctx is None:
        t = x.shape[1]
        first_lat = 1
        x_spec = pl.BlockSpec((1, tm, d), lambda bb, i: (bb, i, 0))
        ctx_arr, ctx_spec = x, pl.BlockSpec((1, tm, d), lambda bb, i: (bb, 0, 0))
    else:
        assert ctx.shape[1] == tm
        t = x.shape[1] + tm
        first_lat = 0
        x_spec = pl.BlockSpec((1, tm, d), lambda bb, i: (bb, jnp.maximum(i - 1, 0), 0))
        ctx_arr, ctx_spec = ctx, pl.BlockSpec((1, tm, d), lambda bb, i: (bb, 0, 0))
    nrow = t // tm
    last8 = x.shape[1] // SUBLANE - 1
    xp_spec = pl.BlockSpec((1, SUBLANE, d), lambda bb, i: (bb, jnp.clip((i - 1 + first_lat) * hp - 1, 0, last8), 0))
    xn_spec = pl.BlockSpec((1, SUBLANE, d), lambda bb, i: (bb, jnp.clip((i + first_lat) * hp, 0, last8), 0))
    per = tm // LANE
    row = lambda w: pl.BlockSpec((1, tm, w), lambda bb, i: (bb, i, 0))
    lat = pl.BlockSpec((1, tm, W_GROUP), lambda bb, i: (bb, jnp.maximum(i - 1, 0), 0))
    cblk = pl.BlockSpec((1, tm, W_GROUP), lambda bb, i: (bb, 0, 0))
    full = lambda a: pl.BlockSpec(a.shape, lambda bb, i: (0, 0))
    f32 = lambda w: jax.ShapeDtypeStruct((b, t, w), F32)
    lat_shape = jax.ShapeDtypeStruct((b, t - tm, W_GROUP), F32)
    ctx_shape = jax.ShapeDtypeStruct((b, tm, W_GROUP), F32)
    q_spec = pl.BlockSpec((1, N_HEADS, HEAD_DIM, tm), lambda bb, i: (bb, 0, 0, i))
    k_spec = pl.BlockSpec((1, N_KV, tm, HEAD_DIM), lambda bb, i: (bb, 0, i, 0))
    v_spec = pl.BlockSpec((1, N_KV, per, V_ROWS, LANE), lambda bb, i: (bb, 0, i, 0, 0))
    q_shape = jax.ShapeDtypeStruct((b, N_HEADS, HEAD_DIM, t), MXU_DTYPE)
    k_shape = jax.ShapeDtypeStruct((b, N_KV, t, HEAD_DIM), MXU_DTYPE)
    v_shape = jax.ShapeDtypeStruct((b, N_KV, t // LANE, V_ROWS, LANE), MXU_DTYPE)
    gain = lambda v: jnp.broadcast_to(v.reshape(HEAD_DIM, 1), (HEAD_DIM, LANE))
    qnt, knt = gain(qn), gain(kn)
    return pl.pallas_call(
        _inproj_kernel,
        grid=(b, nrow),
        in_specs=[x_spec, xp_spec, xn_spec, ctx_spec,
                  pl.BlockSpec((1, 1, 6, d), lambda bb, i: (bb, jnp.minimum(i, 1), 0, 0)),
                  pl.BlockSpec((1, d), lambda bb, i: (0, 0)),
                  full(w_row), full(w_att_t),
                  pl.BlockSpec((HEAD_DIM, tm), lambda bb, i: (0, i)),
                  pl.BlockSpec((HEAD_DIM, tm), lambda bb, i: (0, i)),
                  full(qnt), full(knt), full(ssd_cw), full(ssd_cb), full(hy_cw), full(hy_cb)],
        out_specs=[row(W_GROUP), row(SSD_XBC), row(LANE), lat, lat, lat, cblk, cblk, cblk,
                   q_spec, k_spec, v_spec, q_spec, k_spec, v_spec],
        out_shape=[f32(W_GROUP), f32(SSD_XBC), f32(LANE), lat_shape, lat_shape, lat_shape,
                   ctx_shape, ctx_shape, ctx_shape,
                   q_shape, k_shape, v_shape, q_shape, k_shape, v_shape],
        scratch_shapes=[pltpu.VMEM((tm + 2 * SUBLANE, C_HY[1] - C_XBC[0]), F32)],
        compiler_params=_cparams(("parallel", "arbitrary")),
        name="in_projection",
    )(x, x, x, ctx_arr, mod6, g_pre, w_row, w_att_t, cst, snt, qnt, knt, ssd_cw, ssd_cb, hy_cw, hy_cb)


def _ssd_kernel(nc, nt, par_ref, uf_ref, dtf_ref, ub_ref, dtb_ref, yf_ref, yb_ref, st_ref):
    j = pl.program_id(1)
    q = CHUNK
    rep = N_HEADS // N_KV

    @pl.when(j == 0)
    def _():
        st_ref[...] = jnp.zeros(st_ref.shape, F32)

    a_all = -jnp.exp(par_ref[0:1, :])
    ri = lax.broadcasted_iota(jnp.int32, (q, q), 0)
    ci = lax.broadcasted_iota(jnp.int32, (q, q), 1)
    dirs = []
    for d, (u_ref, dt_ref) in enumerate(((uf_ref, dtf_ref), (ub_ref, dtb_ref))):
        u = u_ref[0]
        dtv = _softplus(dt_ref[0] + par_ref[1:2, :])
        mask = (ri >= ci) if d == 0 else (ri <= ci)
        cum = _dot_f32(mask.astype(F32), dtv * a_all)
        dirs.append(dict(
            d=d, mask=mask, cum=cum, cum_t=cum.T, dt_t=dtv.T, end=q - 1 if d == 0 else 0,
            xs=u[:, :W_GROUP], bm_t=u[:, W_GROUP:W_GROUP + N_KV * SSD_STATE].T,
            cm=u[:, W_GROUP + N_KV * SSD_STATE:]))

    def head_terms(v, h):
        hl = N_HEADS * v["d"] + h
        g = h // rep
        sl = slice(g * SSD_STATE, (g + 1) * SSD_STATE)
        return dict(col=v["cum"][:, hl:hl + 1], row=v["cum_t"][hl:hl + 1, :], dt_row=v["dt_t"][hl:hl + 1, :],
                    last=v["cum"][v["end"]:v["end"] + 1, hl:hl + 1], cg=v["cm"][:, sl], bg_t=v["bm_t"][sl, :],
                    xh=v["xs"][:, h * HEAD_DIM:(h + 1) * HEAD_DIM])

    terms = [[head_terms(v, h) for h in range(N_HEADS)] for v in dirs]
    scores = [[_dot(ts[g * rep]["cg"], ts[g * rep]["bg_t"]) for g in range(N_KV)] for ts in terms]
    carried = [[_dot(t["cg"], st_ref[v["d"], h]) for h, t in enumerate(ts)] for v, ts in zip(dirs, terms)]
    states = [[_dot(t["bg_t"] * (jnp.exp(t["last"] - t["row"]) * t["dt_row"]), t["xh"]) for t in ts]
              for ts in terms]
    diag = [[_dot(scores[v["d"]][h // rep] * jnp.exp(jnp.where(v["mask"], t["col"] - t["row"], NEG)) * t["dt_row"],
                  t["xh"]) for h, t in enumerate(ts)] for v, ts in zip(dirs, terms)]
    for v, ts, y_ref in zip(dirs, terms, (yf_ref, yb_ref)):
        d = v["d"]
        outs = []
        for h, t in enumerate(ts):
            y = diag[d][h] + carried[d][h] * jnp.exp(t["col"])
            st_ref[d, h] = jnp.exp(t["last"]) * st_ref[d, h] + states[d][h]
            if d == 0:
                y = y + par_ref[2:3, h:h + 1] * t["xh"]
            outs.append(y)
        y_ref[0] = jnp.concatenate(outs, axis=1)


def _ssd(u, dt, par, n_ctx):
    b, t, w = u.shape
    q = CHUNK
    nc, nt = n_ctx // q, t // q
    fwd = lambda j: j
    bwd = lambda j: jnp.where(j < nc, nc - 1 - j, nt + nc - 1 - j)

    def specs(cmap):
        return [pl.BlockSpec((1, q, w), lambda bb, j: (bb, cmap(j), 0)),
                pl.BlockSpec((1, q, LANE), lambda bb, j: (bb, cmap(j), 0))]

    return pl.pallas_call(
        functools.partial(_ssd_kernel, nc, nt),
        grid=(b, nt),
        in_specs=[pl.BlockSpec((SUBLANE, LANE), lambda bb, j: (0, 0))] + specs(fwd) + specs(bwd),
        out_specs=[pl.BlockSpec((1, q, W_GROUP), lambda bb, j: (bb, fwd(j), 0)),
                   pl.BlockSpec((1, q, W_GROUP), lambda bb, j: (bb, bwd(j), 0))],
        out_shape=[jax.ShapeDtypeStruct((b, t, W_GROUP), F32)] * 2,
        scratch_shapes=[pltpu.VMEM((2, N_HEADS, SSD_STATE, HEAD_DIM), F32)],
        compiler_params=_cparams(("parallel", "arbitrary")),
        name="ssd_scan",
    )(par, u, dt, u, dt)


def _filter_kernel(zf_ref, zb_ref, w1_ref, b1_ref, f1_ref, w2_ref, b2_ref, f2_ref,
                   w3f_ref, w3b_ref, b3f_ref, b3b_ref, dl_ref, o_ref, ss_ref):
    i = pl.program_id(0)
    tr = zf_ref.shape[0]

    def half(z_ref, w3_ref, b3_ref):
        z = z_ref[...]
        h = jnp.sin(f1_ref[...] * (_dot_split(z, w1_ref[...]) + b1_ref[...]))
        h = jnp.sin(f2_ref[...] * (_dot_split(h, w2_ref[...]) + b2_ref[...]))
        k = _dot_split(h, w3_ref[...]) + b3_ref[...]
        return k * jnp.exp(-z[:, 0:1] * dl_ref[...])

    kf = half(zf_ref, w3f_ref, b3f_ref)
    kb = half(zb_ref, w3b_ref, b3b_ref)

    @pl.when(i == 0)
    def _():
        ss_ref[...] = jnp.zeros(ss_ref.shape, F32)

    ss_ref[...] += jnp.sum(kf * kf + kb * kb, axis=0, keepdims=True)
    rows = i * tr + lax.broadcasted_iota(jnp.int32, (tr, 1), 0)
    o_ref[0] = kf
    o_ref[1] = jnp.where(rows == 0, 0.0, kb)


def _hyena_filters(n, w1, b1, f1, w2, b2, f2, w3, b3):
    pos = np.arange(n, dtype=np.float64)
    t = np.linspace(0.0, 1.0, n)
    f = np.linspace(1e-4, HY_BANDS - 1, HY_BANDS)
    ang = 2.0 * math.pi * pos[:, None] * f[None, :] / n
    emb = np.concatenate([t[:, None], np.cos(ang), -np.sin(ang)], axis=-1)
    emb = np.pad(emb, ((0, 0), (0, LANE - HY_EMB)))
    emb_b = np.roll(np.flip(emb, axis=0), 1, axis=0)
    emb, emb_b = jnp.asarray(emb, F32), jnp.asarray(emb_b, F32)
    w1p = jnp.pad(w1, ((0, LANE - HY_EMB), (0, 0)))
    w3r = w3.reshape(HY_FILT, HY_ORDER, 2, W_GROUP)
    b3r = b3.reshape(HY_ORDER, 2, W_GROUP)
    wc = HY_ORDER * W_GROUP
    w3f, w3b = w3r[:, :, 0].reshape(HY_FILT, wc), w3r[:, :, 1].reshape(HY_FILT, wc)
    b3f, b3b = b3r[:, 0].reshape(1, wc), b3r[:, 1].reshape(1, wc)
    deltas = np.abs(np.linspace(HY_MIN_DECAY, HY_MAX_DECAY, W_GROUP))
    deltas = jnp.asarray(np.tile(deltas, HY_ORDER).reshape(1, wc), F32)
    tr = math.gcd(n, 512)
    const = lambda s: pl.BlockSpec(s, lambda i: (0, 0))
    rows = pl.BlockSpec((tr, LANE), lambda i: (i, 0))
    return pl.pallas_call(
        _filter_kernel,
        grid=(n // tr,),
        in_specs=[rows, rows, const((LANE, HY_FILT)), const((1, HY_FILT)), const((1, HY_FILT)),
                  const((HY_FILT, HY_FILT)), const((1, HY_FILT)), const((1, HY_FILT)),
                  const((HY_FILT, wc)), const((HY_FILT, wc)), const((1, wc)), const((1, wc)), const((1, wc))],
        out_specs=[pl.BlockSpec((2, tr, wc), lambda i: (0, i, 0)), const((1, wc))],
        out_shape=[jax.ShapeDtypeStruct((2, n, wc), F32), jax.ShapeDtypeStruct((1, wc), F32)],
        compiler_params=_cparams(("arbitrary",)),
        name="hyena_filter",
    )(emb, emb_b, w1p, b1.reshape(1, -1), f1.reshape(1, -1), w2, b2.reshape(1, -1), f2.reshape(1, -1),
      w3f, w3b, b3f, b3b, deltas)


def _dft_tables(n):
    nn = 2 * n
    n1, n2 = nn // LANE, LANE
    a = n1 // 2
    k1 = np.arange(n1)[:, None]
    j1 = np.arange(n1)[None, :]
    ang1 = 2.0 * np.pi * (k1 * j1 % n1) / n1
    fr, fi = np.cos(ang1), -np.sin(ang1)
    m1 = np.block([[fr[:, :a], -fi[:, :a]], [fi[:, :a], fr[:, :a]]])
    m1_real = np.concatenate([fr, fi], axis=0)
    er, ei = fr.T[:a], -fi.T[:a]
    m3 = np.stack([np.concatenate([er, -ei], axis=1), np.concatenate([ei, er], axis=1)]) / nn
    k2 = np.arange(n2)[:, None]
    j2 = np.arange(n2)[None, :]
    ang2 = 2.0 * np.pi * (k2 * j2 % n2) / n2
    f2 = np.stack([np.cos(ang2), -np.sin(ang2)])
    angt = 2.0 * np.pi * (np.arange(n1)[:, None] * j2) / nn
    tw = np.stack([np.cos(angt), -np.sin(angt)], axis=1)
    return tuple(jnp.asarray(t, F32) for t in (m1, m1_real, m3, f2, tw))


def _fft_first_kernel(ur_ref, ui_ref, m_ref, o_ref, acc_ref):
    n1 = o_ref.shape[2]
    for s in range(ur_ref.shape[2]):
        r = _dot(m_ref[...], jnp.concatenate([ur_ref[0, :, s, :], ui_ref[0, :, s, :]], axis=0))
        acc_ref[0, :, s, :] = r[:n1]
        acc_ref[1, :, s, :] = r[n1:]
    o_ref[0] = acc_ref[...].astype(o_ref.dtype)


def _fft_first(u, m1):
    b2, a, _, c = u.shape
    p = b2 // 2
    n1 = m1.shape[0] // 2
    js = FFT_JS
    return pl.pallas_call(
        _fft_first_kernel,
        grid=(p, LANE // js),
        in_specs=[pl.BlockSpec((1, a, js, c), lambda pp, j: (2 * pp, 0, j, 0)),
                  pl.BlockSpec((1, a, js, c), lambda pp, j: (2 * pp + 1, 0, j, 0)),
                  pl.BlockSpec(m1.shape, lambda pp, j: (0, 0))],
        out_specs=pl.BlockSpec((1, 2, n1, js, c), lambda pp, j: (pp, 0, 0, j, 0)),
        out_shape=jax.ShapeDtypeStruct((p, 2, n1, LANE, c), MXU_DTYPE),
        scratch_shapes=[pltpu.VMEM((2, n1, js, c), F32)],
        compiler_params=_cparams(("parallel", "arbitrary")),
        name="fft_first",
    )(u, u, m1)


def _twiddled_dft(f2_ref, tw_ref, u):
    fr, fi = f2_ref[0], f2_ref[1]
    tr, ti = tw_ref[u, 0:1, :], tw_ref[u, 1:2, :]
    return fr * tr - fi * ti, fr * ti + fi * tr


def _real_form(gr, gi):
    return jnp.concatenate([jnp.concatenate([gr, -gi], axis=1), jnp.concatenate([gi, gr], axis=1)], axis=0)


def _spectrum_kernel(a_ref, ss_ref, f2_ref, tw_ref, h_ref):
    scale = lax.rsqrt(ss_ref[...] + EPS)
    for u in range(a_ref.shape[2]):
        xin = jnp.concatenate([a_ref[0, 0, u], a_ref[0, 1, u]], axis=0)
        x = _dot(_real_form(*_twiddled_dft(f2_ref, tw_ref, u)), xin) * scale
        h_ref[0, u] = x[:LANE]
        h_ref[1, u] = x[LANE:]


def _filter_spectrum(a5, ss, f2, tw):
    _, _, n1, _, c = a5.shape
    kb = math.gcd(n1, FFT_KB)
    return pl.pallas_call(
        _spectrum_kernel,
        grid=(n1 // kb,),
        in_specs=[pl.BlockSpec((1, 2, kb, LANE, c), lambda k: (0, 0, k, 0, 0)),
                  pl.BlockSpec((1, c), lambda k: (0, 0)),
                  pl.BlockSpec((2, LANE, LANE), lambda k: (0, 0, 0)),
                  pl.BlockSpec((kb, 2, LANE), lambda k: (k, 0, 0))],
        out_specs=pl.BlockSpec((2, kb, LANE, c), lambda k: (0, k, 0, 0)),
        out_shape=jax.ShapeDtypeStruct((2, n1, LANE, c), F32),
        compiler_params=_cparams(("arbitrary",)),
        name="filter_spectrum",
    )(a5, ss, f2, tw)


def _fft_mid_kernel(a_ref, h_ref, f2_ref, tw_ref, v_ref):
    npair = a_ref.shape[0]
    c = a_ref.shape[4]
    kb = a_ref.shape[2]
    gs = [_twiddled_dft(f2_ref, tw_ref, u) for u in range(kb)]
    xs = []
    for u, (gr, gi) in enumerate(gs):
        xin = jnp.concatenate(
            [jnp.concatenate([a_ref[p, 0, u], a_ref[p, 1, u]], axis=0) for p in range(npair)], axis=1)
        xs.append(_dot(_real_form(gr, gi), xin))
    vs = []
    for u, ((gr, gi), x) in enumerate(zip(gs, xs)):
        xr, xi = x[:LANE], x[LANE:]
        hr = jnp.concatenate([h_ref[0, u]] * npair, axis=1)
        hi = jnp.concatenate([h_ref[1, u]] * npair, axis=1)
        y = jnp.concatenate([xr * hr - xi * hi, xr * hi + xi * hr], axis=0)
        vs.append(_dot(_real_form(gr.T, -gi.T), y))
    for u, v in enumerate(vs):
        for p in range(npair):
            v_ref[p, 0, u] = v[:LANE, p * c:(p + 1) * c].astype(v_ref.dtype)
            v_ref[p, 1, u] = v[LANE:, p * c:(p + 1) * c].astype(v_ref.dtype)


def _fft_mid(a5, hspec, order, f2, tw):
    npair, _, n1, _, c = a5.shape
    kb = math.gcd(n1, FFT_KB)
    blk = pl.BlockSpec((npair, 2, kb, LANE, c), lambda k: (0, 0, k, 0, 0))
    return pl.pallas_call(
        _fft_mid_kernel,
        grid=(n1 // kb,),
        in_specs=[blk,
                  pl.BlockSpec((2, kb, LANE, c), lambda k: (0, k, 0, order)),
                  pl.BlockSpec((2, LANE, LANE), lambda k: (0, 0, 0)),
                  pl.BlockSpec((kb, 2, LANE), lambda k: (k, 0, 0))],
        out_specs=blk,
        out_shape=jax.ShapeDtypeStruct(a5.shape, MXU_DTYPE),
        compiler_params=_cparams(("arbitrary",)),
        name="fft_mid",
    )(a5, hspec, f2, tw)


def _fft_last_kernel(v_ref, m_ref, z_ref, gate_ref, bias_ref, o_ref, vf_ref):
    vf_ref[...] = v_ref[0].astype(F32)
    for s in range(z_ref.shape[2]):
        vs = jnp.concatenate([vf_ref[0, :, s, :], vf_ref[1, :, s, :]], axis=0)
        zf = _dot(m_ref[0], vs)
        o_ref[0, :, s, :] = gate_ref[0, :, s, :] * (zf + z_ref[0, :, s, :] * bias_ref[...])


def _fft_last(v5, m3, z, gate, bias):
    b, a, _, c = z.shape
    n1 = v5.shape[2]
    js = FFT_JS
    row = pl.BlockSpec((1, a, js, c), lambda j, bb: (bb, 0, j, 0))
    return pl.pallas_call(
        _fft_last_kernel,
        grid=(LANE // js, b),
        in_specs=[pl.BlockSpec((1, 2, n1, js, c), lambda j, bb: (bb // 2, 0, 0, j, 0)),
                  pl.BlockSpec((1, a, 2 * n1), lambda j, bb: (bb % 2, 0, 0)),
                  row, row,
                  pl.BlockSpec((1, c), lambda j, bb: (0, 0))],
        out_specs=row,
        out_shape=jax.ShapeDtypeStruct(z.shape, F32),
        scratch_shapes=[pltpu.VMEM((2, n1, js, c), F32)],
        compiler_params=_cparams(("parallel", "arbitrary")),
        name="fft_last",
    )(v5, m3, z, gate, bias)


def _hyena_long(v, x1, x2, filt, ss, bias):
    b, n, c = v.shape
    a = n // LANE
    m1, m1_real, m3, f2, tw = _dft_tables(n)
    hspec = _filter_spectrum(_fft_first(filt.reshape(2, a, LANE, filt.shape[2]), m1_real), ss, f2, tw)
    z = v.reshape(b, a, LANE, c)
    gates = (x1.reshape(b, a, LANE, c), x2.reshape(b, a, LANE, c))
    for o in range(HY_ORDER):
        z = _fft_last(_fft_mid(_fft_first(z, m1), hspec, o, f2, tw), m3, z, gates[o], bias[o].reshape(1, c))
    return z.reshape(b, n, c)


def _hyena_ctx_kernel(v_ref, x1_ref, x2_ref, filt_ref, ss_ref, bias_ref, ff_ref, fc_ref, fi_ref, o_ref):
    b, n, c = v_ref.shape
    npair = b // 2
    hs = _dot(ff_ref[...], jnp.concatenate([filt_ref[0], filt_ref[1]], axis=0))
    hs = hs * lax.rsqrt(ss_ref[...] + EPS)
    z = [v_ref[i] for i in range(b)]
    gates = (x1_ref, x2_ref)
    for o in range(HY_ORDER):
        hr = jnp.concatenate([hs[:2 * n, o * c:(o + 1) * c]] * npair, axis=1)
        hi = jnp.concatenate([hs[2 * n:, o * c:(o + 1) * c]] * npair, axis=1)
        xin = jnp.concatenate([jnp.concatenate([z[2 * p] for p in range(npair)], axis=1),
                               jnp.concatenate([z[2 * p + 1] for p in range(npair)], axis=1)], axis=0)
        x = _dot(fc_ref[...], xin)
        xr, xi = x[:2 * n], x[2 * n:]
        y = jnp.concatenate([xr * hr - xi * hi, xr * hi + xi * hr], axis=0)
        zf = _dot(fi_ref[...], y)
        bo = bias_ref[o:o + 1, :]
        for i in range(b):
            p, part = i // 2, i % 2
            conv = zf[part * n:(part + 1) * n, p * c:(p + 1) * c]
            z[i] = gates[o][i] * (conv + z[i] * bo)
    for i in range(b):
        o_ref[i] = z[i]


def _hyena_ctx(v, x1, x2, filt, ss, bias):
    b, n, c = v.shape
    nn = 2 * n
    k = np.arange(nn)[:, None]
    j = np.arange(nn)[None, :]
    ang = 2.0 * np.pi * (k * j % nn) / nn
    fr, fi = np.cos(ang), -np.sin(ang)
    ff = np.concatenate([fr, fi], axis=0)
    fc = np.block([[fr[:, :n], -fi[:, :n]], [fi[:, :n], fr[:, :n]]])
    er, ei = fr[:n], -fi[:n]
    finv = np.block([[er, -ei], [ei, er]]) / nn
    return pl.pallas_call(
        _hyena_ctx_kernel,
        out_shape=jax.ShapeDtypeStruct((b, n, c), F32),
        compiler_params=pltpu.CompilerParams(vmem_limit_bytes=VMEM_LIMIT),
        name="hyena_ctx",
    )(v, x1, x2, filt, ss, bias, jnp.asarray(ff, F32), jnp.asarray(fc, F32), jnp.asarray(finv, F32))


def _window_kernel(n_ctx, sink_ref, q_ref, k_ref, v_ref, o_ref):
    g = pl.program_id(1)
    i = pl.program_id(2)
    rep, tq = q_ref.shape[1], q_ref.shape[3]
    t = k_ref.shape[2]
    ql, win = ATT_QL, CHUNK
    wk = ql + 2 * win
    ids = [(c, h) for c in range(tq // ql) for h in range(rep)]
    q0 = [i * tq + c * ql for c in range(tq // ql)]
    start = [pl.multiple_of(jnp.clip(q - win, 0, t - wk), LANE) for q in q0]

    kx = k_ref[0, 0, 0:n_ctx, :]
    vx = jnp.concatenate([v_ref[0, 0, u] for u in range(n_ctx // LANE)], axis=1)
    kl = [k_ref[0, 0, pl.ds(s, wk), :] for s in start]
    vl = [jnp.concatenate([v_ref[0, 0, s // LANE + u] for u in range(wk // LANE)], axis=1) for s in start]

    qs = [q_ref[0, h, :, c * ql:(c + 1) * ql] for c, h in ids]
    s_loc = [jnp.dot(kl[c], q, preferred_element_type=F32) for (c, h), q in zip(ids, qs)]
    s_ctx = [jnp.dot(kx, q, preferred_element_type=F32) for q in qs]

    diff = lax.broadcasted_iota(jnp.int32, (wk, ql), 0) - lax.broadcasted_iota(jnp.int32, (wk, ql), 1)
    krow = lax.broadcasted_iota(jnp.int32, (wk, 1), 0)
    p_loc, p_ctx, e_snk = [], [], []
    for n, (c, h) in enumerate(ids):
        d = diff + (start[c] - q0[c])
        ok = jnp.logical_and(jnp.abs(d) <= win, krow >= n_ctx - start[c])
        ok = jnp.logical_and(ok, q0[c] >= n_ctx)
        sl = jnp.where(ok, s_loc[n], NEG)
        snk = jnp.where(g == 0, sink_ref[0:1, h:h + 1], sink_ref[0:1, rep + h:rep + h + 1]) * LOG2E
        m = jnp.maximum(jnp.maximum(jnp.max(sl, axis=0, keepdims=True),
                                    jnp.max(s_ctx[n], axis=0, keepdims=True)), snk)
        p_loc.append(jnp.exp2(sl - m).astype(MXU_DTYPE))
        p_ctx.append(jnp.exp2(s_ctx[n] - m).astype(MXU_DTYPE))
        e_snk.append(jnp.exp2(snk - m))

    acc = [jnp.dot(vl[c], p_loc[n], preferred_element_type=F32)
           + jnp.dot(vx, p_ctx[n], preferred_element_type=F32) for n, (c, h) in enumerate(ids)]
    outs = [a[:HEAD_DIM] * (1.0 / (a[HEAD_DIM:HEAD_DIM + 1] + e)) for a, e in zip(acc, e_snk)]
    for c in range(tq // ql):
        o_ref[0, c * ql:(c + 1) * ql, :] = jnp.concatenate(outs[c * rep:(c + 1) * rep], axis=0).T


def _window_attention(qt, k, vt, sink, n_ctx):
    b, _, hd, t = qt.shape
    rep = N_HEADS // N_KV
    tq = ATT_TQ if t % ATT_TQ == 0 else ATT_QL
    assert n_ctx == ATT_QL and t % ATT_QL == 0 and rep * hd == LANE and t >= ATT_QL + 2 * CHUNK
    return pl.pallas_call(
        functools.partial(_window_kernel, n_ctx),
        grid=(b, N_KV, t // tq),
        in_specs=[pl.BlockSpec((SUBLANE, LANE), lambda bb, g, i: (0, 0)),
                  pl.BlockSpec((1, rep, hd, tq), lambda bb, g, i: (bb, g, 0, i)),
                  pl.BlockSpec((1, 1, t, hd), lambda bb, g, i: (bb, g, 0, 0)),
                  pl.BlockSpec((1, 1) + vt.shape[2:], lambda bb, g, i: (bb, g, 0, 0, 0))],
        out_specs=pl.BlockSpec((1, tq, rep * hd), lambda bb, g, i: (bb, i, g)),
        out_shape=jax.ShapeDtypeStruct((b, t, N_HEADS * hd), F32),
        compiler_params=_cparams(("parallel", "parallel", "arbitrary")),
        name="window_attention",
    )(sink, qt, k, vt)


def _dense_kernel(n_ctx, q_ref, k_ref, v_ref, o_ref, m_ref, alpha_ref, acc_ref, p_ref):
    i = pl.program_id(2)
    rep, tq = q_ref.shape[1], q_ref.shape[3]
    ql = ATT_QL
    nc = tq // ql

    def run(chunks, kt, nk):
        per = kt // LANE
        ids = [(c, h) for c in chunks for h in range(rep)]
        for c, h in ids:
            m_ref[c * rep + h] = jnp.full((1, ql), NEG, F32)
            alpha_ref[c * rep + h] = jnp.ones((1, ql), F32)
            acc_ref[c * rep + h] = jnp.zeros(acc_ref.shape[1:], F32)
            p_ref[c * rep + h, 0:kt] = jnp.zeros((kt, ql), p_ref.dtype)

        def scores(j):
            kb = k_ref[0, 0, pl.ds(pl.multiple_of(j * kt, kt), kt), :]
            return [jnp.dot(kb, q_ref[0, h, :, c * ql:(c + 1) * ql], preferred_element_type=F32) for c, h in ids]

        def values(j):
            vb = jnp.concatenate([v_ref[0, 0, j * per + u] for u in range(per)], axis=1)
            pvs = [jnp.dot(vb, p_ref[c * rep + h, 0:kt], preferred_element_type=F32) for c, h in ids]
            for (c, h), pv in zip(ids, pvs):
                n = c * rep + h
                acc_ref[n] = alpha_ref[n] * acc_ref[n] + pv

        def softmax(ss):
            for (c, h), s in zip(ids, ss):
                n = c * rep + h
                m = m_ref[n]
                mn = jnp.maximum(m, jnp.max(s, axis=0, keepdims=True))
                p_ref[n, 0:kt] = jnp.exp2(s - mn).astype(p_ref.dtype)
                alpha_ref[n] = jnp.exp2(m - mn)
                m_ref[n] = mn

        def body(j, carry):
            ss = scores(j)
            values(jnp.maximum(j - 1, 0))
            softmax(ss)
            return carry

        lax.fori_loop(0, nk, body, 0)
        values(nk - 1)
        for c in chunks:
            o = [acc_ref[c * rep + h] for h in range(rep)]
            o = [a[:HEAD_DIM] * (1.0 / a[HEAD_DIM:HEAD_DIM + 1]) for a in o]
            o_ref[0, c * ql:(c + 1) * ql, :] = jnp.concatenate(o, axis=0).T

    kt_all = p_ref.shape[1]
    nk_all = k_ref.shape[2] // kt_all

    @pl.when(i == 0)
    def _():
        run([0], n_ctx, 1)
        if nc > 1:
            run(list(range(1, nc)), kt_all, nk_all)

    @pl.when(i != 0)
    def _():
        run(list(range(nc)), kt_all, nk_all)


def _dense_attention(qt, k, vt, n_ctx):
    b, _, hd, t = qt.shape
    rep = N_HEADS // N_KV
    tq = DENSE_TQ if t % DENSE_TQ == 0 else ATT_QL
    kt = ATT_KT if t % ATT_KT == 0 else ATT_QL
    assert n_ctx == ATT_QL and t % ATT_QL == 0 and rep * hd == LANE
    return pl.pallas_call(
        functools.partial(_dense_kernel, n_ctx),
        grid=(b, N_KV, t // tq),
        in_specs=[pl.BlockSpec((1, rep, hd, tq), lambda bb, g, i: (bb, g, 0, i)),
                  pl.BlockSpec((1, 1, t, hd), lambda bb, g, i: (bb, g, 0, 0)),
                  pl.BlockSpec((1, 1) + vt.shape[2:], lambda bb, g, i: (bb, g, 0, 0, 0))],
        out_specs=pl.BlockSpec((1, tq, rep * hd), lambda bb, g, i: (bb, i, g)),
        out_shape=jax.ShapeDtypeStruct((b, t, N_HEADS * hd), F32),
        scratch_shapes=[pltpu.VMEM((rep * tq // ATT_QL, 1, ATT_QL), F32),
                        pltpu.VMEM((rep * tq // ATT_QL, 1, ATT_QL), F32),
                        pltpu.VMEM((rep * tq // ATT_QL, vt.shape[3], ATT_QL), F32),
                        pltpu.VMEM((rep * tq // ATT_QL, kt, ATT_QL), MXU_DTYPE)],
        compiler_params=_cparams(("parallel", "parallel", "arbitrary")),
        name="dense_attention",
    )(qt, k, vt)


def _mix_mlp_kernel(first, x_ref, ctx_ref, mod_ref, yf_ref, yb_ref, z_ref, gs_ref, hy_ref, hyc_ref,
                    yw_ref, yd_ref, gpost_ref, gpre_ref, gpost2_ref, wo_ref, w1_ref, w2_ref, o_ref):
    i = pl.program_id(1)
    if first:
        xv = jnp.where(i == 0, ctx_ref[0], x_ref[0])
        yh = jnp.where(i == 0, hyc_ref[0], hy_ref[0])
    else:
        xv = x_ref[0]
        yh = hy_ref[0]
    g1 = mod_ref[0, 0, 2:3, :]
    sh2 = mod_ref[0, 0, 3:4, :]
    sc2 = mod_ref[0, 0, 4:5, :]
    g2 = mod_ref[0, 0, 5:6, :]
    ya = _rms((yf_ref[0] + yb_ref[0]) * _silu(z_ref[0]), gs_ref[...])
    w = W_GROUP
    tm, d = xv.shape
    halves = [slice(r, r + tm // MLP_SPLIT) for r in range(0, tm, tm // MLP_SPLIT)]
    branches = (ya, yh, yw_ref[0], yd_ref[0])
    ys = [sum(_dot(br[rows], wo_ref[j * w:(j + 1) * w, :]) for j, br in enumerate(branches)) for rows in halves]
    x1s = [xv[rows] + g1 * _rms(y, gpost_ref[...]) for rows, y in zip(halves, ys)]
    hbs = [(_rms(x1, gpre_ref[...]) * (1.0 + sc2) + sh2).astype(MXU_DTYPE) for x1 in x1s]
    accs = [jnp.zeros((tm // MLP_SPLIT, d), F32) for _ in halves]
    for c in range(w1_ref.shape[1] // d):
        hidden = [jnp.maximum(jnp.dot(hb, w1_ref[:, c * d:(c + 1) * d], preferred_element_type=F32), 0.0)
                  for hb in hbs]
        accs = [acc + _dot(a * a, w2_ref[c * d:(c + 1) * d, :]) for acc, a in zip(accs, hidden)]
    for rows, x1, acc in zip(halves, x1s, accs):
        o_ref[0, rows] = x1 + g2 * _rms(acc, gpost2_ref[...])


def _mix_mlp(first, x, ctx, mod6, yf, yb, z, g_ssd, hy, hyc, yw, yd, g_post, g_pre2, g_post2, wo, w1, w2):
    b, t, _ = yf.shape
    d = wo.shape[1]
    tm = ROW_TILE
    off = 0 if first else 1
    nrow = t // tm - off
    if first:
        x_spec = pl.BlockSpec((1, tm, d), lambda bb, i: (bb, jnp.maximum(i - 1, 0), 0))
        hy_spec = pl.BlockSpec((1, tm, W_GROUP), lambda bb, i: (bb, jnp.maximum(i - 1, 0), 0))
        mod_spec = pl.BlockSpec((1, 1, 6, d), lambda bb, i: (bb, jnp.minimum(i, 1), 0, 0))
    else:
        x_spec = pl.BlockSpec((1, tm, d), lambda bb, i: (bb, i + 1, 0))
        hy_spec = pl.BlockSpec((1, tm, W_GROUP), lambda bb, i: (bb, i, 0))
        mod_spec = pl.BlockSpec((1, 1, 6, d), lambda bb, i: (bb, 1, 0, 0))
    first_blk = lambda w: pl.BlockSpec((1, tm, w), lambda bb, i: (bb, 0, 0))
    row = lambda w: pl.BlockSpec((1, tm, w), lambda bb, i: (bb, i + off, 0))
    vec = lambda w: pl.BlockSpec((1, w), lambda bb, i: (0, 0))
    full = lambda a: pl.BlockSpec(a.shape, lambda bb, i: (0, 0))
    return pl.pallas_call(
        functools.partial(_mix_mlp_kernel, first),
        grid=(b, nrow),
        in_specs=[x_spec, first_blk(d), mod_spec, row(W_GROUP), row(W_GROUP), row(W_GROUP), vec(W_GROUP),
                  hy_spec, first_blk(W_GROUP), row(W_GROUP), row(W_GROUP), vec(d), vec(d), vec(d),
                  full(wo), full(w1), full(w2)],
        out_specs=pl.BlockSpec((1, tm, d), lambda bb, i: (bb, i, 0)),
        out_shape=jax.ShapeDtypeStruct((b, nrow * tm, d), F32),
        compiler_params=_cparams(("parallel", "arbitrary")),
        name="mix_mlp",
    )(x, ctx, mod6, yf, yb, z, g_ssd, hy, hyc, yw, yd, g_post, g_pre2, g_post2, wo, w1, w2)


def _rope_tables(n, n_ctx):
    rows = n // GRID_W
    row = np.repeat(np.arange(rows, dtype=np.float64), GRID_W)
    col = np.tile(np.arange(GRID_W, dtype=np.float64), rows)
    n_freq = HEAD_DIM // 4
    inv = ROPE_THETA ** (-np.arange(n_freq, dtype=np.float64) / n_freq)
    ang = np.concatenate([row[:, None] * inv, col[:, None] * inv], axis=-1)
    cos, sin = np.cos(ang), np.sin(ang)
    cs = np.concatenate([np.ones((n_ctx, HEAD_DIM)), np.concatenate([cos, cos], axis=1)], axis=0)
    sn = np.concatenate([np.zeros((n_ctx, HEAD_DIM)), np.concatenate([-sin, sin], axis=1)], axis=0)
    return jnp.asarray(cs.T, F32), jnp.asarray(sn.T, F32)


def _pad_rows(a, rows):
    return jnp.pad(a, ((0, rows - a.shape[0]), (0, 0)))


def kernel(x, c, ctx, c_ctx, w_mod, b_mod, norm_mix_pre, norm_mix_post, norm_mlp_pre, norm_mlp_post, w_in, w_out, ssd_conv_w, ssd_conv_b, ssd_a_log, ssd_dt_bias, ssd_d, ssd_norm, hy_conv_w, hy_conv_b, hy_w1, hy_b1, hy_freq1, hy_w2, hy_b2, hy_freq2, hy_w3, hy_b3, hy_bias, attn_sink, q_norm, k_norm, mlp_w1, mlp_w2):
    b, n, d = x.shape
    n_ctx = ctx.shape[1]
    depth = w_mod.shape[0]
    assert n_ctx == ROW_TILE and n % ROW_TILE == 0 and b % 2 == 0 and b + 1 <= SUBLANE

    cc = _pad_rows(jnp.concatenate([c, c_ctx[None, :]], axis=0), SUBLANE)
    mod = _modulation(cc, w_mod, b_mod)
    cs, sn = _rope_tables(n, n_ctx)

    off_b = W_GROUP + SSD_XBC + 2 * N_HEADS
    xall = None
    for l in range(depth):
        first = l == 0
        need_ctx = l < depth - 1
        mod_lat = mod[l, :b].reshape(b, 1, 6, d)
        mod_ctx = jnp.broadcast_to(mod[l, b].reshape(1, 1, 6, d), (b, 1, 6, d))
        mod6 = jnp.concatenate([mod_ctx, mod_lat], axis=1)

        wl = w_in[l]
        off_c = off_b + 3 * W_GROUP
        off_d = off_c + (N_HEADS + 2 * N_KV) * HEAD_DIM
        w_row = jnp.concatenate(
            [wl[:, :W_GROUP + SSD_XBC], wl[:, off_b:off_c], wl[:, W_GROUP + SSD_XBC:off_b],
             jnp.zeros((d, D_ROW_PAD - off_c), F32)], axis=1).astype(MXU_DTYPE)
        w_att_t = jnp.concatenate([wl[:, off_d:], wl[:, off_c:off_d]], axis=1).T.astype(MXU_DTYPE)
        xin, cin = (x, ctx) if first else (xall, None)
        (z, u, dt, v, x1, x2, vc, x1c, x2c, qwt, kw, vwt, qdt, kd, vdt) = _in_projection(
            xin, cin, mod6, norm_mix_pre[l].reshape(1, d), w_row, w_att_t, cs, sn, q_norm[l], k_norm[l],
            ssd_conv_w[l], ssd_conv_b[l].reshape(1, -1), hy_conv_w[l], hy_conv_b[l].reshape(1, -1))

        par = jnp.zeros((SUBLANE, LANE), F32)
        par = par.at[0, :2 * N_HEADS].set(ssd_a_log[l].reshape(-1))
        par = par.at[1, :2 * N_HEADS].set(ssd_dt_bias[l].reshape(-1))
        par = par.at[2, :N_HEADS].set(ssd_d[l])
        yf, yb = _ssd(u, dt, par, n_ctx)

        filt_args = (hy_w1[l], hy_b1[l], hy_freq1[l], hy_w2[l], hy_b2[l], hy_freq2[l], hy_w3[l], hy_b3[l])
        yhy = _hyena_long(v, x1, x2, *_hyena_filters(n, *filt_args), hy_bias[l])
        if need_ctx:
            yhy_ctx = _hyena_ctx(vc, x1c, x2c, *_hyena_filters(n_ctx, *filt_args), hy_bias[l])
        else:
            yhy_ctx = yhy

        sink = jnp.zeros((SUBLANE, LANE), F32).at[0, :N_HEADS].set(attn_sink[l])
        yw = _window_attention(qwt, kw, vwt, sink, n_ctx)
        yd = _dense_attention(qdt, kd, vdt, n_ctx)

        xres, cres = (x, ctx) if first else (xall, xall)
        xall = _mix_mlp(first, xres, cres, mod6, yf, yb, z, ssd_norm[l].reshape(1, -1), yhy, yhy_ctx, yw, yd,
                        norm_mix_post[l].reshape(1, d), norm_mlp_pre[l].reshape(1, d),
                        norm_mlp_post[l].reshape(1, d), w_out[l].astype(MXU_DTYPE),
                        mlp_w1[l].astype(MXU_DTYPE), mlp_w2[l].astype(MXU_DTYPE))
    return xall
```

```python
import functools
import math

import numpy as np
import jax
import jax.numpy as jnp
from jax import lax
from jax.experimental import pallas as pl
from jax.experimental.pallas import tpu as pltpu

F32 = jnp.float32
MXU_DTYPE = jnp.bfloat16

EPS = 1e-6
HEAD_DIM = 64
GRID_W = 64
ROPE_THETA = 10000.0
N_HEADS = 4
N_KV = 2
W_GROUP = N_HEADS * HEAD_DIM
SSD_STATE = 64
SSD_XBC = W_GROUP + 2 * N_KV * SSD_STATE
HY_ORDER = 2
HY_BANDS = 16
HY_EMB = 2 * HY_BANDS + 1
HY_FILT = 64
HY_MAX_DECAY = math.log(1e-2) / 0.3
HY_MIN_DECAY = math.log(1e-2) / 1.5
CHUNK = 128
LANE = 128
SUBLANE = 8
ROW_TILE = 256
ATT_TQ = 2816
DENSE_TQ = 8448
FFT_JS = 32
FFT_KB = 16
MLP_SPLIT = 2
VMEM_LIMIT = 56 * 1024 * 1024
NEG = -1e30

C_Z = (0, 256)
C_XBC = (256, 768)
C_HY = (768, 1536)
C_DT = (1536, 1664)
D_ROW_PAD = 1664
A_QD, A_KD, A_VD = 0, 256, 384
A_QW, A_KW, A_VW = 512, 768, 896
V_ROWS = HEAD_DIM + 16
ATT_QL = 256
ATT_KT = 256
LOG2E = math.log2(math.e)


def _cparams(sem):
    return pltpu.CompilerParams(dimension_semantics=sem, vmem_limit_bytes=VMEM_LIMIT)


def _rms(x, g):
    return x * lax.rsqrt(jnp.mean(x * x, axis=-1, keepdims=True) + EPS) * g


def _silu(x):
    return x * (1.0 / (1.0 + jnp.exp(-x)))


def _softplus(x):
    return jnp.maximum(x, 0.0) + jnp.log(1.0 + jnp.exp(-jnp.abs(x)))


def _dot(a, b):
    return jnp.dot(a.astype(MXU_DTYPE), b.astype(MXU_DTYPE), preferred_element_type=F32)


def _dot_nt(a, b):
    return lax.dot_general(a.astype(MXU_DTYPE), b.astype(MXU_DTYPE), (((1,), (1,)), ((), ())),
                           preferred_element_type=F32)


def _dot_f32(a, b):
    return jnp.dot(a, b, preferred_element_type=F32, precision=lax.Precision.HIGHEST)


def _dot_split(a, b):
    ah, bh = a.astype(MXU_DTYPE), b.astype(MXU_DTYPE)
    al = (a - ah.astype(F32)).astype(MXU_DTYPE)
    bl = (b - bh.astype(F32)).astype(MXU_DTYPE)
    dot = functools.partial(jnp.dot, preferred_element_type=F32)
    return dot(ah, bh) + dot(ah, bl) + dot(al, bh)


def _mod_kernel(c_ref, w_ref, b_ref, o_ref):
    o_ref[0] = _dot_f32(_silu(c_ref[...]), w_ref[0]) + b_ref[0]


def _modulation(cc, w_mod, b_mod):
    depth, d, d6 = w_mod.shape
    tn = d6 // 4
    return pl.pallas_call(
        _mod_kernel,
        grid=(depth, d6 // tn),
        in_specs=[pl.BlockSpec((SUBLANE, d), lambda l, j: (0, 0)),
                  pl.BlockSpec((1, d, tn), lambda l, j: (l, 0, j)),
                  pl.BlockSpec((1, 1, tn), lambda l, j: (l, 0, j))],
        out_specs=pl.BlockSpec((1, SUBLANE, tn), lambda l, j: (l, 0, j)),
        out_shape=jax.ShapeDtypeStruct((depth, SUBLANE, d6), F32),
        compiler_params=_cparams(("arbitrary", "arbitrary")),
        name="modulation",
    )(cc, w_mod, b_mod.reshape(depth, 1, d6))


def _inproj_kernel(x_ref, xp_ref, xn_ref, ctx_ref, mod_ref, g_ref, w_ref, wt_ref, cst_ref, snt_ref, qnt_ref, knt_ref,
                   scw_ref, scb_ref, hcw_ref, hcb_ref,
                   z_ref, u_ref, dt_ref, v_ref, x1_ref, x2_ref, vc_ref, x1c_ref, x2c_ref,
                   qwt_ref, kw_ref, vwt_ref, qdt_ref, kd_ref, vdt_ref, ext_ref):
    i = pl.program_id(1)
    nrow = pl.num_programs(1)
    xv = jnp.where(i == 0, ctx_ref[0], x_ref[0])
    tm = xv.shape[0]
    sh = mod_ref[0, 0, 0:1, :]
    sc = mod_ref[0, 0, 1:2, :]

    def modulated(rows):
        return (_rms(rows, g_ref[...]) * (1.0 + sc) + sh).astype(MXU_DTYPE)

    hb = modulated(xv)

    def proj(cols):
        return jnp.dot(hb, w_ref[:, cols[0]:cols[1]], preferred_element_type=F32)

    z_ref[0] = proj(C_Z)
    dt_ref[0] = proj(C_DT)

    conv_cols = (C_XBC[0], C_HY[1])
    halo = jnp.dot(modulated(jnp.concatenate([xp_ref[0], xn_ref[0]], axis=0)),
                   w_ref[:, conv_cols[0]:conv_cols[1]], preferred_element_type=F32)
    prev_ok = i > 1
    next_ok = jnp.logical_and(i >= 1, i < nrow - 1)
    ext_ref[0:SUBLANE] = jnp.where(prev_ok, halo[0:SUBLANE], 0.0)
    ext_ref[SUBLANE:SUBLANE + tm] = proj(conv_cols)
    ext_ref[SUBLANE + tm:] = jnp.where(next_ok, halo[SUBLANE:], 0.0)

    def conv(w_r, b_r, lo, hi):
        taps = w_r.shape[0]
        acc = b_r[...]
        for k in range(taps):
            off = SUBLANE - taps // 2 + k
            acc = acc + w_r[k:k + 1, :] * ext_ref[off:off + tm, lo:hi]
        return acc

    nx = C_XBC[1] - C_XBC[0]
    u_ref[0] = _silu(conv(scw_ref, scb_ref, 0, nx))
    hyc = conv(hcw_ref, hcb_ref, nx, nx + C_HY[1] - C_HY[0])
    parts = [hyc[:, j * W_GROUP:(j + 1) * W_GROUP] for j in range(3)]
    for ref, val in zip((v_ref, x1_ref, x2_ref), parts):
        ref[0] = val

    @pl.when(i == 0)
    def _():
        for ref, val in zip((vc_ref, x1c_ref, x2c_ref), parts):
            ref[0] = val

    pt = _dot_nt(wt_ref[...], hb)
    cst = cst_ref[...]
    snt = snt_ref[...]
    tile = lambda r: jnp.concatenate([r[...]] * (tm // LANE), axis=1)
    qnt, knt = tile(qnt_ref), tile(knt_ref)
    half = HEAD_DIM // 2
    qscale = HEAD_DIM ** -0.5 * LOG2E

    def head(row0):
        return pt[row0:row0 + HEAD_DIM]

    def norm(t, gain):
        return t * lax.rsqrt(jnp.mean(t * t, axis=0, keepdims=True) + EPS) * gain

    def rope(t):
        return t * cst + jnp.concatenate([t[half:], t[:half]], axis=0) * snt

    def put_values(v_ref, row0):
        ones = jnp.ones((v_ref.shape[3] - HEAD_DIM, LANE), v_ref.dtype)
        for g in range(N_KV):
            vt = head(row0 + g * HEAD_DIM)
            for j in range(tm // LANE):
                v_ref[0, g, j, 0:HEAD_DIM, :] = vt[:, j * LANE:(j + 1) * LANE].astype(v_ref.dtype)
                v_ref[0, g, j, HEAD_DIM:, :] = ones

    def put_keys(k_ref, kt_pair):
        k_rows = jnp.concatenate(kt_pair, axis=0).T
        for g in range(N_KV):
            k_ref[0, g] = k_rows[:, g * HEAD_DIM:(g + 1) * HEAD_DIM].astype(k_ref.dtype)

    for h in range(N_HEADS):
        qdt_ref[0, h] = (rope(norm(head(A_QD + h * HEAD_DIM), qnt)) * qscale).astype(qdt_ref.dtype)
        qwt_ref[0, h] = (rope(head(A_QW + h * HEAD_DIM)) * qscale).astype(qwt_ref.dtype)
    put_keys(kd_ref, [rope(norm(head(A_KD + g * HEAD_DIM), knt)) for g in range(N_KV)])
    put_keys(kw_ref, [rope(head(A_KW + g * HEAD_DIM)) for g in range(N_KV)])
    put_values(vdt_ref, A_VD)
    put_values(vwt_ref, A_VW)


def _in_projection(x, ctx, mod6, g_pre, w_row, w_att_t, cst, snt, qn, kn, ssd_cw, ssd_cb, hy_cw, hy_cb):
    b, _, d = x.shape
    tm = ROW_TILE
    hp = tm // SUBLANE
    if ctx is None:
        t = x.shape[1]
        first_lat = 1
        x_spec = pl.BlockSpec((1, tm, d), lambda bb, i: (bb, i, 0))
        ctx_arr, ctx_spec = x, pl.BlockSpec((1, tm, d), lambda bb, i: (bb, 0, 0))
    else:
        assert ctx.shape[1] == tm
        t = x.shape[1] + tm
        first_lat = 0
        x_spec = pl.BlockSpec((1, tm, d), lambda bb, i: (bb, jnp.maximum(i - 1, 0), 0))
        ctx_arr, ctx_spec = ctx, pl.BlockSpec((1, tm, d), lambda bb, i: (bb, 0, 0))
    nrow = t // tm
    last8 = x.shape[1] // SUBLANE - 1
    xp_spec = pl.BlockSpec((1, SUBLANE, d), lambda bb, i: (bb, jnp.clip((i - 1 + first_lat) * hp - 1, 0, last8), 0))
    xn_spec = pl.BlockSpec((1, SUBLANE, d), lambda bb, i: (bb, jnp.clip((i + first_lat) * hp, 0, last8), 0))
    per = tm // LANE
    row = lambda w: pl.BlockSpec((1, tm, w), lambda bb, i: (bb, i, 0))
    lat = pl.BlockSpec((1, tm, W_GROUP), lambda bb, i: (bb, jnp.maximum(i - 1, 0), 0))
    cblk = pl.BlockSpec((1, tm, W_GROUP), lambda bb, i: (bb, 0, 0))
    full = lambda a: pl.BlockSpec(a.shape, lambda bb, i: (0, 0))
    f32 = lambda w: jax.ShapeDtypeStruct((b, t, w), F32)
    lat_shape = jax.ShapeDtypeStruct((b, t - tm, W_GROUP), F32)
    ctx_shape = jax.ShapeDtypeStruct((b, tm, W_GROUP), F32)
    q_spec = pl.BlockSpec((1, N_HEADS, HEAD_DIM, tm), lambda bb, i: (bb, 0, 0, i))
    k_spec = pl.BlockSpec((1, N_KV, tm, HEAD_DIM), lambda bb, i: (bb, 0, i, 0))
    v_spec = pl.BlockSpec((1, N_KV, per, V_ROWS, LANE), lambda bb, i: (bb, 0, i, 0, 0))
    q_shape = jax.ShapeDtypeStruct((b, N_HEADS, HEAD_DIM, t), MXU_DTYPE)
    k_shape = jax.ShapeDtypeStruct((b, N_KV, t, HEAD_DIM), MXU_DTYPE)
    v_shape = jax.ShapeDtypeStruct((b, N_KV, t // LANE, V_ROWS, LANE), MXU_DTYPE)
    gain = lambda v: jnp.broadcast_to(v.reshape(HEAD_DIM, 1), (HEAD_DIM, LANE))
    qnt, knt = gain(qn), gain(kn)
    return pl.pallas_call(
        _inproj_kernel,
        grid=(b, nrow),
        in_specs=[x_spec, xp_spec, xn_spec, ctx_spec,
                  pl.BlockSpec((1, 1, 6, d), lambda bb, i: (bb, jnp.minimum(i, 1), 0, 0)),
                  pl.BlockSpec((1, d), lambda bb, i: (0, 0)),
                  full(w_row), full(w_att_t),
                  pl.BlockSpec((HEAD_DIM, tm), lambda bb, i: (0, i)),
                  pl.BlockSpec((HEAD_DIM, tm), lambda bb, i: (0, i)),
                  full(qnt), full(knt), full(ssd_cw), full(ssd_cb), full(hy_cw), full(hy_cb)],
        out_specs=[row(W_GROUP), row(SSD_XBC), row(LANE), lat, lat, lat, cblk, cblk, cblk,
                   q_spec, k_spec, v_spec, q_spec, k_spec, v_spec],
        out_shape=[f32(W_GROUP), f32(SSD_XBC), f32(LANE), lat_shape, lat_shape, lat_shape,
                   ctx_shape, ctx_shape, ctx_shape,
                   q_shape, k_shape, v_shape, q_shape, k_shape, v_shape],
        scratch_shapes=[pltpu.VMEM((tm + 2 * SUBLANE, C_HY[1] - C_XBC[0]), F32)],
        compiler_params=_cparams(("parallel", "arbitrary")),
        name="in_projection",
    )(x, x, x, ctx_arr, mod6, g_pre, w_row, w_att_t, cst, snt, qnt, knt, ssd_cw, ssd_cb, hy_cw, hy_cb)


def _ssd_kernel(nc, nt, par_ref, uf_ref, dtf_ref, ub_ref, dtb_ref, yf_ref, yb_ref, st_ref):
    j = pl.program_id(0)
    q = CHUNK
    rep = N_HEADS // N_KV

    @pl.when(j == 0)
    def _():
        st_ref[...] = jnp.zeros(st_ref.shape, F32)

    a_all = -jnp.exp(par_ref[0:1, :])
    ri = lax.broadcasted_iota(jnp.int32, (q, q), 0)
    ci = lax.broadcasted_iota(jnp.int32, (q, q), 1)
    dirs = []
    for bi in range(uf_ref.shape[0]):
        for d, (u_ref, dt_ref, y_ref) in enumerate(((uf_ref, dtf_ref, yf_ref), (ub_ref, dtb_ref, yb_ref))):
            u = u_ref[bi]
            dtv = _softplus(dt_ref[bi] + par_ref[1:2, :])
            mask = (ri >= ci) if d == 0 else (ri <= ci)
            cum = _dot_f32(mask.astype(F32), dtv * a_all)
            dirs.append(dict(
                n=len(dirs), b=bi, d=d, y_ref=y_ref, mask=mask, cum=cum, cum_t=cum.T, dt_t=dtv.T,
                end=q - 1 if d == 0 else 0, xs=u[:, :W_GROUP],
                bm_t=u[:, W_GROUP:W_GROUP + N_KV * SSD_STATE].T,
                cm=u[:, W_GROUP + N_KV * SSD_STATE:]))

    def head_terms(v, h):
        hl = N_HEADS * v["d"] + h
        g = h // rep
        sl = slice(g * SSD_STATE, (g + 1) * SSD_STATE)
        return dict(col=v["cum"][:, hl:hl + 1], row=v["cum_t"][hl:hl + 1, :], dt_row=v["dt_t"][hl:hl + 1, :],
                    last=v["cum"][v["end"]:v["end"] + 1, hl:hl + 1], cg=v["cm"][:, sl], bg_t=v["bm_t"][sl, :],
                    xh=v["xs"][:, h * HEAD_DIM:(h + 1) * HEAD_DIM])

    terms = [[head_terms(v, h) for h in range(N_HEADS)] for v in dirs]
    scores = [[_dot(ts[g * rep]["cg"], ts[g * rep]["bg_t"]) for g in range(N_KV)] for ts in terms]
    carried = [[_dot(t["cg"], st_ref[v["b"], v["d"], h]) for h, t in enumerate(ts)] for v, ts in zip(dirs, terms)]
    states = [[_dot(t["bg_t"] * (jnp.exp(t["last"] - t["row"]) * t["dt_row"]), t["xh"]) for t in ts]
              for ts in terms]
    diag = [[_dot(scores[v["n"]][h // rep] * jnp.exp(jnp.where(v["mask"], t["col"] - t["row"], NEG)) * t["dt_row"],
                  t["xh"]) for h, t in enumerate(ts)] for v, ts in zip(dirs, terms)]
    for v, ts in zip(dirs, terms):
        n, bi, d = v["n"], v["b"], v["d"]
        outs = []
        for h, t in enumerate(ts):
            y = diag[n][h] + carried[n][h] * jnp.exp(t["col"])
            st_ref[bi, d, h] = jnp.exp(t["last"]) * st_ref[bi, d, h] + states[n][h]
            if d == 0:
                y = y + par_ref[2:3, h:h + 1] * t["xh"]
            outs.append(y)
        v["y_ref"][bi] = jnp.concatenate(outs, axis=1)


def _ssd(u, dt, par, n_ctx):
    b, t, w = u.shape
    q = CHUNK
    nc, nt = n_ctx // q, t // q
    fwd = lambda j: j
    bwd = lambda j: jnp.where(j < nc, nc - 1 - j, nt + nc - 1 - j)

    def specs(cmap):
        return [pl.BlockSpec((b, q, w), lambda j: (0, cmap(j), 0)),
                pl.BlockSpec((b, q, LANE), lambda j: (0, cmap(j), 0))]

    return pl.pallas_call(
        functools.partial(_ssd_kernel, nc, nt),
        grid=(nt,),
        in_specs=[pl.BlockSpec((SUBLANE, LANE), lambda j: (0, 0))] + specs(fwd) + specs(bwd),
        out_specs=[pl.BlockSpec((b, q, W_GROUP), lambda j: (0, fwd(j), 0)),
                   pl.BlockSpec((b, q, W_GROUP), lambda j: (0, bwd(j), 0))],
        out_shape=[jax.ShapeDtypeStruct((b, t, W_GROUP), F32)] * 2,
        scratch_shapes=[pltpu.VMEM((b, 2, N_HEADS, SSD_STATE, HEAD_DIM), F32)],
        compiler_params=_cparams(("arbitrary",)),
        name="ssd_scan",
    )(par, u, dt, u, dt)


def _filter_kernel(zf_ref, zb_ref, w1_ref, b1_ref, f1_ref, w2_ref, b2_ref, f2_ref,
                   w3f_ref, w3b_ref, b3f_ref, b3b_ref, dl_ref, o_ref, ss_ref):
    i = pl.program_id(0)
    tr = zf_ref.shape[0]

    def half(z_ref, w3_ref, b3_ref):
        z = z_ref[...]
        h = jnp.sin(f1_ref[...] * (_dot_split(z, w1_ref[...]) + b1_ref[...]))
        h = jnp.sin(f2_ref[...] * (_dot_split(h, w2_ref[...]) + b2_ref[...]))
        k = _dot_split(h, w3_ref[...]) + b3_ref[...]
        return k * jnp.exp(-z[:, 0:1] * dl_ref[...])

    kf = half(zf_ref, w3f_ref, b3f_ref)
    kb = half(zb_ref, w3b_ref, b3b_ref)

    @pl.when(i == 0)
    def _():
        ss_ref[...] = jnp.zeros(ss_ref.shape, F32)

    ss_ref[...] += jnp.sum(kf * kf + kb * kb, axis=0, keepdims=True)
    rows = i * tr + lax.broadcasted_iota(jnp.int32, (tr, 1), 0)
    o_ref[0] = kf
    o_ref[1] = jnp.where(rows == 0, 0.0, kb)


def _hyena_filters(n, w1, b1, f1, w2, b2, f2, w3, b3):
    pos = np.arange(n, dtype=np.float64)
    t = np.linspace(0.0, 1.0, n)
    f = np.linspace(1e-4, HY_BANDS - 1, HY_BANDS)
    ang = 2.0 * math.pi * pos[:, None] * f[None, :] / n
    emb = np.concatenate([t[:, None], np.cos(ang), -np.sin(ang)], axis=-1)
    emb = np.pad(emb, ((0, 0), (0, LANE - HY_EMB)))
    emb_b = np.roll(np.flip(emb, axis=0), 1, axis=0)
    emb, emb_b = jnp.asarray(emb, F32), jnp.asarray(emb_b, F32)
    w1p = jnp.pad(w1, ((0, LANE - HY_EMB), (0, 0)))
    w3r = w3.reshape(HY_FILT, HY_ORDER, 2, W_GROUP)
    b3r = b3.reshape(HY_ORDER, 2, W_GROUP)
    wc = HY_ORDER * W_GROUP
    w3f, w3b = w3r[:, :, 0].reshape(HY_FILT, wc), w3r[:, :, 1].reshape(HY_FILT, wc)
    b3f, b3b = b3r[:, 0].reshape(1, wc), b3r[:, 1].reshape(1, wc)
    deltas = np.abs(np.linspace(HY_MIN_DECAY, HY_MAX_DECAY, W_GROUP))
    deltas = jnp.asarray(np.tile(deltas, HY_ORDER).reshape(1, wc), F32)
    tr = math.gcd(n, 512)
    const = lambda s: pl.BlockSpec(s, lambda i: (0, 0))
    rows = pl.BlockSpec((tr, LANE), lambda i: (i, 0))
    return pl.pallas_call(
        _filter_kernel,
        grid=(n // tr,),
        in_specs=[rows, rows, const((LANE, HY_FILT)), const((1, HY_FILT)), const((1, HY_FILT)),
                  const((HY_FILT, HY_FILT)), const((1, HY_FILT)), const((1, HY_FILT)),
                  const((HY_FILT, wc)), const((HY_FILT, wc)), const((1, wc)), const((1, wc)), const((1, wc))],
        out_specs=[pl.BlockSpec((2, tr, wc), lambda i: (0, i, 0)), const((1, wc))],
        out_shape=[jax.ShapeDtypeStruct((2, n, wc), F32), jax.ShapeDtypeStruct((1, wc), F32)],
        compiler_params=_cparams(("arbitrary",)),
        name="hyena_filter",
    )(emb, emb_b, w1p, b1.reshape(1, -1), f1.reshape(1, -1), w2, b2.reshape(1, -1), f2.reshape(1, -1),
      w3f, w3b, b3f, b3b, deltas)


def _dft_tables(n):
    nn = 2 * n
    n1, n2 = nn // LANE, LANE
    a = n1 // 2
    k1 = np.arange(n1)[:, None]
    j1 = np.arange(n1)[None, :]
    ang1 = 2.0 * np.pi * (k1 * j1 % n1) / n1
    fr, fi = np.cos(ang1), -np.sin(ang1)
    m1 = np.block([[fr[:, :a], -fi[:, :a]], [fi[:, :a], fr[:, :a]]])
    m1_real = np.concatenate([fr, fi], axis=0)
    er, ei = fr.T[:a], -fi.T[:a]
    m3 = np.stack([np.concatenate([er, -ei], axis=1), np.concatenate([ei, er], axis=1)]) / nn
    k2 = np.arange(n2)[:, None]
    j2 = np.arange(n2)[None, :]
    ang2 = 2.0 * np.pi * (k2 * j2 % n2) / n2
    f2 = np.stack([np.cos(ang2), -np.sin(ang2)])
    angt = 2.0 * np.pi * (np.arange(n1)[:, None] * j2) / nn
    tw = np.stack([np.cos(angt), -np.sin(angt)], axis=1)
    return tuple(jnp.asarray(t, F32) for t in (m1, m1_real, m3, f2, tw))


def _fft_first_kernel(ur_ref, ui_ref, m_ref, o_ref, acc_ref):
    n1 = o_ref.shape[2]
    for s in range(ur_ref.shape[2]):
        r = _dot(m_ref[...], jnp.concatenate([ur_ref[0, :, s, :], ui_ref[0, :, s, :]], axis=0))
        acc_ref[0, :, s, :] = r[:n1]
        acc_ref[1, :, s, :] = r[n1:]
    o_ref[0] = acc_ref[...].astype(o_ref.dtype)


def _fft_first(u, m1):
    b2, a, _, c = u.shape
    p = b2 // 2
    n1 = m1.shape[0] // 2
    js = FFT_JS
    return pl.pallas_call(
        _fft_first_kernel,
        grid=(p, LANE // js),
        in_specs=[pl.BlockSpec((1, a, js, c), lambda pp, j: (2 * pp, 0, j, 0)),
                  pl.BlockSpec((1, a, js, c), lambda pp, j: (2 * pp + 1, 0, j, 0)),
                  pl.BlockSpec(m1.shape, lambda pp, j: (0, 0))],
        out_specs=pl.BlockSpec((1, 2, n1, js, c), lambda pp, j: (pp, 0, 0, j, 0)),
        out_shape=jax.ShapeDtypeStruct((p, 2, n1, LANE, c), MXU_DTYPE),
        scratch_shapes=[pltpu.VMEM((2, n1, js, c), F32)],
        compiler_params=_cparams(("parallel", "arbitrary")),
        name="fft_first",
    )(u, u, m1)


def _twiddled_dft(f2_ref, tw_ref, u):
    fr, fi = f2_ref[0], f2_ref[1]
    tr, ti = tw_ref[u, 0:1, :], tw_ref[u, 1:2, :]
    return fr * tr - fi * ti, fr * ti + fi * tr


def _real_form(gr, gi):
    return jnp.concatenate([jnp.concatenate([gr, -gi], axis=1), jnp.concatenate([gi, gr], axis=1)], axis=0)


def _spectrum_kernel(a_ref, ss_ref, f2_ref, tw_ref, h_ref):
    scale = lax.rsqrt(ss_ref[...] + EPS)
    for u in range(a_ref.shape[2]):
        xin = jnp.concatenate([a_ref[0, 0, u], a_ref[0, 1, u]], axis=0)
        x = _dot(_real_form(*_twiddled_dft(f2_ref, tw_ref, u)), xin) * scale
        h_ref[0, u] = x[:LANE]
        h_ref[1, u] = x[LANE:]


def _filter_spectrum(a5, ss, f2, tw):
    _, _, n1, _, c = a5.shape
    kb = math.gcd(n1, FFT_KB)
    return pl.pallas_call(
        _spectrum_kernel,
        grid=(n1 // kb,),
        in_specs=[pl.BlockSpec((1, 2, kb, LANE, c), lambda k: (0, 0, k, 0, 0)),
                  pl.BlockSpec((1, c), lambda k: (0, 0)),
                  pl.BlockSpec((2, LANE, LANE), lambda k: (0, 0, 0)),
                  pl.BlockSpec((kb, 2, LANE), lambda k: (k, 0, 0))],
        out_specs=pl.BlockSpec((2, kb, LANE, c), lambda k: (0, k, 0, 0)),
        out_shape=jax.ShapeDtypeStruct((2, n1, LANE, c), F32),
        compiler_params=_cparams(("arbitrary",)),
        name="filter_spectrum",
    )(a5, ss, f2, tw)


def _fft_mid_kernel(a_ref, h_ref, f2_ref, tw_ref, v_ref):
    npair = a_ref.shape[0]
    c = a_ref.shape[4]
    kb = a_ref.shape[2]
    gs = [_twiddled_dft(f2_ref, tw_ref, u) for u in range(kb)]
    xs = []
    for u, (gr, gi) in enumerate(gs):
        xin = jnp.concatenate(
            [jnp.concatenate([a_ref[p, 0, u], a_ref[p, 1, u]], axis=0) for p in range(npair)], axis=1)
        xs.append(_dot(_real_form(gr, gi), xin))
    vs = []
    for u, ((gr, gi), x) in enumerate(zip(gs, xs)):
        xr, xi = x[:LANE], x[LANE:]
        hr = jnp.concatenate([h_ref[0, u]] * npair, axis=1)
        hi = jnp.concatenate([h_ref[1, u]] * npair, axis=1)
        y = jnp.concatenate([xr * hr - xi * hi, xr * hi + xi * hr], axis=0)
        vs.append(_dot(_real_form(gr.T, -gi.T), y))
    for u, v in enumerate(vs):
        for p in range(npair):
            v_ref[p, 0, u] = v[:LANE, p * c:(p + 1) * c].astype(v_ref.dtype)
            v_ref[p, 1, u] = v[LANE:, p * c:(p + 1) * c].astype(v_ref.dtype)


def _fft_mid(a5, hspec, order, f2, tw):
    npair, _, n1, _, c = a5.shape
    kb = math.gcd(n1, FFT_KB)
    blk = pl.BlockSpec((npair, 2, kb, LANE, c), lambda k: (0, 0, k, 0, 0))
    return pl.pallas_call(
        _fft_mid_kernel,
        grid=(n1 // kb,),
        in_specs=[blk,
                  pl.BlockSpec((2, kb, LANE, c), lambda k: (0, k, 0, order)),
                  pl.BlockSpec((2, LANE, LANE), lambda k: (0, 0, 0)),
                  pl.BlockSpec((kb, 2, LANE), lambda k: (k, 0, 0))],
        out_specs=blk,
        out_shape=jax.ShapeDtypeStruct(a5.shape, MXU_DTYPE),
        compiler_params=_cparams(("arbitrary",)),
        name="fft_mid",
    )(a5, hspec, f2, tw)


def _fft_last_kernel(v_ref, m_ref, z_ref, gate_ref, bias_ref, o_ref, vf_ref):
    vf_ref[...] = v_ref[0].astype(F32)
    for s in range(z_ref.shape[2]):
        vs = jnp.concatenate([vf_ref[0, :, s, :], vf_ref[1, :, s, :]], axis=0)
        zf = _dot(m_ref[0], vs)
        o_ref[0, :, s, :] = gate_ref[0, :, s, :] * (zf + z_ref[0, :, s, :] * bias_ref[...])


def _fft_last(v5, m3, z, gate, bias):
    b, a, _, c = z.shape
    n1 = v5.shape[2]
    js = FFT_JS
    row = pl.BlockSpec((1, a, js, c), lambda j, bb: (bb, 0, j, 0))
    return pl.pallas_call(
        _fft_last_kernel,
        grid=(LANE // js, b),
        in_specs=[pl.BlockSpec((1, 2, n1, js, c), lambda j, bb: (bb // 2, 0, 0, j, 0)),
                  pl.BlockSpec((1, a, 2 * n1), lambda j, bb: (bb % 2, 0, 0)),
                  row, row,
                  pl.BlockSpec((1, c), lambda j, bb: (0, 0))],
        out_specs=row,
        out_shape=jax.ShapeDtypeStruct(z.shape, F32),
        scratch_shapes=[pltpu.VMEM((2, n1, js, c), F32)],
        compiler_params=_cparams(("parallel", "arbitrary")),
        name="fft_last",
    )(v5, m3, z, gate, bias)


def _hyena_long(v, x1, x2, filt, ss, bias):
    b, n, c = v.shape
    a = n // LANE
    m1, m1_real, m3, f2, tw = _dft_tables(n)
    hspec = _filter_spectrum(_fft_first(filt.reshape(2, a, LANE, filt.shape[2]), m1_real), ss, f2, tw)
    z = v.reshape(b, a, LANE, c)
    gates = (x1.reshape(b, a, LANE, c), x2.reshape(b, a, LANE, c))
    for o in range(HY_ORDER):
        z = _fft_last(_fft_mid(_fft_first(z, m1), hspec, o, f2, tw), m3, z, gates[o], bias[o].reshape(1, c))
    return z.reshape(b, n, c)


def _hyena_ctx_kernel(v_ref, x1_ref, x2_ref, filt_ref, ss_ref, bias_ref, ff_ref, fc_ref, fi_ref, o_ref):
    b, n, c = v_ref.shape
    npair = b // 2
    hs = _dot(ff_ref[...], jnp.concatenate([filt_ref[0], filt_ref[1]], axis=0))
    hs = hs * lax.rsqrt(ss_ref[...] + EPS)
    z = [v_ref[i] for i in range(b)]
    gates = (x1_ref, x2_ref)
    for o in range(HY_ORDER):
        hr = jnp.concatenate([hs[:2 * n, o * c:(o + 1) * c]] * npair, axis=1)
        hi = jnp.concatenate([hs[2 * n:, o * c:(o + 1) * c]] * npair, axis=1)
        xin = jnp.concatenate([jnp.concatenate([z[2 * p] for p in range(npair)], axis=1),
                               jnp.concatenate([z[2 * p + 1] for p in range(npair)], axis=1)], axis=0)
        x = _dot(fc_ref[...], xin)
        xr, xi = x[:2 * n], x[2 * n:]
        y = jnp.concatenate([xr * hr - xi * hi, xr * hi + xi * hr], axis=0)
        zf = _dot(fi_ref[...], y)
        bo = bias_ref[o:o + 1, :]
        for i in range(b):
            p, part = i // 2, i % 2
            conv = zf[part * n:(part + 1) * n, p * c:(p + 1) * c]
            z[i] = gates[o][i] * (conv + z[i] * bo)
    for i in range(b):
        o_ref[i] = z[i]


def _hyena_ctx(v, x1, x2, filt, ss, bias):
    b, n, c = v.shape
    nn = 2 * n
    k = np.arange(nn)[:, None]
    j = np.arange(nn)[None, :]
    ang = 2.0 * np.pi * (k * j % nn) / nn
    fr, fi = np.cos(ang), -np.sin(ang)
    ff = np.concatenate([fr, fi], axis=0)
    fc = np.block([[fr[:, :n], -fi[:, :n]], [fi[:, :n], fr[:, :n]]])
    er, ei = fr[:n], -fi[:n]
    finv = np.block([[er, -ei], [ei, er]]) / nn
    return pl.pallas_call(
        _hyena_ctx_kernel,
        out_shape=jax.ShapeDtypeStruct((b, n, c), F32),
        compiler_params=pltpu.CompilerParams(vmem_limit_bytes=VMEM_LIMIT),
        name="hyena_ctx",
    )(v, x1, x2, filt, ss, bias, jnp.asarray(ff, F32), jnp.asarray(fc, F32), jnp.asarray(finv, F32))


def _window_kernel(n_ctx, sink_ref, q_ref, k_ref, v_ref, o_ref):
    g = pl.program_id(1)
    i = pl.program_id(2)
    rep, tq = q_ref.shape[1], q_ref.shape[3]
    t = k_ref.shape[2]
    ql, win = ATT_QL, CHUNK
    wk = ql + 2 * win
    ids = [(c, h) for c in range(tq // ql) for h in range(rep)]
    q0 = [i * tq + c * ql for c in range(tq // ql)]
    start = [pl.multiple_of(jnp.clip(q - win, 0, t - wk), LANE) for q in q0]

    kx = k_ref[0, 0, 0:n_ctx, :]
    vx = jnp.concatenate([v_ref[0, 0, u] for u in range(n_ctx // LANE)], axis=1)
    kl = [k_ref[0, 0, pl.ds(s, wk), :] for s in start]
    vl = [jnp.concatenate([v_ref[0, 0, s // LANE + u] for u in range(wk // LANE)], axis=1) for s in start]

    qs = [q_ref[0, h, :, c * ql:(c + 1) * ql] for c, h in ids]
    s_loc = [jnp.dot(kl[c], q, preferred_element_type=F32) for (c, h), q in zip(ids, qs)]
    s_ctx = [jnp.dot(kx, q, preferred_element_type=F32) for q in qs]

    diff = lax.broadcasted_iota(jnp.int32, (wk, ql), 0) - lax.broadcasted_iota(jnp.int32, (wk, ql), 1)
    krow = lax.broadcasted_iota(jnp.int32, (wk, 1), 0)
    p_loc, p_ctx, e_snk = [], [], []
    for n, (c, h) in enumerate(ids):
        d = diff + (start[c] - q0[c])
        ok = jnp.logical_and(jnp.abs(d) <= win, krow >= n_ctx - start[c])
        ok = jnp.logical_and(ok, q0[c] >= n_ctx)
        sl = jnp.where(ok, s_loc[n], NEG)
        snk = jnp.where(g == 0, sink_ref[0:1, h:h + 1], sink_ref[0:1, rep + h:rep + h + 1]) * LOG2E
        m = jnp.maximum(jnp.maximum(jnp.max(sl, axis=0, keepdims=True),
                                    jnp.max(s_ctx[n], axis=0, keepdims=True)), snk)
        p_loc.append(jnp.exp2(sl - m).astype(MXU_DTYPE))
        p_ctx.append(jnp.exp2(s_ctx[n] - m).astype(MXU_DTYPE))
        e_snk.append(jnp.exp2(snk - m))

    acc = [jnp.dot(vl[c], p_loc[n], preferred_element_type=F32)
           + jnp.dot(vx, p_ctx[n], preferred_element_type=F32) for n, (c, h) in enumerate(ids)]
    outs = [a[:HEAD_DIM] * (1.0 / (a[HEAD_DIM:HEAD_DIM + 1] + e)) for a, e in zip(acc, e_snk)]
    for c in range(tq // ql):
        o_ref[0, c * ql:(c + 1) * ql, :] = jnp.concatenate(outs[c * rep:(c + 1) * rep], axis=0).T


def _window_attention(qt, k, vt, sink, n_ctx):
    b, _, hd, t = qt.shape
    rep = N_HEADS // N_KV
    tq = ATT_TQ if t % ATT_TQ == 0 else ATT_QL
    assert n_ctx == ATT_QL and t % ATT_QL == 0 and rep * hd == LANE and t >= ATT_QL + 2 * CHUNK
    return pl.pallas_call(
        functools.partial(_window_kernel, n_ctx),
        grid=(b, N_KV, t // tq),
        in_specs=[pl.BlockSpec((SUBLANE, LANE), lambda bb, g, i: (0, 0)),
                  pl.BlockSpec((1, rep, hd, tq), lambda bb, g, i: (bb, g, 0, i)),
                  pl.BlockSpec((1, 1, t, hd), lambda bb, g, i: (bb, g, 0, 0)),
                  pl.BlockSpec((1, 1) + vt.shape[2:], lambda bb, g, i: (bb, g, 0, 0, 0))],
        out_specs=pl.BlockSpec((1, tq, rep * hd), lambda bb, g, i: (bb, i, g)),
        out_shape=jax.ShapeDtypeStruct((b, t, N_HEADS * hd), F32),
        compiler_params=_cparams(("parallel", "parallel", "arbitrary")),
        name="window_attention",
    )(sink, qt, k, vt)


def _dense_kernel(n_ctx, q_ref, k_ref, v_ref, o_ref, m_ref, alpha_ref, acc_ref, p_ref):
    i = pl.program_id(2)
    rep, tq = q_ref.shape[1], q_ref.shape[3]
    ql = ATT_QL
    nc = tq // ql

    def run(chunks, kt, nk):
        per = kt // LANE
        ids = [(c, h) for c in chunks for h in range(rep)]
        for c, h in ids:
            m_ref[c * rep + h] = jnp.full((1, ql), NEG, F32)
            alpha_ref[c * rep + h] = jnp.ones((1, ql), F32)
            acc_ref[c * rep + h] = jnp.zeros(acc_ref.shape[1:], F32)
            p_ref[c * rep + h, 0:kt] = jnp.zeros((kt, ql), p_ref.dtype)

        def scores(j):
            kb = k_ref[0, 0, pl.ds(pl.multiple_of(j * kt, kt), kt), :]
            return [jnp.dot(kb, q_ref[0, h, :, c * ql:(c + 1) * ql], preferred_element_type=F32) for c, h in ids]

        def values(j):
            vb = jnp.concatenate([v_ref[0, 0, j * per + u] for u in range(per)], axis=1)
            pvs = [jnp.dot(vb, p_ref[c * rep + h, 0:kt], preferred_element_type=F32) for c, h in ids]
            for (c, h), pv in zip(ids, pvs):
                n = c * rep + h
                acc_ref[n] = alpha_ref[n] * acc_ref[n] + pv

        def softmax(ss):
            for (c, h), s in zip(ids, ss):
                n = c * rep + h
                m = m_ref[n]
                mn = jnp.maximum(m, jnp.max(s, axis=0, keepdims=True))
                p_ref[n, 0:kt] = jnp.exp2(s - mn).astype(p_ref.dtype)
                alpha_ref[n] = jnp.exp2(m - mn)
                m_ref[n] = mn

        def body(j, carry):
            ss = scores(j)
            values(jnp.maximum(j - 1, 0))
            softmax(ss)
            return carry

        lax.fori_loop(0, nk, body, 0)
        values(nk - 1)
        for c in chunks:
            o = [acc_ref[c * rep + h] for h in range(rep)]
            o = [a[:HEAD_DIM] * (1.0 / a[HEAD_DIM:HEAD_DIM + 1]) for a in o]
            o_ref[0, c * ql:(c + 1) * ql, :] = jnp.concatenate(o, axis=0).T

    kt_all = p_ref.shape[1]
    nk_all = k_ref.shape[2] // kt_all

    @pl.when(i == 0)
    def _():
        run([0], n_ctx, 1)
        if nc > 1:
            run(list(range(1, nc)), kt_all, nk_all)

    @pl.when(i != 0)
    def _():
        run(list(range(nc)), kt_all, nk_all)


def _dense_attention(qt, k, vt, n_ctx):
    b, _, hd, t = qt.shape
    rep = N_HEADS // N_KV
    tq = DENSE_TQ if t % DENSE_TQ == 0 else ATT_QL
    kt = ATT_KT if t % ATT_KT == 0 else ATT_QL
    assert n_ctx == ATT_QL and t % ATT_QL == 0 and rep * hd == LANE
    return pl.pallas_call(
        functools.partial(_dense_kernel, n_ctx),
        grid=(b, N_KV, t // tq),
        in_specs=[pl.BlockSpec((1, rep, hd, tq), lambda bb, g, i: (bb, g, 0, i)),
                  pl.BlockSpec((1, 1, t, hd), lambda bb, g, i: (bb, g, 0, 0)),
                  pl.BlockSpec((1, 1) + vt.shape[2:], lambda bb, g, i: (bb, g, 0, 0, 0))],
        out_specs=pl.BlockSpec((1, tq, rep * hd), lambda bb, g, i: (bb, i, g)),
        out_shape=jax.ShapeDtypeStruct((b, t, N_HEADS * hd), F32),
        scratch_shapes=[pltpu.VMEM((rep * tq // ATT_QL, 1, ATT_QL), F32),
                        pltpu.VMEM((rep * tq // ATT_QL, 1, ATT_QL), F32),
                        pltpu.VMEM((rep * tq // ATT_QL, vt.shape[3], ATT_QL), F32),
                        pltpu.VMEM((rep * tq // ATT_QL, kt, ATT_QL), MXU_DTYPE)],
        compiler_params=_cparams(("parallel", "parallel", "arbitrary")),
        name="dense_attention",
    )(qt, k, vt)


def _mix_mlp_kernel(first, x_ref, ctx_ref, mod_ref, yf_ref, yb_ref, z_ref, gs_ref, hy_ref, hyc_ref,
                    yw_ref, yd_ref, gpost_ref, gpre_ref, gpost2_ref, wo_ref, w1_ref, w2_ref, o_ref):
    i = pl.program_id(1)
    if first:
        xv = jnp.where(i == 0, ctx_ref[0], x_ref[0])
        yh = jnp.where(i == 0, hyc_ref[0], hy_ref[0])
    else:
        xv = x_ref[0]
        yh = hy_ref[0]
    g1 = mod_ref[0, 0, 2:3, :]
    sh2 = mod_ref[0, 0, 3:4, :]
    sc2 = mod_ref[0, 0, 4:5, :]
    g2 = mod_ref[0, 0, 5:6, :]
    ya = _rms((yf_ref[0] + yb_ref[0]) * _silu(z_ref[0]), gs_ref[...])
    w = W_GROUP
    tm, d = xv.shape
    halves = [slice(r, r + tm // MLP_SPLIT) for r in range(0, tm, tm // MLP_SPLIT)]
    branches = (ya, yh, yw_ref[0], yd_ref[0])
    ys = [sum(_dot(br[rows], wo_ref[j * w:(j + 1) * w, :]) for j, br in enumerate(branches)) for rows in halves]
    x1s = [xv[rows] + g1 * _rms(y, gpost_ref[...]) for rows, y in zip(halves, ys)]
    hbs = [(_rms(x1, gpre_ref[...]) * (1.0 + sc2) + sh2).astype(MXU_DTYPE) for x1 in x1s]
    accs = [jnp.zeros((tm // MLP_SPLIT, d), F32) for _ in halves]
    for c in range(w1_ref.shape[1] // d):
        hidden = [jnp.maximum(jnp.dot(hb, w1_ref[:, c * d:(c + 1) * d], preferred_element_type=F32), 0.0)
                  for hb in hbs]
        accs = [acc + _dot(a * a, w2_ref[c * d:(c + 1) * d, :]) for acc, a in zip(accs, hidden)]
    for rows, x1, acc in zip(halves, x1s, accs):
        o_ref[0, rows] = x1 + g2 * _rms(acc, gpost2_ref[...])


def _mix_mlp(first, x, ctx, mod6, yf, yb, z, g_ssd, hy, hyc, yw, yd, g_post, g_pre2, g_post2, wo, w1, w2):
    b, t, _ = yf.shape
    d = wo.shape[1]
    tm = ROW_TILE
    off = 0 if first else 1
    nrow = t // tm - off
    if first:
        x_spec = pl.BlockSpec((1, tm, d), lambda bb, i: (bb, jnp.maximum(i - 1, 0), 0))
        hy_spec = pl.BlockSpec((1, tm, W_GROUP), lambda bb, i: (bb, jnp.maximum(i - 1, 0), 0))
        mod_spec = pl.BlockSpec((1, 1, 6, d), lambda bb, i: (bb, jnp.minimum(i, 1), 0, 0))
    else:
        x_spec = pl.BlockSpec((1, tm, d), lambda bb, i: (bb, i + 1, 0))
        hy_spec = pl.BlockSpec((1, tm, W_GROUP), lambda bb, i: (bb, i, 0))
        mod_spec = pl.BlockSpec((1, 1, 6, d), lambda bb, i: (bb, 1, 0, 0))
    first_blk = lambda w: pl.BlockSpec((1, tm, w), lambda bb, i: (bb, 0, 0))
    row = lambda w: pl.BlockSpec((1, tm, w), lambda bb, i: (bb, i + off, 0))
    vec = lambda w: pl.BlockSpec((1, w), lambda bb, i: (0, 0))
    full = lambda a: pl.BlockSpec(a.shape, lambda bb, i: (0, 0))
    return pl.pallas_call(
        functools.partial(_mix_mlp_kernel, first),
        grid=(b, nrow),
        in_specs=[x_spec, first_blk(d), mod_spec, row(W_GROUP), row(W_GROUP), row(W_GROUP), vec(W_GROUP),
                  hy_spec, first_blk(W_GROUP), row(W_GROUP), row(W_GROUP), vec(d), vec(d), vec(d),
                  full(wo), full(w1), full(w2)],
        out_specs=pl.BlockSpec((1, tm, d), lambda bb, i: (bb, i, 0)),
        out_shape=jax.ShapeDtypeStruct((b, nrow * tm, d), F32),
        compiler_params=_cparams(("parallel", "arbitrary")),
        name="mix_mlp",
    )(x, ctx, mod6, yf, yb, z, g_ssd, hy, hyc, yw, yd, g_post, g_pre2, g_post2, wo, w1, w2)


def _rope_tables(n, n_ctx):
    rows = n // GRID_W
    row = np.repeat(np.arange(rows, dtype=np.float64), GRID_W)
    col = np.tile(np.arange(GRID_W, dtype=np.float64), rows)
    n_freq = HEAD_DIM // 4
    inv = ROPE_THETA ** (-np.arange(n_freq, dtype=np.float64) / n_freq)
    ang = np.concatenate([row[:, None] * inv, col[:, None] * inv], axis=-1)
    cos, sin = np.cos(ang), np.sin(ang)
    cs = np.concatenate([np.ones((n_ctx, HEAD_DIM)), np.concatenate([cos, cos], axis=1)], axis=0)
    sn = np.concatenate([np.zeros((n_ctx, HEAD_DIM)), np.concatenate([-sin, sin], axis=1)], axis=0)
    return jnp.asarray(cs.T, F32), jnp.asarray(sn.T, F32)


def _pad_rows(a, rows):
    return jnp.pad(a, ((0, rows - a.shape[0]), (0, 0)))


def kernel(x, c, ctx, c_ctx, w_mod, b_mod, norm_mix_pre, norm_mix_post, norm_mlp_pre, norm_mlp_post, w_in, w_out, ssd_conv_w, ssd_conv_b, ssd_a_log, ssd_dt_bias, ssd_d, ssd_norm, hy_conv_w, hy_conv_b, hy_w1, hy_b1, hy_freq1, hy_w2, hy_b2, hy_freq2, hy_w3, hy_b3, hy_bias, attn_sink, q_norm, k_norm, mlp_w1, mlp_w2):
    b, n, d = x.shape
    n_ctx = ctx.shape[1]
    depth = w_mod.shape[0]
    assert n_ctx == ROW_TILE and n % ROW_TILE == 0 and b % 2 == 0 and b + 1 <= SUBLANE

    cc = _pad_rows(jnp.concatenate([c, c_ctx[None, :]], axis=0), SUBLANE)
    mod = _modulation(cc, w_mod, b_mod)
    cs, sn = _rope_tables(n, n_ctx)

    off_b = W_GROUP + SSD_XBC + 2 * N_HEADS
    xall = None
    for l in range(depth):
        first = l == 0
        need_ctx = l < depth - 1
        mod_lat = mod[l, :b].reshape(b, 1, 6, d)
        mod_ctx = jnp.broadcast_to(mod[l, b].reshape(1, 1, 6, d), (b, 1, 6, d))
        mod6 = jnp.concatenate([mod_ctx, mod_lat], axis=1)

        wl = w_in[l]
        off_c = off_b + 3 * W_GROUP
        off_d = off_c + (N_HEADS + 2 * N_KV) * HEAD_DIM
        w_row = jnp.concatenate(
            [wl[:, :W_GROUP + SSD_XBC], wl[:, off_b:off_c], wl[:, W_GROUP + SSD_XBC:off_b],
             jnp.zeros((d, D_ROW_PAD - off_c), F32)], axis=1).astype(MXU_DTYPE)
        w_att_t = jnp.concatenate([wl[:, off_d:], wl[:, off_c:off_d]], axis=1).T.astype(MXU_DTYPE)
        xin, cin = (x, ctx) if first else (xall, None)
        (z, u, dt, v, x1, x2, vc, x1c, x2c, qwt, kw, vwt, qdt, kd, vdt) = _in_projection(
            xin, cin, mod6, norm_mix_pre[l].reshape(1, d), w_row, w_att_t, cs, sn, q_norm[l], k_norm[l],
            ssd_conv_w[l], ssd_conv_b[l].reshape(1, -1), hy_conv_w[l], hy_conv_b[l].reshape(1, -1))

        par = jnp.zeros((SUBLANE, LANE), F32)
        par = par.at[0, :2 * N_HEADS].set(ssd_a_log[l].reshape(-1))
        par = par.at[1, :2 * N_HEADS].set(ssd_dt_bias[l].reshape(-1))
        par = par.at[2, :N_HEADS].set(ssd_d[l])
        yf, yb = _ssd(u, dt, par, n_ctx)

        filt_args = (hy_w1[l], hy_b1[l], hy_freq1[l], hy_w2[l], hy_b2[l], hy_freq2[l], hy_w3[l], hy_b3[l])
        yhy = _hyena_long(v, x1, x2, *_hyena_filters(n, *filt_args), hy_bias[l])
        if need_ctx:
            yhy_ctx = _hyena_ctx(vc, x1c, x2c, *_hyena_filters(n_ctx, *filt_args), hy_bias[l])
        else:
            yhy_ctx = yhy

        sink = jnp.zeros((SUBLANE, LANE), F32).at[0, :N_HEADS].set(attn_sink[l])
        yw = _window_attention(qwt, kw, vwt, sink, n_ctx)
        yd = _dense_attention(qdt, kd, vdt, n_ctx)

        xres, cres = (x, ctx) if first else (xall, xall)
        xall = _mix_mlp(first, xres, cres, mod6, yf, yb, z, ssd_norm[l].reshape(1, -1), yhy, yhy_ctx, yw, yd,
                        norm_mix_post[l].reshape(1, d), norm_mlp_pre[l].reshape(1, d),
                        norm_mlp_post[l].reshape(1, d), w_out[l].astype(MXU_DTYPE),
                        mlp_w1[l].astype(MXU_DTYPE), mlp_w2[l].astype(MXU_DTYPE))
    return xall
```

```python
import functools
import math

import numpy as np
import jax
import jax.numpy as jnp
from jax import lax
from jax.experimental import pallas as pl
from jax.experimental.pallas import tpu as pltpu

F32 = jnp.float32
MXU_DTYPE = jnp.bfloat16

EPS = 1e-6
HEAD_DIM = 64
GRID_W = 64
ROPE_THETA = 10000.0
N_HEADS = 4
N_KV = 2
W_GROUP = N_HEADS * HEAD_DIM
SSD_STATE = 64
SSD_XBC = W_GROUP + 2 * N_KV * SSD_STATE
HY_ORDER = 2
HY_BANDS = 16
HY_EMB = 2 * HY_BANDS + 1
HY_FILT = 64
HY_MAX_DECAY = math.log(1e-2) / 0.3
HY_MIN_DECAY = math.log(1e-2) / 1.5
CHUNK = 128
LANE = 128
SUBLANE = 8
ROW_TILE = 256
ATT_TQ = 2816
DENSE_TQ = 8448
FFT_JS = 16
FFT_KB = 16
MLP_SPLIT = 2
VMEM_LIMIT = 56 * 1024 * 1024
NEG = -1e30

C_Z = (0, 256)
C_XBC = (256, 768)
C_HY = (768, 1536)
C_DT = (1536, 1664)
D_ROW_PAD = 1664
A_QD, A_KD, A_VD = 0, 256, 384
A_QW, A_KW, A_VW = 512, 768, 896
V_ROWS = HEAD_DIM + 16
ATT_QL = 256
ATT_KT = 256
LOG2E = math.log2(math.e)


def _cparams(sem):
    return pltpu.CompilerParams(dimension_semantics=sem, vmem_limit_bytes=VMEM_LIMIT)


def _rms(x, g):
    return x * lax.rsqrt(jnp.mean(x * x, axis=-1, keepdims=True) + EPS) * g


def _silu(x):
    return x * (1.0 / (1.0 + jnp.exp(-x)))


def _softplus(x):
    return jnp.maximum(x, 0.0) + jnp.log(1.0 + jnp.exp(-jnp.abs(x)))


def _dot(a, b):
    return jnp.dot(a.astype(MXU_DTYPE), b.astype(MXU_DTYPE), preferred_element_type=F32)


def _dot_nt(a, b):
    return lax.dot_general(a.astype(MXU_DTYPE), b.astype(MXU_DTYPE), (((1,), (1,)), ((), ())),
                           preferred_element_type=F32)


def _dot_f32(a, b):
    return jnp.dot(a, b, preferred_element_type=F32, precision=lax.Precision.HIGHEST)


def _dot_split(a, b):
    ah, bh = a.astype(MXU_DTYPE), b.astype(MXU_DTYPE)
    al = (a - ah.astype(F32)).astype(MXU_DTYPE)
    bl = (b - bh.astype(F32)).astype(MXU_DTYPE)
    dot = functools.partial(jnp.dot, preferred_element_type=F32)
    return dot(ah, bh) + dot(ah, bl) + dot(al, bh)


def _mod_kernel(c_ref, w_ref, b_ref, o_ref):
    o_ref[0] = _dot_f32(_silu(c_ref[...]), w_ref[0]) + b_ref[0]


def _modulation(cc, w_mod, b_mod):
    depth, d, d6 = w_mod.shape
    tn = d6 // 4
    return pl.pallas_call(
        _mod_kernel,
        grid=(depth, d6 // tn),
        in_specs=[pl.BlockSpec((SUBLANE, d), lambda l, j: (0, 0)),
                  pl.BlockSpec((1, d, tn), lambda l, j: (l, 0, j)),
                  pl.BlockSpec((1, 1, tn), lambda l, j: (l, 0, j))],
        out_specs=pl.BlockSpec((1, SUBLANE, tn), lambda l, j: (l, 0, j)),
        out_shape=jax.ShapeDtypeStruct((depth, SUBLANE, d6), F32),
        compiler_params=_cparams(("arbitrary", "arbitrary")),
        name="modulation",
    )(cc, w_mod, b_mod.reshape(depth, 1, d6))


def _inproj_kernel(x_ref, xp_ref, xn_ref, ctx_ref, mod_ref, g_ref, w_ref, wt_ref, cst_ref, snt_ref, qnt_ref, knt_ref,
                   scw_ref, scb_ref, hcw_ref, hcb_ref,
                   z_ref, u_ref, dt_ref, v_ref, x1_ref, x2_ref, vc_ref, x1c_ref, x2c_ref,
                   qwt_ref, kw_ref, vwt_ref, qdt_ref, kd_ref, vdt_ref, ext_ref):
    i = pl.program_id(1)
    nrow = pl.num_programs(1)
    xv = jnp.where(i == 0, ctx_ref[0], x_ref[0])
    tm = xv.shape[0]
    sh = mod_ref[0, 0, 0:1, :]
    sc = mod_ref[0, 0, 1:2, :]

    def modulated(rows):
        return (_rms(rows, g_ref[...]) * (1.0 + sc) + sh).astype(MXU_DTYPE)

    hb = modulated(xv)

    def proj(cols):
        return jnp.dot(hb, w_ref[:, cols[0]:cols[1]], preferred_element_type=F32)

    z_ref[0] = proj(C_Z)
    dt_ref[0] = proj(C_DT)

    conv_cols = (C_XBC[0], C_HY[1])
    halo = jnp.dot(modulated(jnp.concatenate([xp_ref[0], xn_ref[0]], axis=0)),
                   w_ref[:, conv_cols[0]:conv_cols[1]], preferred_element_type=F32)
    prev_ok = i > 1
    next_ok = jnp.logical_and(i >= 1, i < nrow - 1)
    ext_ref[0:SUBLANE] = jnp.where(prev_ok, halo[0:SUBLANE], 0.0)
    ext_ref[SUBLANE:SUBLANE + tm] = proj(conv_cols)
    ext_ref[SUBLANE + tm:] = jnp.where(next_ok, halo[SUBLANE:], 0.0)

    def conv(w_r, b_r, lo, hi):
        taps = w_r.shape[0]
        acc = b_r[...]
        for k in range(taps):
            off = SUBLANE - taps // 2 + k
            acc = acc + w_r[k:k + 1, :] * ext_ref[off:off + tm, lo:hi]
        return acc

    nx = C_XBC[1] - C_XBC[0]
    u_ref[0] = _silu(conv(scw_ref, scb_ref, 0, nx))
    hyc = conv(hcw_ref, hcb_ref, nx, nx + C_HY[1] - C_HY[0])
    parts = [hyc[:, j * W_GROUP:(j + 1) * W_GROUP] for j in range(3)]
    for ref, val in zip((v_ref, x1_ref, x2_ref), parts):
        ref[0] = val

    @pl.when(i == 0)
    def _():
        for ref, val in zip((vc_ref, x1c_ref, x2c_ref), parts):
            ref[0] = val

    pt = _dot_nt(wt_ref[...], hb)
    cst = cst_ref[...]
    snt = snt_ref[...]
    tile = lambda r: jnp.concatenate([r[...]] * (tm // LANE), axis=1)
    qnt, knt = tile(qnt_ref), tile(knt_ref)
    half = HEAD_DIM // 2
    qscale = HEAD_DIM ** -0.5 * LOG2E

    def head(row0):
        return pt[row0:row0 + HEAD_DIM]

    def norm(t, gain):
        return t * lax.rsqrt(jnp.mean(t * t, axis=0, keepdims=True) + EPS) * gain

    def rope(t):
        return t * cst + jnp.concatenate([t[half:], t[:half]], axis=0) * snt

    def put_values(v_ref, row0):
        ones = jnp.ones((v_ref.shape[3] - HEAD_DIM, LANE), v_ref.dtype)
        for g in range(N_KV):
            vt = head(row0 + g * HEAD_DIM)
            for j in range(tm // LANE):
                v_ref[0, g, j, 0:HEAD_DIM, :] = vt[:, j * LANE:(j + 1) * LANE].astype(v_ref.dtype)
                v_ref[0, g, j, HEAD_DIM:, :] = ones

    def put_keys(k_ref, kt_pair):
        k_rows = jnp.concatenate(kt_pair, axis=0).T
        for g in range(N_KV):
            k_ref[0, g] = k_rows[:, g * HEAD_DIM:(g + 1) * HEAD_DIM].astype(k_ref.dtype)

    for h in range(N_HEADS):
        qdt_ref[0, h] = (rope(norm(head(A_QD + h * HEAD_DIM), qnt)) * qscale).astype(qdt_ref.dtype)
        qwt_ref[0, h] = (rope(head(A_QW + h * HEAD_DIM)) * qscale).astype(qwt_ref.dtype)
    put_keys(kd_ref, [rope(norm(head(A_KD + g * HEAD_DIM), knt)) for g in range(N_KV)])
    put_keys(kw_ref, [rope(head(A_KW + g * HEAD_DIM)) for g in range(N_KV)])
    put_values(vdt_ref, A_VD)
    put_values(vwt_ref, A_VW)


def _in_projection(x, ctx, mod6, g_pre, w_row, w_att_t, cst, snt, qn, kn, ssd_cw, ssd_cb, hy_cw, hy_cb):
    b, _, d = x.shape
    tm = ROW_TILE
    hp = tm // SUBLANE
    if ctx is None:
        t = x.shape[1]
        first_lat = 1
        x_spec = pl.BlockSpec((1, tm, d), lambda bb, i: (bb, i, 0))
        ctx_arr, ctx_spec = x, pl.BlockSpec((1, tm, d), lambda bb, i: (bb, 0, 0))
    else:
        assert ctx.shape[1] == tm
        t = x.shape[1] + tm
        first_lat = 0
        x_spec = pl.BlockSpec((1, tm, d), lambda bb, i: (bb, jnp.maximum(i - 1, 0), 0))
        ctx_arr, ctx_spec = ctx, pl.BlockSpec((1, tm, d), lambda bb, i: (bb, 0, 0))
    nrow = t // tm
    last8 = x.shape[1] // SUBLANE - 1
    xp_spec = pl.BlockSpec((1, SUBLANE, d), lambda bb, i: (bb, jnp.clip((i - 1 + first_lat) * hp - 1, 0, last8), 0))
    xn_spec = pl.BlockSpec((1, SUBLANE, d), lambda bb, i: (bb, jnp.clip((i + first_lat) * hp, 0, last8), 0))
    per = tm // LANE
    row = lambda w: pl.BlockSpec((1, tm, w), lambda bb, i: (bb, i, 0))
    lat = pl.BlockSpec((1, tm, W_GROUP), lambda bb, i: (bb, jnp.maximum(i - 1, 0), 0))
    cblk = pl.BlockSpec((1, tm, W_GROUP), lambda bb, i: (bb, 0, 0))
    full = lambda a: pl.BlockSpec(a.shape, lambda bb, i: (0, 0))
    f32 = lambda w: jax.ShapeDtypeStruct((b, t, w), F32)
    lat_shape = jax.ShapeDtypeStruct((b, t - tm, W_GROUP), F32)
    ctx_shape = jax.ShapeDtypeStruct((b, tm, W_GROUP), F32)
    q_spec = pl.BlockSpec((1, N_HEADS, HEAD_DIM, tm), lambda bb, i: (bb, 0, 0, i))
    k_spec = pl.BlockSpec((1, N_KV, tm, HEAD_DIM), lambda bb, i: (bb, 0, i, 0))
    v_spec = pl.BlockSpec((1, N_KV, per, V_ROWS, LANE), lambda bb, i: (bb, 0, i, 0, 0))
    q_shape = jax.ShapeDtypeStruct((b, N_HEADS, HEAD_DIM, t), MXU_DTYPE)
    k_shape = jax.ShapeDtypeStruct((b, N_KV, t, HEAD_DIM), MXU_DTYPE)
    v_shape = jax.ShapeDtypeStruct((b, N_KV, t // LANE, V_ROWS, LANE), MXU_DTYPE)
    gain = lambda v: jnp.broadcast_to(v.reshape(HEAD_DIM, 1), (HEAD_DIM, LANE))
    qnt, knt = gain(qn), gain(kn)
    return pl.pallas_call(
        _inproj_kernel,
        grid=(b, nrow),
        in_specs=[x_spec, xp_spec, xn_spec, ctx_spec,
                  pl.BlockSpec((1, 1, 6, d), lambda bb, i: (bb, jnp.minimum(i, 1), 0, 0)),
                  pl.BlockSpec((1, d), lambda bb, i: (0, 0)),
                  full(w_row), full(w_att_t),
                  pl.BlockSpec((HEAD_DIM, tm), lambda bb, i: (0, i)),
                  pl.BlockSpec((HEAD_DIM, tm), lambda bb, i: (0, i)),
                  full(qnt), full(knt), full(ssd_cw), full(ssd_cb), full(hy_cw), full(hy_cb)],
        out_specs=[row(W_GROUP), row(SSD_XBC), row(LANE), lat, lat, lat, cblk, cblk, cblk,
                   q_spec, k_spec, v_spec, q_spec, k_spec, v_spec],
        out_shape=[f32(W_GROUP), f32(SSD_XBC), f32(LANE), lat_shape, lat_shape, lat_shape,
                   ctx_shape, ctx_shape, ctx_shape,
                   q_shape, k_shape, v_shape, q_shape, k_shape, v_shape],
        scratch_shapes=[pltpu.VMEM((tm + 2 * SUBLANE, C_HY[1] - C_XBC[0]), F32)],
        compiler_params=_cparams(("parallel", "arbitrary")),
        name="in_projection",
    )(x, x, x, ctx_arr, mod6, g_pre, w_row, w_att_t, cst, snt, qnt, knt, ssd_cw, ssd_cb, hy_cw, hy_cb)


def _ssd_kernel(nc, nt, par_ref, uf_ref, dtf_ref, ub_ref, dtb_ref, yf_ref, yb_ref, st_ref):
    j = pl.program_id(0)
    q = CHUNK
    rep = N_HEADS // N_KV

    @pl.when(j == 0)
    def _():
        st_ref[...] = jnp.zeros(st_ref.shape, F32)

    a_all = -jnp.exp(par_ref[0:1, :])
    ri = lax.broadcasted_iota(jnp.int32, (q, q), 0)
    ci = lax.broadcasted_iota(jnp.int32, (q, q), 1)
    dirs = []
    for bi in range(uf_ref.shape[0]):
        for d, (u_ref, dt_ref, y_ref) in enumerate(((uf_ref, dtf_ref, yf_ref), (ub_ref, dtb_ref, yb_ref))):
            u = u_ref[bi]
            dtv = _softplus(dt_ref[bi] + par_ref[1:2, :])
            mask = (ri >= ci) if d == 0 else (ri <= ci)
            cum = _dot_f32(mask.astype(F32), dtv * a_all)
            dirs.append(dict(
                n=len(dirs), b=bi, d=d, y_ref=y_ref, mask=mask, cum=cum, cum_t=cum.T, dt_t=dtv.T,
                end=q - 1 if d == 0 else 0, xs=u[:, :W_GROUP],
                bm_t=u[:, W_GROUP:W_GROUP + N_KV * SSD_STATE].T,
                cm=u[:, W_GROUP + N_KV * SSD_STATE:]))

    def head_terms(v, h):
        hl = N_HEADS * v["d"] + h
        g = h // rep
        sl = slice(g * SSD_STATE, (g + 1) * SSD_STATE)
        return dict(col=jnp.broadcast_to(v["cum"][:, hl:hl + 1], (q, q)), row=v["cum_t"][hl:hl + 1, :],
                    dt_row=v["dt_t"][hl:hl + 1, :],
                    last=v["cum"][v["end"]:v["end"] + 1, hl:hl + 1], cg=v["cm"][:, sl], bg_t=v["bm_t"][sl, :],
                    xh=v["xs"][:, h * HEAD_DIM:(h + 1) * HEAD_DIM])

    terms = [[head_terms(v, h) for h in range(N_HEADS)] for v in dirs]
    scores = [[_dot(ts[g * rep]["cg"], ts[g * rep]["bg_t"]) for g in range(N_KV)] for ts in terms]
    carried = [[_dot(t["cg"], st_ref[v["b"], v["d"], h]) for h, t in enumerate(ts)] for v, ts in zip(dirs, terms)]
    states = [[_dot(t["bg_t"] * (jnp.exp(t["last"] - t["row"]) * t["dt_row"]), t["xh"]) for t in ts]
              for ts in terms]
    diag = [[_dot(scores[v["n"]][h // rep] * jnp.exp(jnp.where(v["mask"], t["col"] - t["row"], NEG)) * t["dt_row"],
                  t["xh"]) for h, t in enumerate(ts)] for v, ts in zip(dirs, terms)]
    for v, ts in zip(dirs, terms):
        n, bi, d = v["n"], v["b"], v["d"]
        outs = []
        for h, t in enumerate(ts):
            y = diag[n][h] + carried[n][h] * jnp.exp(t["col"][:, :HEAD_DIM])
            st_ref[bi, d, h] = jnp.exp(t["last"]) * st_ref[bi, d, h] + states[n][h]
            if d == 0:
                y = y + par_ref[2:3, h:h + 1] * t["xh"]
            outs.append(y)
        v["y_ref"][bi] = jnp.concatenate(outs, axis=1)


def _ssd(u, dt, par, n_ctx):
    b, t, w = u.shape
    q = CHUNK
    nc, nt = n_ctx // q, t // q
    fwd = lambda j: j
    bwd = lambda j: jnp.where(j < nc, nc - 1 - j, nt + nc - 1 - j)

    def specs(cmap):
        return [pl.BlockSpec((b, q, w), lambda j: (0, cmap(j), 0)),
                pl.BlockSpec((b, q, LANE), lambda j: (0, cmap(j), 0))]

    return pl.pallas_call(
        functools.partial(_ssd_kernel, nc, nt),
        grid=(nt,),
        in_specs=[pl.BlockSpec((SUBLANE, LANE), lambda j: (0, 0))] + specs(fwd) + specs(bwd),
        out_specs=[pl.BlockSpec((b, q, W_GROUP), lambda j: (0, fwd(j), 0)),
                   pl.BlockSpec((b, q, W_GROUP), lambda j: (0, bwd(j), 0))],
        out_shape=[jax.ShapeDtypeStruct((b, t, W_GROUP), F32)] * 2,
        scratch_shapes=[pltpu.VMEM((b, 2, N_HEADS, SSD_STATE, HEAD_DIM), F32)],
        compiler_params=_cparams(("arbitrary",)),
        name="ssd_scan",
    )(par, u, dt, u, dt)


def _filter_kernel(zf_ref, zb_ref, w1_ref, b1_ref, f1_ref, w2_ref, b2_ref, f2_ref,
                   w3f_ref, w3b_ref, b3f_ref, b3b_ref, dl_ref, o_ref, ss_ref):
    i = pl.program_id(0)
    tr = zf_ref.shape[0]

    def half(z_ref, w3_ref, b3_ref):
        z = z_ref[...]
        h = jnp.sin(f1_ref[...] * (_dot_split(z, w1_ref[...]) + b1_ref[...]))
        h = jnp.sin(f2_ref[...] * (_dot_split(h, w2_ref[...]) + b2_ref[...]))
        k = _dot_split(h, w3_ref[...]) + b3_ref[...]
        return k * jnp.exp(-z[:, 0:1] * dl_ref[...])

    kf = half(zf_ref, w3f_ref, b3f_ref)
    kb = half(zb_ref, w3b_ref, b3b_ref)

    @pl.when(i == 0)
    def _():
        ss_ref[...] = jnp.zeros(ss_ref.shape, F32)

    ss_ref[...] += jnp.sum(kf * kf + kb * kb, axis=0, keepdims=True)
    rows = i * tr + lax.broadcasted_iota(jnp.int32, (tr, 1), 0)
    o_ref[0] = kf
    o_ref[1] = jnp.where(rows == 0, 0.0, kb)


def _hyena_filters(n, w1, b1, f1, w2, b2, f2, w3, b3):
    pos = np.arange(n, dtype=np.float64)
    t = np.linspace(0.0, 1.0, n)
    f = np.linspace(1e-4, HY_BANDS - 1, HY_BANDS)
    ang = 2.0 * math.pi * pos[:, None] * f[None, :] / n
    emb = np.concatenate([t[:, None], np.cos(ang), -np.sin(ang)], axis=-1)
    emb = np.pad(emb, ((0, 0), (0, LANE - HY_EMB)))
    emb_b = np.roll(np.flip(emb, axis=0), 1, axis=0)
    emb, emb_b = jnp.asarray(emb, F32), jnp.asarray(emb_b, F32)
    w1p = jnp.pad(w1, ((0, LANE - HY_EMB), (0, 0)))
    w3r = w3.reshape(HY_FILT, HY_ORDER, 2, W_GROUP)
    b3r = b3.reshape(HY_ORDER, 2, W_GROUP)
    wc = HY_ORDER * W_GROUP
    w3f, w3b = w3r[:, :, 0].reshape(HY_FILT, wc), w3r[:, :, 1].reshape(HY_FILT, wc)
    b3f, b3b = b3r[:, 0].reshape(1, wc), b3r[:, 1].reshape(1, wc)
    deltas = np.abs(np.linspace(HY_MIN_DECAY, HY_MAX_DECAY, W_GROUP))
    deltas = jnp.asarray(np.tile(deltas, HY_ORDER).reshape(1, wc), F32)
    tr = math.gcd(n, 512)
    const = lambda s: pl.BlockSpec(s, lambda i: (0, 0))
    rows = pl.BlockSpec((tr, LANE), lambda i: (i, 0))
    return pl.pallas_call(
        _filter_kernel,
        grid=(n // tr,),
        in_specs=[rows, rows, const((LANE, HY_FILT)), const((1, HY_FILT)), const((1, HY_FILT)),
                  const((HY_FILT, HY_FILT)), const((1, HY_FILT)), const((1, HY_FILT)),
                  const((HY_FILT, wc)), const((HY_FILT, wc)), const((1, wc)), const((1, wc)), const((1, wc))],
        out_specs=[pl.BlockSpec((2, tr, wc), lambda i: (0, i, 0)), const((1, wc))],
        out_shape=[jax.ShapeDtypeStruct((2, n, wc), F32), jax.ShapeDtypeStruct((1, wc), F32)],
        compiler_params=_cparams(("arbitrary",)),
        name="hyena_filter",
    )(emb, emb_b, w1p, b1.reshape(1, -1), f1.reshape(1, -1), w2, b2.reshape(1, -1), f2.reshape(1, -1),
      w3f, w3b, b3f, b3b, deltas)


def _dft_tables(n):
    nn = 2 * n
    n1, n2 = nn // LANE, LANE
    a = n1 // 2
    k1 = np.arange(n1)[:, None]
    j1 = np.arange(n1)[None, :]
    ang1 = 2.0 * np.pi * (k1 * j1 % n1) / n1
    fr, fi = np.cos(ang1), -np.sin(ang1)
    m1 = np.block([[fr[:, :a], -fi[:, :a]], [fi[:, :a], fr[:, :a]]])
    m1_real = np.concatenate([fr, fi], axis=0)
    er, ei = fr.T[:a], -fi.T[:a]
    m3 = np.stack([np.concatenate([er, -ei], axis=1), np.concatenate([ei, er], axis=1)]) / nn
    k2 = np.arange(n2)[:, None]
    j2 = np.arange(n2)[None, :]
    ang2 = 2.0 * np.pi * (k2 * j2 % n2) / n2
    f2 = np.stack([np.cos(ang2), -np.sin(ang2)])
    angt = 2.0 * np.pi * (np.arange(n1)[:, None] * j2) / nn
    tw = np.stack([np.cos(angt), -np.sin(angt)], axis=1)
    return tuple(jnp.asarray(t, F32) for t in (m1, m1_real, m3, f2, tw))


def _fft_first_kernel(ur_ref, ui_ref, m_ref, o_ref, acc_ref):
    n1 = o_ref.shape[2]
    for s in range(ur_ref.shape[2]):
        r = _dot(m_ref[...], jnp.concatenate([ur_ref[0, :, s, :], ui_ref[0, :, s, :]], axis=0))
        acc_ref[0, :, s, :] = r[:n1]
        acc_ref[1, :, s, :] = r[n1:]
    o_ref[0] = acc_ref[...].astype(o_ref.dtype)


def _fft_first(u, m1):
    b2, a, _, c = u.shape
    p = b2 // 2
    n1 = m1.shape[0] // 2
    js = FFT_JS
    return pl.pallas_call(
        _fft_first_kernel,
        grid=(p, LANE // js),
        in_specs=[pl.BlockSpec((1, a, js, c), lambda pp, j: (2 * pp, 0, j, 0)),
                  pl.BlockSpec((1, a, js, c), lambda pp, j: (2 * pp + 1, 0, j, 0)),
                  pl.BlockSpec(m1.shape, lambda pp, j: (0, 0))],
        out_specs=pl.BlockSpec((1, 2, n1, js, c), lambda pp, j: (pp, 0, 0, j, 0)),
        out_shape=jax.ShapeDtypeStruct((p, 2, n1, LANE, c), MXU_DTYPE),
        scratch_shapes=[pltpu.VMEM((2, n1, js, c), F32)],
        compiler_params=_cparams(("parallel", "arbitrary")),
        name="fft_first",
    )(u, u, m1)


def _twiddled_dft(f2_ref, tw_ref, u):
    fr, fi = f2_ref[0], f2_ref[1]
    tr, ti = tw_ref[u, 0:1, :], tw_ref[u, 1:2, :]
    return fr * tr - fi * ti, fr * ti + fi * tr


def _real_form(gr, gi):
    return jnp.concatenate([jnp.concatenate([gr, -gi], axis=1), jnp.concatenate([gi, gr], axis=1)], axis=0)


def _spectrum_kernel(a_ref, ss_ref, f2_ref, tw_ref, h_ref):
    scale = lax.rsqrt(ss_ref[...] + EPS)
    for u in range(a_ref.shape[2]):
        xin = jnp.concatenate([a_ref[0, 0, u], a_ref[0, 1, u]], axis=0)
        x = _dot(_real_form(*_twiddled_dft(f2_ref, tw_ref, u)), xin) * scale
        h_ref[0, u] = x[:LANE]
        h_ref[1, u] = x[LANE:]


def _filter_spectrum(a5, ss, f2, tw):
    _, _, n1, _, c = a5.shape
    kb = math.gcd(n1, FFT_KB)
    return pl.pallas_call(
        _spectrum_kernel,
        grid=(n1 // kb,),
        in_specs=[pl.BlockSpec((1, 2, kb, LANE, c), lambda k: (0, 0, k, 0, 0)),
                  pl.BlockSpec((1, c), lambda k: (0, 0)),
                  pl.BlockSpec((2, LANE, LANE), lambda k: (0, 0, 0)),
                  pl.BlockSpec((kb, 2, LANE), lambda k: (k, 0, 0))],
        out_specs=pl.BlockSpec((2, kb, LANE, c), lambda k: (0, k, 0, 0)),
        out_shape=jax.ShapeDtypeStruct((2, n1, LANE, c), F32),
        compiler_params=_cparams(("arbitrary",)),
        name="filter_spectrum",
    )(a5, ss, f2, tw)


def _fft_mid_kernel(a_ref, h_ref, f2_ref, tw_ref, v_ref):
    npair = a_ref.shape[0]
    c = a_ref.shape[4]
    kb = a_ref.shape[2]
    gs = [_twiddled_dft(f2_ref, tw_ref, u) for u in range(kb)]
    xs = []
    for u, (gr, gi) in enumerate(gs):
        xin = jnp.concatenate(
            [jnp.concatenate([a_ref[p, 0, u], a_ref[p, 1, u]], axis=0) for p in range(npair)], axis=1)
        xs.append(_dot(_real_form(gr, gi), xin))
    vs = []
    for u, ((gr, gi), x) in enumerate(zip(gs, xs)):
        xr, xi = x[:LANE], x[LANE:]
        hr = jnp.concatenate([h_ref[0, u]] * npair, axis=1)
        hi = jnp.concatenate([h_ref[1, u]] * npair, axis=1)
        y = jnp.concatenate([xr * hr - xi * hi, xr * hi + xi * hr], axis=0)
        vs.append(_dot(_real_form(gr.T, -gi.T), y))
    for u, v in enumerate(vs):
        for p in range(npair):
            v_ref[p, 0, u] = v[:LANE, p * c:(p + 1) * c].astype(v_ref.dtype)
            v_ref[p, 1, u] = v[LANE:, p * c:(p + 1) * c].astype(v_ref.dtype)


def _fft_mid(a5, hspec, order, f2, tw):
    npair, _, n1, _, c = a5.shape
    kb = math.gcd(n1, FFT_KB)
    blk = pl.BlockSpec((npair, 2, kb, LANE, c), lambda k: (0, 0, k, 0, 0))
    return pl.pallas_call(
        _fft_mid_kernel,
        grid=(n1 // kb,),
        in_specs=[blk,
                  pl.BlockSpec((2, kb, LANE, c), lambda k: (0, k, 0, order)),
                  pl.BlockSpec((2, LANE, LANE), lambda k: (0, 0, 0)),
                  pl.BlockSpec((kb, 2, LANE), lambda k: (k, 0, 0))],
        out_specs=blk,
        out_shape=jax.ShapeDtypeStruct(a5.shape, MXU_DTYPE),
        compiler_params=_cparams(("arbitrary",)),
        name="fft_mid",
    )(a5, hspec, f2, tw)


def _fft_last_kernel(v_ref, m_ref, z_ref, gate_ref, bias_ref, o_ref, vf_ref):
    vf_ref[...] = v_ref[0].astype(F32)
    for s in range(z_ref.shape[2]):
        vs = jnp.concatenate([vf_ref[0, :, s, :], vf_ref[1, :, s, :]], axis=0)
        zf = _dot(m_ref[0], vs)
        o_ref[0, :, s, :] = gate_ref[0, :, s, :] * (zf + z_ref[0, :, s, :] * bias_ref[...])


def _fft_last(v5, m3, z, gate, bias):
    b, a, _, c = z.shape
    n1 = v5.shape[2]
    js = FFT_JS
    row = pl.BlockSpec((1, a, js, c), lambda j, bb: (bb, 0, j, 0))
    return pl.pallas_call(
        _fft_last_kernel,
        grid=(LANE // js, b),
        in_specs=[pl.BlockSpec((1, 2, n1, js, c), lambda j, bb: (bb // 2, 0, 0, j, 0)),
                  pl.BlockSpec((1, a, 2 * n1), lambda j, bb: (bb % 2, 0, 0)),
                  row, row,
                  pl.BlockSpec((1, c), lambda j, bb: (0, 0))],
        out_specs=row,
        out_shape=jax.ShapeDtypeStruct(z.shape, F32),
        scratch_shapes=[pltpu.VMEM((2, n1, js, c), F32)],
        compiler_params=_cparams(("parallel", "arbitrary")),
        name="fft_last",
    )(v5, m3, z, gate, bias)


def _hyena_long(v, x1, x2, filt, ss, bias):
    b, n, c = v.shape
    a = n // LANE
    m1, m1_real, m3, f2, tw = _dft_tables(n)
    hspec = _filter_spectrum(_fft_first(filt.reshape(2, a, LANE, filt.shape[2]), m1_real), ss, f2, tw)
    z = v.reshape(b, a, LANE, c)
    gates = (x1.reshape(b, a, LANE, c), x2.reshape(b, a, LANE, c))
    for o in range(HY_ORDER):
        z = _fft_last(_fft_mid(_fft_first(z, m1), hspec, o, f2, tw), m3, z, gates[o], bias[o].reshape(1, c))
    return z.reshape(b, n, c)


def _hyena_ctx_kernel(v_ref, x1_ref, x2_ref, filt_ref, ss_ref, bias_ref, ff_ref, fc_ref, fi_ref, o_ref):
    b, n, c = v_ref.shape
    npair = b // 2
    hs = _dot(ff_ref[...], jnp.concatenate([filt_ref[0], filt_ref[1]], axis=0))
    hs = hs * lax.rsqrt(ss_ref[...] + EPS)
    z = [v_ref[i] for i in range(b)]
    gates = (x1_ref, x2_ref)
    for o in range(HY_ORDER):
        hr = jnp.concatenate([hs[:2 * n, o * c:(o + 1) * c]] * npair, axis=1)
        hi = jnp.concatenate([hs[2 * n:, o * c:(o + 1) * c]] * npair, axis=1)
        xin = jnp.concatenate([jnp.concatenate([z[2 * p] for p in range(npair)], axis=1),
                               jnp.concatenate([z[2 * p + 1] for p in range(npair)], axis=1)], axis=0)
        x = _dot(fc_ref[...], xin)
        xr, xi = x[:2 * n], x[2 * n:]
        y = jnp.concatenate([xr * hr - xi * hi, xr * hi + xi * hr], axis=0)
        zf = _dot(fi_ref[...], y)
        bo = bias_ref[o:o + 1, :]
        for i in range(b):
            p, part = i // 2, i % 2
            conv = zf[part * n:(part + 1) * n, p * c:(p + 1) * c]
            z[i] = gates[o][i] * (conv + z[i] * bo)
    for i in range(b):
        o_ref[i] = z[i]


def _hyena_ctx(v, x1, x2, filt, ss, bias):
    b, n, c = v.shape
    nn = 2 * n
    k = np.arange(nn)[:, None]
    j = np.arange(nn)[None, :]
    ang = 2.0 * np.pi * (k * j % nn) / nn
    fr, fi = np.cos(ang), -np.sin(ang)
    ff = np.concatenate([fr, fi], axis=0)
    fc = np.block([[fr[:, :n], -fi[:, :n]], [fi[:, :n], fr[:, :n]]])
    er, ei = fr[:n], -fi[:n]
    finv = np.block([[er, -ei], [ei, er]]) / nn
    return pl.pallas_call(
        _hyena_ctx_kernel,
        out_shape=jax.ShapeDtypeStruct((b, n, c), F32),
        compiler_params=pltpu.CompilerParams(vmem_limit_bytes=VMEM_LIMIT),
        name="hyena_ctx",
    )(v, x1, x2, filt, ss, bias, jnp.asarray(ff, F32), jnp.asarray(fc, F32), jnp.asarray(finv, F32))


def _window_kernel(n_ctx, sink_ref, q_ref, k_ref, v_ref, o_ref):
    g = pl.program_id(1)
    i = pl.program_id(2)
    rep, tq = q_ref.shape[1], q_ref.shape[3]
    t = k_ref.shape[2]
    ql, win = ATT_QL, CHUNK
    wk = ql + 2 * win
    ids = [(c, h) for c in range(tq // ql) for h in range(rep)]
    q0 = [i * tq + c * ql for c in range(tq // ql)]
    start = [pl.multiple_of(jnp.clip(q - win, 0, t - wk), LANE) for q in q0]

    kx = k_ref[0, 0, 0:n_ctx, :]
    vx = jnp.concatenate([v_ref[0, 0, u] for u in range(n_ctx // LANE)], axis=1)
    kl = [k_ref[0, 0, pl.ds(s, wk), :] for s in start]
    vl = [jnp.concatenate([v_ref[0, 0, s // LANE + u] for u in range(wk // LANE)], axis=1) for s in start]

    qs = [q_ref[0, h, :, c * ql:(c + 1) * ql] for c, h in ids]
    s_loc = [jnp.dot(kl[c], q, preferred_element_type=F32) for (c, h), q in zip(ids, qs)]
    s_ctx = [jnp.dot(kx, q, preferred_element_type=F32) for q in qs]

    diff = lax.broadcasted_iota(jnp.int32, (wk, ql), 0) - lax.broadcasted_iota(jnp.int32, (wk, ql), 1)
    krow = lax.broadcasted_iota(jnp.int32, (wk, 1), 0)
    p_loc, p_ctx, e_snk = [], [], []
    for n, (c, h) in enumerate(ids):
        d = diff + (start[c] - q0[c])
        ok = jnp.logical_and(jnp.abs(d) <= win, krow >= n_ctx - start[c])
        ok = jnp.logical_and(ok, q0[c] >= n_ctx)
        sl = jnp.where(ok, s_loc[n], NEG)
        snk = jnp.where(g == 0, sink_ref[0:1, h:h + 1], sink_ref[0:1, rep + h:rep + h + 1]) * LOG2E
        m = jnp.maximum(jnp.maximum(jnp.max(sl, axis=0, keepdims=True),
                                    jnp.max(s_ctx[n], axis=0, keepdims=True)), snk)
        p_loc.append(jnp.exp2(sl - m).astype(MXU_DTYPE))
        p_ctx.append(jnp.exp2(s_ctx[n] - m).astype(MXU_DTYPE))
        e_snk.append(jnp.exp2(snk - m))

    acc = [jnp.dot(vl[c], p_loc[n], preferred_element_type=F32)
           + jnp.dot(vx, p_ctx[n], preferred_element_type=F32) for n, (c, h) in enumerate(ids)]
    outs = [a[:HEAD_DIM] * (1.0 / (a[HEAD_DIM:HEAD_DIM + 1] + e)) for a, e in zip(acc, e_snk)]
    for c in range(tq // ql):
        o_ref[0, c * ql:(c + 1) * ql, :] = jnp.concatenate(outs[c * rep:(c + 1) * rep], axis=0).T


def _window_attention(qt, k, vt, sink, n_ctx):
    b, _, hd, t = qt.shape
    rep = N_HEADS // N_KV
    tq = ATT_TQ if t % ATT_TQ == 0 else ATT_QL
    assert n_ctx == ATT_QL and t % ATT_QL == 0 and rep * hd == LANE and t >= ATT_QL + 2 * CHUNK
    return pl.pallas_call(
        functools.partial(_window_kernel, n_ctx),
        grid=(b, N_KV, t // tq),
        in_specs=[pl.BlockSpec((SUBLANE, LANE), lambda bb, g, i: (0, 0)),
                  pl.BlockSpec((1, rep, hd, tq), lambda bb, g, i: (bb, g, 0, i)),
                  pl.BlockSpec((1, 1, t, hd), lambda bb, g, i: (bb, g, 0, 0)),
                  pl.BlockSpec((1, 1) + vt.shape[2:], lambda bb, g, i: (bb, g, 0, 0, 0))],
        out_specs=pl.BlockSpec((1, tq, rep * hd), lambda bb, g, i: (bb, i, g)),
        out_shape=jax.ShapeDtypeStruct((b, t, N_HEADS * hd), F32),
        compiler_params=_cparams(("parallel", "parallel", "arbitrary")),
        name="window_attention",
    )(sink, qt, k, vt)


def _dense_kernel(n_ctx, q_ref, k_ref, v_ref, o_ref, m_ref, alpha_ref, acc_ref, p_ref):
    i = pl.program_id(2)
    rep, tq = q_ref.shape[1], q_ref.shape[3]
    ql = ATT_QL
    nc = tq // ql

    def run(chunks, kt, nk):
        per = kt // LANE
        ids = [(c, h) for c in chunks for h in range(rep)]
        for c, h in ids:
            m_ref[c * rep + h] = jnp.full((1, ql), NEG, F32)
            alpha_ref[c * rep + h] = jnp.ones((1, ql), F32)
            acc_ref[c * rep + h] = jnp.zeros(acc_ref.shape[1:], F32)
            p_ref[c * rep + h, 0:kt] = jnp.zeros((kt, ql), p_ref.dtype)

        def scores(j):
            kb = k_ref[0, 0, pl.ds(pl.multiple_of(j * kt, kt), kt), :]
            return [jnp.dot(kb, q_ref[0, h, :, c * ql:(c + 1) * ql], preferred_element_type=F32) for c, h in ids]

        def values(j):
            vb = jnp.concatenate([v_ref[0, 0, j * per + u] for u in range(per)], axis=1)
            pvs = [jnp.dot(vb, p_ref[c * rep + h, 0:kt], preferred_element_type=F32) for c, h in ids]
            for (c, h), pv in zip(ids, pvs):
                n = c * rep + h
                acc_ref[n] = alpha_ref[n] * acc_ref[n] + pv

        def softmax(ss):
            for (c, h), s in zip(ids, ss):
                n = c * rep + h
                m = m_ref[n]
                mn = jnp.maximum(m, jnp.max(s, axis=0, keepdims=True))
                p_ref[n, 0:kt] = jnp.exp2(s - mn).astype(p_ref.dtype)
                alpha_ref[n] = jnp.exp2(m - mn)
                m_ref[n] = mn

        def body(j, carry):
            ss = scores(j)
            values(jnp.maximum(j - 1, 0))
            softmax(ss)
            return carry

        lax.fori_loop(0, nk, body, 0)
        values(nk - 1)
        for c in chunks:
            o = [acc_ref[c * rep + h] for h in range(rep)]
            o = [a[:HEAD_DIM] * (1.0 / a[HEAD_DIM:HEAD_DIM + 1]) for a in o]
            o_ref[0, c * ql:(c + 1) * ql, :] = jnp.concatenate(o, axis=0).T

    kt_all = p_ref.shape[1]
    nk_all = k_ref.shape[2] // kt_all

    @pl.when(i == 0)
    def _():
        run([0], n_ctx, 1)
        if nc > 1:
            run(list(range(1, nc)), kt_all, nk_all)

    @pl.when(i != 0)
    def _():
        run(list(range(nc)), kt_all, nk_all)


def _dense_attention(qt, k, vt, n_ctx):
    b, _, hd, t = qt.shape
    rep = N_HEADS // N_KV
    tq = DENSE_TQ if t % DENSE_TQ == 0 else ATT_QL
    kt = ATT_KT if t % ATT_KT == 0 else ATT_QL
    assert n_ctx == ATT_QL and t % ATT_QL == 0 and rep * hd == LANE
    return pl.pallas_call(
        functools.partial(_dense_kernel, n_ctx),
        grid=(b, N_KV, t // tq),
        in_specs=[pl.BlockSpec((1, rep, hd, tq), lambda bb, g, i: (bb, g, 0, i)),
                  pl.BlockSpec((1, 1, t, hd), lambda bb, g, i: (bb, g, 0, 0)),
                  pl.BlockSpec((1, 1) + vt.shape[2:], lambda bb, g, i: (bb, g, 0, 0, 0))],
        out_specs=pl.BlockSpec((1, tq, rep * hd), lambda bb, g, i: (bb, i, g)),
        out_shape=jax.ShapeDtypeStruct((b, t, N_HEADS * hd), F32),
        scratch_shapes=[pltpu.VMEM((rep * tq // ATT_QL, 1, ATT_QL), F32),
                        pltpu.VMEM((rep * tq // ATT_QL, 1, ATT_QL), F32),
                        pltpu.VMEM((rep * tq // ATT_QL, vt.shape[3], ATT_QL), F32),
                        pltpu.VMEM((rep * tq // ATT_QL, kt, ATT_QL), MXU_DTYPE)],
        compiler_params=_cparams(("parallel", "parallel", "arbitrary")),
        name="dense_attention",
    )(qt, k, vt)


def _mix_mlp_kernel(first, x_ref, ctx_ref, mod_ref, yf_ref, yb_ref, z_ref, gs_ref, hy_ref, hyc_ref,
                    yw_ref, yd_ref, gpost_ref, gpre_ref, gpost2_ref, wo_ref, w1_ref, w2_ref, o_ref):
    i = pl.program_id(1)
    if first:
        xv = jnp.where(i == 0, ctx_ref[0], x_ref[0])
        yh = jnp.where(i == 0, hyc_ref[0], hy_ref[0])
    else:
        xv = x_ref[0]
        yh = hy_ref[0]
    g1 = mod_ref[0, 0, 2:3, :]
    sh2 = mod_ref[0, 0, 3:4, :]
    sc2 = mod_ref[0, 0, 4:5, :]
    g2 = mod_ref[0, 0, 5:6, :]
    ya = _rms((yf_ref[0] + yb_ref[0]) * _silu(z_ref[0]), gs_ref[...])
    w = W_GROUP
    tm, d = xv.shape
    halves = [slice(r, r + tm // MLP_SPLIT) for r in range(0, tm, tm // MLP_SPLIT)]
    branches = (ya, yh, yw_ref[0], yd_ref[0])
    ys = [sum(_dot(br[rows], wo_ref[j * w:(j + 1) * w, :]) for j, br in enumerate(branches)) for rows in halves]
    x1s = [xv[rows] + g1 * _rms(y, gpost_ref[...]) for rows, y in zip(halves, ys)]
    hbs = [(_rms(x1, gpre_ref[...]) * (1.0 + sc2) + sh2).astype(MXU_DTYPE) for x1 in x1s]
    accs = [jnp.zeros((tm // MLP_SPLIT, d), F32) for _ in halves]
    for c in range(w1_ref.shape[1] // d):
        hidden = [jnp.maximum(jnp.dot(hb, w1_ref[:, c * d:(c + 1) * d], preferred_element_type=F32), 0.0)
                  for hb in hbs]
        accs = [acc + _dot(a * a, w2_ref[c * d:(c + 1) * d, :]) for acc, a in zip(accs, hidden)]
    for rows, x1, acc in zip(halves, x1s, accs):
        o_ref[0, rows] = x1 + g2 * _rms(acc, gpost2_ref[...])


def _mix_mlp(first, x, ctx, mod6, yf, yb, z, g_ssd, hy, hyc, yw, yd, g_post, g_pre2, g_post2, wo, w1, w2):
    b, t, _ = yf.shape
    d = wo.shape[1]
    tm = ROW_TILE
    off = 0 if first else 1
    nrow = t // tm - off
    if first:
        x_spec = pl.BlockSpec((1, tm, d), lambda bb, i: (bb, jnp.maximum(i - 1, 0), 0))
        hy_spec = pl.BlockSpec((1, tm, W_GROUP), lambda bb, i: (bb, jnp.maximum(i - 1, 0), 0))
        mod_spec = pl.BlockSpec((1, 1, 6, d), lambda bb, i: (bb, jnp.minimum(i, 1), 0, 0))
    else:
        x_spec = pl.BlockSpec((1, tm, d), lambda bb, i: (bb, i + 1, 0))
        hy_spec = pl.BlockSpec((1, tm, W_GROUP), lambda bb, i: (bb, i, 0))
        mod_spec = pl.BlockSpec((1, 1, 6, d), lambda bb, i: (bb, 1, 0, 0))
    first_blk = lambda w: pl.BlockSpec((1, tm, w), lambda bb, i: (bb, 0, 0))
    row = lambda w: pl.BlockSpec((1, tm, w), lambda bb, i: (bb, i + off, 0))
    vec = lambda w: pl.BlockSpec((1, w), lambda bb, i: (0, 0))
    full = lambda a: pl.BlockSpec(a.shape, lambda bb, i: (0, 0))
    return pl.pallas_call(
        functools.partial(_mix_mlp_kernel, first),
        grid=(b, nrow),
        in_specs=[x_spec, first_blk(d), mod_spec, row(W_GROUP), row(W_GROUP), row(W_GROUP), vec(W_GROUP),
                  hy_spec, first_blk(W_GROUP), row(W_GROUP), row(W_GROUP), vec(d), vec(d), vec(d),
                  full(wo), full(w1), full(w2)],
        out_specs=pl.BlockSpec((1, tm, d), lambda bb, i: (bb, i, 0)),
        out_shape=jax.ShapeDtypeStruct((b, nrow * tm, d), F32),
        compiler_params=_cparams(("parallel", "arbitrary")),
        name="mix_mlp",
    )(x, ctx, mod6, yf, yb, z, g_ssd, hy, hyc, yw, yd, g_post, g_pre2, g_post2, wo, w1, w2)


def _rope_tables(n, n_ctx):
    rows = n // GRID_W
    row = np.repeat(np.arange(rows, dtype=np.float64), GRID_W)
    col = np.tile(np.arange(GRID_W, dtype=np.float64), rows)
    n_freq = HEAD_DIM // 4
    inv = ROPE_THETA ** (-np.arange(n_freq, dtype=np.float64) / n_freq)
    ang = np.concatenate([row[:, None] * inv, col[:, None] * inv], axis=-1)
    cos, sin = np.cos(ang), np.sin(ang)
    cs = np.concatenate([np.ones((n_ctx, HEAD_DIM)), np.concatenate([cos, cos], axis=1)], axis=0)
    sn = np.concatenate([np.zeros((n_ctx, HEAD_DIM)), np.concatenate([-sin, sin], axis=1)], axis=0)
    return jnp.asarray(cs.T, F32), jnp.asarray(sn.T, F32)


def _pad_rows(a, rows):
    return jnp.pad(a, ((0, rows - a.shape[0]), (0, 0)))


def kernel(x, c, ctx, c_ctx, w_mod, b_mod, norm_mix_pre, norm_mix_post, norm_mlp_pre, norm_mlp_post, w_in, w_out, ssd_conv_w, ssd_conv_b, ssd_a_log, ssd_dt_bias, ssd_d, ssd_norm, hy_conv_w, hy_conv_b, hy_w1, hy_b1, hy_freq1, hy_w2, hy_b2, hy_freq2, hy_w3, hy_b3, hy_bias, attn_sink, q_norm, k_norm, mlp_w1, mlp_w2):
    b, n, d = x.shape
    n_ctx = ctx.shape[1]
    depth = w_mod.shape[0]
    assert n_ctx == ROW_TILE and n % ROW_TILE == 0 and b % 2 == 0 and b + 1 <= SUBLANE

    cc = _pad_rows(jnp.concatenate([c, c_ctx[None, :]], axis=0), SUBLANE)
    mod = _modulation(cc, w_mod, b_mod)
    cs, sn = _rope_tables(n, n_ctx)

    off_b = W_GROUP + SSD_XBC + 2 * N_HEADS
    xall = None
    for l in range(depth):
        first = l == 0
        need_ctx = l < depth - 1
        mod_lat = mod[l, :b].reshape(b, 1, 6, d)
        mod_ctx = jnp.broadcast_to(mod[l, b].reshape(1, 1, 6, d), (b, 1, 6, d))
        mod6 = jnp.concatenate([mod_ctx, mod_lat], axis=1)

        wl = w_in[l]
        off_c = off_b + 3 * W_GROUP
        off_d = off_c + (N_HEADS + 2 * N_KV) * HEAD_DIM
        w_row = jnp.concatenate(
            [wl[:, :W_GROUP + SSD_XBC], wl[:, off_b:off_c], wl[:, W_GROUP + SSD_XBC:off_b],
             jnp.zeros((d, D_ROW_PAD - off_c), F32)], axis=1).astype(MXU_DTYPE)
        w_att_t = jnp.concatenate([wl[:, off_d:], wl[:, off_c:off_d]], axis=1).T.astype(MXU_DTYPE)
        xin, cin = (x, ctx) if first else (xall, None)
        (z, u, dt, v, x1, x2, vc, x1c, x2c, qwt, kw, vwt, qdt, kd, vdt) = _in_projection(
            xin, cin, mod6, norm_mix_pre[l].reshape(1, d), w_row, w_att_t, cs, sn, q_norm[l], k_norm[l],
            ssd_conv_w[l], ssd_conv_b[l].reshape(1, -1), hy_conv_w[l], hy_conv_b[l].reshape(1, -1))

        par = jnp.zeros((SUBLANE, LANE), F32)
        par = par.at[0, :2 * N_HEADS].set(ssd_a_log[l].reshape(-1))
        par = par.at[1, :2 * N_HEADS].set(ssd_dt_bias[l].reshape(-1))
        par = par.at[2, :N_HEADS].set(ssd_d[l])
        yf, yb = _ssd(u, dt, par, n_ctx)

        filt_args = (hy_w1[l], hy_b1[l], hy_freq1[l], hy_w2[l], hy_b2[l], hy_freq2[l], hy_w3[l], hy_b3[l])
        yhy = _hyena_long(v, x1, x2, *_hyena_filters(n, *filt_args), hy_bias[l])
        if need_ctx:
            yhy_ctx = _hyena_ctx(vc, x1c, x2c, *_hyena_filters(n_ctx, *filt_args), hy_bias[l])
        else:
            yhy_ctx = yhy

        sink = jnp.zeros((SUBLANE, LANE), F32).at[0, :N_HEADS].set(attn_sink[l])
        yw = _window_attention(qwt, kw, vwt, sink, n_ctx)
        yd = _dense_attention(qdt, kd, vdt, n_ctx)

        xres, cres = (x, ctx) if first else (xall, xall)
        xall = _mix_mlp(first, xres, cres, mod6, yf, yb, z, ssd_norm[l].reshape(1, -1), yhy, yhy_ctx, yw, yd,
                        norm_mix_post[l].reshape(1, d), norm_mlp_pre[l].reshape(1, d),
                        norm_mlp_post[l].reshape(1, d), w_out[l].astype(MXU_DTYPE),
                        mlp_w1[l].astype(MXU_DTYPE), mlp_w2[l].astype(MXU_DTYPE))
    return xall
```

```python
import functools
import math

import numpy as np
import jax
import jax.numpy as jnp
from jax import lax
from jax.experimental import pallas as pl
from jax.experimental.pallas import tpu as pltpu

F32 = jnp.float32
MXU_DTYPE = jnp.bfloat16

EPS = 1e-6
HEAD_DIM = 64
GRID_W = 64
ROPE_THETA = 10000.0
N_HEADS = 4
N_KV = 2
W_GROUP = N_HEADS * HEAD_DIM
SSD_STATE = 64
SSD_XBC = W_GROUP + 2 * N_KV * SSD_STATE
HY_ORDER = 2
HY_BANDS = 16
HY_EMB = 2 * HY_BANDS + 1
HY_FILT = 64
HY_MAX_DECAY = math.log(1e-2) / 0.3
HY_MIN_DECAY = math.log(1e-2) / 1.5
CHUNK = 128
LANE = 128
SUBLANE = 8
ROW_TILE = 256
ATT_TQ = 2816
DENSE_TQ = 8448
FFT_JS = 16
FFT_KB = 16
MLP_SPLIT = 2
VMEM_LIMIT = 56 * 1024 * 1024
NEG = -1e30

C_Z = (0, 256)
C_XBC = (256, 768)
C_HY = (768, 1536)
C_DT = (1536, 1664)
D_ROW_PAD = 1664
A_QD, A_KD, A_VD = 0, 256, 384
A_QW, A_KW, A_VW = 512, 768, 896
V_ROWS = HEAD_DIM + 16
ATT_QL = 256
ATT_KT = 256
LOG2E = math.log2(math.e)


def _cparams(sem):
    return pltpu.CompilerParams(dimension_semantics=sem, vmem_limit_bytes=VMEM_LIMIT)


def _rms(x, g):
    return x * lax.rsqrt(jnp.mean(x * x, axis=-1, keepdims=True) + EPS) * g


def _silu(x):
    return x * (1.0 / (1.0 + jnp.exp(-x)))


def _softplus(x):
    return jnp.maximum(x, 0.0) + jnp.log(1.0 + jnp.exp(-jnp.abs(x)))


def _dot(a, b):
    return jnp.dot(a.astype(MXU_DTYPE), b.astype(MXU_DTYPE), preferred_element_type=F32)


def _dot_nt(a, b):
    return lax.dot_general(a.astype(MXU_DTYPE), b.astype(MXU_DTYPE), (((1,), (1,)), ((), ())),
                           preferred_element_type=F32)


def _dot_f32(a, b):
    return jnp.dot(a, b, preferred_element_type=F32, precision=lax.Precision.HIGHEST)


def _dot_split(a, b):
    ah, bh = a.astype(MXU_DTYPE), b.astype(MXU_DTYPE)
    al = (a - ah.astype(F32)).astype(MXU_DTYPE)
    bl = (b - bh.astype(F32)).astype(MXU_DTYPE)
    dot = functools.partial(jnp.dot, preferred_element_type=F32)
    return dot(ah, bh) + dot(ah, bl) + dot(al, bh)


def _mod_kernel(c_ref, w_ref, b_ref, o_ref):
    o_ref[0] = _dot_f32(_silu(c_ref[...]), w_ref[0]) + b_ref[0]


def _modulation(cc, w_mod, b_mod):
    depth, d, d6 = w_mod.shape
    tn = d6 // 4
    return pl.pallas_call(
        _mod_kernel,
        grid=(depth, d6 // tn),
        in_specs=[pl.BlockSpec((SUBLANE, d), lambda l, j: (0, 0)),
                  pl.BlockSpec((1, d, tn), lambda l, j: (l, 0, j)),
                  pl.BlockSpec((1, 1, tn), lambda l, j: (l, 0, j))],
        out_specs=pl.BlockSpec((1, SUBLANE, tn), lambda l, j: (l, 0, j)),
        out_shape=jax.ShapeDtypeStruct((depth, SUBLANE, d6), F32),
        compiler_params=_cparams(("arbitrary", "arbitrary")),
        name="modulation",
    )(cc, w_mod, b_mod.reshape(depth, 1, d6))


def _inproj_kernel(x_ref, xp_ref, xn_ref, ctx_ref, mod_ref, g_ref, w_ref, wt_ref, cst_ref, snt_ref, qnt_ref, knt_ref,
                   scw_ref, scb_ref, hcw_ref, hcb_ref,
                   z_ref, u_ref, dt_ref, v_ref, x1_ref, x2_ref, vc_ref, x1c_ref, x2c_ref,
                   qwt_ref, kw_ref, vwt_ref, qdt_ref, kd_ref, vdt_ref, ext_ref):
    i = pl.program_id(1)
    nrow = pl.num_programs(1)
    xv = jnp.where(i == 0, ctx_ref[0], x_ref[0])
    tm = xv.shape[0]
    sh = mod_ref[0, 0, 0:1, :]
    sc = mod_ref[0, 0, 1:2, :]

    def modulated(rows):
        return (_rms(rows, g_ref[...]) * (1.0 + sc) + sh).astype(MXU_DTYPE)

    hb = modulated(xv)

    def proj(cols):
        return jnp.dot(hb, w_ref[:, cols[0]:cols[1]], preferred_element_type=F32)

    z_ref[0] = proj(C_Z)
    dt_ref[0] = proj(C_DT)

    conv_cols = (C_XBC[0], C_HY[1])
    halo = jnp.dot(modulated(jnp.concatenate([xp_ref[0], xn_ref[0]], axis=0)),
                   w_ref[:, conv_cols[0]:conv_cols[1]], preferred_element_type=F32)
    prev_ok = i > 1
    next_ok = jnp.logical_and(i >= 1, i < nrow - 1)
    ext_ref[0:SUBLANE] = jnp.where(prev_ok, halo[0:SUBLANE], 0.0)
    ext_ref[SUBLANE:SUBLANE + tm] = proj(conv_cols)
    ext_ref[SUBLANE + tm:] = jnp.where(next_ok, halo[SUBLANE:], 0.0)

    def conv(w_r, b_r, lo, hi):
        taps = w_r.shape[0]
        acc = b_r[...]
        for k in range(taps):
            off = SUBLANE - taps // 2 + k
            acc = acc + w_r[k:k + 1, :] * ext_ref[off:off + tm, lo:hi]
        return acc

    nx = C_XBC[1] - C_XBC[0]
    u_ref[0] = _silu(conv(scw_ref, scb_ref, 0, nx))
    hyc = conv(hcw_ref, hcb_ref, nx, nx + C_HY[1] - C_HY[0])
    parts = [hyc[:, j * W_GROUP:(j + 1) * W_GROUP] for j in range(3)]
    for ref, val in zip((v_ref, x1_ref, x2_ref), parts):
        ref[0] = val

    @pl.when(i == 0)
    def _():
        for ref, val in zip((vc_ref, x1c_ref, x2c_ref), parts):
            ref[0] = val

    pt = _dot_nt(wt_ref[...], hb)
    cst = cst_ref[...]
    snt = snt_ref[...]
    tile = lambda r: jnp.concatenate([r[...]] * (tm // LANE), axis=1)
    qnt, knt = tile(qnt_ref), tile(knt_ref)
    half = HEAD_DIM // 2
    qscale = HEAD_DIM ** -0.5 * LOG2E

    def head(row0):
        return pt[row0:row0 + HEAD_DIM]

    def norm(t, gain):
        return t * lax.rsqrt(jnp.mean(t * t, axis=0, keepdims=True) + EPS) * gain

    def rope(t):
        return t * cst + jnp.concatenate([t[half:], t[:half]], axis=0) * snt

    def put_values(v_ref, row0):
        ones = jnp.ones((v_ref.shape[3] - HEAD_DIM, LANE), v_ref.dtype)
        for g in range(N_KV):
            vt = head(row0 + g * HEAD_DIM)
            for j in range(tm // LANE):
                v_ref[0, g, j, 0:HEAD_DIM, :] = vt[:, j * LANE:(j + 1) * LANE].astype(v_ref.dtype)
                v_ref[0, g, j, HEAD_DIM:, :] = ones

    def put_keys(k_ref, kt_pair):
        k_rows = jnp.concatenate(kt_pair, axis=0).T
        for g in range(N_KV):
            k_ref[0, g] = k_rows[:, g * HEAD_DIM:(g + 1) * HEAD_DIM].astype(k_ref.dtype)

    for h in range(N_HEADS):
        qdt_ref[0, h] = (rope(norm(head(A_QD + h * HEAD_DIM), qnt)) * qscale).astype(qdt_ref.dtype)
        qwt_ref[0, h] = (rope(head(A_QW + h * HEAD_DIM)) * qscale).astype(qwt_ref.dtype)
    put_keys(kd_ref, [rope(norm(head(A_KD + g * HEAD_DIM), knt)) for g in range(N_KV)])
    put_keys(kw_ref, [rope(head(A_KW + g * HEAD_DIM)) for g in range(N_KV)])
    put_values(vdt_ref, A_VD)
    put_values(vwt_ref, A_VW)


def _in_projection(x, ctx, mod6, g_pre, w_row, w_att_t, cst, snt, qn, kn, ssd_cw, ssd_cb, hy_cw, hy_cb):
    b, _, d = x.shape
    tm = ROW_TILE
    hp = tm // SUBLANE
    if ctx is None:
        t = x.shape[1]
        first_lat = 1
        x_spec = pl.BlockSpec((1, tm, d), lambda bb, i: (bb, i, 0))
        ctx_arr, ctx_spec = x, pl.BlockSpec((1, tm, d), lambda bb, i: (bb, 0, 0))
    else:
        assert ctx.shape[1] == tm
        t = x.shape[1] + tm
        first_lat = 0
        x_spec = pl.BlockSpec((1, tm, d), lambda bb, i: (bb, jnp.maximum(i - 1, 0), 0))
        ctx_arr, ctx_spec = ctx, pl.BlockSpec((1, tm, d), lambda bb, i: (bb, 0, 0))
    nrow = t // tm
    last8 = x.shape[1] // SUBLANE - 1
    xp_spec = pl.BlockSpec((1, SUBLANE, d), lambda bb, i: (bb, jnp.clip((i - 1 + first_lat) * hp - 1, 0, last8), 0))
    xn_spec = pl.BlockSpec((1, SUBLANE, d), lambda bb, i: (bb, jnp.clip((i + first_lat) * hp, 0, last8), 0))
    per = tm // LANE
    row = lambda w: pl.BlockSpec((1, tm, w), lambda bb, i: (bb, i, 0))
    lat = pl.BlockSpec((1, tm, W_GROUP), lambda bb, i: (bb, jnp.maximum(i - 1, 0), 0))
    cblk = pl.BlockSpec((1, tm, W_GROUP), lambda bb, i: (bb, 0, 0))
    full = lambda a: pl.BlockSpec(a.shape, lambda bb, i: (0, 0))
    f32 = lambda w: jax.ShapeDtypeStruct((b, t, w), F32)
    lat_shape = jax.ShapeDtypeStruct((b, t - tm, W_GROUP), F32)
    ctx_shape = jax.ShapeDtypeStruct((b, tm, W_GROUP), F32)
    q_spec = pl.BlockSpec((1, N_HEADS, HEAD_DIM, tm), lambda bb, i: (bb, 0, 0, i))
    k_spec = pl.BlockSpec((1, N_KV, tm, HEAD_DIM), lambda bb, i: (bb, 0, i, 0))
    v_spec = pl.BlockSpec((1, N_KV, per, V_ROWS, LANE), lambda bb, i: (bb, 0, i, 0, 0))
    q_shape = jax.ShapeDtypeStruct((b, N_HEADS, HEAD_DIM, t), MXU_DTYPE)
    k_shape = jax.ShapeDtypeStruct((b, N_KV, t, HEAD_DIM), MXU_DTYPE)
    v_shape = jax.ShapeDtypeStruct((b, N_KV, t // LANE, V_ROWS, LANE), MXU_DTYPE)
    gain = lambda v: jnp.broadcast_to(v.reshape(HEAD_DIM, 1), (HEAD_DIM, LANE))
    qnt, knt = gain(qn), gain(kn)
    return pl.pallas_call(
        _inproj_kernel,
        grid=(b, nrow),
        in_specs=[x_spec, xp_spec, xn_spec, ctx_spec,
                  pl.BlockSpec((1, 1, 6, d), lambda bb, i: (bb, jnp.minimum(i, 1), 0, 0)),
                  pl.BlockSpec((1, d), lambda bb, i: (0, 0)),
                  full(w_row), full(w_att_t),
                  pl.BlockSpec((HEAD_DIM, tm), lambda bb, i: (0, i)),
                  pl.BlockSpec((HEAD_DIM, tm), lambda bb, i: (0, i)),
                  full(qnt), full(knt), full(ssd_cw), full(ssd_cb), full(hy_cw), full(hy_cb)],
        out_specs=[row(W_GROUP), row(SSD_XBC), row(LANE), lat, lat, lat, cblk, cblk, cblk,
                   q_spec, k_spec, v_spec, q_spec, k_spec, v_spec],
        out_shape=[f32(W_GROUP), f32(SSD_XBC), f32(LANE), lat_shape, lat_shape, lat_shape,
                   ctx_shape, ctx_shape, ctx_shape,
                   q_shape, k_shape, v_shape, q_shape, k_shape, v_shape],
        scratch_shapes=[pltpu.VMEM((tm + 2 * SUBLANE, C_HY[1] - C_XBC[0]), F32)],
        compiler_params=_cparams(("parallel", "arbitrary")),
        name="in_projection",
    )(x, x, x, ctx_arr, mod6, g_pre, w_row, w_att_t, cst, snt, qnt, knt, ssd_cw, ssd_cb, hy_cw, hy_cb)


def _ssd_kernel(nc, nt, par_ref, uf_ref, dtf_ref, ub_ref, dtb_ref, yf_ref, yb_ref, st_ref):
    j = pl.program_id(0)
    q = CHUNK
    rep = N_HEADS // N_KV

    @pl.when(j == 0)
    def _():
        st_ref[...] = jnp.zeros(st_ref.shape, F32)

    a_all = -jnp.exp(par_ref[0:1, :])
    ri = lax.broadcasted_iota(jnp.int32, (q, q), 0)
    ci = lax.broadcasted_iota(jnp.int32, (q, q), 1)
    dirs = []
    for bi in range(uf_ref.shape[0]):
        for d, (u_ref, dt_ref, y_ref) in enumerate(((uf_ref, dtf_ref, yf_ref), (ub_ref, dtb_ref, yb_ref))):
            u = u_ref[bi]
            dtv = _softplus(dt_ref[bi] + par_ref[1:2, :])
            mask = (ri >= ci) if d == 0 else (ri <= ci)
            cum = _dot_f32(mask.astype(F32), dtv * a_all)
            dirs.append(dict(
                n=len(dirs), b=bi, d=d, y_ref=y_ref, mask=mask, cum=cum, cum_t=cum.T, dt_t=dtv.T,
                end=q - 1 if d == 0 else 0, xs=u[:, :W_GROUP],
                bm_t=u[:, W_GROUP:W_GROUP + N_KV * SSD_STATE].T,
                cm=u[:, W_GROUP + N_KV * SSD_STATE:]))

    def head_terms(v, h):
        hl = N_HEADS * v["d"] + h
        g = h // rep
        sl = slice(g * SSD_STATE, (g + 1) * SSD_STATE)
        return dict(col=jnp.broadcast_to(v["cum"][:, hl:hl + 1], (q, q)), row=v["cum_t"][hl:hl + 1, :],
                    dt_row=v["dt_t"][hl:hl + 1, :],
                    last=v["cum"][v["end"]:v["end"] + 1, hl:hl + 1], cg=v["cm"][:, sl], bg_t=v["bm_t"][sl, :],
                    xp=v["xs"][:, g * LANE:(g + 1) * LANE])

    assert rep * HEAD_DIM == LANE and q == LANE
    lo = lax.broadcasted_iota(jnp.int32, (q, LANE), 1) < HEAD_DIM
    lo_st = lax.broadcasted_iota(jnp.int32, (SSD_STATE, LANE), 1) < HEAD_DIM
    terms = [[head_terms(v, h) for h in range(N_HEADS)] for v in dirs]
    scores = [[_dot(ts[g * rep]["cg"], ts[g * rep]["bg_t"]) for g in range(N_KV)] for ts in terms]
    carried = [[_dot(ts[g * rep]["cg"], st_ref[v["b"], v["d"], g]) for g in range(N_KV)]
               for v, ts in zip(dirs, terms)]
    states = [[_dot(t["bg_t"] * (jnp.exp(t["last"] - t["row"]) * t["dt_row"]), t["xp"]) for t in ts]
              for ts in terms]
    diag = [[_dot(scores[v["n"]][h // rep] * jnp.exp(jnp.where(v["mask"], t["col"] - t["row"], NEG)) * t["dt_row"],
                  t["xp"]) for h, t in enumerate(ts)] for v, ts in zip(dirs, terms)]
    for v, ts in zip(dirs, terms):
        n, bi, d = v["n"], v["b"], v["d"]
        outs = []
        for g in range(N_KV):
            h0, h1 = g * rep, g * rep + 1
            t0, t1 = ts[h0], ts[h1]
            y = (jnp.where(lo, diag[n][h0], diag[n][h1])
                 + carried[n][g] * jnp.exp(jnp.where(lo, t0["col"], t1["col"])))
            st_ref[bi, d, g] = (jnp.where(lo_st, jnp.exp(t0["last"]), jnp.exp(t1["last"])) * st_ref[bi, d, g]
                                + jnp.where(lo_st, states[n][h0], states[n][h1]))
            if d == 0:
                y = y + jnp.where(lo, par_ref[2:3, h0:h0 + 1], par_ref[2:3, h1:h1 + 1]) * t0["xp"]
            outs.append(y)
        v["y_ref"][bi] = jnp.concatenate(outs, axis=1)


def _ssd(u, dt, par, n_ctx):
    b, t, w = u.shape
    q = CHUNK
    nc, nt = n_ctx // q, t // q
    fwd = lambda j: j
    bwd = lambda j: jnp.where(j < nc, nc - 1 - j, nt + nc - 1 - j)

    def specs(cmap):
        return [pl.BlockSpec((b, q, w), lambda j: (0, cmap(j), 0)),
                pl.BlockSpec((b, q, LANE), lambda j: (0, cmap(j), 0))]

    return pl.pallas_call(
        functools.partial(_ssd_kernel, nc, nt),
        grid=(nt,),
        in_specs=[pl.BlockSpec((SUBLANE, LANE), lambda j: (0, 0))] + specs(fwd) + specs(bwd),
        out_specs=[pl.BlockSpec((b, q, W_GROUP), lambda j: (0, fwd(j), 0)),
                   pl.BlockSpec((b, q, W_GROUP), lambda j: (0, bwd(j), 0))],
        out_shape=[jax.ShapeDtypeStruct((b, t, W_GROUP), F32)] * 2,
        scratch_shapes=[pltpu.VMEM((b, 2, N_KV, SSD_STATE, LANE), F32)],
        compiler_params=_cparams(("arbitrary",)),
        name="ssd_scan",
    )(par, u, dt, u, dt)


def _filter_kernel(zf_ref, zb_ref, w1_ref, b1_ref, f1_ref, w2_ref, b2_ref, f2_ref,
                   w3f_ref, w3b_ref, b3f_ref, b3b_ref, dl_ref, o_ref, ss_ref):
    i = pl.program_id(0)
    tr = zf_ref.shape[0]

    def half(z_ref, w3_ref, b3_ref):
        z = z_ref[...]
        h = jnp.sin(f1_ref[...] * (_dot_split(z, w1_ref[...]) + b1_ref[...]))
        h = jnp.sin(f2_ref[...] * (_dot_split(h, w2_ref[...]) + b2_ref[...]))
        k = _dot_split(h, w3_ref[...]) + b3_ref[...]
        return k * jnp.exp(-z[:, 0:1] * dl_ref[...])

    kf = half(zf_ref, w3f_ref, b3f_ref)
    kb = half(zb_ref, w3b_ref, b3b_ref)

    @pl.when(i == 0)
    def _():
        ss_ref[...] = jnp.zeros(ss_ref.shape, F32)

    ss_ref[...] += jnp.sum(kf * kf + kb * kb, axis=0, keepdims=True)
    rows = i * tr + lax.broadcasted_iota(jnp.int32, (tr, 1), 0)
    o_ref[0] = kf
    o_ref[1] = jnp.where(rows == 0, 0.0, kb)


def _hyena_filters(n, w1, b1, f1, w2, b2, f2, w3, b3):
    pos = np.arange(n, dtype=np.float64)
    t = np.linspace(0.0, 1.0, n)
    f = np.linspace(1e-4, HY_BANDS - 1, HY_BANDS)
    ang = 2.0 * math.pi * pos[:, None] * f[None, :] / n
    emb = np.concatenate([t[:, None], np.cos(ang), -np.sin(ang)], axis=-1)
    emb = np.pad(emb, ((0, 0), (0, LANE - HY_EMB)))
    emb_b = np.roll(np.flip(emb, axis=0), 1, axis=0)
    emb, emb_b = jnp.asarray(emb, F32), jnp.asarray(emb_b, F32)
    w1p = jnp.pad(w1, ((0, LANE - HY_EMB), (0, 0)))
    w3r = w3.reshape(HY_FILT, HY_ORDER, 2, W_GROUP)
    b3r = b3.reshape(HY_ORDER, 2, W_GROUP)
    wc = HY_ORDER * W_GROUP
    w3f, w3b = w3r[:, :, 0].reshape(HY_FILT, wc), w3r[:, :, 1].reshape(HY_FILT, wc)
    b3f, b3b = b3r[:, 0].reshape(1, wc), b3r[:, 1].reshape(1, wc)
    deltas = np.abs(np.linspace(HY_MIN_DECAY, HY_MAX_DECAY, W_GROUP))
    deltas = jnp.asarray(np.tile(deltas, HY_ORDER).reshape(1, wc), F32)
    tr = math.gcd(n, 512)
    const = lambda s: pl.BlockSpec(s, lambda i: (0, 0))
    rows = pl.BlockSpec((tr, LANE), lambda i: (i, 0))
    return pl.pallas_call(
        _filter_kernel,
        grid=(n // tr,),
        in_specs=[rows, rows, const((LANE, HY_FILT)), const((1, HY_FILT)), const((1, HY_FILT)),
                  const((HY_FILT, HY_FILT)), const((1, HY_FILT)), const((1, HY_FILT)),
                  const((HY_FILT, wc)), const((HY_FILT, wc)), const((1, wc)), const((1, wc)), const((1, wc))],
        out_specs=[pl.BlockSpec((2, tr, wc), lambda i: (0, i, 0)), const((1, wc))],
        out_shape=[jax.ShapeDtypeStruct((2, n, wc), F32), jax.ShapeDtypeStruct((1, wc), F32)],
        compiler_params=_cparams(("arbitrary",)),
        name="hyena_filter",
    )(emb, emb_b, w1p, b1.reshape(1, -1), f1.reshape(1, -1), w2, b2.reshape(1, -1), f2.reshape(1, -1),
      w3f, w3b, b3f, b3b, deltas)


def _dft_tables(n):
    nn = 2 * n
    n1, n2 = nn // LANE, LANE
    a = n1 // 2
    k1 = np.arange(n1)[:, None]
    j1 = np.arange(n1)[None, :]
    ang1 = 2.0 * np.pi * (k1 * j1 % n1) / n1
    fr, fi = np.cos(ang1), -np.sin(ang1)
    m1 = np.block([[fr[:, :a], -fi[:, :a]], [fi[:, :a], fr[:, :a]]])
    m1_real = np.concatenate([fr, fi], axis=0)
    er, ei = fr.T[:a], -fi.T[:a]
    m3 = np.stack([np.concatenate([er, -ei], axis=1), np.concatenate([ei, er], axis=1)]) / nn
    k2 = np.arange(n2)[:, None]
    j2 = np.arange(n2)[None, :]
    ang2 = 2.0 * np.pi * (k2 * j2 % n2) / n2
    f2 = np.stack([np.cos(ang2), -np.sin(ang2)])
    angt = 2.0 * np.pi * (np.arange(n1)[:, None] * j2) / nn
    tw = np.stack([np.cos(angt), -np.sin(angt)], axis=1)
    return tuple(jnp.asarray(t, F32) for t in (m1, m1_real, m3, f2, tw))


def _fft_first_kernel(ur_ref, ui_ref, m_ref, o_ref, acc_ref):
    n1 = o_ref.shape[2]
    for s in range(ur_ref.shape[2]):
        r = _dot(m_ref[...], jnp.concatenate([ur_ref[0, :, s, :], ui_ref[0, :, s, :]], axis=0))
        acc_ref[0, :, s, :] = r[:n1]
        acc_ref[1, :, s, :] = r[n1:]
    o_ref[0] = acc_ref[...].astype(o_ref.dtype)


def _fft_first(u, m1):
    b2, a, _, c = u.shape
    p = b2 // 2
    n1 = m1.shape[0] // 2
    js = FFT_JS
    return pl.pallas_call(
        _fft_first_kernel,
        grid=(p, LANE // js),
        in_specs=[pl.BlockSpec((1, a, js, c), lambda pp, j: (2 * pp, 0, j, 0)),
                  pl.BlockSpec((1, a, js, c), lambda pp, j: (2 * pp + 1, 0, j, 0)),
                  pl.BlockSpec(m1.shape, lambda pp, j: (0, 0))],
        out_specs=pl.BlockSpec((1, 2, n1, js, c), lambda pp, j: (pp, 0, 0, j, 0)),
        out_shape=jax.ShapeDtypeStruct((p, 2, n1, LANE, c), MXU_DTYPE),
        scratch_shapes=[pltpu.VMEM((2, n1, js, c), F32)],
        compiler_params=_cparams(("parallel", "arbitrary")),
        name="fft_first",
    )(u, u, m1)


def _twiddled_dft(f2_ref, tw_ref, u):
    fr, fi = f2_ref[0], f2_ref[1]
    tr, ti = tw_ref[u, 0:1, :], tw_ref[u, 1:2, :]
    return fr * tr - fi * ti, fr * ti + fi * tr


def _real_form(gr, gi):
    return jnp.concatenate([jnp.concatenate([gr, -gi], axis=1), jnp.concatenate([gi, gr], axis=1)], axis=0)


def _spectrum_kernel(a_ref, ss_ref, f2_ref, tw_ref, h_ref):
    scale = lax.rsqrt(ss_ref[...] + EPS)
    for u in range(a_ref.shape[2]):
        xin = jnp.concatenate([a_ref[0, 0, u], a_ref[0, 1, u]], axis=0)
        x = _dot(_real_form(*_twiddled_dft(f2_ref, tw_ref, u)), xin) * scale
        h_ref[0, u] = x[:LANE]
        h_ref[1, u] = x[LANE:]


def _filter_spectrum(a5, ss, f2, tw):
    _, _, n1, _, c = a5.shape
    kb = math.gcd(n1, FFT_KB)
    return pl.pallas_call(
        _spectrum_kernel,
        grid=(n1 // kb,),
        in_specs=[pl.BlockSpec((1, 2, kb, LANE, c), lambda k: (0, 0, k, 0, 0)),
                  pl.BlockSpec((1, c), lambda k: (0, 0)),
                  pl.BlockSpec((2, LANE, LANE), lambda k: (0, 0, 0)),
                  pl.BlockSpec((kb, 2, LANE), lambda k: (k, 0, 0))],
        out_specs=pl.BlockSpec((2, kb, LANE, c), lambda k: (0, k, 0, 0)),
        out_shape=jax.ShapeDtypeStruct((2, n1, LANE, c), F32),
        compiler_params=_cparams(("arbitrary",)),
        name="filter_spectrum",
    )(a5, ss, f2, tw)


def _fft_mid_kernel(a_ref, h_ref, f2_ref, tw_ref, v_ref):
    npair = a_ref.shape[0]
    c = a_ref.shape[4]
    kb = a_ref.shape[2]
    gs = [_twiddled_dft(f2_ref, tw_ref, u) for u in range(kb)]
    xs = []
    for u, (gr, gi) in enumerate(gs):
        xin = jnp.concatenate(
            [jnp.concatenate([a_ref[p, 0, u], a_ref[p, 1, u]], axis=0) for p in range(npair)], axis=1)
        xs.append(_dot(_real_form(gr, gi), xin))
    vs = []
    for u, ((gr, gi), x) in enumerate(zip(gs, xs)):
        xr, xi = x[:LANE], x[LANE:]
        hr = jnp.concatenate([h_ref[0, u]] * npair, axis=1)
        hi = jnp.concatenate([h_ref[1, u]] * npair, axis=1)
        y = jnp.concatenate([xr * hr - xi * hi, xr * hi + xi * hr], axis=0)
        vs.append(_dot(_real_form(gr.T, -gi.T), y))
    for u, v in enumerate(vs):
        for p in range(npair):
            v_ref[p, 0, u] = v[:LANE, p * c:(p + 1) * c].astype(v_ref.dtype)
            v_ref[p, 1, u] = v[LANE:, p * c:(p + 1) * c].astype(v_ref.dtype)


def _fft_mid(a5, hspec, order, f2, tw):
    npair, _, n1, _, c = a5.shape
    kb = math.gcd(n1, FFT_KB)
    blk = pl.BlockSpec((npair, 2, kb, LANE, c), lambda k: (0, 0, k, 0, 0))
    return pl.pallas_call(
        _fft_mid_kernel,
        grid=(n1 // kb,),
        in_specs=[blk,
                  pl.BlockSpec((2, kb, LANE, c), lambda k: (0, k, 0, order)),
                  pl.BlockSpec((2, LANE, LANE), lambda k: (0, 0, 0)),
                  pl.BlockSpec((kb, 2, LANE), lambda k: (k, 0, 0))],
        out_specs=blk,
        out_shape=jax.ShapeDtypeStruct(a5.shape, MXU_DTYPE),
        compiler_params=_cparams(("arbitrary",)),
        name="fft_mid",
    )(a5, hspec, f2, tw)


def _fft_last_kernel(v_ref, m_ref, z_ref, gate_ref, bias_ref, o_ref, vf_ref):
    vf_ref[...] = v_ref[0].astype(F32)
    for s in range(z_ref.shape[2]):
        vs = jnp.concatenate([vf_ref[0, :, s, :], vf_ref[1, :, s, :]], axis=0)
        zf = _dot(m_ref[0], vs)
        o_ref[0, :, s, :] = gate_ref[0, :, s, :] * (zf + z_ref[0, :, s, :] * bias_ref[...])


def _fft_last(v5, m3, z, gate, bias):
    b, a, _, c = z.shape
    n1 = v5.shape[2]
    js = FFT_JS
    row = pl.BlockSpec((1, a, js, c), lambda j, bb: (bb, 0, j, 0))
    return pl.pallas_call(
        _fft_last_kernel,
        grid=(LANE // js, b),
        in_specs=[pl.BlockSpec((1, 2, n1, js, c), lambda j, bb: (bb // 2, 0, 0, j, 0)),
                  pl.BlockSpec((1, a, 2 * n1), lambda j, bb: (bb % 2, 0, 0)),
                  row, row,
                  pl.BlockSpec((1, c), lambda j, bb: (0, 0))],
        out_specs=row,
        out_shape=jax.ShapeDtypeStruct(z.shape, F32),
        scratch_shapes=[pltpu.VMEM((2, n1, js, c), F32)],
        compiler_params=_cparams(("parallel", "arbitrary")),
        name="fft_last",
    )(v5, m3, z, gate, bias)


def _hyena_long(v, x1, x2, filt, ss, bias):
    b, n, c = v.shape
    a = n // LANE
    m1, m1_real, m3, f2, tw = _dft_tables(n)
    hspec = _filter_spectrum(_fft_first(filt.reshape(2, a, LANE, filt.shape[2]), m1_real), ss, f2, tw)
    z = v.reshape(b, a, LANE, c)
    gates = (x1.reshape(b, a, LANE, c), x2.reshape(b, a, LANE, c))
    for o in range(HY_ORDER):
        z = _fft_last(_fft_mid(_fft_first(z, m1), hspec, o, f2, tw), m3, z, gates[o], bias[o].reshape(1, c))
    return z.reshape(b, n, c)


def _hyena_ctx_kernel(v_ref, x1_ref, x2_ref, filt_ref, ss_ref, bias_ref, ff_ref, fc_ref, fi_ref, o_ref):
    b, n, c = v_ref.shape
    npair = b // 2
    hs = _dot(ff_ref[...], jnp.concatenate([filt_ref[0], filt_ref[1]], axis=0))
    hs = hs * lax.rsqrt(ss_ref[...] + EPS)
    z = [v_ref[i] for i in range(b)]
    gates = (x1_ref, x2_ref)
    for o in range(HY_ORDER):
        hr = jnp.concatenate([hs[:2 * n, o * c:(o + 1) * c]] * npair, axis=1)
        hi = jnp.concatenate([hs[2 * n:, o * c:(o + 1) * c]] * npair, axis=1)
        xin = jnp.concatenate([jnp.concatenate([z[2 * p] for p in range(npair)], axis=1),
                               jnp.concatenate([z[2 * p + 1] for p in range(npair)], axis=1)], axis=0)
        x = _dot(fc_ref[...], xin)
        xr, xi = x[:2 * n], x[2 * n:]
        y = jnp.concatenate([xr * hr - xi * hi, xr * hi + xi * hr], axis=0)
        zf = _dot(fi_ref[...], y)
        bo = bias_ref[o:o + 1, :]
        for i in range(b):
            p, part = i // 2, i % 2
            conv = zf[part * n:(part + 1) * n, p * c:(p + 1) * c]
            z[i] = gates[o][i] * (conv + z[i] * bo)
    for i in range(b):
        o_ref[i] = z[i]


def _hyena_ctx(v, x1, x2, filt, ss, bias):
    b, n, c = v.shape
    nn = 2 * n
    k = np.arange(nn)[:, None]
    j = np.arange(nn)[None, :]
    ang = 2.0 * np.pi * (k * j % nn) / nn
    fr, fi = np.cos(ang), -np.sin(ang)
    ff = np.concatenate([fr, fi], axis=0)
    fc = np.block([[fr[:, :n], -fi[:, :n]], [fi[:, :n], fr[:, :n]]])
    er, ei = fr[:n], -fi[:n]
    finv = np.block([[er, -ei], [ei, er]]) / nn
    return pl.pallas_call(
        _hyena_ctx_kernel,
        out_shape=jax.ShapeDtypeStruct((b, n, c), F32),
        compiler_params=pltpu.CompilerParams(vmem_limit_bytes=VMEM_LIMIT),
        name="hyena_ctx",
    )(v, x1, x2, filt, ss, bias, jnp.asarray(ff, F32), jnp.asarray(fc, F32), jnp.asarray(finv, F32))


def _window_kernel(n_ctx, sink_ref, q_ref, k_ref, v_ref, o_ref):
    g = pl.program_id(1)
    i = pl.program_id(2)
    rep, tq = q_ref.shape[1], q_ref.shape[3]
    t = k_ref.shape[2]
    ql, win = ATT_QL, CHUNK
    wk = ql + 2 * win
    ids = [(c, h) for c in range(tq // ql) for h in range(rep)]
    q0 = [i * tq + c * ql for c in range(tq // ql)]
    start = [pl.multiple_of(jnp.clip(q - win, 0, t - wk), LANE) for q in q0]

    kx = k_ref[0, 0, 0:n_ctx, :]
    vx = jnp.concatenate([v_ref[0, 0, u] for u in range(n_ctx // LANE)], axis=1)
    kl = [k_ref[0, 0, pl.ds(s, wk), :] for s in start]
    vl = [jnp.concatenate([v_ref[0, 0, s // LANE + u] for u in range(wk // LANE)], axis=1) for s in start]

    qs = [q_ref[0, h, :, c * ql:(c + 1) * ql] for c, h in ids]
    s_loc = [jnp.dot(kl[c], q, preferred_element_type=F32) for (c, h), q in zip(ids, qs)]
    s_ctx = [jnp.dot(kx, q, preferred_element_type=F32) for q in qs]

    diff = lax.broadcasted_iota(jnp.int32, (wk, ql), 0) - lax.broadcasted_iota(jnp.int32, (wk, ql), 1)
    krow = lax.broadcasted_iota(jnp.int32, (wk, 1), 0)
    p_loc, p_ctx, e_snk = [], [], []
    for n, (c, h) in enumerate(ids):
        d = diff + (start[c] - q0[c])
        ok = jnp.logical_and(jnp.abs(d) <= win, krow >= n_ctx - start[c])
        ok = jnp.logical_and(ok, q0[c] >= n_ctx)
        sl = jnp.where(ok, s_loc[n], NEG)
        snk = jnp.where(g == 0, sink_ref[0:1, h:h + 1], sink_ref[0:1, rep + h:rep + h + 1]) * LOG2E
        m = jnp.maximum(jnp.maximum(jnp.max(sl, axis=0, keepdims=True),
                                    jnp.max(s_ctx[n], axis=0, keepdims=True)), snk)
        p_loc.append(jnp.exp2(sl - m).astype(MXU_DTYPE))
        p_ctx.append(jnp.exp2(s_ctx[n] - m).astype(MXU_DTYPE))
        e_snk.append(jnp.exp2(snk - m))

    acc = [jnp.dot(vl[c], p_loc[n], preferred_element_type=F32)
           + jnp.dot(vx, p_ctx[n], preferred_element_type=F32) for n, (c, h) in enumerate(ids)]
    outs = [a[:HEAD_DIM] * (1.0 / (a[HEAD_DIM:HEAD_DIM + 1] + e)) for a, e in zip(acc, e_snk)]
    for c in range(tq // ql):
        o_ref[0, c * ql:(c + 1) * ql, :] = jnp.concatenate(outs[c * rep:(c + 1) * rep], axis=0).T


def _window_attention(qt, k, vt, sink, n_ctx):
    b, _, hd, t = qt.shape
    rep = N_HEADS // N_KV
    tq = ATT_TQ if t % ATT_TQ == 0 else ATT_QL
    assert n_ctx == ATT_QL and t % ATT_QL == 0 and rep * hd == LANE and t >= ATT_QL + 2 * CHUNK
    return pl.pallas_call(
        functools.partial(_window_kernel, n_ctx),
        grid=(b, N_KV, t // tq),
        in_specs=[pl.BlockSpec((SUBLANE, LANE), lambda bb, g, i: (0, 0)),
                  pl.BlockSpec((1, rep, hd, tq), lambda bb, g, i: (bb, g, 0, i)),
                  pl.BlockSpec((1, 1, t, hd), lambda bb, g, i: (bb, g, 0, 0)),
                  pl.BlockSpec((1, 1) + vt.shape[2:], lambda bb, g, i: (bb, g, 0, 0, 0))],
        out_specs=pl.BlockSpec((1, tq, rep * hd), lambda bb, g, i: (bb, i, g)),
        out_shape=jax.ShapeDtypeStruct((b, t, N_HEADS * hd), F32),
        compiler_params=_cparams(("parallel", "parallel", "arbitrary")),
        name="window_attention",
    )(sink, qt, k, vt)


def _dense_kernel(n_ctx, q_ref, k_ref, v_ref, o_ref, m_ref, alpha_ref, acc_ref, p_ref):
    i = pl.program_id(2)
    rep, tq = q_ref.shape[1], q_ref.shape[3]
    ql = ATT_QL
    nc = tq // ql

    def run(chunks, kt, nk):
        per = kt // LANE
        ids = [(c, h) for c in chunks for h in range(rep)]
        for c, h in ids:
            m_ref[c * rep + h] = jnp.full((1, ql), NEG, F32)
            alpha_ref[c * rep + h] = jnp.ones((1, ql), F32)
            acc_ref[c * rep + h] = jnp.zeros(acc_ref.shape[1:], F32)
            p_ref[c * rep + h, 0:kt] = jnp.zeros((kt, ql), p_ref.dtype)

        def scores(j):
            kb = k_ref[0, 0, pl.ds(pl.multiple_of(j * kt, kt), kt), :]
            return [jnp.dot(kb, q_ref[0, h, :, c * ql:(c + 1) * ql], preferred_element_type=F32) for c, h in ids]

        def values(j):
            vb = jnp.concatenate([v_ref[0, 0, j * per + u] for u in range(per)], axis=1)
            pvs = [jnp.dot(vb, p_ref[c * rep + h, 0:kt], preferred_element_type=F32) for c, h in ids]
            for (c, h), pv in zip(ids, pvs):
                n = c * rep + h
                acc_ref[n] = alpha_ref[n] * acc_ref[n] + pv

        def softmax(ss):
            for (c, h), s in zip(ids, ss):
                n = c * rep + h
                m = m_ref[n]
                mn = jnp.maximum(m, jnp.max(s, axis=0, keepdims=True))
                p_ref[n, 0:kt] = jnp.exp2(s - mn).astype(p_ref.dtype)
                alpha_ref[n] = jnp.exp2(m - mn)
                m_ref[n] = mn

        def body(j, carry):
            ss = scores(j)
            values(jnp.maximum(j - 1, 0))
            softmax(ss)
            return carry

        lax.fori_loop(0, nk, body, 0)
        values(nk - 1)
        for c in chunks:
            o = [acc_ref[c * rep + h] for h in range(rep)]
            o = [a[:HEAD_DIM] * (1.0 / a[HEAD_DIM:HEAD_DIM + 1]) for a in o]
            o_ref[0, c * ql:(c + 1) * ql, :] = jnp.concatenate(o, axis=0).T

    kt_all = p_ref.shape[1]
    nk_all = k_ref.shape[2] // kt_all

    @pl.when(i == 0)
    def _():
        run([0], n_ctx, 1)
        if nc > 1:
            run(list(range(1, nc)), kt_all, nk_all)

    @pl.when(i != 0)
    def _():
        run(list(range(nc)), kt_all, nk_all)


def _dense_attention(qt, k, vt, n_ctx):
    b, _, hd, t = qt.shape
    rep = N_HEADS // N_KV
    tq = DENSE_TQ if t % DENSE_TQ == 0 else ATT_QL
    kt = ATT_KT if t % ATT_KT == 0 else ATT_QL
    assert n_ctx == ATT_QL and t % ATT_QL == 0 and rep * hd == LANE
    return pl.pallas_call(
        functools.partial(_dense_kernel, n_ctx),
        grid=(b, N_KV, t // tq),
        in_specs=[pl.BlockSpec((1, rep, hd, tq), lambda bb, g, i: (bb, g, 0, i)),
                  pl.BlockSpec((1, 1, t, hd), lambda bb, g, i: (bb, g, 0, 0)),
                  pl.BlockSpec((1, 1) + vt.shape[2:], lambda bb, g, i: (bb, g, 0, 0, 0))],
        out_specs=pl.BlockSpec((1, tq, rep * hd), lambda bb, g, i: (bb, i, g)),
        out_shape=jax.ShapeDtypeStruct((b, t, N_HEADS * hd), F32),
        scratch_shapes=[pltpu.VMEM((rep * tq // ATT_QL, 1, ATT_QL), F32),
                        pltpu.VMEM((rep * tq // ATT_QL, 1, ATT_QL), F32),
                        pltpu.VMEM((rep * tq // ATT_QL, vt.shape[3], ATT_QL), F32),
                        pltpu.VMEM((rep * tq // ATT_QL, kt, ATT_QL), MXU_DTYPE)],
        compiler_params=_cparams(("parallel", "parallel", "arbitrary")),
        name="dense_attention",
    )(qt, k, vt)


def _mix_mlp_kernel(first, x_ref, ctx_ref, mod_ref, yf_ref, yb_ref, z_ref, gs_ref, hy_ref, hyc_ref,
                    yw_ref, yd_ref, gpost_ref, gpre_ref, gpost2_ref, wo_ref, w1_ref, w2_ref, o_ref):
    i = pl.program_id(1)
    if first:
        xv = jnp.where(i == 0, ctx_ref[0], x_ref[0])
        yh = jnp.where(i == 0, hyc_ref[0], hy_ref[0])
    else:
        xv = x_ref[0]
        yh = hy_ref[0]
    g1 = mod_ref[0, 0, 2:3, :]
    sh2 = mod_ref[0, 0, 3:4, :]
    sc2 = mod_ref[0, 0, 4:5, :]
    g2 = mod_ref[0, 0, 5:6, :]
    ya = _rms((yf_ref[0] + yb_ref[0]) * _silu(z_ref[0]), gs_ref[...])
    w = W_GROUP
    tm, d = xv.shape
    halves = [slice(r, r + tm // MLP_SPLIT) for r in range(0, tm, tm // MLP_SPLIT)]
    branches = (ya, yh, yw_ref[0], yd_ref[0])
    ys = [sum(_dot(br[rows], wo_ref[j * w:(j + 1) * w, :]) for j, br in enumerate(branches)) for rows in halves]
    x1s = [xv[rows] + g1 * _rms(y, gpost_ref[...]) for rows, y in zip(halves, ys)]
    hbs = [(_rms(x1, gpre_ref[...]) * (1.0 + sc2) + sh2).astype(MXU_DTYPE) for x1 in x1s]
    accs = [jnp.zeros((tm // MLP_SPLIT, d), F32) for _ in halves]
    for c in range(w1_ref.shape[1] // d):
        hidden = [jnp.maximum(jnp.dot(hb, w1_ref[:, c * d:(c + 1) * d], preferred_element_type=F32), 0.0)
                  for hb in hbs]
        accs = [acc + _dot(a * a, w2_ref[c * d:(c + 1) * d, :]) for acc, a in zip(accs, hidden)]
    for rows, x1, acc in zip(halves, x1s, accs):
        o_ref[0, rows] = x1 + g2 * _rms(acc, gpost2_ref[...])


def _mix_mlp(first, x, ctx, mod6, yf, yb, z, g_ssd, hy, hyc, yw, yd, g_post, g_pre2, g_post2, wo, w1, w2):
    b, t, _ = yf.shape
    d = wo.shape[1]
    tm = ROW_TILE
    off = 0 if first else 1
    nrow = t // tm - off
    if first:
        x_spec = pl.BlockSpec((1, tm, d), lambda bb, i: (bb, jnp.maximum(i - 1, 0), 0))
        hy_spec = pl.BlockSpec((1, tm, W_GROUP), lambda bb, i: (bb, jnp.maximum(i - 1, 0), 0))
        mod_spec = pl.BlockSpec((1, 1, 6, d), lambda bb, i: (bb, jnp.minimum(i, 1), 0, 0))
    else:
        x_spec = pl.BlockSpec((1, tm, d), lambda bb, i: (bb, i + 1, 0))
        hy_spec = pl.BlockSpec((1, tm, W_GROUP), lambda bb, i: (bb, i, 0))
        mod_spec = pl.BlockSpec((1, 1, 6, d), lambda bb, i: (bb, 1, 0, 0))
    first_blk = lambda w: pl.BlockSpec((1, tm, w), lambda bb, i: (bb, 0, 0))
    row = lambda w: pl.BlockSpec((1, tm, w), lambda bb, i: (bb, i + off, 0))
    vec = lambda w: pl.BlockSpec((1, w), lambda bb, i: (0, 0))
    full = lambda a: pl.BlockSpec(a.shape, lambda bb, i: (0, 0))
    return pl.pallas_call(
        functools.partial(_mix_mlp_kernel, first),
        grid=(b, nrow),
        in_specs=[x_spec, first_blk(d), mod_spec, row(W_GROUP), row(W_GROUP), row(W_GROUP), vec(W_GROUP),
                  hy_spec, first_blk(W_GROUP), row(W_GROUP), row(W_GROUP), vec(d), vec(d), vec(d),
                  full(wo), full(w1), full(w2)],
        out_specs=pl.BlockSpec((1, tm, d), lambda bb, i: (bb, i, 0)),
        out_shape=jax.ShapeDtypeStruct((b, nrow * tm, d), F32),
        compiler_params=_cparams(("parallel", "arbitrary")),
        name="mix_mlp",
    )(x, ctx, mod6, yf, yb, z, g_ssd, hy, hyc, yw, yd, g_post, g_pre2, g_post2, wo, w1, w2)


def _rope_tables(n, n_ctx):
    rows = n // GRID_W
    row = np.repeat(np.arange(rows, dtype=np.float64), GRID_W)
    col = np.tile(np.arange(GRID_W, dtype=np.float64), rows)
    n_freq = HEAD_DIM // 4
    inv = ROPE_THETA ** (-np.arange(n_freq, dtype=np.float64) / n_freq)
    ang = np.concatenate([row[:, None] * inv, col[:, None] * inv], axis=-1)
    cos, sin = np.cos(ang), np.sin(ang)
    cs = np.concatenate([np.ones((n_ctx, HEAD_DIM)), np.concatenate([cos, cos], axis=1)], axis=0)
    sn = np.concatenate([np.zeros((n_ctx, HEAD_DIM)), np.concatenate([-sin, sin], axis=1)], axis=0)
    return jnp.asarray(cs.T, F32), jnp.asarray(sn.T, F32)


def _pad_rows(a, rows):
    return jnp.pad(a, ((0, rows - a.shape[0]), (0, 0)))


def kernel(x, c, ctx, c_ctx, w_mod, b_mod, norm_mix_pre, norm_mix_post, norm_mlp_pre, norm_mlp_post, w_in, w_out, ssd_conv_w, ssd_conv_b, ssd_a_log, ssd_dt_bias, ssd_d, ssd_norm, hy_conv_w, hy_conv_b, hy_w1, hy_b1, hy_freq1, hy_w2, hy_b2, hy_freq2, hy_w3, hy_b3, hy_bias, attn_sink, q_norm, k_norm, mlp_w1, mlp_w2):
    b, n, d = x.shape
    n_ctx = ctx.shape[1]
    depth = w_mod.shape[0]
    assert n_ctx == ROW_TILE and n % ROW_TILE == 0 and b % 2 == 0 and b + 1 <= SUBLANE

    cc = _pad_rows(jnp.concatenate([c, c_ctx[None, :]], axis=0), SUBLANE)
    mod = _modulation(cc, w_mod, b_mod)
    cs, sn = _rope_tables(n, n_ctx)

    off_b = W_GROUP + SSD_XBC + 2 * N_HEADS
    xall = None
    for l in range(depth):
        first = l == 0
        need_ctx = l < depth - 1
        mod_lat = mod[l, :b].reshape(b, 1, 6, d)
        mod_ctx = jnp.broadcast_to(mod[l, b].reshape(1, 1, 6, d), (b, 1, 6, d))
        mod6 = jnp.concatenate([mod_ctx, mod_lat], axis=1)

        wl = w_in[l]
        off_c = off_b + 3 * W_GROUP
        off_d = off_c + (N_HEADS + 2 * N_KV) * HEAD_DIM
        w_row = jnp.concatenate(
            [wl[:, :W_GROUP + SSD_XBC], wl[:, off_b:off_c], wl[:, W_GROUP + SSD_XBC:off_b],
             jnp.zeros((d, D_ROW_PAD - off_c), F32)], axis=1).astype(MXU_DTYPE)
        w_att_t = jnp.concatenate([wl[:, off_d:], wl[:, off_c:off_d]], axis=1).T.astype(MXU_DTYPE)
        xin, cin = (x, ctx) if first else (xall, None)
        (z, u, dt, v, x1, x2, vc, x1c, x2c, qwt, kw, vwt, qdt, kd, vdt) = _in_projection(
            xin, cin, mod6, norm_mix_pre[l].reshape(1, d), w_row, w_att_t, cs, sn, q_norm[l], k_norm[l],
            ssd_conv_w[l], ssd_conv_b[l].reshape(1, -1), hy_conv_w[l], hy_conv_b[l].reshape(1, -1))

        par = jnp.zeros((SUBLANE, LANE), F32)
        par = par.at[0, :2 * N_HEADS].set(ssd_a_log[l].reshape(-1))
        par = par.at[1, :2 * N_HEADS].set(ssd_dt_bias[l].reshape(-1))
        par = par.at[2, :N_HEADS].set(ssd_d[l])
        yf, yb = _ssd(u, dt, par, n_ctx)

        filt_args = (hy_w1[l], hy_b1[l], hy_freq1[l], hy_w2[l], hy_b2[l], hy_freq2[l], hy_w3[l], hy_b3[l])
        yhy = _hyena_long(v, x1, x2, *_hyena_filters(n, *filt_args), hy_bias[l])
        if need_ctx:
            yhy_ctx = _hyena_ctx(vc, x1c, x2c, *_hyena_filters(n_ctx, *filt_args), hy_bias[l])
        else:
            yhy_ctx = yhy

        sink = jnp.zeros((SUBLANE, LANE), F32).at[0, :N_HEADS].set(attn_sink[l])
        yw = _window_attention(qwt, kw, vwt, sink, n_ctx)
        yd = _dense_attention(qdt, kd, vdt, n_ctx)

        xres, cres = (x, ctx) if first else (xall, xall)
        xall = _mix_mlp(first, xres, cres, mod6, yf, yb, z, ssd_norm[l].reshape(1, -1), yhy, yhy_ctx, yw, yd,
                        norm_mix_post[l].reshape(1, d), norm_mlp_pre[l].reshape(1, d),
                        norm_mlp_post[l].reshape(1, d), w_out[l].astype(MXU_DTYPE),
                        mlp_w1[l].astype(MXU_DTYPE), mlp_w2[l].astype(MXU_DTYPE))
    return xall
```
